```python
import math
import jax
import jax.numpy as jnp
from jax import lax
import numpy as np

D_MODEL = 1024
BATCH = 2
SEQ = 8192
DEPTH = 2

GRID_W = 64
CTX_LEN = 256
HEAD_DIM = 64
Q_BLOCK = 128
ROPE_THETA = 10000.0
EPS = 1e-6
N_EVEN = (DEPTH + 1) // 2
N_ODD = DEPTH // 2
A_WIDTH = D_MODEL // 2
A_GROUPS = A_WIDTH // HEAD_DIM
CHUNK = 128
B_HEADS = D_MODEL // (4 * HEAD_DIM)
B_WIDTH = B_HEADS * 2 * HEAD_DIM
E_SPLITS = (A_WIDTH, 2 * A_WIDTH, 2 * A_WIDTH + B_WIDTH, 2 * A_WIDTH + 2 * B_WIDTH)
E_IN = 2 * A_WIDTH + 3 * B_WIDTH
C_HEADS = (D_MODEL // 2) // HEAD_DIM
C_KV_HEADS = 2
C_GROUP = C_HEADS // C_KV_HEADS
C_WIDTH = C_HEADS * HEAD_DIM
C_KV_WIDTH = C_KV_HEADS * HEAD_DIM
D_WIDTH = D_MODEL // 2
D_BLOCKS = D_WIDTH // HEAD_DIM
CONV_W = 4
LRU_C = 8.0
O_SPLITS = (C_WIDTH, C_WIDTH + C_KV_WIDTH, C_WIDTH + 2 * C_KV_WIDTH, C_WIDTH + 2 * C_KV_WIDTH + D_WIDTH)
O_IN = C_WIDTH + 2 * C_KV_WIDTH + 2 * D_WIDTH
FFN_DIM = 2816
N_EXPERTS = 8
TOP_K = 2

kernel_name = 'hybrid_diffusion_gmlp_diffattn_gqa_rglru_moe'


def rms_norm(x, g):
    xf = x.astype(jnp.float32)
    y = xf * lax.rsqrt(jnp.mean(xf * xf, axis=-1, keepdims=True) + EPS)
    return (y * g.astype(jnp.float32)).astype(x.dtype)


def rope_1d(x, pos):
    half = x.shape[-1] // 2
    freqs = ROPE_THETA ** (-jnp.arange(half, dtype=jnp.float32) / half)
    ang = pos.astype(jnp.float32)[:, None] * freqs[None, :]
    cos = jnp.cos(ang)[None, :, None, :]
    sin = jnp.sin(ang)[None, :, None, :]
    xf = x.astype(jnp.float32)
    x1, x2 = xf[..., :half], xf[..., half:]
    return jnp.concatenate([x1 * cos - x2 * sin, x1 * sin + x2 * cos], axis=-1).astype(x.dtype)


def axial_rope(x, row, col):
    h = x.shape[-1] // 2
    return jnp.concatenate([rope_1d(x[..., :h], row), rope_1d(x[..., h:], col)], axis=-1)


def sweep_query_blocks(fn, q):
    b, l = q.shape[:2]
    nb = l // Q_BLOCK
    qb = jnp.moveaxis(q.reshape(b, nb, Q_BLOCK, *q.shape[2:]), 1, 0)
    out = lax.map(fn, qb)
    return jnp.moveaxis(out, 0, 1).reshape(b, l, *out.shape[3:])


def diff_attention(q, k, v, lam):
    s = jnp.einsum('bqhcd,bkhcd->bhcqk', q, k, preferred_element_type=jnp.float32) * (HEAD_DIM ** -0.5)
    p = jax.nn.softmax(s, axis=-1)
    w = p[:, :, 0] - lam * p[:, :, 1]
    return jnp.einsum('bhqk,bkhe->bqhe', w.astype(v.dtype), v)


def gqa_attention(q, k, v):
    s = jnp.einsum('bqngd,bknd->bngqk', q, k, preferred_element_type=jnp.float32) * (HEAD_DIM ** -0.5)
    p = jax.nn.softmax(s, axis=-1).astype(v.dtype)
    return jnp.einsum('bngqk,bknd->bqngd', p, v)


def chunk_gmlp(ua, va, norm_g, ws, bs):
    b, l, _ = ua.shape
    u = jax.nn.gelu(ua).reshape(b, l // CHUNK, CHUNK, A_GROUPS, HEAD_DIM)
    v = rms_norm(jax.nn.gelu(va), norm_g).reshape(b, l // CHUNK, CHUNK, A_GROUPS, HEAD_DIM)
    mixed = jnp.einsum('gpq,bnqgc->bnpgc', ws, v) + bs.T[:, :, None]
    return (u * mixed).reshape(b, l, A_WIDTH)


def depthwise_conv(x, w, b):
    left = CONV_W // 2
    out = lax.conv_general_dilated(x, w[:, None, :], window_strides=(1,), padding=[(left, CONV_W - 1 - left)],
                                   dimension_numbers=('NWC', 'WIO', 'NWC'), feature_group_count=x.shape[-1])
    return out + b


def block_diag(x, w, b):
    xb = x.reshape(*x.shape[:-1], D_BLOCKS, -1)
    return jnp.einsum('blnc,ncd->blnd', xb, w).reshape(x.shape) + b


def lru_coeffs(xd, wa, ba, wx, bx, lam):
    r = jax.nn.sigmoid(block_diag(xd, wa, ba).astype(jnp.float32))
    i = jax.nn.sigmoid(block_diag(xd, wx, bx).astype(jnp.float32))
    log_a = -LRU_C * r * jax.nn.softplus(-lam.astype(jnp.float32))
    a = jnp.exp(log_a)
    bterm = jnp.sqrt(-jnp.expm1(2.0 * log_a)) * (i * xd.astype(jnp.float32))
    return a, bterm


def linear_scan(a, b, h0, reverse):
    if h0 is not None:
        idx = -1 if reverse else 0
        b = b.at[:, idx].add(a[:, idx] * h0)

    def combine(left, right):
        a_l, b_l = left
        a_r, b_r = right
        return a_l * a_r, a_r * b_l + b_r

    _, h = lax.associative_scan(combine, (a, b), axis=1, reverse=reverse)
    return h


def swiglu(t, w1, w3, w2):
    return (jax.nn.silu(t @ w1) * (t @ w3)) @ w2


def moe_swiglu(t, wr, br, w1, w3, w2):
    shape = t.shape
    tf = t.reshape(-1, shape[-1])
    logits = (tf @ wr + br).astype(jnp.float32)
    top_v, top_i = lax.top_k(logits, TOP_K)
    gates = jax.nn.softmax(top_v, axis=-1)
    dense_gate = jnp.sum(jax.nn.one_hot(top_i, N_EXPERTS, dtype=jnp.float32) * gates[..., None], axis=1).astype(t.dtype)
    out = jnp.zeros_like(tf)
    for e in range(N_EXPERTS):
        out = out + dense_gate[:, e:e + 1] * swiglu(tf, w1[e], w3[e], w2[e])
    return out.reshape(shape)


def even_mixer(h, hc, row, col, w_in, a_norm_g, a_ws, a_bs, qn_g, kn_g, lq1, lk1, lq2, lk2, subln_g, lam_init, need_ctx):
    b, l, _ = h.shape
    lc = hc.shape[1]
    ua, va, q, k, v = jnp.split(h @ w_in, E_SPLITS, axis=-1)
    uac, vac, qc, kc, vc = jnp.split(hc @ w_in, E_SPLITS, axis=-1)

    def qk(t, g, n):
        return rms_norm(t.reshape(b, n, 2 * B_HEADS, HEAD_DIM), g)

    q = axial_rope(qk(q, qn_g, l), row, col).reshape(b, l, B_HEADS, 2, HEAD_DIM)
    k = axial_rope(qk(k, kn_g, l), row, col).reshape(b, l, B_HEADS, 2, HEAD_DIM)
    kc = qk(kc, kn_g, lc).reshape(b, lc, B_HEADS, 2, HEAD_DIM)
    vc = vc.reshape(b, lc, B_HEADS, 2 * HEAD_DIM)
    k_all = jnp.concatenate([kc, k], axis=1)
    v_all = jnp.concatenate([vc, v.reshape(b, l, B_HEADS, 2 * HEAD_DIM)], axis=1)
    lam = (jnp.exp(jnp.sum(lq1.astype(jnp.float32) * lk1.astype(jnp.float32)))
           - jnp.exp(jnp.sum(lq2.astype(jnp.float32) * lk2.astype(jnp.float32))) + lam_init)

    def diff_out(o):
        return (rms_norm(o, subln_g) * (1.0 - lam_init)).reshape(*o.shape[:2], B_WIDTH)

    yb = diff_out(sweep_query_blocks(lambda qb: diff_attention(qb, k_all, v_all, lam), q))
    ya = chunk_gmlp(ua, va, a_norm_g, a_ws, a_bs)
    y = jnp.concatenate([ya, yb], axis=-1)
    if not need_ctx:
        return y, None
    qc = qk(qc, qn_g, lc).reshape(b, lc, B_HEADS, 2, HEAD_DIM)
    ybc = diff_out(diff_attention(qc, kc, vc, lam))
    yac = chunk_gmlp(uac, vac, a_norm_g, a_ws, a_bs)
    return y, jnp.concatenate([yac, ybc], axis=-1)


def odd_mixer(h, hc, row, col, w_in, qn_g, kn_g, conv_w, conv_b, wa, ba, wx, bx, lam, need_ctx):
    b, l, _ = h.shape
    lc = hc.shape[1]
    q, k, v, gate, xr = jnp.split(h @ w_in, O_SPLITS, axis=-1)
    qc, kc, vc, gatec, xrc = jnp.split(hc @ w_in, O_SPLITS, axis=-1)
    q = axial_rope(rms_norm(q.reshape(b, l, C_HEADS, HEAD_DIM), qn_g), row, col).reshape(b, l, C_KV_HEADS, C_GROUP, HEAD_DIM)
    k = axial_rope(rms_norm(k.reshape(b, l, C_KV_HEADS, HEAD_DIM), kn_g), row, col)
    kc = rms_norm(kc.reshape(b, lc, C_KV_HEADS, HEAD_DIM), kn_g)
    vc = vc.reshape(b, lc, C_KV_HEADS, HEAD_DIM)
    k_all = jnp.concatenate([kc, k], axis=1)
    v_all = jnp.concatenate([vc, v.reshape(b, l, C_KV_HEADS, HEAD_DIM)], axis=1)
    y_attn = sweep_query_blocks(lambda qb: gqa_attention(qb, k_all, v_all), q).reshape(b, l, C_WIDTH)
    xd = depthwise_conv(xr, conv_w, conv_b)
    xdc = depthwise_conv(xrc, conv_w, conv_b)
    h_lat, h_ctx = [], []
    for d, rev in enumerate((False, True)):
        a_c, b_c = lru_coeffs(xdc, wa[d], ba[d], wx[d], bx[d], lam[d])
        hs_c = linear_scan(a_c, b_c, None, rev)
        a_l, b_l = lru_coeffs(xd, wa[d], ba[d], wx[d], bx[d], lam[d])
        h_lat.append(linear_scan(a_l, b_l, hs_c[:, 0] if rev else hs_c[:, -1], rev))
        h_ctx.append(hs_c)
    y_rec = (h_lat[0] + h_lat[1]).astype(h.dtype) * jax.nn.gelu(gate)
    y = jnp.concatenate([y_attn, y_rec], axis=-1)
    if not need_ctx:
        return y, None
    qc = rms_norm(qc.reshape(b, lc, C_HEADS, HEAD_DIM), qn_g).reshape(b, lc, C_KV_HEADS, C_GROUP, HEAD_DIM)
    yc_attn = gqa_attention(qc, kc, vc).reshape(b, lc, C_WIDTH)
    yc_rec = (h_ctx[0] + h_ctx[1]).astype(hc.dtype) * jax.nn.gelu(gatec)
    return y, jnp.concatenate([yc_attn, yc_rec], axis=-1)


def setup_inputs(seed: int = 0) -> dict:
    key = jax.random.key(seed)
    ks = iter(jax.random.split(key, 64))

    def nrm(shape, scale=1.0):
        return jax.random.normal(next(ks), shape, jnp.float32) * scale

    def gain(shape):
        return 1.0 + 0.05 * nrm(shape)

    D, F, E, NE, NO = D_MODEL, FFN_DIM, N_EXPERTS, N_EVEN, N_ODD
    bs = D_WIDTH // D_BLOCKS
    u = jax.random.uniform(next(ks), (NO, 2, D_WIDTH), jnp.float32, 0.9, 0.999)
    p = u ** (1.0 / LRU_C)
    d_lambda = jnp.log(p) - jnp.log1p(-p)
    return {
        'x': nrm((BATCH, SEQ, D)),
        'c': nrm((BATCH, D)),
        'ctx': nrm((BATCH, CTX_LEN, D)),
        'c_ctx': nrm((D,)),
        'w_mod': nrm((DEPTH, D, 6 * D), 0.5 * D ** -0.5),
        'b_mod': nrm((DEPTH, 6 * D), 0.01),
        'norm_mix_g': gain((DEPTH, D)),
        'norm_ffn_g': gain((DEPTH, D)),
        'w_out': nrm((DEPTH, D, D), D ** -0.5),
        'e_w_in': nrm((NE, D, E_IN), D ** -0.5),
        'a_norm_g': gain((NE, A_WIDTH)),
        'a_ws': nrm((NE, A_GROUPS, CHUNK, CHUNK), CHUNK ** -0.5),
        'a_bs': 1.0 + nrm((NE, A_GROUPS, CHUNK), 0.02),
        'b_qnorm_g': gain((NE, HEAD_DIM)),
        'b_knorm_g': gain((NE, HEAD_DIM)),
        'b_lq1': nrm((NE, HEAD_DIM), 0.1),
        'b_lk1': nrm((NE, HEAD_DIM), 0.1),
        'b_lq2': nrm((NE, HEAD_DIM), 0.1),
        'b_lk2': nrm((NE, HEAD_DIM), 0.1),
        'b_subln_g': gain((NE, 2 * HEAD_DIM)),
        'ffn_w1': nrm((NE, D, F), D ** -0.5),
        'ffn_w3': nrm((NE, D, F), D ** -0.5),
        'ffn_w2': nrm((NE, F, D), F ** -0.5),
        'o_w_in': nrm((NO, D, O_IN), D ** -0.5),
        'c_qnorm_g': gain((NO, HEAD_DIM)),
        'c_knorm_g': gain((NO, HEAD_DIM)),
        'd_conv_w': nrm((NO, CONV_W, D_WIDTH), CONV_W ** -0.5),
        'd_conv_b': nrm((NO, D_WIDTH), 0.01),
        'd_wa': nrm((NO, 2, D_BLOCKS, bs, bs), bs ** -0.5),
        'd_ba': nrm((NO, 2, D_WIDTH), 0.01),
        'd_wx': nrm((NO, 2, D_BLOCKS, bs, bs), bs ** -0.5),
        'd_bx': nrm((NO, 2, D_WIDTH), 0.01),
        'd_lambda': d_lambda,
        'router_w': nrm((NO, D, E), D ** -0.5),
        'router_b': nrm((NO, E), 0.01),
        'moe_w1': nrm((NO, E, D, F), D ** -0.5),
        'moe_w3': nrm((NO, E, D, F), D ** -0.5),
        'moe_w2': nrm((NO, E, F, D), F ** -0.5),
    }


def reference(x, c, ctx, c_ctx, w_mod, b_mod, norm_mix_g, norm_ffn_g, w_out,
              e_w_in, a_norm_g, a_ws, a_bs, b_qnorm_g, b_knorm_g, b_lq1, b_lk1, b_lq2, b_lk2, b_subln_g,
              ffn_w1, ffn_w3, ffn_w2,
              o_w_in, c_qnorm_g, c_knorm_g, d_conv_w, d_conv_b, d_wa, d_ba, d_wx, d_bx, d_lambda,
              router_w, router_b, moe_w1, moe_w3, moe_w2):
    n_lat = x.shape[1]
    rows = n_lat // GRID_W
    row = jnp.repeat(jnp.arange(rows, dtype=jnp.int32), GRID_W)
    col = jnp.tile(jnp.arange(GRID_W, dtype=jnp.int32), rows)
    xc = ctx
    for layer in range(DEPTH):
        need_ctx = layer < DEPTH - 1
        j = layer // 2
        mod = jax.nn.silu(c) @ w_mod[layer] + b_mod[layer]
        modc = jax.nn.silu(c_ctx) @ w_mod[layer] + b_mod[layer]
        sh1, sc1, g1, sh2, sc2, g2 = jnp.split(mod[:, None, :], 6, axis=-1)
        sh1c, sc1c, g1c, sh2c, sc2c, g2c = jnp.split(modc, 6, axis=-1)
        h = rms_norm(x, norm_mix_g[layer]) * (1.0 + sc1) + sh1
        hc = rms_norm(xc, norm_mix_g[layer]) * (1.0 + sc1c) + sh1c
        if layer % 2 == 0:
            lam_init = 0.8 - 0.6 * math.exp(-0.3 * layer)
            y, yc = even_mixer(h, hc, row, col, e_w_in[j], a_norm_g[j], a_ws[j], a_bs[j], b_qnorm_g[j], b_knorm_g[j],
                               b_lq1[j], b_lk1[j], b_lq2[j], b_lk2[j], b_subln_g[j], lam_init, need_ctx)

            def channel(t):
                return swiglu(t, ffn_w1[j], ffn_w3[j], ffn_w2[j])
        else:
            y, yc = odd_mixer(h, hc, row, col, o_w_in[j], c_qnorm_g[j], c_knorm_g[j], d_conv_w[j], d_conv_b[j],
                              d_wa[j], d_ba[j], d_wx[j], d_bx[j], d_lambda[j], need_ctx)

            def channel(t):
                return moe_swiglu(t, router_w[j], router_b[j], moe_w1[j], moe_w3[j], moe_w2[j])
        x = x + g1 * (y @ w_out[layer])
        x = x + g2 * channel(rms_norm(x, norm_ffn_g[layer]) * (1.0 + sc2) + sh2)
        if need_ctx:
            xc = xc + g1c * (yc @ w_out[layer])
            xc = xc + g2c * channel(rms_norm(xc, norm_ffn_g[layer]) * (1.0 + sc2c) + sh2c)
    return x
```

```python
import functools
import math

import jax
import jax.numpy as jnp
from jax import lax
from jax.experimental import pallas as pl
from jax.experimental.pallas import tpu as pltpu

F32 = jnp.float32
BF16 = jnp.bfloat16

D_MODEL = 1024
HEAD_DIM = 64
GRID_W = 64
ROPE_THETA = 10000.0
EPS = 1e-6
CHUNK = 128
A_WIDTH = D_MODEL // 2
A_GROUPS = A_WIDTH // HEAD_DIM
B_HEADS = D_MODEL // (4 * HEAD_DIM)
B_WIDTH = B_HEADS * 2 * HEAD_DIM
E_IN = 2 * A_WIDTH + 3 * B_WIDTH
C_HEADS = (D_MODEL // 2) // HEAD_DIM
C_KV_HEADS = 2
C_GROUP = C_HEADS // C_KV_HEADS
C_WIDTH = C_HEADS * HEAD_DIM
C_KV_WIDTH = C_KV_HEADS * HEAD_DIM
D_WIDTH = D_MODEL // 2
D_BLOCKS = D_WIDTH // HEAD_DIM
CONV_W = 4
LRU_C = 8.0
O_IN = C_WIDTH + 2 * C_KV_WIDTH + 2 * D_WIDTH
FFN_DIM = 2816
N_EXPERTS = 8

LANES = 128
SUBLANES = 8
KV_BLOCK = 256
VMEM_LIMIT = 56 * 1024 * 1024


def _cparams(sem):
    return pltpu.CompilerParams(dimension_semantics=sem, vmem_limit_bytes=VMEM_LIMIT)


def _modnorm(x, g, scp, sh):
    y = x * lax.rsqrt(jnp.mean(x * x, axis=-1, keepdims=True) + EPS)
    return (y * g) * scp + sh


def _mod_kernel(c_ref, w_ref, b_ref, o_ref):
    c = c_ref[...]
    s = c * jax.nn.sigmoid(c)
    o_ref[0] = jnp.dot(s.astype(BF16), w_ref[0].astype(BF16), preferred_element_type=F32) + b_ref[0]


def _modulation(cc, w_mod, b_mod):
    depth, d, n = w_mod.shape
    tn = 1536
    return pl.pallas_call(
        _mod_kernel,
        grid=(depth, n // tn),
        in_specs=[
            pl.BlockSpec((SUBLANES, d), lambda l, j: (0, 0)),
            pl.BlockSpec((1, d, tn), lambda l, j: (l, 0, j)),
            pl.BlockSpec((1, 1, tn), lambda l, j: (l, 0, j)),
        ],
        out_specs=pl.BlockSpec((1, SUBLANES, tn), lambda l, j: (l, 0, j)),
        out_shape=jax.ShapeDtypeStruct((depth, SUBLANES, n), F32),
        compiler_params=_cparams(("arbitrary", "arbitrary")),
        name="modulation",
    )(cc, w_mod, b_mod.reshape(depth, 1, n))


def _head_norm_rope_t(xh, g, cos_t, sin_t):
    xn = xh * lax.rsqrt(jnp.mean(xh * xh, axis=0, keepdims=True) + EPS) * g
    sw = jnp.concatenate([xn[16:32], xn[0:16], xn[48:64], xn[32:48]], axis=0)
    return xn * cos_t + sw * sin_t


def _proj_even_kernel(x_ref, scp_ref, sh_ref, ng_ref, w_ref, ang_ref, qg_ref, kg_ref, cos_ref, sin_ref,
                      u_ref, vn_ref, qt_ref, k_ref, vt_ref, *, tm):
    h = _modnorm(x_ref[...], ng_ref[...], scp_ref[0], sh_ref[0])
    y = jnp.dot(h.astype(BF16), w_ref[...], preferred_element_type=F32)
    ua = y[:, 0:A_WIDTH]
    va = y[:, A_WIDTH:2 * A_WIDTH]
    q = y[:, 2 * A_WIDTH:2 * A_WIDTH + B_WIDTH]
    k = y[:, 2 * A_WIDTH + B_WIDTH:2 * A_WIDTH + 2 * B_WIDTH]
    v = y[:, 2 * A_WIDTH + 2 * B_WIDTH:]
    u_ref[...] = jax.nn.gelu(ua)
    gv = jax.nn.gelu(va)
    vn = gv * lax.rsqrt(jnp.mean(gv * gv, axis=-1, keepdims=True) + EPS) * ang_ref[...]
    vn_ref[...] = vn.astype(BF16)
    cos_t = cos_ref[...]
    sin_t = sin_ref[...]
    q_t = q.T
    k_t = k.T
    zeros = jnp.zeros((HEAD_DIM, tm), F32)
    k_parts = []
    for j in range(2 * B_HEADS):
        sl = slice(j * HEAD_DIM, (j + 1) * HEAD_DIM)
        qh = _head_norm_rope_t(q_t[sl], qg_ref[...], cos_t, sin_t) * (HEAD_DIM ** -0.5)
        blk = jnp.concatenate([qh, zeros] if j % 2 == 0 else [zeros, qh], axis=0)
        qt_ref[j] = blk.astype(BF16)
        k_parts.append(_head_norm_rope_t(k_t[sl], kg_ref[...], cos_t, sin_t))
    k_ref[...] = jnp.concatenate(k_parts, axis=0).T.astype(BF16)
    v_t = v.T.astype(BF16)
    for c in range(tm // KV_BLOCK):
        vt_ref[c] = v_t[:, c * KV_BLOCK:(c + 1) * KV_BLOCK]


def _proj_odd_kernel(x_ref, scp_ref, sh_ref, ng_ref, w_ref, qg_ref, kg_ref, cos_ref, sin_ref,
                     qt_ref, k_ref, vt_ref, gg_ref, xr_ref, *, tm):
    h = _modnorm(x_ref[...], ng_ref[...], scp_ref[0], sh_ref[0])
    y = jnp.dot(h.astype(BF16), w_ref[...], preferred_element_type=F32)
    q = y[:, 0:C_WIDTH]
    k = y[:, C_WIDTH:C_WIDTH + C_KV_WIDTH]
    v = y[:, C_WIDTH + C_KV_WIDTH:C_WIDTH + 2 * C_KV_WIDTH]
    gate = y[:, C_WIDTH + 2 * C_KV_WIDTH:C_WIDTH + 2 * C_KV_WIDTH + D_WIDTH]
    xr = y[:, C_WIDTH + 2 * C_KV_WIDTH + D_WIDTH:]
    gg_ref[...] = jax.nn.gelu(gate)
    xr_ref[...] = xr
    cos_t = cos_ref[...]
    sin_t = sin_ref[...]
    q_t = q.T
    k_t = k.T
    zeros = jnp.zeros((HEAD_DIM, tm), F32)
    for j in range(C_HEADS):
        sl = slice(j * HEAD_DIM, (j + 1) * HEAD_DIM)
        qh = _head_norm_rope_t(q_t[sl], qg_ref[...], cos_t, sin_t) * (HEAD_DIM ** -0.5)
        blk = jnp.concatenate([qh, zeros] if j // C_GROUP == 0 else [zeros, qh], axis=0)
        qt_ref[j] = blk.astype(BF16)
    k_parts = [_head_norm_rope_t(k_t[j * HEAD_DIM:(j + 1) * HEAD_DIM], kg_ref[...], cos_t, sin_t)
               for j in range(C_KV_HEADS)]
    k_ref[...] = jnp.concatenate(k_parts, axis=0).T.astype(BF16)
    v_t = v.T.astype(BF16)
    for c in range(tm // KV_BLOCK):
        vt_ref[c] = v_t[:, c * KV_BLOCK:(c + 1) * KV_BLOCK]


def _projection(even, x2, scp, sh, ng, w, gains, cos_t, sin_t, tm):
    r = x2.shape[0]
    nt = r // tm
    tiles_per_group = nt // scp.shape[0]
    pos_tiles = cos_t.shape[1] // tm
    n_in = w.shape[1]
    row = lambda i: (i, 0)
    grp = lambda i: (i // tiles_per_group, 0, 0)
    fixed = lambda i: (0, 0)
    pos = lambda i: (0, i % pos_tiles)
    in_specs = [
        pl.BlockSpec((tm, D_MODEL), row),
        pl.BlockSpec((1, 1, D_MODEL), grp),
        pl.BlockSpec((1, 1, D_MODEL), grp),
        pl.BlockSpec((1, D_MODEL), fixed),
        pl.BlockSpec((D_MODEL, n_in), fixed),
    ]
    in_specs += [pl.BlockSpec(g.shape, fixed) for g in gains]
    in_specs += [pl.BlockSpec((HEAD_DIM, tm), pos), pl.BlockSpec((HEAD_DIM, tm), pos)]
    if even:
        kw, vw, nh = B_WIDTH, B_WIDTH, 2 * B_HEADS
        body = functools.partial(_proj_even_kernel, tm=tm)
        out_shape = [
            jax.ShapeDtypeStruct((r, A_WIDTH), F32),
            jax.ShapeDtypeStruct((r, A_WIDTH), BF16),
            jax.ShapeDtypeStruct((nh, LANES, r), BF16),
            jax.ShapeDtypeStruct((r, kw), BF16),
            jax.ShapeDtypeStruct((r // KV_BLOCK, vw, KV_BLOCK), BF16),
        ]
        out_specs = [
            pl.BlockSpec((tm, A_WIDTH), row),
            pl.BlockSpec((tm, A_WIDTH), row),
            pl.BlockSpec((nh, LANES, tm), lambda i: (0, 0, i)),
            pl.BlockSpec((tm, kw), row),
            pl.BlockSpec((tm // KV_BLOCK, vw, KV_BLOCK), lambda i: (i, 0, 0)),
        ]
    else:
        kw, vw, nh = C_KV_WIDTH, C_KV_WIDTH, C_HEADS
        body = functools.partial(_proj_odd_kernel, tm=tm)
        out_shape = [
            jax.ShapeDtypeStruct((nh, LANES, r), BF16),
            jax.ShapeDtypeStruct((r, kw), BF16),
            jax.ShapeDtypeStruct((r // KV_BLOCK, vw, KV_BLOCK), BF16),
            jax.ShapeDtypeStruct((r, D_WIDTH), F32),
            jax.ShapeDtypeStruct((r, D_WIDTH), F32),
        ]
        out_specs = [
            pl.BlockSpec((nh, LANES, tm), lambda i: (0, 0, i)),
            pl.BlockSpec((tm, kw), row),
            pl.BlockSpec((tm // KV_BLOCK, vw, KV_BLOCK), lambda i: (i, 0, 0)),
            pl.BlockSpec((tm, D_WIDTH), row),
            pl.BlockSpec((tm, D_WIDTH), row),
        ]
    return pl.pallas_call(
        body, grid=(nt,), in_specs=in_specs, out_specs=out_specs, out_shape=out_shape,
        compiler_params=_cparams(("arbitrary",)),
        name="proj_even" if even else "proj_odd",
    )(x2, scp, sh, ng, w, *gains, cos_t, sin_t)


def _attn_kernel(*refs, mode, n_lat, lam_init):
    qt_ref, kc_ref, vtc_ref = refs[:3]
    pos = 3
    if n_lat:
        kl_ref, vtl_ref = refs[3:5]
        pos = 5
    if mode == "diff":
        lamp_ref, sg_ref = refs[pos:pos + 2]
        pos += 2
    o_ref, acc_ref = refs[pos:pos + 2]
    tq = qt_ref.shape[-1]
    q = (qt_ref[0], qt_ref[1])
    acc_ref[...] = jnp.zeros(acc_ref.shape, F32)

    def step(kb, vtb, carry):
        out = []
        for c in range(2):
            m_old, l_old = carry[2 * c], carry[2 * c + 1]
            s = jnp.dot(kb, q[c], preferred_element_type=F32)
            m_new = jnp.maximum(m_old, jnp.max(s, axis=0, keepdims=True))
            p = jnp.exp(s - m_new)
            alpha = jnp.exp(m_old - m_new)
            l_new = alpha * l_old + jnp.sum(p, axis=0, keepdims=True)
            acc_ref[c] = alpha * acc_ref[c] + jnp.dot(vtb, p.astype(BF16), preferred_element_type=F32)
            out += [m_new, l_new]
        return tuple(out)

    neg = jnp.full((1, tq), -jnp.inf, F32)
    zero = jnp.zeros((1, tq), F32)
    carry = step(kc_ref[...], vtc_ref[0], (neg, zero, neg, zero))
    if n_lat:
        def body(j, carry):
            start = pl.multiple_of(j * KV_BLOCK, KV_BLOCK)
            return step(kl_ref[pl.ds(start, KV_BLOCK), :], vtl_ref[j], carry)
        carry = lax.fori_loop(0, n_lat, body, carry)
    _, l0, _, l1 = carry
    if mode == "diff":
        lp = lamp_ref[...]
        lam = (jnp.exp(jnp.sum(lp[0:1] * lp[1:2], axis=-1, keepdims=True))
               - jnp.exp(jnp.sum(lp[2:3] * lp[3:4], axis=-1, keepdims=True)) + lam_init)
        o = acc_ref[0] * (1.0 / l0) - lam * (acc_ref[1] * (1.0 / l1))
        o = o * lax.rsqrt(jnp.mean(o * o, axis=0, keepdims=True) + EPS) * sg_ref[...] * (1.0 - lam_init)
    else:
        row0 = pl.multiple_of((pl.program_id(1) // (C_GROUP // 2)) * HEAD_DIM, HEAD_DIM)
        o = jnp.concatenate([acc_ref[0, pl.ds(row0, HEAD_DIM), :] * (1.0 / l0),
                             acc_ref[1, pl.ds(row0, HEAD_DIM), :] * (1.0 / l1)], axis=0)
    o_ref[...] = o.T.astype(o_ref.dtype)


def _attention(mode, qt, kc, vtc, kl, vtl, extra, batch, tq, lam_init=0.0):
    units = qt.shape[0] // 2
    rq = qt.shape[2]
    nq = rq // batch // tq
    lc = kc.shape[0] // batch
    col = (lambda u: u) if mode == "diff" else (lambda u: 0)
    in_specs = [
        pl.BlockSpec((2, LANES, tq), lambda b, u, i: (u, 0, b * nq + i)),
        pl.BlockSpec((lc, LANES), lambda b, u, i: (b, col(u))),
        pl.BlockSpec((lc // KV_BLOCK, LANES, KV_BLOCK), lambda b, u, i: (b, col(u), 0)),
    ]
    args = [qt, kc, vtc]
    n_lat = 0
    if kl is not None:
        ll = kl.shape[0] // batch
        n_lat = ll // KV_BLOCK
        in_specs += [
            pl.BlockSpec((ll, LANES), lambda b, u, i: (b, col(u))),
            pl.BlockSpec((n_lat, LANES, KV_BLOCK), lambda b, u, i: (b, col(u), 0)),
        ]
        args += [kl, vtl]
    for e in extra:
        in_specs.append(pl.BlockSpec(e.shape, lambda b, u, i: (0, 0)))
        args.append(e)
    return pl.pallas_call(
        functools.partial(_attn_kernel, mode=mode, n_lat=n_lat, lam_init=lam_init),
        grid=(batch, units, nq),
        in_specs=in_specs,
        out_specs=pl.BlockSpec((tq, LANES), lambda b, u, i: (b * nq + i, u)),
        out_shape=jax.ShapeDtypeStruct((rq, units * LANES), BF16),
        scratch_shapes=[pltpu.VMEM((2, LANES, tq), F32)],
        compiler_params=_cparams(("arbitrary", "arbitrary", "arbitrary")),
        name="attn_" + mode,
    )(*args)


def _gmlp_kernel(u_ref, vn_ref, ws_ref, bias_ref, o_ref, *, tm):
    lane = lax.broadcasted_iota(jnp.int32, (CHUNK, LANES), 1)
    for c in range(tm // CHUNK):
        rows = slice(c * CHUNK, (c + 1) * CHUNK)
        for j in range(A_GROUPS // 2):
            cols = slice(j * LANES, (j + 1) * LANES)
            rhs = vn_ref[rows, cols]
            lo = jnp.dot(ws_ref[2 * j], rhs, preferred_element_type=F32)
            hi = jnp.dot(ws_ref[2 * j + 1], rhs, preferred_element_type=F32)
            mixed = jnp.where(lane < HEAD_DIM, lo, hi) + bias_ref[:, cols]
            o_ref[rows, cols] = (u_ref[rows, cols] * mixed).astype(o_ref.dtype)


def _gmlp(u, vn, ws, bias_map, tm):
    r = u.shape[0]
    row = lambda i: (i, 0)
    return pl.pallas_call(
        functools.partial(_gmlp_kernel, tm=tm),
        grid=(r // tm,),
        in_specs=[
            pl.BlockSpec((tm, A_WIDTH), row),
            pl.BlockSpec((tm, A_WIDTH), row),
            pl.BlockSpec(ws.shape, lambda i: (0, 0, 0)),
            pl.BlockSpec(bias_map.shape, lambda i: (0, 0)),
        ],
        out_specs=pl.BlockSpec((tm, A_WIDTH), row),
        out_shape=jax.ShapeDtypeStruct((r, A_WIDTH), BF16),
        compiler_params=_cparams(("arbitrary",)),
        name="gmlp",
    )(u, vn, ws, bias_map)


def _out_kernel(x_ref, ya_ref, yb_ref, w_ref, g_ref, o_ref):
    half = ya_ref.shape[1]
    y = (jnp.dot(ya_ref[...], w_ref[0:half, :], preferred_element_type=F32)
         + jnp.dot(yb_ref[...], w_ref[half:, :], preferred_element_type=F32))
    o_ref[...] = x_ref[...] + g_ref[0] * y


def _out_proj(x2, ya, yb, w, g, tm):
    r = x2.shape[0]
    tiles_per_group = (r // tm) // g.shape[0]
    row = lambda i: (i, 0)
    return pl.pallas_call(
        _out_kernel,
        grid=(r // tm,),
        in_specs=[
            pl.BlockSpec((tm, D_MODEL), row),
            pl.BlockSpec((tm, ya.shape[1]), row),
            pl.BlockSpec((tm, yb.shape[1]), row),
            pl.BlockSpec(w.shape, lambda i: (0, 0)),
            pl.BlockSpec((1, 1, D_MODEL), lambda i: (i // tiles_per_group, 0, 0)),
        ],
        out_specs=pl.BlockSpec((tm, D_MODEL), row),
        out_shape=jax.ShapeDtypeStruct((r, D_MODEL), F32),
        compiler_params=_cparams(("arbitrary",)),
        name="out_proj",
    )(x2, ya, yb, w, g)


def _ffn_kernel(*refs, gated):
    if gated:
        x_ref, scp_ref, sh_ref, g_ref, ng_ref, w1_ref, w3_ref, w2_ref, gate_ref, o_ref, h_ref, acc_ref = refs
    else:
        x_ref, scp_ref, sh_ref, g_ref, ng_ref, w1_ref, w3_ref, w2_ref, o_ref, h_ref, acc_ref = refs
    e = pl.program_id(1)
    f = pl.program_id(2)
    first = jnp.logical_and(e == 0, f == 0)
    last = jnp.logical_and(e == pl.num_programs(1) - 1, f == pl.num_programs(2) - 1)

    @pl.when(first)
    def _():
        h_ref[...] = _modnorm(x_ref[...], ng_ref[...], scp_ref[0], sh_ref[0]).astype(BF16)
        acc_ref[...] = jnp.zeros(acc_ref.shape, F32)

    h = h_ref[...]
    a = jnp.dot(h, w1_ref[0], preferred_element_type=F32)
    b = jnp.dot(h, w3_ref[0], preferred_element_type=F32)
    t = (a * jax.nn.sigmoid(a)) * b
    y = jnp.dot(t.astype(BF16), w2_ref[0], preferred_element_type=F32)
    if gated:
        lane = lax.broadcasted_iota(jnp.int32, gate_ref.shape, 1)
        ge = jnp.sum(jnp.where(lane == e, gate_ref[...], 0.0), axis=-1, keepdims=True)
        y = ge * y
    acc_ref[...] += y

    @pl.when(last)
    def _():
        o_ref[...] = x_ref[...] + g_ref[0] * acc_ref[...]


def _ffn(x2, scp, sh, g, ng, w1, w3, w2, gate, tm, tf):
    r = x2.shape[0]
    ne, _, fdim = w1.shape
    tiles_per_group = (r // tm) // g.shape[0]
    row = lambda i, e, f: (i, 0)
    grp = lambda i, e, f: (i // tiles_per_group, 0, 0)
    in_specs = [
        pl.BlockSpec((tm, D_MODEL), row),
        pl.BlockSpec((1, 1, D_MODEL), grp),
        pl.BlockSpec((1, 1, D_MODEL), grp),
        pl.BlockSpec((1, 1, D_MODEL), grp),
        pl.BlockSpec((1, D_MODEL), lambda i, e, f: (0, 0)),
        pl.BlockSpec((1, D_MODEL, tf), lambda i, e, f: (e, 0, f)),
        pl.BlockSpec((1, D_MODEL, tf), lambda i, e, f: (e, 0, f)),
        pl.BlockSpec((1, tf, D_MODEL), lambda i, e, f: (e, f, 0)),
    ]
    args = [x2, scp, sh, g, ng, w1, w3, w2]
    if gate is not None:
        in_specs.append(pl.BlockSpec((tm, LANES), row))
        args.append(gate)
    return pl.pallas_call(
        functools.partial(_ffn_kernel, gated=gate is not None),
        grid=(r // tm, ne, fdim // tf),
        in_specs=in_specs,
        out_specs=pl.BlockSpec((tm, D_MODEL), row),
        out_shape=jax.ShapeDtypeStruct((r, D_MODEL), F32),
        scratch_shapes=[pltpu.VMEM((tm, D_MODEL), BF16), pltpu.VMEM((tm, D_MODEL), F32)],
        compiler_params=_cparams(("arbitrary", "arbitrary", "arbitrary")),
        name="ffn_gated" if gate is not None else "ffn",
    )(*args)


def _router_kernel(x_ref, scp_ref, sh_ref, ng_ref, whi_ref, wlo_ref, b_ref, o_ref):
    h = _modnorm(x_ref[...], ng_ref[...], scp_ref[0], sh_ref[0])
    h_hi = h.astype(BF16)
    h_lo = (h - h_hi.astype(F32)).astype(BF16)
    logits = (jnp.dot(h_hi, whi_ref[...], preferred_element_type=F32)
              + jnp.dot(h_lo, whi_ref[...], preferred_element_type=F32)
              + jnp.dot(h_hi, wlo_ref[...], preferred_element_type=F32)) + b_ref[...]
    lane = lax.broadcasted_iota(jnp.int32, logits.shape, 1).astype(F32)
    logits = jnp.where(lane < N_EXPERTS, logits, -jnp.inf)
    m1 = jnp.max(logits, axis=-1, keepdims=True)
    i1 = jnp.min(jnp.where(logits == m1, lane, float(LANES)), axis=-1, keepdims=True)
    rest = jnp.where(lane == i1, -jnp.inf, logits)
    m2 = jnp.max(rest, axis=-1, keepdims=True)
    i2 = jnp.min(jnp.where(rest == m2, lane, float(LANES)), axis=-1, keepdims=True)
    e2 = jnp.exp(m2 - m1)
    den = 1.0 + e2
    o_ref[...] = jnp.where(lane == i1, 1.0 / den, 0.0) + jnp.where(lane == i2, e2 / den, 0.0)


def _router(x2, scp, sh, ng, w_hi, w_lo, b, tm):
    r = x2.shape[0]
    tiles_per_group = (r // tm) // scp.shape[0]
    row = lambda i: (i, 0)
    grp = lambda i: (i // tiles_per_group, 0, 0)
    fixed = lambda i: (0, 0)
    return pl.pallas_call(
        _router_kernel,
        grid=(r // tm,),
        in_specs=[
            pl.BlockSpec((tm, D_MODEL), row),
            pl.BlockSpec((1, 1, D_MODEL), grp),
            pl.BlockSpec((1, 1, D_MODEL), grp),
            pl.BlockSpec((1, D_MODEL), fixed),
            pl.BlockSpec(w_hi.shape, fixed),
            pl.BlockSpec(w_lo.shape, fixed),
            pl.BlockSpec(b.shape, fixed),
        ],
        out_specs=pl.BlockSpec((tm, LANES), row),
        out_shape=jax.ShapeDtypeStruct((r, LANES), F32),
        compiler_params=_cparams(("arbitrary",)),
        name="router",
    )(x2, scp, sh, ng, w_hi, w_lo, b)


LRU_TILE = 512
LRU_CHUNKS = D_WIDTH // LANES


def _tile_scan(a_ref, b_ref, h_ref, p_ref, carry0, reverse):
    seg = a_ref.shape[1] // SUBLANES

    def local(i, c):
        hs, ps = c
        t = seg - 1 - i if reverse else i
        rows = pl.ds(t, SUBLANES, stride=seg)
        hn, pn = [], []
        for cc in range(LRU_CHUNKS):
            at = a_ref[cc, rows, :]
            h = at * hs[cc] + b_ref[cc, rows, :]
            p = at * ps[cc]
            h_ref[cc, rows, :] = h
            p_ref[cc, rows, :] = p
            hn.append(h)
            pn.append(p)
        return tuple(hn), tuple(pn)

    z = tuple(jnp.zeros((SUBLANES, LANES), F32) for _ in range(LRU_CHUNKS))
    o = tuple(jnp.ones((SUBLANES, LANES), F32) for _ in range(LRU_CHUNKS))
    h_end, p_end = lax.fori_loop(0, seg, local, (z, o), unroll=8)

    order = range(SUBLANES - 1, -1, -1) if reverse else range(SUBLANES)
    carries, outs = [], []
    for cc in range(LRU_CHUNKS):
        cin = carry0[cc]
        rows = [None] * SUBLANES
        for s in order:
            rows[s] = cin
            cin = h_end[cc][s:s + 1] + p_end[cc][s:s + 1] * cin
        carries.append(jnp.concatenate(rows, axis=0))
        outs.append(cin)

    def fix(i, _):
        rows = pl.ds(i, SUBLANES, stride=seg)
        for cc in range(LRU_CHUNKS):
            h_ref[cc, rows, :] = h_ref[cc, rows, :] + p_ref[cc, rows, :] * carries[cc]
        return 0

    lax.fori_loop(0, seg, fix, 0, unroll=8)
    return outs


def _lru_fwd_kernel(prev_ref, cur_ref, next_ref, cw_ref, cb_ref, w_ref, bias_ref, lam_ref, h0_ref,
                    hf_ref, ab_ref, bb_ref, hl_ref, ext_ref, a_ref, b_ref, h_ref, p_ref, st_ref):
    i = pl.program_id(1)
    nt = pl.num_programs(1)
    t = cur_ref.shape[0]
    pad = SUBLANES

    @pl.when(i == 0)
    def _():
        st_ref[...] = h0_ref[0]

    ext_ref[0:pad, :] = jnp.where(i == 0, 0.0, prev_ref[...])
    ext_ref[pad:pad + t, :] = cur_ref[...]
    ext_ref[pad + t:, :] = jnp.where(i == nt - 1, 0.0, next_ref[...])
    left = CONV_W // 2
    xd = cb_ref[...] + sum(cw_ref[j:j + 1, :] * ext_ref[pad + j - left:pad + j - left + t, :] for j in range(CONV_W))
    z = jnp.dot(xd.astype(BF16), w_ref[...], preferred_element_type=F32) + bias_ref[...]
    sp = jax.nn.softplus(-lam_ref[...])
    coeffs = []
    for d in range(2):
        r = jax.nn.sigmoid(z[:, (2 * d) * D_WIDTH:(2 * d + 1) * D_WIDTH])
        g = jax.nn.sigmoid(z[:, (2 * d + 1) * D_WIDTH:(2 * d + 2) * D_WIDTH])
        log_a = -LRU_C * r * sp[:, d * D_WIDTH:(d + 1) * D_WIDTH]
        a = jnp.exp(log_a)
        coeffs.append((a, jnp.sqrt(-jnp.tanh(log_a) * (a * a + 1.0)) * (g * xd)))
    ab_ref[...] = coeffs[1][0]
    bb_ref[...] = coeffs[1][1]
    for cc in range(LRU_CHUNKS):
        a_ref[cc] = coeffs[0][0][:, cc * LANES:(cc + 1) * LANES]
        b_ref[cc] = coeffs[0][1][:, cc * LANES:(cc + 1) * LANES]
    carry0 = [st_ref[0:1, cc * LANES:(cc + 1) * LANES] for cc in range(LRU_CHUNKS)]
    outs = _tile_scan(a_ref, b_ref, h_ref, p_ref, carry0, reverse=False)
    hf_ref[...] = jnp.concatenate([h_ref[cc] for cc in range(LRU_CHUNKS)], axis=-1)
    state = jnp.concatenate(outs, axis=-1)
    st_ref[...] = jnp.broadcast_to(state, st_ref.shape)
    hl_ref[0] = jnp.broadcast_to(state, st_ref.shape)


def _lru_bwd_kernel(a_in_ref, b_in_ref, hf_ref, gg_ref, h0_ref, y_ref, hl_ref, a_ref, b_ref, h_ref, p_ref, st_ref):
    i = pl.program_id(1)

    @pl.when(i == 0)
    def _():
        st_ref[...] = h0_ref[0]

    for cc in range(LRU_CHUNKS):
        a_ref[cc] = a_in_ref[:, cc * LANES:(cc + 1) * LANES]
        b_ref[cc] = b_in_ref[:, cc * LANES:(cc + 1) * LANES]
    carry0 = [st_ref[0:1, cc * LANES:(cc + 1) * LANES] for cc in range(LRU_CHUNKS)]
    outs = _tile_scan(a_ref, b_ref, h_ref, p_ref, carry0, reverse=True)
    hb = jnp.concatenate([h_ref[cc] for cc in range(LRU_CHUNKS)], axis=-1)
    y_ref[...] = ((hf_ref[...] + hb) * gg_ref[...]).astype(y_ref.dtype)
    state = jnp.concatenate(outs, axis=-1)
    st_ref[...] = jnp.broadcast_to(state, st_ref.shape)
    hl_ref[0] = jnp.broadcast_to(state, st_ref.shape)


def _lru_scratch(tile):
    chunked = pltpu.VMEM((LRU_CHUNKS, tile, LANES), F32)
    return [chunked, chunked, chunked, chunked, pltpu.VMEM((SUBLANES, D_WIDTH), F32)]


def _lru_forward(xr, cw, cb, w_cat, b_cat, lam_cat, h0, batch, tile):
    r = xr.shape[0]
    nt = r // batch // tile
    sub = tile // SUBLANES
    row = lambda b, i: (b * nt + i, 0)
    fixed = lambda b, i: (0, 0)
    return pl.pallas_call(
        _lru_fwd_kernel,
        grid=(batch, nt),
        in_specs=[
            pl.BlockSpec((SUBLANES, D_WIDTH), lambda b, i: ((b * nt + jnp.maximum(i, 1)) * sub - 1, 0)),
            pl.BlockSpec((tile, D_WIDTH), row),
            pl.BlockSpec((SUBLANES, D_WIDTH), lambda b, i: ((b * nt + jnp.minimum(i + 1, nt - 1)) * sub, 0)),
            pl.BlockSpec(cw.shape, fixed),
            pl.BlockSpec(cb.shape, fixed),
            pl.BlockSpec(w_cat.shape, fixed),
            pl.BlockSpec(b_cat.shape, fixed),
            pl.BlockSpec(lam_cat.shape, fixed),
            pl.BlockSpec((1, SUBLANES, D_WIDTH), lambda b, i: (b, 0, 0)),
        ],
        out_specs=[
            pl.BlockSpec((tile, D_WIDTH), row),
            pl.BlockSpec((tile, D_WIDTH), row),
            pl.BlockSpec((tile, D_WIDTH), row),
            pl.BlockSpec((1, SUBLANES, D_WIDTH), lambda b, i: (b, 0, 0)),
        ],
        out_shape=[
            jax.ShapeDtypeStruct((r, D_WIDTH), F32),
            jax.ShapeDtypeStruct((r, D_WIDTH), F32),
            jax.ShapeDtypeStruct((r, D_WIDTH), F32),
            jax.ShapeDtypeStruct((batch, SUBLANES, D_WIDTH), F32),
        ],
        scratch_shapes=[pltpu.VMEM((tile + 2 * SUBLANES, D_WIDTH), F32)] + _lru_scratch(tile),
        compiler_params=_cparams(("arbitrary", "arbitrary")),
        name="lru_fwd",
    )(xr, xr, xr, cw, cb, w_cat, b_cat, lam_cat, h0)


def _lru_backward(a_b, b_b, hf, gg, h0, batch, tile):
    r = a_b.shape[0]
    nt = r // batch // tile
    row = lambda b, i: (b * nt + (nt - 1 - i), 0)
    return pl.pallas_call(
        _lru_bwd_kernel,
        grid=(batch, nt),
        in_specs=[pl.BlockSpec((tile, D_WIDTH), row)] * 4
        + [pl.BlockSpec((1, SUBLANES, D_WIDTH), lambda b, i: (b, 0, 0))],
        out_specs=[
            pl.BlockSpec((tile, D_WIDTH), row),
            pl.BlockSpec((1, SUBLANES, D_WIDTH), lambda b, i: (b, 0, 0)),
        ],
        out_shape=[
            jax.ShapeDtypeStruct((r, D_WIDTH), BF16),
            jax.ShapeDtypeStruct((batch, SUBLANES, D_WIDTH), F32),
        ],
        scratch_shapes=_lru_scratch(tile),
        compiler_params=_cparams(("arbitrary", "arbitrary")),
        name="lru_bwd",
    )(a_b, b_b, hf, gg, h0)


def _rope_tables(n_lat):
    half = HEAD_DIM // 4
    freqs = ROPE_THETA ** (-jnp.arange(half, dtype=F32) / half)
    t = jnp.arange(n_lat, dtype=jnp.int32)
    row = (t // GRID_W).astype(F32)
    col = (t % GRID_W).astype(F32)
    ar = freqs[:, None] * row[None, :]
    ac = freqs[:, None] * col[None, :]
    cos_t = jnp.concatenate([jnp.cos(ar), jnp.cos(ar), jnp.cos(ac), jnp.cos(ac)], axis=0)
    sin_t = jnp.concatenate([-jnp.sin(ar), jnp.sin(ar), -jnp.sin(ac), jnp.sin(ac)], axis=0)
    return cos_t, sin_t


def _block_diag(w):
    nb, bs, _ = w.shape
    eye = jnp.eye(nb, dtype=w.dtype)
    return (eye[:, None, :, None] * w[:, :, None, :]).reshape(nb * bs, nb * bs)


def kernel(x, c, ctx, c_ctx, w_mod, b_mod, norm_mix_g, norm_ffn_g, w_out, e_w_in, a_norm_g, a_ws, a_bs, b_qnorm_g, b_knorm_g, b_lq1, b_lk1, b_lq2, b_lk2, b_subln_g, ffn_w1, ffn_w3, ffn_w2, o_w_in, c_qnorm_g, c_knorm_g, d_conv_w, d_conv_b, d_wa, d_ba, d_wx, d_bx, d_lambda, router_w, router_b, moe_w1, moe_w3, moe_w2):
    batch, n_lat, d = x.shape
    n_ctx = ctx.shape[1]
    depth = w_mod.shape[0]
    assert depth == 2 and d == D_MODEL and batch + 1 <= SUBLANES
    xl = x.reshape(batch * n_lat, d)
    xc = ctx.reshape(batch * n_ctx, d)

    cc = jnp.concatenate([c, c_ctx[None, :], jnp.zeros((SUBLANES - batch - 1, d), F32)], axis=0)
    mod = _modulation(cc, w_mod, b_mod)

    def mods(layer):
        parts = [mod[layer, :, k * d:(k + 1) * d] for k in range(6)]
        lat = [p[:batch, None, :] for p in parts]
        cx = [p[batch:batch + 1, None, :] for p in parts]
        return lat, cx

    cos_t, sin_t = _rope_tables(n_lat)
    cos_c = jnp.ones((HEAD_DIM, n_ctx), F32)
    sin_c = jnp.zeros((HEAD_DIM, n_ctx), F32)
    col = lambda v: v.reshape(-1, 1)
    rowv = lambda v: v.reshape(1, -1)
    tm = 512
    tm_c = 256

    (sh1, sc1, g1, sh2, sc2, g2), (sh1c, sc1c, g1c, sh2c, sc2c, g2c) = mods(0)
    lam_init = 0.8 - 0.6 * math.exp(-0.3 * 0)
    w_in = e_w_in[0].astype(BF16)
    gains = [rowv(a_norm_g[0]), col(b_qnorm_g[0]), col(b_knorm_g[0])]
    ng = rowv(norm_mix_g[0])
    u, vn, qt, k, vt = _projection(True, xl, 1.0 + sc1, sh1, ng, w_in, gains, cos_t, sin_t, tm)
    uc, vnc, qtc, kc, vtc = _projection(True, xc, 1.0 + sc1c, sh1c, ng, w_in, gains, cos_c, sin_c, tm_c)
    lamp = jnp.stack([b_lq1[0], b_lk1[0], b_lq2[0], b_lk2[0]], axis=0)
    extra = [lamp, col(b_subln_g[0])]
    yb = _attention("diff", qt, kc, vtc, k, vt, extra, batch, 512, lam_init)
    ybc = _attention("diff", qtc, kc, vtc, None, None, extra, batch, n_ctx, lam_init)
    ws = a_ws[0].astype(BF16)
    bias_map = jnp.repeat(a_bs[0].T, HEAD_DIM, axis=1)
    ya = _gmlp(u, vn, ws, bias_map, tm)
    yac = _gmlp(uc, vnc, ws, bias_map, tm_c)
    wo = w_out[0].astype(BF16)
    xl = _out_proj(xl, ya, yb, wo, g1, tm)
    xc = _out_proj(xc, yac, ybc, wo, g1c, tm_c)
    w1 = ffn_w1.astype(BF16)
    w3 = ffn_w3.astype(BF16)
    w2 = ffn_w2.astype(BF16)
    ngf = rowv(norm_ffn_g[0])
    tf = FFN_DIM // 2
    xl = _ffn(xl, 1.0 + sc2, sh2, g2, ngf, w1, w3, w2, None, tm, tf)
    xc = _ffn(xc, 1.0 + sc2c, sh2c, g2c, ngf, w1, w3, w2, None, tm_c, tf)

    (sh1, sc1, g1, sh2, sc2, g2), (sh1c, sc1c, _, _, _, _) = mods(1)
    w_in = o_w_in[0].astype(BF16)
    gains = [col(c_qnorm_g[0]), col(c_knorm_g[0])]
    ng = rowv(norm_mix_g[1])
    qt, k, vt, gg, xr = _projection(False, xl, 1.0 + sc1, sh1, ng, w_in, gains, cos_t, sin_t, tm)
    _, kc, vtc, _, xrc = _projection(False, xc, 1.0 + sc1c, sh1c, ng, w_in, gains, cos_c, sin_c, tm_c)
    y_attn = _attention("gqa", qt, kc, vtc, k, vt, [], batch, 512)

    w_cat = jnp.concatenate([_block_diag(d_wa[0, 0]), _block_diag(d_wx[0, 0]),
                             _block_diag(d_wa[0, 1]), _block_diag(d_wx[0, 1])], axis=1).astype(BF16)
    b_cat = jnp.concatenate([d_ba[0, 0], d_bx[0, 0], d_ba[0, 1], d_bx[0, 1]]).reshape(1, -1)
    lam_cat = d_lambda[0].reshape(1, -1)
    cw = d_conv_w[0]
    cb = rowv(d_conv_b[0])
    zero_state = jnp.zeros((batch, SUBLANES, D_WIDTH), F32)
    hf_c, ab_c, bb_c, st_f = _lru_forward(xrc, cw, cb, w_cat, b_cat, lam_cat, zero_state, batch, n_ctx)
    _, st_b = _lru_backward(ab_c, bb_c, hf_c, hf_c, zero_state, batch, n_ctx)
    hf, ab, bb, _ = _lru_forward(xr, cw, cb, w_cat, b_cat, lam_cat, st_f, batch, LRU_TILE)
    y_rec, _ = _lru_backward(ab, bb, hf, gg, st_b, batch, LRU_TILE)

    xl = _out_proj(xl, y_attn, y_rec, w_out[1].astype(BF16), g1, tm)
    ngf = rowv(norm_ffn_g[1])
    wr = jnp.pad(router_w[0], ((0, 0), (0, LANES - N_EXPERTS)))
    wr_hi = wr.astype(BF16)
    wr_lo = (wr - wr_hi.astype(F32)).astype(BF16)
    br = jnp.pad(router_b[0], (0, LANES - N_EXPERTS)).reshape(1, LANES)
    gate = _router(xl, 1.0 + sc2, sh2, ngf, wr_hi, wr_lo, br, tm)
    xl = _ffn(xl, 1.0 + sc2, sh2, g2, ngf, moe_w1[0].astype(BF16), moe_w3[0].astype(BF16),
              moe_w2[0].astype(BF16), gate, tm, tf)
    return xl.reshape(batch, n_lat, d)
```

```python
import functools
import math

import jax
import jax.numpy as jnp
from jax import lax
from jax.experimental import pallas as pl
from jax.experimental.pallas import tpu as pltpu

F32 = jnp.float32
BF16 = jnp.bfloat16

D_MODEL = 1024
HEAD_DIM = 64
GRID_W = 64
ROPE_THETA = 10000.0
EPS = 1e-6
CHUNK = 128
A_WIDTH = D_MODEL // 2
A_GROUPS = A_WIDTH // HEAD_DIM
B_HEADS = D_MODEL // (4 * HEAD_DIM)
B_WIDTH = B_HEADS * 2 * HEAD_DIM
E_IN = 2 * A_WIDTH + 3 * B_WIDTH
C_HEADS = (D_MODEL // 2) // HEAD_DIM
C_KV_HEADS = 2
C_GROUP = C_HEADS // C_KV_HEADS
C_WIDTH = C_HEADS * HEAD_DIM
C_KV_WIDTH = C_KV_HEADS * HEAD_DIM
D_WIDTH = D_MODEL // 2
D_BLOCKS = D_WIDTH // HEAD_DIM
CONV_W = 4
LRU_C = 8.0
O_IN = C_WIDTH + 2 * C_KV_WIDTH + 2 * D_WIDTH
FFN_DIM = 2816
N_EXPERTS = 8

LANES = 128
SUBLANES = 8
KV_BLOCK = 256
ATTN_TQ = 512
ATTN_TK = 512
DEN_ROWS = 16
QUERY_SCALE =HEAD_DIM ** -0.5 * math.log2(math.e)
VMEM_LIMIT = 56 * 1024 * 1024


def _cparams(sem):
    return pltpu.CompilerParams(dimension_semantics=sem, vmem_limit_bytes=VMEM_LIMIT)


def _modnorm(x, g, scp, sh):
    y = x * lax.rsqrt(jnp.mean(x * x, axis=-1, keepdims=True) + EPS)
    return (y * g) * scp + sh


def _mod_kernel(c_ref, w_ref, b_ref, o_ref):
    c = c_ref[...]
    s = c * jax.nn.sigmoid(c)
    o_ref[0] = jnp.dot(s.astype(BF16), w_ref[0].astype(BF16), preferred_element_type=F32) + b_ref[0]


def _modulation(cc, w_mod, b_mod):
    depth, d, n = w_mod.shape
    tn = 1536
    return pl.pallas_call(
        _mod_kernel,
        grid=(depth, n // tn),
        in_specs=[
            pl.BlockSpec((SUBLANES, d), lambda l, j: (0, 0)),
            pl.BlockSpec((1, d, tn), lambda l, j: (l, 0, j)),
            pl.BlockSpec((1, 1, tn), lambda l, j: (l, 0, j)),
        ],
        out_specs=pl.BlockSpec((1, SUBLANES, tn), lambda l, j: (l, 0, j)),
        out_shape=jax.ShapeDtypeStruct((depth, SUBLANES, n), F32),
        compiler_params=_cparams(("arbitrary", "arbitrary")),
        name="modulation",
    )(cc, w_mod, b_mod.reshape(depth, 1, n))


def _head_norm_rope_t(xh, g, cos_t, sin_t):
    xn = xh * lax.rsqrt(jnp.mean(xh * xh, axis=0, keepdims=True) + EPS) * g
    sw = jnp.concatenate([xn[16:32], xn[0:16], xn[48:64], xn[32:48]], axis=0)
    return xn * cos_t + sw * sin_t


def _proj_even_kernel(x_ref, scp_ref, sh_ref, ng_ref, w_ref, ang_ref, qg_ref, kg_ref, cos_ref, sin_ref,
                      u_ref, vn_ref, qt_ref, k_ref, vt_ref, *, tm):
    h = _modnorm(x_ref[...], ng_ref[...], scp_ref[0], sh_ref[0])
    y = jnp.dot(h.astype(BF16), w_ref[...], preferred_element_type=F32)
    ua = y[:, 0:A_WIDTH]
    va = y[:, A_WIDTH:2 * A_WIDTH]
    q = y[:, 2 * A_WIDTH:2 * A_WIDTH + B_WIDTH]
    k = y[:, 2 * A_WIDTH + B_WIDTH:2 * A_WIDTH + 2 * B_WIDTH]
    v = y[:, 2 * A_WIDTH + 2 * B_WIDTH:]
    u_ref[...] = jax.nn.gelu(ua)
    gv = jax.nn.gelu(va)
    vn = gv * lax.rsqrt(jnp.mean(gv * gv, axis=-1, keepdims=True) + EPS) * ang_ref[...]
    vn_ref[...] = vn.astype(BF16)
    cos_t = cos_ref[...]
    sin_t = sin_ref[...]
    q_t = q.T
    k_t = k.T
    zeros = jnp.zeros((HEAD_DIM, tm), F32)
    k_parts = []
    for j in range(2 * B_HEADS):
        sl = slice(j * HEAD_DIM, (j + 1) * HEAD_DIM)
        qh = _head_norm_rope_t(q_t[sl], qg_ref[...], cos_t, sin_t) * QUERY_SCALE
        blk = jnp.concatenate([qh, zeros] if j % 2 == 0 else [zeros, qh], axis=0)
        qt_ref[j] = blk.astype(BF16)
        k_parts.append(_head_norm_rope_t(k_t[sl], kg_ref[...], cos_t, sin_t))
    k_ref[...] = jnp.concatenate(k_parts, axis=0).T.astype(BF16)
    v_t = v.T.astype(BF16)
    for c in range(tm // KV_BLOCK):
        vt_ref[c] = v_t[:, c * KV_BLOCK:(c + 1) * KV_BLOCK]


def _proj_odd_kernel(x_ref, scp_ref, sh_ref, ng_ref, w_ref, qg_ref, kg_ref, cos_ref, sin_ref,
                     qt_ref, k_ref, vt_ref, gg_ref, xr_ref, *, tm):
    h = _modnorm(x_ref[...], ng_ref[...], scp_ref[0], sh_ref[0])
    y = jnp.dot(h.astype(BF16), w_ref[...], preferred_element_type=F32)
    q = y[:, 0:C_WIDTH]
    k = y[:, C_WIDTH:C_WIDTH + C_KV_WIDTH]
    v = y[:, C_WIDTH + C_KV_WIDTH:C_WIDTH + 2 * C_KV_WIDTH]
    gate = y[:, C_WIDTH + 2 * C_KV_WIDTH:C_WIDTH + 2 * C_KV_WIDTH + D_WIDTH]
    xr = y[:, C_WIDTH + 2 * C_KV_WIDTH + D_WIDTH:]
    gg_ref[...] = jax.nn.gelu(gate)
    xr_ref[...] = xr
    cos_t = cos_ref[...]
    sin_t = sin_ref[...]
    q_t = q.T
    k_t = k.T
    zeros = jnp.zeros((HEAD_DIM, tm), F32)
    for j in range(C_HEADS):
        sl = slice(j * HEAD_DIM, (j + 1) * HEAD_DIM)
        qh = _head_norm_rope_t(q_t[sl], qg_ref[...], cos_t, sin_t) * QUERY_SCALE
        blk = jnp.concatenate([qh, zeros] if j // C_GROUP == 0 else [zeros, qh], axis=0)
        qt_ref[j] = blk.astype(BF16)
    k_parts = [_head_norm_rope_t(k_t[j * HEAD_DIM:(j + 1) * HEAD_DIM], kg_ref[...], cos_t, sin_t)
               for j in range(C_KV_HEADS)]
    k_ref[...] = jnp.concatenate(k_parts, axis=0).T.astype(BF16)
    v_t = v.T.astype(BF16)
    for c in range(tm // KV_BLOCK):
        vt_ref[c] = v_t[:, c * KV_BLOCK:(c + 1) * KV_BLOCK]


def _projection(even, x2, scp, sh, ng, w, gains, cos_t, sin_t, tm):
    r = x2.shape[0]
    nt = r // tm
    tiles_per_group = nt // scp.shape[0]
    pos_tiles = cos_t.shape[1] // tm
    n_in = w.shape[1]
    row = lambda i: (i, 0)
    grp = lambda i: (i // tiles_per_group, 0, 0)
    fixed = lambda i: (0, 0)
    pos = lambda i: (0, i % pos_tiles)
    in_specs = [
        pl.BlockSpec((tm, D_MODEL), row),
        pl.BlockSpec((1, 1, D_MODEL), grp),
        pl.BlockSpec((1, 1, D_MODEL), grp),
        pl.BlockSpec((1, D_MODEL), fixed),
        pl.BlockSpec((D_MODEL, n_in), fixed),
    ]
    in_specs += [pl.BlockSpec(g.shape, fixed) for g in gains]
    in_specs += [pl.BlockSpec((HEAD_DIM, tm), pos), pl.BlockSpec((HEAD_DIM, tm), pos)]
    if even:
        kw, vw, nh = B_WIDTH, B_WIDTH, 2 * B_HEADS
        body = functools.partial(_proj_even_kernel, tm=tm)
        out_shape = [
            jax.ShapeDtypeStruct((r, A_WIDTH), F32),
            jax.ShapeDtypeStruct((r, A_WIDTH), BF16),
            jax.ShapeDtypeStruct((nh, LANES, r), BF16),
            jax.ShapeDtypeStruct((r, kw), BF16),
            jax.ShapeDtypeStruct((r // KV_BLOCK, vw, KV_BLOCK), BF16),
        ]
        out_specs = [
            pl.BlockSpec((tm, A_WIDTH), row),
            pl.BlockSpec((tm, A_WIDTH), row),
            pl.BlockSpec((nh, LANES, tm), lambda i: (0, 0, i)),
            pl.BlockSpec((tm, kw), row),
            pl.BlockSpec((tm // KV_BLOCK, vw, KV_BLOCK), lambda i: (i, 0, 0)),
        ]
    else:
        kw, vw, nh = C_KV_WIDTH, C_KV_WIDTH, C_HEADS
        body = functools.partial(_proj_odd_kernel, tm=tm)
        out_shape = [
            jax.ShapeDtypeStruct((nh, LANES, r), BF16),
            jax.ShapeDtypeStruct((r, kw), BF16),
            jax.ShapeDtypeStruct((r // KV_BLOCK, vw, KV_BLOCK), BF16),
            jax.ShapeDtypeStruct((r, D_WIDTH), F32),
            jax.ShapeDtypeStruct((r, D_WIDTH), F32),
        ]
        out_specs = [
            pl.BlockSpec((nh, LANES, tm), lambda i: (0, 0, i)),
            pl.BlockSpec((tm, kw), row),
            pl.BlockSpec((tm // KV_BLOCK, vw, KV_BLOCK), lambda i: (i, 0, 0)),
            pl.BlockSpec((tm, D_WIDTH), row),
            pl.BlockSpec((tm, D_WIDTH), row),
        ]
    return pl.pallas_call(
        body, grid=(nt,), in_specs=in_specs, out_specs=out_specs, out_shape=out_shape,
        compiler_params=_cparams(("arbitrary",)),
        name="proj_even" if even else "proj_odd",
    )(x2, scp, sh, ng, w, *gains, cos_t, sin_t)


def _attn_kernel(*refs, mode, n_lat, tk, lam_init):
    qt_ref, kc_ref, vtc_ref = refs[:3]
    pos = 3
    if n_lat:
        kl_ref, vtl_ref = refs[3:5]
        pos = 5
    if mode == "diff":
        lamp_ref, sg_ref = refs[pos:pos + 2]
        pos += 2
    o_ref, acc_ref, s_ref, p_ref = refs[pos:pos + 4]
    tq = qt_ref.shape[-1]
    q = (qt_ref[0], qt_ref[1])
    acc_ref[...] = jnp.zeros(acc_ref.shape, F32)

    lc = kc_ref.shape[0]
    ones = jnp.ones((DEN_ROWS, KV_BLOCK), BF16)

    def scores(kb, slot):
        n = kb.shape[0]
        mb = []
        for c in range(2):
            s = jnp.dot(kb, q[c], preferred_element_type=F32)
            s_ref[slot, c, 0:n] = s
            mb.append(jnp.max(s, axis=0, keepdims=True))
        return tuple(mb)

    def softmax(slot, n, mb, m):
        out, alphas = [], []
        for c in range(2):
            m_new = jnp.maximum(m[c], mb[c])
            p_ref[slot, c, 0:n] = jnp.exp2(s_ref[slot, c, 0:n] - m_new).astype(BF16)
            alphas.append(jnp.exp2(m[c] - m_new))
            out.append(m_new)
        return tuple(out), tuple(alphas)

    def values(vt_blocks, slot, alphas):
        for c in range(2):
            upd = alphas[c] * acc_ref[c]
            for i, vtb in enumerate(vt_blocks):
                lhs = jnp.concatenate([vtb, ones], axis=0)
                upd = upd + jnp.dot(lhs, p_ref[slot, c, i * KV_BLOCK:(i + 1) * KV_BLOCK],
                                    preferred_element_type=F32)
            acc_ref[c] = upd

    per = tk // KV_BLOCK

    def k_lat(j):
        return kl_ref[pl.ds(pl.multiple_of(j * tk, tk), tk), :]

    def vt_lat(j):
        return [vtl_ref[j * per + i] for i in range(per)]

    vt_ctx = [vtc_ref[i] for i in range(lc // KV_BLOCK)]
    neg = jnp.full((1, tq), -jnp.inf, F32)
    m = (neg, neg)
    mb = scores(kc_ref[...], 0)
    if not n_lat:
        m, al = softmax(0, lc, mb, m)
        values(vt_ctx, 0, al)
    else:
        assert n_lat % 2 == 0
        mb_next = scores(k_lat(0), 1)
        m, al = softmax(0, lc, mb, m)
        mb = mb_next
        mb_next = scores(k_lat(1), 0)
        m, al_next = softmax(1, tk, mb, m)
        values(vt_ctx, 0, al)
        mb, al = mb_next, al_next

        def body(i, carry):
            m, mb, al = carry
            t = 2 * i + 2
            mb_next = scores(k_lat(t), 1)
            m, al_next = softmax(0, tk, mb, m)
            values(vt_lat(t - 2), 1, al)
            mb, al = mb_next, al_next
            mb_next = scores(k_lat(t + 1), 0)
            m, al_next = softmax(1, tk, mb, m)
            values(vt_lat(t - 1), 0, al)
            return m, mb_next, al_next

        m, mb, al = lax.fori_loop(0, (n_lat - 2) // 2, body, (m, mb, al))
        m, al_next = softmax(0, tk, mb, m)
        values(vt_lat(n_lat - 2), 1, al)
        values(vt_lat(n_lat - 1), 0, al_next)
    l0 = acc_ref[0, LANES:LANES + 1, :]
    l1 = acc_ref[1, LANES:LANES + 1, :]
    if mode == "diff":
        lp = lamp_ref[...]
        lam = (jnp.exp(jnp.sum(lp[0:1] * lp[1:2], axis=-1, keepdims=True))
               - jnp.exp(jnp.sum(lp[2:3] * lp[3:4], axis=-1, keepdims=True)) + lam_init)
        o = acc_ref[0, 0:LANES, :] * (1.0 / l0) - lam * (acc_ref[1, 0:LANES, :] * (1.0 / l1))
        o = o * lax.rsqrt(jnp.mean(o * o, axis=0, keepdims=True) + EPS) * sg_ref[...] * (1.0 - lam_init)
    else:
        row0 = pl.multiple_of((pl.program_id(1) // (C_GROUP // 2)) * HEAD_DIM, HEAD_DIM)
        o = jnp.concatenate([acc_ref[0, pl.ds(row0, HEAD_DIM), :] * (1.0 / l0),
                             acc_ref[1, pl.ds(row0, HEAD_DIM), :] * (1.0 / l1)], axis=0)
    o_ref[...] = o.T.astype(o_ref.dtype)


def _attention(mode, qt, kc, vtc, kl, vtl, extra, batch, tq, tk, lam_init=0.0):
    units = qt.shape[0] // 2
    rq = qt.shape[2]
    nq = rq // batch // tq
    lc = kc.shape[0] // batch
    col = (lambda u: u) if mode == "diff" else (lambda u: 0)
    in_specs = [
        pl.BlockSpec((2, LANES, tq), lambda b, u, i: (u, 0, b * nq + i)),
        pl.BlockSpec((lc, LANES), lambda b, u, i: (b, col(u))),
        pl.BlockSpec((lc // KV_BLOCK, LANES, KV_BLOCK), lambda b, u, i: (b, col(u), 0)),
    ]
    args = [qt, kc, vtc]
    n_lat = 0
    if kl is not None:
        ll = kl.shape[0] // batch
        n_lat = ll // tk
        in_specs += [
            pl.BlockSpec((ll, LANES), lambda b, u, i: (b, col(u))),
            pl.BlockSpec((ll // KV_BLOCK, LANES, KV_BLOCK), lambda b, u, i: (b, col(u), 0)),
        ]
        args += [kl, vtl]
    for e in extra:
        in_specs.append(pl.BlockSpec(e.shape, lambda b, u, i: (0, 0)))
        args.append(e)
    rows = max(tk, lc)
    return pl.pallas_call(
        functools.partial(_attn_kernel, mode=mode, n_lat=n_lat, tk=tk, lam_init=lam_init),
        grid=(batch, units, nq),
        in_specs=in_specs,
        out_specs=pl.BlockSpec((tq, LANES), lambda b, u, i: (b * nq + i, u)),
        out_shape=jax.ShapeDtypeStruct((rq, units * LANES), BF16),
        scratch_shapes=[pltpu.VMEM((2, LANES + DEN_ROWS, tq), F32),
                        pltpu.VMEM((2, 2, rows, tq), F32),
                        pltpu.VMEM((2, 2, rows, tq), BF16)],
        compiler_params=_cparams(("arbitrary", "arbitrary", "arbitrary")),
        name="attn_" + mode,
    )(*args)


def _gmlp_kernel(u_ref, vn_ref, ws_ref, bias_ref, o_ref, *, tm):
    lane = lax.broadcasted_iota(jnp.int32, (CHUNK, LANES), 1)
    for c in range(tm // CHUNK):
        rows = slice(c * CHUNK, (c + 1) * CHUNK)
        for j in range(A_GROUPS // 2):
            cols = slice(j * LANES, (j + 1) * LANES)
            rhs = vn_ref[rows, cols]
            lo = jnp.dot(ws_ref[2 * j], rhs, preferred_element_type=F32)
            hi = jnp.dot(ws_ref[2 * j + 1], rhs, preferred_element_type=F32)
            mixed = jnp.where(lane < HEAD_DIM, lo, hi) + bias_ref[:, cols]
            o_ref[rows, cols] = (u_ref[rows, cols] * mixed).astype(o_ref.dtype)


def _gmlp(u, vn, ws, bias_map, tm):
    r = u.shape[0]
    row = lambda i: (i, 0)
    return pl.pallas_call(
        functools.partial(_gmlp_kernel, tm=tm),
        grid=(r // tm,),
        in_specs=[
            pl.BlockSpec((tm, A_WIDTH), row),
            pl.BlockSpec((tm, A_WIDTH), row),
            pl.BlockSpec(ws.shape, lambda i: (0, 0, 0)),
            pl.BlockSpec(bias_map.shape, lambda i: (0, 0)),
        ],
        out_specs=pl.BlockSpec((tm, A_WIDTH), row),
        out_shape=jax.ShapeDtypeStruct((r, A_WIDTH), BF16),
        compiler_params=_cparams(("arbitrary",)),
        name="gmlp",
    )(u, vn, ws, bias_map)


def _out_kernel(x_ref, ya_ref, yb_ref, w_ref, g_ref, o_ref):
    half = ya_ref.shape[1]
    y = (jnp.dot(ya_ref[...], w_ref[0:half, :], preferred_element_type=F32)
         + jnp.dot(yb_ref[...], w_ref[half:, :], preferred_element_type=F32))
    o_ref[...] = x_ref[...] + g_ref[0] * y


def _out_proj(x2, ya, yb, w, g, tm):
    r = x2.shape[0]
    tiles_per_group = (r // tm) // g.shape[0]
    row = lambda i: (i, 0)
    return pl.pallas_call(
        _out_kernel,
        grid=(r // tm,),
        in_specs=[
            pl.BlockSpec((tm, D_MODEL), row),
            pl.BlockSpec((tm, ya.shape[1]), row),
            pl.BlockSpec((tm, yb.shape[1]), row),
            pl.BlockSpec(w.shape, lambda i: (0, 0)),
            pl.BlockSpec((1, 1, D_MODEL), lambda i: (i // tiles_per_group, 0, 0)),
        ],
        out_specs=pl.BlockSpec((tm, D_MODEL), row),
        out_shape=jax.ShapeDtypeStruct((r, D_MODEL), F32),
        compiler_params=_cparams(("arbitrary",)),
        name="out_proj",
    )(x2, ya, yb, w, g)


def _ffn_kernel(*refs, gated):
    if gated:
        x_ref, scp_ref, sh_ref, g_ref, ng_ref, w1_ref, w3_ref, w2_ref, gate_ref, o_ref, h_ref, acc_ref = refs
    else:
        x_ref, scp_ref, sh_ref, g_ref, ng_ref, w1_ref, w3_ref, w2_ref, o_ref, h_ref, acc_ref = refs
    e = pl.program_id(1)
    f = pl.program_id(2)
    first = jnp.logical_and(e == 0, f == 0)
    last = jnp.logical_and(e == pl.num_programs(1) - 1, f == pl.num_programs(2) - 1)

    @pl.when(first)
    def _():
        h_ref[...] = _modnorm(x_ref[...], ng_ref[...], scp_ref[0], sh_ref[0]).astype(BF16)
        acc_ref[...] = jnp.zeros(acc_ref.shape, F32)

    h = h_ref[...]
    a = jnp.dot(h, w1_ref[0], preferred_element_type=F32)
    b = jnp.dot(h, w3_ref[0], preferred_element_type=F32)
    t = (a * jax.nn.sigmoid(a)) * b
    y = jnp.dot(t.astype(BF16), w2_ref[0], preferred_element_type=F32)
    if gated:
        lane = lax.broadcasted_iota(jnp.int32, gate_ref.shape, 1)
        ge = jnp.sum(jnp.where(lane == e, gate_ref[...], 0.0), axis=-1, keepdims=True)
        y = ge * y
    acc_ref[...] += y

    @pl.when(last)
    def _():
        o_ref[...] = x_ref[...] + g_ref[0] * acc_ref[...]


def _ffn(x2, scp, sh, g, ng, w1, w3, w2, gate, tm, tf):
    r = x2.shape[0]
    ne, _, fdim = w1.shape
    tiles_per_group = (r // tm) // g.shape[0]
    row = lambda i, e, f: (i, 0)
    grp = lambda i, e, f: (i // tiles_per_group, 0, 0)
    in_specs = [
        pl.BlockSpec((tm, D_MODEL), row),
        pl.BlockSpec((1, 1, D_MODEL), grp),
        pl.BlockSpec((1, 1, D_MODEL), grp),
        pl.BlockSpec((1, 1, D_MODEL), grp),
        pl.BlockSpec((1, D_MODEL), lambda i, e, f: (0, 0)),
        pl.BlockSpec((1, D_MODEL, tf), lambda i, e, f: (e, 0, f)),
        pl.BlockSpec((1, D_MODEL, tf), lambda i, e, f: (e, 0, f)),
        pl.BlockSpec((1, tf, D_MODEL), lambda i, e, f: (e, f, 0)),
    ]
    args = [x2, scp, sh, g, ng, w1, w3, w2]
    if gate is not None:
        in_specs.append(pl.BlockSpec((tm, LANES), row))
        args.append(gate)
    return pl.pallas_call(
        functools.partial(_ffn_kernel, gated=gate is not None),
        grid=(r // tm, ne, fdim // tf),
        in_specs=in_specs,
        out_specs=pl.BlockSpec((tm, D_MODEL), row),
        out_shape=jax.ShapeDtypeStruct((r, D_MODEL), F32),
        scratch_shapes=[pltpu.VMEM((tm, D_MODEL), BF16), pltpu.VMEM((tm, D_MODEL), F32)],
        compiler_params=_cparams(("arbitrary", "arbitrary", "arbitrary")),
        name="ffn_gated" if gate is not None else "ffn",
    )(*args)


def _router_kernel(x_ref, scp_ref, sh_ref, ng_ref, whi_ref, wlo_ref, b_ref, o_ref):
    h = _modnorm(x_ref[...], ng_ref[...], scp_ref[0], sh_ref[0])
    h_hi = h.astype(BF16)
    h_lo = (h - h_hi.astype(F32)).astype(BF16)
    logits = (jnp.dot(h_hi, whi_ref[...], preferred_element_type=F32)
              + jnp.dot(h_lo, whi_ref[...], preferred_element_type=F32)
              + jnp.dot(h_hi, wlo_ref[...], preferred_element_type=F32)) + b_ref[...]
    lane = lax.broadcasted_iota(jnp.int32, logits.shape, 1).astype(F32)
    logits = jnp.where(lane < N_EXPERTS, logits, -jnp.inf)
    m1 = jnp.max(logits, axis=-1, keepdims=True)
    i1 = jnp.min(jnp.where(logits == m1, lane, float(LANES)), axis=-1, keepdims=True)
    rest = jnp.where(lane == i1, -jnp.inf, logits)
    m2 = jnp.max(rest, axis=-1, keepdims=True)
    i2 = jnp.min(jnp.where(rest == m2, lane, float(LANES)), axis=-1, keepdims=True)
    e2 = jnp.exp(m2 - m1)
    den = 1.0 + e2
    o_ref[...] = jnp.where(lane == i1, 1.0 / den, 0.0) + jnp.where(lane == i2, e2 / den, 0.0)


def _router(x2, scp, sh, ng, w_hi, w_lo, b, tm):
    r = x2.shape[0]
    tiles_per_group = (r // tm) // scp.shape[0]
    row = lambda i: (i, 0)
    grp = lambda i: (i // tiles_per_group, 0, 0)
    fixed = lambda i: (0, 0)
    return pl.pallas_call(
        _router_kernel,
        grid=(r // tm,),
        in_specs=[
            pl.BlockSpec((tm, D_MODEL), row),
            pl.BlockSpec((1, 1, D_MODEL), grp),
            pl.BlockSpec((1, 1, D_MODEL), grp),
            pl.BlockSpec((1, D_MODEL), fixed),
            pl.BlockSpec(w_hi.shape, fixed),
            pl.BlockSpec(w_lo.shape, fixed),
            pl.BlockSpec(b.shape, fixed),
        ],
        out_specs=pl.BlockSpec((tm, LANES), row),
        out_shape=jax.ShapeDtypeStruct((r, LANES), F32),
        compiler_params=_cparams(("arbitrary",)),
        name="router",
    )(x2, scp, sh, ng, w_hi, w_lo, b)


LRU_TILE = 512
LRU_CHUNKS = D_WIDTH // LANES


def _tile_scan(a_ref, b_ref, h_ref, p_ref, carry0, reverse):
    seg = a_ref.shape[1] // SUBLANES

    def local(i, c):
        hs, ps = c
        t = seg - 1 - i if reverse else i
        rows = pl.ds(t, SUBLANES, stride=seg)
        hn, pn = [], []
        for cc in range(LRU_CHUNKS):
            at = a_ref[cc, rows, :]
            h = at * hs[cc] + b_ref[cc, rows, :]
            p = at * ps[cc]
            h_ref[cc, rows, :] = h
            p_ref[cc, rows, :] = p
            hn.append(h)
            pn.append(p)
        return tuple(hn), tuple(pn)

    z = tuple(jnp.zeros((SUBLANES, LANES), F32) for _ in range(LRU_CHUNKS))
    o = tuple(jnp.ones((SUBLANES, LANES), F32) for _ in range(LRU_CHUNKS))
    h_end, p_end = lax.fori_loop(0, seg, local, (z, o), unroll=8)

    order = range(SUBLANES - 1, -1, -1) if reverse else range(SUBLANES)
    carries, outs = [], []
    for cc in range(LRU_CHUNKS):
        cin = carry0[cc]
        rows = [None] * SUBLANES
        for s in order:
            rows[s] = cin
            cin = h_end[cc][s:s + 1] + p_end[cc][s:s + 1] * cin
        carries.append(jnp.concatenate(rows, axis=0))
        outs.append(cin)

    def fix(i, _):
        rows = pl.ds(i, SUBLANES, stride=seg)
        for cc in range(LRU_CHUNKS):
            h_ref[cc, rows, :] = h_ref[cc, rows, :] + p_ref[cc, rows, :] * carries[cc]
        return 0

    lax.fori_loop(0, seg, fix, 0, unroll=8)
    return outs


def _lru_fwd_kernel(prev_ref, cur_ref, next_ref, cw_ref, cb_ref, w_ref, bias_ref, lam_ref, h0_ref,
                    hf_ref, ab_ref, bb_ref, hl_ref, ext_ref, a_ref, b_ref, h_ref, p_ref, st_ref):
    i = pl.program_id(1)
    nt = pl.num_programs(1)
    t = cur_ref.shape[0]
    pad = SUBLANES

    @pl.when(i == 0)
    def _():
        st_ref[...] = h0_ref[0]

    ext_ref[0:pad, :] = jnp.where(i == 0, 0.0, prev_ref[...])
    ext_ref[pad:pad + t, :] = cur_ref[...]
    ext_ref[pad + t:, :] = jnp.where(i == nt - 1, 0.0, next_ref[...])
    left = CONV_W // 2
    xd = cb_ref[...] + sum(cw_ref[j:j + 1, :] * ext_ref[pad + j - left:pad + j - left + t, :] for j in range(CONV_W))
    z = jnp.dot(xd.astype(BF16), w_ref[...], preferred_element_type=F32) + bias_ref[...]
    sp = jax.nn.softplus(-lam_ref[...])
    coeffs = []
    for d in range(2):
        r = jax.nn.sigmoid(z[:, (2 * d) * D_WIDTH:(2 * d + 1) * D_WIDTH])
        g = jax.nn.sigmoid(z[:, (2 * d + 1) * D_WIDTH:(2 * d + 2) * D_WIDTH])
        log_a = -LRU_C * r * sp[:, d * D_WIDTH:(d + 1) * D_WIDTH]
        a = jnp.exp(log_a)
        coeffs.append((a, jnp.sqrt(-jnp.tanh(log_a) * (a * a + 1.0)) * (g * xd)))
    ab_ref[...] = coeffs[1][0]
    bb_ref[...] = coeffs[1][1]
    for cc in range(LRU_CHUNKS):
        a_ref[cc] = coeffs[0][0][:, cc * LANES:(cc + 1) * LANES]
        b_ref[cc] = coeffs[0][1][:, cc * LANES:(cc + 1) * LANES]
    carry0 = [st_ref[0:1, cc * LANES:(cc + 1) * LANES] for cc in range(LRU_CHUNKS)]
    outs = _tile_scan(a_ref, b_ref, h_ref, p_ref, carry0, reverse=False)
    hf_ref[...] = jnp.concatenate([h_ref[cc] for cc in range(LRU_CHUNKS)], axis=-1)
    state = jnp.concatenate(outs, axis=-1)
    st_ref[...] = jnp.broadcast_to(state, st_ref.shape)
    hl_ref[0] = jnp.broadcast_to(state, st_ref.shape)


def _lru_bwd_kernel(a_in_ref, b_in_ref, hf_ref, gg_ref, h0_ref, y_ref, hl_ref, a_ref, b_ref, h_ref, p_ref, st_ref):
    i = pl.program_id(1)

    @pl.when(i == 0)
    def _():
        st_ref[...] = h0_ref[0]

    for cc in range(LRU_CHUNKS):
        a_ref[cc] = a_in_ref[:, cc * LANES:(cc + 1) * LANES]
        b_ref[cc] = b_in_ref[:, cc * LANES:(cc + 1) * LANES]
    carry0 = [st_ref[0:1, cc * LANES:(cc + 1) * LANES] for cc in range(LRU_CHUNKS)]
    outs = _tile_scan(a_ref, b_ref, h_ref, p_ref, carry0, reverse=True)
    hb = jnp.concatenate([h_ref[cc] for cc in range(LRU_CHUNKS)], axis=-1)
    y_ref[...] = ((hf_ref[...] + hb) * gg_ref[...]).astype(y_ref.dtype)
    state = jnp.concatenate(outs, axis=-1)
    st_ref[...] = jnp.broadcast_to(state, st_ref.shape)
    hl_ref[0] = jnp.broadcast_to(state, st_ref.shape)


def _lru_scratch(tile):
    chunked = pltpu.VMEM((LRU_CHUNKS, tile, LANES), F32)
    return [chunked, chunked, chunked, chunked, pltpu.VMEM((SUBLANES, D_WIDTH), F32)]


def _lru_forward(xr, cw, cb, w_cat, b_cat, lam_cat, h0, batch, tile):
    r = xr.shape[0]
    nt = r // batch // tile
    sub = tile // SUBLANES
    row = lambda b, i: (b * nt + i, 0)
    fixed = lambda b, i: (0, 0)
    return pl.pallas_call(
        _lru_fwd_kernel,
        grid=(batch, nt),
        in_specs=[
            pl.BlockSpec((SUBLANES, D_WIDTH), lambda b, i: ((b * nt + jnp.maximum(i, 1)) * sub - 1, 0)),
            pl.BlockSpec((tile, D_WIDTH), row),
            pl.BlockSpec((SUBLANES, D_WIDTH), lambda b, i: ((b * nt + jnp.minimum(i + 1, nt - 1)) * sub, 0)),
            pl.BlockSpec(cw.shape, fixed),
            pl.BlockSpec(cb.shape, fixed),
            pl.BlockSpec(w_cat.shape, fixed),
            pl.BlockSpec(b_cat.shape, fixed),
            pl.BlockSpec(lam_cat.shape, fixed),
            pl.BlockSpec((1, SUBLANES, D_WIDTH), lambda b, i: (b, 0, 0)),
        ],
        out_specs=[
            pl.BlockSpec((tile, D_WIDTH), row),
            pl.BlockSpec((tile, D_WIDTH), row),
            pl.BlockSpec((tile, D_WIDTH), row),
            pl.BlockSpec((1, SUBLANES, D_WIDTH), lambda b, i: (b, 0, 0)),
        ],
        out_shape=[
            jax.ShapeDtypeStruct((r, D_WIDTH), F32),
            jax.ShapeDtypeStruct((r, D_WIDTH), F32),
            jax.ShapeDtypeStruct((r, D_WIDTH), F32),
            jax.ShapeDtypeStruct((batch, SUBLANES, D_WIDTH), F32),
        ],
        scratch_shapes=[pltpu.VMEM((tile + 2 * SUBLANES, D_WIDTH), F32)] + _lru_scratch(tile),
        compiler_params=_cparams(("arbitrary", "arbitrary")),
        name="lru_fwd",
    )(xr, xr, xr, cw, cb, w_cat, b_cat, lam_cat, h0)


def _lru_backward(a_b, b_b, hf, gg, h0, batch, tile):
    r = a_b.shape[0]
    nt = r // batch // tile
    row = lambda b, i: (b * nt + (nt - 1 - i), 0)
    return pl.pallas_call(
        _lru_bwd_kernel,
        grid=(batch, nt),
        in_specs=[pl.BlockSpec((tile, D_WIDTH), row)] * 4
        + [pl.BlockSpec((1, SUBLANES, D_WIDTH), lambda b, i: (b, 0, 0))],
        out_specs=[
            pl.BlockSpec((tile, D_WIDTH), row),
            pl.BlockSpec((1, SUBLANES, D_WIDTH), lambda b, i: (b, 0, 0)),
        ],
        out_shape=[
            jax.ShapeDtypeStruct((r, D_WIDTH), BF16),
            jax.ShapeDtypeStruct((batch, SUBLANES, D_WIDTH), F32),
        ],
        scratch_shapes=_lru_scratch(tile),
        compiler_params=_cparams(("arbitrary", "arbitrary")),
        name="lru_bwd",
    )(a_b, b_b, hf, gg, h0)


def _rope_tables(n_lat):
    half = HEAD_DIM // 4
    freqs = ROPE_THETA ** (-jnp.arange(half, dtype=F32) / half)
    t = jnp.arange(n_lat, dtype=jnp.int32)
    row = (t // GRID_W).astype(F32)
    col = (t % GRID_W).astype(F32)
    ar = freqs[:, None] * row[None, :]
    ac = freqs[:, None] * col[None, :]
    cos_t = jnp.concatenate([jnp.cos(ar), jnp.cos(ar), jnp.cos(ac), jnp.cos(ac)], axis=0)
    sin_t = jnp.concatenate([-jnp.sin(ar), jnp.sin(ar), -jnp.sin(ac), jnp.sin(ac)], axis=0)
    return cos_t, sin_t


def _block_diag(w):
    nb, bs, _ = w.shape
    eye = jnp.eye(nb, dtype=w.dtype)
    return (eye[:, None, :, None] * w[:, :, None, :]).reshape(nb * bs, nb * bs)


def kernel(x, c, ctx, c_ctx, w_mod, b_mod, norm_mix_g, norm_ffn_g, w_out, e_w_in, a_norm_g, a_ws, a_bs, b_qnorm_g, b_knorm_g, b_lq1, b_lk1, b_lq2, b_lk2, b_subln_g, ffn_w1, ffn_w3, ffn_w2, o_w_in, c_qnorm_g, c_knorm_g, d_conv_w, d_conv_b, d_wa, d_ba, d_wx, d_bx, d_lambda, router_w, router_b, moe_w1, moe_w3, moe_w2):
    batch, n_lat, d = x.shape
    n_ctx = ctx.shape[1]
    depth = w_mod.shape[0]
    assert depth == 2 and d == D_MODEL and batch + 1 <= SUBLANES
    xl = x.reshape(batch * n_lat, d)
    xc = ctx.reshape(batch * n_ctx, d)

    cc = jnp.concatenate([c, c_ctx[None, :], jnp.zeros((SUBLANES - batch - 1, d), F32)], axis=0)
    mod = _modulation(cc, w_mod, b_mod)

    def mods(layer):
        parts = [mod[layer, :, k * d:(k + 1) * d] for k in range(6)]
        lat = [p[:batch, None, :] for p in parts]
        cx = [p[batch:batch + 1, None, :] for p in parts]
        return lat, cx

    cos_t, sin_t = _rope_tables(n_lat)
    cos_c = jnp.ones((HEAD_DIM, n_ctx), F32)
    sin_c = jnp.zeros((HEAD_DIM, n_ctx), F32)
    col = lambda v: v.reshape(-1, 1)
    rowv = lambda v: v.reshape(1, -1)
    tm = 512
    tm_c = 256

    (sh1, sc1, g1, sh2, sc2, g2), (sh1c, sc1c, g1c, sh2c, sc2c, g2c) = mods(0)
    lam_init = 0.8 - 0.6 * math.exp(-0.3 * 0)
    w_in = e_w_in[0].astype(BF16)
    gains = [rowv(a_norm_g[0]), col(b_qnorm_g[0]), col(b_knorm_g[0])]
    ng = rowv(norm_mix_g[0])
    u, vn, qt, k, vt = _projection(True, xl, 1.0 + sc1, sh1, ng, w_in, gains, cos_t, sin_t, tm)
    uc, vnc, qtc, kc, vtc = _projection(True, xc, 1.0 + sc1c, sh1c, ng, w_in, gains, cos_c, sin_c, tm_c)
    lamp = jnp.stack([b_lq1[0], b_lk1[0], b_lq2[0], b_lk2[0]], axis=0)
    extra = [lamp, col(b_subln_g[0])]
    yb = _attention("diff", qt, kc, vtc, k, vt, extra, batch, ATTN_TQ, ATTN_TK, lam_init)
    ybc = _attention("diff", qtc, kc, vtc, None, None, extra, batch, n_ctx, ATTN_TK, lam_init)
    ws = a_ws[0].astype(BF16)
    bias_map = jnp.repeat(a_bs[0].T, HEAD_DIM, axis=1)
    ya = _gmlp(u, vn, ws, bias_map, tm)
    yac = _gmlp(uc, vnc, ws, bias_map, tm_c)
    wo = w_out[0].astype(BF16)
    xl = _out_proj(xl, ya, yb, wo, g1, tm)
    xc = _out_proj(xc, yac, ybc, wo, g1c, tm_c)
    w1 = ffn_w1.astype(BF16)
    w3 = ffn_w3.astype(BF16)
    w2 = ffn_w2.astype(BF16)
    ngf = rowv(norm_ffn_g[0])
    tf = FFN_DIM // 2
    xl = _ffn(xl, 1.0 + sc2, sh2, g2, ngf, w1, w3, w2, None, tm, tf)
    xc = _ffn(xc, 1.0 + sc2c, sh2c, g2c, ngf, w1, w3, w2, None, tm_c, tf)

    (sh1, sc1, g1, sh2, sc2, g2), (sh1c, sc1c, _, _, _, _) = mods(1)
    w_in = o_w_in[0].astype(BF16)
    gains = [col(c_qnorm_g[0]), col(c_knorm_g[0])]
    ng = rowv(norm_mix_g[1])
    qt, k, vt, gg, xr = _projection(False, xl, 1.0 + sc1, sh1, ng, w_in, gains, cos_t, sin_t, tm)
    _, kc, vtc, _, xrc = _projection(False, xc, 1.0 + sc1c, sh1c, ng, w_in, gains, cos_c, sin_c, tm_c)
    y_attn = _attention("gqa", qt, kc, vtc, k, vt, [], batch, ATTN_TQ, ATTN_TK)

    w_cat = jnp.concatenate([_block_diag(d_wa[0, 0]), _block_diag(d_wx[0, 0]),
                             _block_diag(d_wa[0, 1]), _block_diag(d_wx[0, 1])], axis=1).astype(BF16)
    b_cat = jnp.concatenate([d_ba[0, 0], d_bx[0, 0], d_ba[0, 1], d_bx[0, 1]]).reshape(1, -1)
    lam_cat = d_lambda[0].reshape(1, -1)
    cw = d_conv_w[0]
    cb = rowv(d_conv_b[0])
    zero_state = jnp.zeros((batch, SUBLANES, D_WIDTH), F32)
    hf_c, ab_c, bb_c, st_f = _lru_forward(xrc, cw, cb, w_cat, b_cat, lam_cat, zero_state, batch, n_ctx)
    _, st_b = _lru_backward(ab_c, bb_c, hf_c, hf_c, zero_state, batch, n_ctx)
    hf, ab, bb, _ = _lru_forward(xr, cw, cb, w_cat, b_cat, lam_cat, st_f, batch, LRU_TILE)
    y_rec, _ = _lru_backward(ab, bb, hf, gg, st_b, batch, LRU_TILE)

    xl = _out_proj(xl, y_attn, y_rec, w_out[1].astype(BF16), g1, tm)
    ngf = rowv(norm_ffn_g[1])
    wr = jnp.pad(router_w[0], ((0, 0), (0, LANES - N_EXPERTS)))
    wr_hi = wr.astype(BF16)
    wr_lo = (wr - wr_hi.astype(F32)).astype(BF16)
    br = jnp.pad(router_b[0], (0, LANES - N_EXPERTS)).reshape(1, LANES)
    gate = _router(xl, 1.0 + sc2, sh2, ngf, wr_hi, wr_lo, br, tm)
    xl = _ffn(xl, 1.0 + sc2, sh2, g2, ngf, moe_w1[0].astype(BF16), moe_w3[0].astype(BF16),
              moe_w2[0].astype(BF16), gate, tm, tf)
    return xl.reshape(batch, n_lat, d)
```

```python
import functools
import math

import jax
import jax.numpy as jnp
from jax import lax
from jax.experimental import pallas as pl
from jax.experimental.pallas import tpu as pltpu

F32 = jnp.float32
BF16 = jnp.bfloat16

D_MODEL = 1024
HEAD_DIM = 64
GRID_W = 64
ROPE_THETA = 10000.0
EPS = 1e-6
CHUNK = 128
A_WIDTH = D_MODEL // 2
A_GROUPS = A_WIDTH // HEAD_DIM
B_HEADS = D_MODEL // (4 * HEAD_DIM)
B_WIDTH = B_HEADS * 2 * HEAD_DIM
E_IN = 2 * A_WIDTH + 3 * B_WIDTH
C_HEADS = (D_MODEL // 2) // HEAD_DIM
C_KV_HEADS = 2
C_GROUP = C_HEADS // C_KV_HEADS
C_WIDTH = C_HEADS * HEAD_DIM
C_KV_WIDTH = C_KV_HEADS * HEAD_DIM
D_WIDTH = D_MODEL // 2
D_BLOCKS = D_WIDTH // HEAD_DIM
CONV_W = 4
LRU_C = 8.0
O_IN = C_WIDTH + 2 * C_KV_WIDTH + 2 * D_WIDTH
FFN_DIM = 2816
N_EXPERTS = 8

LANES = 128
SUBLANES = 8
KV_BLOCK = 256
ATTN_TQ = 512
ATTN_TK = 512
EXPERT_TILE = 512
COMBINE_TILE = 256
DEN_ROWS = 16
QUERY_SCALE =HEAD_DIM ** -0.5 * math.log2(math.e)
VMEM_LIMIT = 56 * 1024 * 1024


def _cparams(sem):
    return pltpu.CompilerParams(dimension_semantics=sem, vmem_limit_bytes=VMEM_LIMIT)


def _modnorm(x, g, scp, sh):
    y = x * lax.rsqrt(jnp.mean(x * x, axis=-1, keepdims=True) + EPS)
    return (y * g) * scp + sh


def _mod_kernel(c_ref, w_ref, b_ref, o_ref):
    c = c_ref[...]
    s = c * jax.nn.sigmoid(c)
    o_ref[0] = jnp.dot(s.astype(BF16), w_ref[0].astype(BF16), preferred_element_type=F32) + b_ref[0]


def _modulation(cc, w_mod, b_mod):
    depth, d, n = w_mod.shape
    tn = 1536
    return pl.pallas_call(
        _mod_kernel,
        grid=(depth, n // tn),
        in_specs=[
            pl.BlockSpec((SUBLANES, d), lambda l, j: (0, 0)),
            pl.BlockSpec((1, d, tn), lambda l, j: (l, 0, j)),
            pl.BlockSpec((1, 1, tn), lambda l, j: (l, 0, j)),
        ],
        out_specs=pl.BlockSpec((1, SUBLANES, tn), lambda l, j: (l, 0, j)),
        out_shape=jax.ShapeDtypeStruct((depth, SUBLANES, n), F32),
        compiler_params=_cparams(("arbitrary", "arbitrary")),
        name="modulation",
    )(cc, w_mod, b_mod.reshape(depth, 1, n))


def _head_norm_rope_t(xh, g, cos_t, sin_t):
    xn = xh * lax.rsqrt(jnp.mean(xh * xh, axis=0, keepdims=True) + EPS) * g
    sw = jnp.concatenate([xn[16:32], xn[0:16], xn[48:64], xn[32:48]], axis=0)
    return xn * cos_t + sw * sin_t


def _proj_even_kernel(x_ref, scp_ref, sh_ref, ng_ref, w_ref, ang_ref, qg_ref, kg_ref, cos_ref, sin_ref,
                      u_ref, vn_ref, qt_ref, k_ref, vt_ref, *, tm):
    h = _modnorm(x_ref[...], ng_ref[...], scp_ref[0], sh_ref[0])
    y = jnp.dot(h.astype(BF16), w_ref[...], preferred_element_type=F32)
    ua = y[:, 0:A_WIDTH]
    va = y[:, A_WIDTH:2 * A_WIDTH]
    q = y[:, 2 * A_WIDTH:2 * A_WIDTH + B_WIDTH]
    k = y[:, 2 * A_WIDTH + B_WIDTH:2 * A_WIDTH + 2 * B_WIDTH]
    v = y[:, 2 * A_WIDTH + 2 * B_WIDTH:]
    u_ref[...] = jax.nn.gelu(ua)
    gv = jax.nn.gelu(va)
    vn = gv * lax.rsqrt(jnp.mean(gv * gv, axis=-1, keepdims=True) + EPS) * ang_ref[...]
    vn_ref[...] = vn.astype(BF16)
    cos_t = cos_ref[...]
    sin_t = sin_ref[...]
    q_t = q.T
    k_t = k.T
    zeros = jnp.zeros((HEAD_DIM, tm), F32)
    k_parts = []
    for j in range(2 * B_HEADS):
        sl = slice(j * HEAD_DIM, (j + 1) * HEAD_DIM)
        qh = _head_norm_rope_t(q_t[sl], qg_ref[...], cos_t, sin_t) * QUERY_SCALE
        blk = jnp.concatenate([qh, zeros] if j % 2 == 0 else [zeros, qh], axis=0)
        qt_ref[j] = blk.astype(BF16)
        k_parts.append(_head_norm_rope_t(k_t[sl], kg_ref[...], cos_t, sin_t))
    k_ref[...] = jnp.concatenate(k_parts, axis=0).T.astype(BF16)
    v_t = v.T.astype(BF16)
    for c in range(tm // KV_BLOCK):
        vt_ref[c] = v_t[:, c * KV_BLOCK:(c + 1) * KV_BLOCK]


def _proj_odd_kernel(x_ref, scp_ref, sh_ref, ng_ref, w_ref, qg_ref, kg_ref, cos_ref, sin_ref,
                     qt_ref, k_ref, vt_ref, gg_ref, xr_ref, *, tm):
    h = _modnorm(x_ref[...], ng_ref[...], scp_ref[0], sh_ref[0])
    y = jnp.dot(h.astype(BF16), w_ref[...], preferred_element_type=F32)
    q = y[:, 0:C_WIDTH]
    k = y[:, C_WIDTH:C_WIDTH + C_KV_WIDTH]
    v = y[:, C_WIDTH + C_KV_WIDTH:C_WIDTH + 2 * C_KV_WIDTH]
    gate = y[:, C_WIDTH + 2 * C_KV_WIDTH:C_WIDTH + 2 * C_KV_WIDTH + D_WIDTH]
    xr = y[:, C_WIDTH + 2 * C_KV_WIDTH + D_WIDTH:]
    gg_ref[...] = jax.nn.gelu(gate)
    xr_ref[...] = xr
    cos_t = cos_ref[...]
    sin_t = sin_ref[...]
    q_t = q.T
    k_t = k.T
    zeros = jnp.zeros((HEAD_DIM, tm), F32)
    for j in range(C_HEADS):
        sl = slice(j * HEAD_DIM, (j + 1) * HEAD_DIM)
        qh = _head_norm_rope_t(q_t[sl], qg_ref[...], cos_t, sin_t) * QUERY_SCALE
        blk = jnp.concatenate([qh, zeros] if j // C_GROUP == 0 else [zeros, qh], axis=0)
        qt_ref[j] = blk.astype(BF16)
    k_parts = [_head_norm_rope_t(k_t[j * HEAD_DIM:(j + 1) * HEAD_DIM], kg_ref[...], cos_t, sin_t)
               for j in range(C_KV_HEADS)]
    k_ref[...] = jnp.concatenate(k_parts, axis=0).T.astype(BF16)
    v_t = v.T.astype(BF16)
    for c in range(tm // KV_BLOCK):
        vt_ref[c] = v_t[:, c * KV_BLOCK:(c + 1) * KV_BLOCK]


def _projection(even, x2, scp, sh, ng, w, gains, cos_t, sin_t, tm):
    r = x2.shape[0]
    nt = r // tm
    tiles_per_group = nt // scp.shape[0]
    pos_tiles = cos_t.shape[1] // tm
    n_in = w.shape[1]
    row = lambda i: (i, 0)
    grp = lambda i: (i // tiles_per_group, 0, 0)
    fixed = lambda i: (0, 0)
    pos = lambda i: (0, i % pos_tiles)
    in_specs = [
        pl.BlockSpec((tm, D_MODEL), row),
        pl.BlockSpec((1, 1, D_MODEL), grp),
        pl.BlockSpec((1, 1, D_MODEL), grp),
        pl.BlockSpec((1, D_MODEL), fixed),
        pl.BlockSpec((D_MODEL, n_in), fixed),
    ]
    in_specs += [pl.BlockSpec(g.shape, fixed) for g in gains]
    in_specs += [pl.BlockSpec((HEAD_DIM, tm), pos), pl.BlockSpec((HEAD_DIM, tm), pos)]
    if even:
        kw, vw, nh = B_WIDTH, B_WIDTH, 2 * B_HEADS
        body = functools.partial(_proj_even_kernel, tm=tm)
        out_shape = [
            jax.ShapeDtypeStruct((r, A_WIDTH), F32),
            jax.ShapeDtypeStruct((r, A_WIDTH), BF16),
            jax.ShapeDtypeStruct((nh, LANES, r), BF16),
            jax.ShapeDtypeStruct((r, kw), BF16),
            jax.ShapeDtypeStruct((r // KV_BLOCK, vw, KV_BLOCK), BF16),
        ]
        out_specs = [
            pl.BlockSpec((tm, A_WIDTH), row),
            pl.BlockSpec((tm, A_WIDTH), row),
            pl.BlockSpec((nh, LANES, tm), lambda i: (0, 0, i)),
            pl.BlockSpec((tm, kw), row),
            pl.BlockSpec((tm // KV_BLOCK, vw, KV_BLOCK), lambda i: (i, 0, 0)),
        ]
    else:
        kw, vw, nh = C_KV_WIDTH, C_KV_WIDTH, C_HEADS
        body = functools.partial(_proj_odd_kernel, tm=tm)
        out_shape = [
            jax.ShapeDtypeStruct((nh, LANES, r), BF16),
            jax.ShapeDtypeStruct((r, kw), BF16),
            jax.ShapeDtypeStruct((r // KV_BLOCK, vw, KV_BLOCK), BF16),
            jax.ShapeDtypeStruct((r, D_WIDTH), F32),
            jax.ShapeDtypeStruct((r, D_WIDTH), F32),
        ]
        out_specs = [
            pl.BlockSpec((nh, LANES, tm), lambda i: (0, 0, i)),
            pl.BlockSpec((tm, kw), row),
            pl.BlockSpec((tm // KV_BLOCK, vw, KV_BLOCK), lambda i: (i, 0, 0)),
            pl.BlockSpec((tm, D_WIDTH), row),
            pl.BlockSpec((tm, D_WIDTH), row),
        ]
    return pl.pallas_call(
        body, grid=(nt,), in_specs=in_specs, out_specs=out_specs, out_shape=out_shape,
        compiler_params=_cparams(("arbitrary",)),
        name="proj_even" if even else "proj_odd",
    )(x2, scp, sh, ng, w, *gains, cos_t, sin_t)


def _attn_kernel(*refs, mode, n_lat, tk, lam_init):
    qt_ref, kc_ref, vtc_ref = refs[:3]
    pos = 3
    if n_lat:
        kl_ref, vtl_ref = refs[3:5]
        pos = 5
    if mode == "diff":
        lamp_ref, sg_ref = refs[pos:pos + 2]
        pos += 2
    o_ref, acc_ref, s_ref, p_ref = refs[pos:pos + 4]
    tq = qt_ref.shape[-1]
    q = (qt_ref[0], qt_ref[1])
    acc_ref[...] = jnp.zeros(acc_ref.shape, F32)

    lc = kc_ref.shape[0]
    ones = jnp.ones((DEN_ROWS, KV_BLOCK), BF16)

    def scores(kb, slot):
        n = kb.shape[0]
        mb = []
        for c in range(2):
            s = jnp.dot(kb, q[c], preferred_element_type=F32)
            s_ref[slot, c, 0:n] = s
            mb.append(jnp.max(s, axis=0, keepdims=True))
        return tuple(mb)

    def softmax(slot, n, mb, m):
        out, alphas = [], []
        for c in range(2):
            m_new = jnp.maximum(m[c], mb[c])
            p_ref[slot, c, 0:n] = jnp.exp2(s_ref[slot, c, 0:n] - m_new).astype(BF16)
            alphas.append(jnp.exp2(m[c] - m_new))
            out.append(m_new)
        return tuple(out), tuple(alphas)

    def values(vt_blocks, slot, alphas):
        for c in range(2):
            upd = alphas[c] * acc_ref[c]
            for i, vtb in enumerate(vt_blocks):
                lhs = jnp.concatenate([vtb, ones], axis=0)
                upd = upd + jnp.dot(lhs, p_ref[slot, c, i * KV_BLOCK:(i + 1) * KV_BLOCK],
                                    preferred_element_type=F32)
            acc_ref[c] = upd

    per = tk // KV_BLOCK

    def k_lat(j):
        return kl_ref[pl.ds(pl.multiple_of(j * tk, tk), tk), :]

    def vt_lat(j):
        return [vtl_ref[j * per + i] for i in range(per)]

    vt_ctx = [vtc_ref[i] for i in range(lc // KV_BLOCK)]
    neg = jnp.full((1, tq), -jnp.inf, F32)
    m = (neg, neg)
    mb = scores(kc_ref[...], 0)
    if not n_lat:
        m, al = softmax(0, lc, mb, m)
        values(vt_ctx, 0, al)
    else:
        assert n_lat % 2 == 0
        mb_next = scores(k_lat(0), 1)
        m, al = softmax(0, lc, mb, m)
        mb = mb_next
        mb_next = scores(k_lat(1), 0)
        m, al_next = softmax(1, tk, mb, m)
        values(vt_ctx, 0, al)
        mb, al = mb_next, al_next

        def body(i, carry):
            m, mb, al = carry
            t = 2 * i + 2
            mb_next = scores(k_lat(t), 1)
            m, al_next = softmax(0, tk, mb, m)
            values(vt_lat(t - 2), 1, al)
            mb, al = mb_next, al_next
            mb_next = scores(k_lat(t + 1), 0)
            m, al_next = softmax(1, tk, mb, m)
            values(vt_lat(t - 1), 0, al)
            return m, mb_next, al_next

        m, mb, al = lax.fori_loop(0, (n_lat - 2) // 2, body, (m, mb, al))
        m, al_next = softmax(0, tk, mb, m)
        values(vt_lat(n_lat - 2), 1, al)
        values(vt_lat(n_lat - 1), 0, al_next)
    l0 = acc_ref[0, LANES:LANES + 1, :]
    l1 = acc_ref[1, LANES:LANES + 1, :]
    if mode == "diff":
        lp = lamp_ref[...]
        lam = (jnp.exp(jnp.sum(lp[0:1] * lp[1:2], axis=-1, keepdims=True))
               - jnp.exp(jnp.sum(lp[2:3] * lp[3:4], axis=-1, keepdims=True)) + lam_init)
        o = acc_ref[0, 0:LANES, :] * (1.0 / l0) - lam * (acc_ref[1, 0:LANES, :] * (1.0 / l1))
        o = o * lax.rsqrt(jnp.mean(o * o, axis=0, keepdims=True) + EPS) * sg_ref[...] * (1.0 - lam_init)
    else:
        row0 = pl.multiple_of((pl.program_id(1) // (C_GROUP // 2)) * HEAD_DIM, HEAD_DIM)
        o = jnp.concatenate([acc_ref[0, pl.ds(row0, HEAD_DIM), :] * (1.0 / l0),
                             acc_ref[1, pl.ds(row0, HEAD_DIM), :] * (1.0 / l1)], axis=0)
    o_ref[...] = o.T.astype(o_ref.dtype)


def _attention(mode, qt, kc, vtc, kl, vtl, extra, batch, tq, tk, lam_init=0.0):
    units = qt.shape[0] // 2
    rq = qt.shape[2]
    nq = rq // batch // tq
    lc = kc.shape[0] // batch
    col = (lambda u: u) if mode == "diff" else (lambda u: 0)
    in_specs = [
        pl.BlockSpec((2, LANES, tq), lambda b, u, i: (u, 0, b * nq + i)),
        pl.BlockSpec((lc, LANES), lambda b, u, i: (b, col(u))),
        pl.BlockSpec((lc // KV_BLOCK, LANES, KV_BLOCK), lambda b, u, i: (b, col(u), 0)),
    ]
    args = [qt, kc, vtc]
    n_lat = 0
    if kl is not None:
        ll = kl.shape[0] // batch
        n_lat = ll // tk
        in_specs += [
            pl.BlockSpec((ll, LANES), lambda b, u, i: (b, col(u))),
            pl.BlockSpec((ll // KV_BLOCK, LANES, KV_BLOCK), lambda b, u, i: (b, col(u), 0)),
        ]
        args += [kl, vtl]
    for e in extra:
        in_specs.append(pl.BlockSpec(e.shape, lambda b, u, i: (0, 0)))
        args.append(e)
    rows = max(tk, lc)
    return pl.pallas_call(
        functools.partial(_attn_kernel, mode=mode, n_lat=n_lat, tk=tk, lam_init=lam_init),
        grid=(batch, units, nq),
        in_specs=in_specs,
        out_specs=pl.BlockSpec((tq, LANES), lambda b, u, i: (b * nq + i, u)),
        out_shape=jax.ShapeDtypeStruct((rq, units * LANES), BF16),
        scratch_shapes=[pltpu.VMEM((2, LANES + DEN_ROWS, tq), F32),
                        pltpu.VMEM((2, 2, rows, tq), F32),
                        pltpu.VMEM((2, 2, rows, tq), BF16)],
        compiler_params=_cparams(("arbitrary", "arbitrary", "arbitrary")),
        name="attn_" + mode,
    )(*args)


def _gmlp_kernel(u_ref, vn_ref, ws_ref, bias_ref, o_ref, *, tm):
    lane = lax.broadcasted_iota(jnp.int32, (CHUNK, LANES), 1)
    for c in range(tm // CHUNK):
        rows = slice(c * CHUNK, (c + 1) * CHUNK)
        for j in range(A_GROUPS // 2):
            cols = slice(j * LANES, (j + 1) * LANES)
            rhs = vn_ref[rows, cols]
            lo = jnp.dot(ws_ref[2 * j], rhs, preferred_element_type=F32)
            hi = jnp.dot(ws_ref[2 * j + 1], rhs, preferred_element_type=F32)
            mixed = jnp.where(lane < HEAD_DIM, lo, hi) + bias_ref[:, cols]
            o_ref[rows, cols] = (u_ref[rows, cols] * mixed).astype(o_ref.dtype)


def _gmlp(u, vn, ws, bias_map, tm):
    r = u.shape[0]
    row = lambda i: (i, 0)
    return pl.pallas_call(
        functools.partial(_gmlp_kernel, tm=tm),
        grid=(r // tm,),
        in_specs=[
            pl.BlockSpec((tm, A_WIDTH), row),
            pl.BlockSpec((tm, A_WIDTH), row),
            pl.BlockSpec(ws.shape, lambda i: (0, 0, 0)),
            pl.BlockSpec(bias_map.shape, lambda i: (0, 0)),
        ],
        out_specs=pl.BlockSpec((tm, A_WIDTH), row),
        out_shape=jax.ShapeDtypeStruct((r, A_WIDTH), BF16),
        compiler_params=_cparams(("arbitrary",)),
        name="gmlp",
    )(u, vn, ws, bias_map)


def _out_kernel(x_ref, ya_ref, yb_ref, w_ref, g_ref, o_ref):
    half = ya_ref.shape[1]
    y = (jnp.dot(ya_ref[...], w_ref[0:half, :], preferred_element_type=F32)
         + jnp.dot(yb_ref[...], w_ref[half:, :], preferred_element_type=F32))
    o_ref[...] = x_ref[...] + g_ref[0] * y


def _out_proj(x2, ya, yb, w, g, tm):
    r = x2.shape[0]
    tiles_per_group = (r // tm) // g.shape[0]
    row = lambda i: (i, 0)
    return pl.pallas_call(
        _out_kernel,
        grid=(r // tm,),
        in_specs=[
            pl.BlockSpec((tm, D_MODEL), row),
            pl.BlockSpec((tm, ya.shape[1]), row),
            pl.BlockSpec((tm, yb.shape[1]), row),
            pl.BlockSpec(w.shape, lambda i: (0, 0)),
            pl.BlockSpec((1, 1, D_MODEL), lambda i: (i // tiles_per_group, 0, 0)),
        ],
        out_specs=pl.BlockSpec((tm, D_MODEL), row),
        out_shape=jax.ShapeDtypeStruct((r, D_MODEL), F32),
        compiler_params=_cparams(("arbitrary",)),
        name="out_proj",
    )(x2, ya, yb, w, g)


def _ffn_kernel(x_ref, scp_ref, sh_ref, g_ref, ng_ref, w1_ref, w3_ref, w2_ref, o_ref, h_ref, acc_ref):
    f = pl.program_id(1)

    @pl.when(f == 0)
    def _():
        h_ref[...] = _modnorm(x_ref[...], ng_ref[...], scp_ref[0], sh_ref[0]).astype(BF16)
        acc_ref[...] = jnp.zeros(acc_ref.shape, F32)

    h = h_ref[...]
    a = jnp.dot(h, w1_ref[...], preferred_element_type=F32)
    b = jnp.dot(h, w3_ref[...], preferred_element_type=F32)
    t = (a * jax.nn.sigmoid(a)) * b
    acc_ref[...] += jnp.dot(t.astype(BF16), w2_ref[...], preferred_element_type=F32)

    @pl.when(f == pl.num_programs(1) - 1)
    def _():
        o_ref[...] = x_ref[...] + g_ref[0] * acc_ref[...]


def _ffn(x2, scp, sh, g, ng, w1, w3, w2, tm, tf):
    r = x2.shape[0]
    fdim = w1.shape[1]
    tiles_per_group = (r // tm) // g.shape[0]
    row = lambda i, f: (i, 0)
    grp = lambda i, f: (i // tiles_per_group, 0, 0)
    return pl.pallas_call(
        _ffn_kernel,
        grid=(r // tm, fdim // tf),
        in_specs=[
            pl.BlockSpec((tm, D_MODEL), row),
            pl.BlockSpec((1, 1, D_MODEL), grp),
            pl.BlockSpec((1, 1, D_MODEL), grp),
            pl.BlockSpec((1, 1, D_MODEL), grp),
            pl.BlockSpec((1, D_MODEL), lambda i, f: (0, 0)),
            pl.BlockSpec((D_MODEL, tf), lambda i, f: (0, f)),
            pl.BlockSpec((D_MODEL, tf), lambda i, f: (0, f)),
            pl.BlockSpec((tf, D_MODEL), lambda i, f: (f, 0)),
        ],
        out_specs=pl.BlockSpec((tm, D_MODEL), row),
        out_shape=jax.ShapeDtypeStruct((r, D_MODEL), F32),
        scratch_shapes=[pltpu.VMEM((tm, D_MODEL), BF16), pltpu.VMEM((tm, D_MODEL), F32)],
        compiler_params=_cparams(("arbitrary", "arbitrary")),
        name="ffn",
    )(x2, scp, sh, g, ng, w1, w3, w2)


META_E1, META_E2, META_G1, META_G2, META_R1, META_R2 = range(6)


def _lane_col(x, lane, k):
    return jnp.sum(jnp.where(lane == k, x, 0.0), axis=-1, keepdims=True)


def _router_kernel(x_ref, scp_ref, sh_ref, ng_ref, whi_ref, wlo_ref, b_ref, hn_ref, meta_ref, cnt_ref, carry_ref):
    @pl.when(pl.program_id(0) == 0)
    def _():
        carry_ref[...] = jnp.zeros(carry_ref.shape, F32)

    h = _modnorm(x_ref[...], ng_ref[...], scp_ref[0], sh_ref[0])
    hn_ref[...] = h
    h_hi = h.astype(BF16)
    h_lo = (h - h_hi.astype(F32)).astype(BF16)
    logits = (jnp.dot(h_hi, whi_ref[...], preferred_element_type=F32)
              + jnp.dot(h_lo, whi_ref[...], preferred_element_type=F32)
              + jnp.dot(h_hi, wlo_ref[...], preferred_element_type=F32)) + b_ref[...]
    tm = logits.shape[0]
    lane = lax.broadcasted_iota(jnp.int32, logits.shape, 1).astype(F32)
    logits = jnp.where(lane < N_EXPERTS, logits, -jnp.inf)
    m1 = jnp.max(logits, axis=-1, keepdims=True)
    i1 = jnp.min(jnp.where(logits == m1, lane, float(LANES)), axis=-1, keepdims=True)
    rest = jnp.where(lane == i1, -jnp.inf, logits)
    m2 = jnp.max(rest, axis=-1, keepdims=True)
    i2 = jnp.min(jnp.where(rest == m2, lane, float(LANES)), axis=-1, keepdims=True)
    e2 = jnp.exp(m2 - m1)
    den = 1.0 + e2
    oh1 = lane == i1
    oh2 = lane == i2
    both = jnp.where(jnp.logical_or(oh1, oh2), 1.0, 0.0)
    tri = (lax.broadcasted_iota(jnp.int32, (tm, tm), 1) < lax.broadcasted_iota(jnp.int32, (tm, tm), 0))
    before = jnp.dot(jnp.where(tri, 1.0, 0.0).astype(BF16), both.astype(BF16),
                     preferred_element_type=F32) + carry_ref[0:1, :]
    r1 = jnp.sum(jnp.where(oh1, before, 0.0), axis=-1, keepdims=True)
    r2 = jnp.sum(jnp.where(oh2, before, 0.0), axis=-1, keepdims=True)
    meta = jnp.zeros(logits.shape, F32)
    for k, v in ((META_E1, i1), (META_E2, i2), (META_G1, 1.0 / den), (META_G2, e2 / den),
                 (META_R1, r1), (META_R2, r2)):
        meta = jnp.where(lane == k, v, meta)
    meta_ref[...] = meta
    carry_ref[...] = carry_ref[...] + jnp.sum(both, axis=0, keepdims=True)
    cnt_ref[...] = carry_ref[...]


def _router(x2, scp, sh, ng, w_hi, w_lo, b, tm):
    r = x2.shape[0]
    tiles_per_group = (r // tm) // scp.shape[0]
    row = lambda i: (i, 0)
    grp = lambda i: (i // tiles_per_group, 0, 0)
    fixed = lambda i: (0, 0)
    return pl.pallas_call(
        _router_kernel,
        grid=(r // tm,),
        in_specs=[
            pl.BlockSpec((tm, D_MODEL), row),
            pl.BlockSpec((1, 1, D_MODEL), grp),
            pl.BlockSpec((1, 1, D_MODEL), grp),
            pl.BlockSpec((1, D_MODEL), fixed),
            pl.BlockSpec(w_hi.shape, fixed),
            pl.BlockSpec(w_lo.shape, fixed),
            pl.BlockSpec(b.shape, fixed),
        ],
        out_specs=[
            pl.BlockSpec((tm, D_MODEL), row),
            pl.BlockSpec((tm, LANES), row),
            pl.BlockSpec((SUBLANES, LANES), fixed),
        ],
        out_shape=[
            jax.ShapeDtypeStruct((r, D_MODEL), F32),
            jax.ShapeDtypeStruct((r, LANES), F32),
            jax.ShapeDtypeStruct((SUBLANES, LANES), F32),
        ],
        scratch_shapes=[pltpu.VMEM((SUBLANES, LANES), F32)],
        compiler_params=_cparams(("arbitrary",)),
        name="router",
    )(x2, scp, sh, ng, w_hi, w_lo, b)


def _row_copy(src_ref, src_row, dst_ref, dst_row, sem):
    return pltpu.make_async_copy(src_ref.at[pl.ds(src_row, 1)], dst_ref.at[pl.ds(dst_row, 1)], sem)


def _dispatch_kernel(pos_ref, h_ref, init_ref, xs_ref, sem):
    del init_ref
    tm = h_ref.shape[0]
    base = pl.program_id(0) * (2 * tm)

    def issue(r, _):
        for k in range(2):
            _row_copy(h_ref, r, xs_ref, pos_ref[base + 2 * r + k], sem).start()
        return 0

    lax.fori_loop(0, tm, issue, 0, unroll=8)
    for k in range(2):
        pltpu.make_async_copy(h_ref, xs_ref.at[pl.ds(0, tm)], sem).wait()


def _dispatch(pos_flat, hn, n_rows, tm):
    r = hn.shape[0]
    return pl.pallas_call(
        _dispatch_kernel,
        grid_spec=pltpu.PrefetchScalarGridSpec(
            num_scalar_prefetch=1,
            grid=(r // tm,),
            in_specs=[pl.BlockSpec((tm, D_MODEL), lambda i, pos: (i, 0)),
                      pl.BlockSpec(memory_space=pl.ANY)],
            out_specs=pl.BlockSpec(memory_space=pl.ANY),
            scratch_shapes=[pltpu.SemaphoreType.DMA(())],
        ),
        out_shape=jax.ShapeDtypeStruct((n_rows, D_MODEL), F32),
        input_output_aliases={2: 0},
        compiler_params=_cparams(("arbitrary",)),
        name="dispatch",
    )(pos_flat, hn, jnp.zeros((n_rows, D_MODEL), F32))


def _expert_kernel(te_ref, nu_ref, xs_ref, w1_ref, w3_ref, w2_ref, o_ref, h_ref, acc_ref):
    del te_ref
    f = pl.program_id(1)

    @pl.when(pl.program_id(0) < nu_ref[0])
    def _():
        @pl.when(f == 0)
        def _():
            h_ref[...] = xs_ref[...].astype(BF16)
            acc_ref[...] = jnp.zeros(acc_ref.shape, F32)

        h = h_ref[...]
        a = jnp.dot(h, w1_ref[0], preferred_element_type=F32)
        b = jnp.dot(h, w3_ref[0], preferred_element_type=F32)
        t = (a * jax.nn.sigmoid(a)) * b
        acc_ref[...] += jnp.dot(t.astype(BF16), w2_ref[0], preferred_element_type=F32)

        @pl.when(f == pl.num_programs(1) - 1)
        def _():
            o_ref[...] = acc_ref[...]

    @pl.when(pl.program_id(0) >= nu_ref[0])
    def _():
        o_ref[...] = jnp.zeros(o_ref.shape, F32)


def _experts(tile_expert, n_used, xs, w1, w3, w2, tm, tf):
    n_tiles = xs.shape[0] // tm
    nf = w1.shape[2] // tf
    row = lambda i, f, te, nu: (jnp.minimum(i, nu[0] - 1), 0)
    fsel = lambda i, f, nu: jnp.where(i < nu[0], f, nf - 1)
    return pl.pallas_call(
        _expert_kernel,
        grid_spec=pltpu.PrefetchScalarGridSpec(
            num_scalar_prefetch=2,
            grid=(n_tiles, nf),
            in_specs=[
                pl.BlockSpec((tm, D_MODEL), row),
                pl.BlockSpec((1, D_MODEL, tf), lambda i, f, te, nu: (te[i], 0, fsel(i, f, nu))),
                pl.BlockSpec((1, D_MODEL, tf), lambda i, f, te, nu: (te[i], 0, fsel(i, f, nu))),
                pl.BlockSpec((1, tf, D_MODEL), lambda i, f, te, nu: (te[i], fsel(i, f, nu), 0)),
            ],
            out_specs=pl.BlockSpec((tm, D_MODEL), lambda i, f, te, nu: (i, 0)),
            scratch_shapes=[pltpu.VMEM((tm, D_MODEL), BF16), pltpu.VMEM((tm, D_MODEL), F32)],
        ),
        out_shape=jax.ShapeDtypeStruct(xs.shape, F32),
        compiler_params=_cparams(("arbitrary", "arbitrary")),
        name="experts",
    )(tile_expert, n_used, xs, w1, w3, w2)


def _combine_kernel(pos_ref, x_ref, meta_ref, g_ref, ys_ref, o_ref, buf_ref, sem):
    tm = x_ref.shape[0]
    base = pl.program_id(0) * (2 * tm)

    def issue(r, _):
        for k in range(2):
            _row_copy(ys_ref, pos_ref[base + 2 * r + k], buf_ref.at[k], r, sem).start()
        return 0

    lax.fori_loop(0, tm, issue, 0, unroll=8)
    for k in range(2):
        pltpu.make_async_copy(ys_ref.at[pl.ds(0, tm)], buf_ref.at[k], sem).wait()
    meta = meta_ref[...]
    lane = lax.broadcasted_iota(jnp.int32, meta.shape, 1)
    y = _lane_col(meta, lane, META_G1) * buf_ref[0] + _lane_col(meta, lane, META_G2) * buf_ref[1]
    o_ref[...] = x_ref[...] + g_ref[0] * y


def _combine(pos_flat, x2, meta, g, ys, tm):
    r = x2.shape[0]
    tiles_per_group = (r // tm) // g.shape[0]
    return pl.pallas_call(
        _combine_kernel,
        grid_spec=pltpu.PrefetchScalarGridSpec(
            num_scalar_prefetch=1,
            grid=(r // tm,),
            in_specs=[
                pl.BlockSpec((tm, D_MODEL), lambda i, pos: (i, 0)),
                pl.BlockSpec((tm, LANES), lambda i, pos: (i, 0)),
                pl.BlockSpec((1, 1, D_MODEL), lambda i, pos: (i // tiles_per_group, 0, 0)),
                pl.BlockSpec(memory_space=pl.ANY),
            ],
            out_specs=pl.BlockSpec((tm, D_MODEL), lambda i, pos: (i, 0)),
            scratch_shapes=[pltpu.VMEM((2, tm, D_MODEL), F32), pltpu.SemaphoreType.DMA(())],
        ),
        out_shape=jax.ShapeDtypeStruct((r, D_MODEL), F32),
        compiler_params=_cparams(("arbitrary",)),
        name="combine",
    )(pos_flat, x2, meta, g, ys)


def _routing_plan(meta, counts, tile):
    n = meta.shape[0]
    n_tiles = 2 * n // tile + N_EXPERTS
    e = meta[:, META_E1:META_E2 + 1].astype(jnp.int32)
    rank = meta[:, META_R1:META_R2 + 1].astype(jnp.int32)
    cnt = counts[0, :N_EXPERTS].astype(jnp.int32)
    tiles_e = (cnt + tile - 1) // tile
    ends = jnp.cumsum(tiles_e)
    starts = (ends - tiles_e) * tile
    pos = (starts[e] + rank).reshape(-1)
    n_used = ends[-1:]
    t_idx = jnp.minimum(jnp.arange(n_tiles, dtype=jnp.int32), n_used[0] - 1)
    tile_expert = jnp.sum((t_idx[:, None] >= ends[None, :]).astype(jnp.int32), axis=1)
    return pos, tile_expert.astype(jnp.int32), n_used.astype(jnp.int32), n_tiles * tile


LRU_TILE = 512
LRU_CHUNKS = D_WIDTH // LANES


def _tile_scan(a_ref, b_ref, h_ref, p_ref, carry0, reverse):
    seg = a_ref.shape[1] // SUBLANES

    def local(i, c):
        hs, ps = c
        t = seg - 1 - i if reverse else i
        rows = pl.ds(t, SUBLANES, stride=seg)
        hn, pn = [], []
        for cc in range(LRU_CHUNKS):
            at = a_ref[cc, rows, :]
            h = at * hs[cc] + b_ref[cc, rows, :]
            p = at * ps[cc]
            h_ref[cc, rows, :] = h
            p_ref[cc, rows, :] = p
            hn.append(h)
            pn.append(p)
        return tuple(hn), tuple(pn)

    z = tuple(jnp.zeros((SUBLANES, LANES), F32) for _ in range(LRU_CHUNKS))
    o = tuple(jnp.ones((SUBLANES, LANES), F32) for _ in range(LRU_CHUNKS))
    h_end, p_end = lax.fori_loop(0, seg, local, (z, o), unroll=8)

    order = range(SUBLANES - 1, -1, -1) if reverse else range(SUBLANES)
    carries, outs = [], []
    for cc in range(LRU_CHUNKS):
        cin = carry0[cc]
        rows = [None] * SUBLANES
        for s in order:
            rows[s] = cin
            cin = h_end[cc][s:s + 1] + p_end[cc][s:s + 1] * cin
        carries.append(jnp.concatenate(rows, axis=0))
        outs.append(cin)

    def fix(i, _):
        rows = pl.ds(i, SUBLANES, stride=seg)
        for cc in range(LRU_CHUNKS):
            h_ref[cc, rows, :] = h_ref[cc, rows, :] + p_ref[cc, rows, :] * carries[cc]
        return 0

    lax.fori_loop(0, seg, fix, 0, unroll=8)
    return outs


def _lru_fwd_kernel(prev_ref, cur_ref, next_ref, cw_ref, cb_ref, w_ref, bias_ref, lam_ref, h0_ref,
                    hf_ref, ab_ref, bb_ref, hl_ref, ext_ref, a_ref, b_ref, h_ref, p_ref, st_ref):
    i = pl.program_id(1)
    nt = pl.num_programs(1)
    t = cur_ref.shape[0]
    pad = SUBLANES

    @pl.when(i == 0)
    def _():
        st_ref[...] = h0_ref[0]

    ext_ref[0:pad, :] = jnp.where(i == 0, 0.0, prev_ref[...])
    ext_ref[pad:pad + t, :] = cur_ref[...]
    ext_ref[pad + t:, :] = jnp.where(i == nt - 1, 0.0, next_ref[...])
    left = CONV_W // 2
    xd = cb_ref[...] + sum(cw_ref[j:j + 1, :] * ext_ref[pad + j - left:pad + j - left + t, :] for j in range(CONV_W))
    z = jnp.dot(xd.astype(BF16), w_ref[...], preferred_element_type=F32) + bias_ref[...]
    sp = jax.nn.softplus(-lam_ref[...])
    coeffs = []
    for d in range(2):
        r = jax.nn.sigmoid(z[:, (2 * d) * D_WIDTH:(2 * d + 1) * D_WIDTH])
        g = jax.nn.sigmoid(z[:, (2 * d + 1) * D_WIDTH:(2 * d + 2) * D_WIDTH])
        log_a = -LRU_C * r * sp[:, d * D_WIDTH:(d + 1) * D_WIDTH]
        a = jnp.exp(log_a)
        coeffs.append((a, jnp.sqrt(-jnp.tanh(log_a) * (a * a + 1.0)) * (g * xd)))
    ab_ref[...] = coeffs[1][0]
    bb_ref[...] = coeffs[1][1]
    for cc in range(LRU_CHUNKS):
        a_ref[cc] = coeffs[0][0][:, cc * LANES:(cc + 1) * LANES]
        b_ref[cc] = coeffs[0][1][:, cc * LANES:(cc + 1) * LANES]
    carry0 = [st_ref[0:1, cc * LANES:(cc + 1) * LANES] for cc in range(LRU_CHUNKS)]
    outs = _tile_scan(a_ref, b_ref, h_ref, p_ref, carry0, reverse=False)
    hf_ref[...] = jnp.concatenate([h_ref[cc] for cc in range(LRU_CHUNKS)], axis=-1)
    state = jnp.concatenate(outs, axis=-1)
    st_ref[...] = jnp.broadcast_to(state, st_ref.shape)
    hl_ref[0] = jnp.broadcast_to(state, st_ref.shape)


def _lru_bwd_kernel(a_in_ref, b_in_ref, hf_ref, gg_ref, h0_ref, y_ref, hl_ref, a_ref, b_ref, h_ref, p_ref, st_ref):
    i = pl.program_id(1)

    @pl.when(i == 0)
    def _():
        st_ref[...] = h0_ref[0]

    for cc in range(LRU_CHUNKS):
        a_ref[cc] = a_in_ref[:, cc * LANES:(cc + 1) * LANES]
        b_ref[cc] = b_in_ref[:, cc * LANES:(cc + 1) * LANES]
    carry0 = [st_ref[0:1, cc * LANES:(cc + 1) * LANES] for cc in range(LRU_CHUNKS)]
    outs = _tile_scan(a_ref, b_ref, h_ref, p_ref, carry0, reverse=True)
    hb = jnp.concatenate([h_ref[cc] for cc in range(LRU_CHUNKS)], axis=-1)
    y_ref[...] = ((hf_ref[...] + hb) * gg_ref[...]).astype(y_ref.dtype)
    state = jnp.concatenate(outs, axis=-1)
    st_ref[...] = jnp.broadcast_to(state, st_ref.shape)
    hl_ref[0] = jnp.broadcast_to(state, st_ref.shape)


def _lru_scratch(tile):
    chunked = pltpu.VMEM((LRU_CHUNKS, tile, LANES), F32)
    return [chunked, chunked, chunked, chunked, pltpu.VMEM((SUBLANES, D_WIDTH), F32)]


def _lru_forward(xr, cw, cb, w_cat, b_cat, lam_cat, h0, batch, tile):
    r = xr.shape[0]
    nt = r // batch // tile
    sub = tile // SUBLANES
    row = lambda b, i: (b * nt + i, 0)
    fixed = lambda b, i: (0, 0)
    return pl.pallas_call(
        _lru_fwd_kernel,
        grid=(batch, nt),
        in_specs=[
            pl.BlockSpec((SUBLANES, D_WIDTH), lambda b, i: ((b * nt + jnp.maximum(i, 1)) * sub - 1, 0)),
            pl.BlockSpec((tile, D_WIDTH), row),
            pl.BlockSpec((SUBLANES, D_WIDTH), lambda b, i: ((b * nt + jnp.minimum(i + 1, nt - 1)) * sub, 0)),
            pl.BlockSpec(cw.shape, fixed),
            pl.BlockSpec(cb.shape, fixed),
            pl.BlockSpec(w_cat.shape, fixed),
            pl.BlockSpec(b_cat.shape, fixed),
            pl.BlockSpec(lam_cat.shape, fixed),
            pl.BlockSpec((1, SUBLANES, D_WIDTH), lambda b, i: (b, 0, 0)),
        ],
        out_specs=[
            pl.BlockSpec((tile, D_WIDTH), row),
            pl.BlockSpec((tile, D_WIDTH), row),
            pl.BlockSpec((tile, D_WIDTH), row),
            pl.BlockSpec((1, SUBLANES, D_WIDTH), lambda b, i: (b, 0, 0)),
        ],
        out_shape=[
            jax.ShapeDtypeStruct((r, D_WIDTH), F32),
            jax.ShapeDtypeStruct((r, D_WIDTH), F32),
            jax.ShapeDtypeStruct((r, D_WIDTH), F32),
            jax.ShapeDtypeStruct((batch, SUBLANES, D_WIDTH), F32),
        ],
        scratch_shapes=[pltpu.VMEM((tile + 2 * SUBLANES, D_WIDTH), F32)] + _lru_scratch(tile),
        compiler_params=_cparams(("arbitrary", "arbitrary")),
        name="lru_fwd",
    )(xr, xr, xr, cw, cb, w_cat, b_cat, lam_cat, h0)


def _lru_backward(a_b, b_b, hf, gg, h0, batch, tile):
    r = a_b.shape[0]
    nt = r // batch // tile
    row = lambda b, i: (b * nt + (nt - 1 - i), 0)
    return pl.pallas_call(
        _lru_bwd_kernel,
        grid=(batch, nt),
        in_specs=[pl.BlockSpec((tile, D_WIDTH), row)] * 4
        + [pl.BlockSpec((1, SUBLANES, D_WIDTH), lambda b, i: (b, 0, 0))],
        out_specs=[
            pl.BlockSpec((tile, D_WIDTH), row),
            pl.BlockSpec((1, SUBLANES, D_WIDTH), lambda b, i: (b, 0, 0)),
        ],
        out_shape=[
            jax.ShapeDtypeStruct((r, D_WIDTH), BF16),
            jax.ShapeDtypeStruct((batch, SUBLANES, D_WIDTH), F32),
        ],
        scratch_shapes=_lru_scratch(tile),
        compiler_params=_cparams(("arbitrary", "arbitrary")),
        name="lru_bwd",
    )(a_b, b_b, hf, gg, h0)


def _rope_tables(n_lat):
    half = HEAD_DIM // 4
    freqs = ROPE_THETA ** (-jnp.arange(half, dtype=F32) / half)
    t = jnp.arange(n_lat, dtype=jnp.int32)
    row = (t // GRID_W).astype(F32)
    col = (t % GRID_W).astype(F32)
    ar = freqs[:, None] * row[None, :]
    ac = freqs[:, None] * col[None, :]
    cos_t = jnp.concatenate([jnp.cos(ar), jnp.cos(ar), jnp.cos(ac), jnp.cos(ac)], axis=0)
    sin_t = jnp.concatenate([-jnp.sin(ar), jnp.sin(ar), -jnp.sin(ac), jnp.sin(ac)], axis=0)
    return cos_t, sin_t


def _block_diag(w):
    nb, bs, _ = w.shape
    eye = jnp.eye(nb, dtype=w.dtype)
    return (eye[:, None, :, None] * w[:, :, None, :]).reshape(nb * bs, nb * bs)


def kernel(x, c, ctx, c_ctx, w_mod, b_mod, norm_mix_g, norm_ffn_g, w_out, e_w_in, a_norm_g, a_ws, a_bs, b_qnorm_g, b_knorm_g, b_lq1, b_lk1, b_lq2, b_lk2, b_subln_g, ffn_w1, ffn_w3, ffn_w2, o_w_in, c_qnorm_g, c_knorm_g, d_conv_w, d_conv_b, d_wa, d_ba, d_wx, d_bx, d_lambda, router_w, router_b, moe_w1, moe_w3, moe_w2):
    batch, n_lat, d = x.shape
    n_ctx = ctx.shape[1]
    depth = w_mod.shape[0]
    assert depth == 2 and d == D_MODEL and batch + 1 <= SUBLANES
    xl = x.reshape(batch * n_lat, d)
    xc = ctx.reshape(batch * n_ctx, d)

    cc = jnp.concatenate([c, c_ctx[None, :], jnp.zeros((SUBLANES - batch - 1, d), F32)], axis=0)
    mod = _modulation(cc, w_mod, b_mod)

    def mods(layer):
        parts = [mod[layer, :, k * d:(k + 1) * d] for k in range(6)]
        lat = [p[:batch, None, :] for p in parts]
        cx = [p[batch:batch + 1, None, :] for p in parts]
        return lat, cx

    cos_t, sin_t = _rope_tables(n_lat)
    cos_c = jnp.ones((HEAD_DIM, n_ctx), F32)
    sin_c = jnp.zeros((HEAD_DIM, n_ctx), F32)
    col = lambda v: v.reshape(-1, 1)
    rowv = lambda v: v.reshape(1, -1)
    tm = 512
    tm_c = 256

    (sh1, sc1, g1, sh2, sc2, g2), (sh1c, sc1c, g1c, sh2c, sc2c, g2c) = mods(0)
    lam_init = 0.8 - 0.6 * math.exp(-0.3 * 0)
    w_in = e_w_in[0].astype(BF16)
    gains = [rowv(a_norm_g[0]), col(b_qnorm_g[0]), col(b_knorm_g[0])]
    ng = rowv(norm_mix_g[0])
    u, vn, qt, k, vt = _projection(True, xl, 1.0 + sc1, sh1, ng, w_in, gains, cos_t, sin_t, tm)
    uc, vnc, qtc, kc, vtc = _projection(True, xc, 1.0 + sc1c, sh1c, ng, w_in, gains, cos_c, sin_c, tm_c)
    lamp = jnp.stack([b_lq1[0], b_lk1[0], b_lq2[0], b_lk2[0]], axis=0)
    extra = [lamp, col(b_subln_g[0])]
    yb = _attention("diff", qt, kc, vtc, k, vt, extra, batch, ATTN_TQ, ATTN_TK, lam_init)
    ybc = _attention("diff", qtc, kc, vtc, None, None, extra, batch, n_ctx, ATTN_TK, lam_init)
    ws = a_ws[0].astype(BF16)
    bias_map = jnp.repeat(a_bs[0].T, HEAD_DIM, axis=1)
    ya = _gmlp(u, vn, ws, bias_map, tm)
    yac = _gmlp(uc, vnc, ws, bias_map, tm_c)
    wo = w_out[0].astype(BF16)
    xl = _out_proj(xl, ya, yb, wo, g1, tm)
    xc = _out_proj(xc, yac, ybc, wo, g1c, tm_c)
    w1 = ffn_w1[0].astype(BF16)
    w3 = ffn_w3[0].astype(BF16)
    w2 = ffn_w2[0].astype(BF16)
    ngf = rowv(norm_ffn_g[0])
    tf = FFN_DIM // 2
    xl = _ffn(xl, 1.0 + sc2, sh2, g2, ngf, w1, w3, w2, tm, tf)
    xc = _ffn(xc, 1.0 + sc2c, sh2c, g2c, ngf, w1, w3, w2, tm_c, tf)

    (sh1, sc1, g1, sh2, sc2, g2), (sh1c, sc1c, _, _, _, _) = mods(1)
    w_in = o_w_in[0].astype(BF16)
    gains = [col(c_qnorm_g[0]), col(c_knorm_g[0])]
    ng = rowv(norm_mix_g[1])
    qt, k, vt, gg, xr = _projection(False, xl, 1.0 + sc1, sh1, ng, w_in, gains, cos_t, sin_t, tm)
    _, kc, vtc, _, xrc = _projection(False, xc, 1.0 + sc1c, sh1c, ng, w_in, gains, cos_c, sin_c, tm_c)
    y_attn = _attention("gqa", qt, kc, vtc, k, vt, [], batch, ATTN_TQ, ATTN_TK)

    w_cat = jnp.concatenate([_block_diag(d_wa[0, 0]), _block_diag(d_wx[0, 0]),
                             _block_diag(d_wa[0, 1]), _block_diag(d_wx[0, 1])], axis=1).astype(BF16)
    b_cat = jnp.concatenate([d_ba[0, 0], d_bx[0, 0], d_ba[0, 1], d_bx[0, 1]]).reshape(1, -1)
    lam_cat = d_lambda[0].reshape(1, -1)
    cw = d_conv_w[0]
    cb = rowv(d_conv_b[0])
    zero_state = jnp.zeros((batch, SUBLANES, D_WIDTH), F32)
    hf_c, ab_c, bb_c, st_f = _lru_forward(xrc, cw, cb, w_cat, b_cat, lam_cat, zero_state, batch, n_ctx)
    _, st_b = _lru_backward(ab_c, bb_c, hf_c, hf_c, zero_state, batch, n_ctx)
    hf, ab, bb, _ = _lru_forward(xr, cw, cb, w_cat, b_cat, lam_cat, st_f, batch, LRU_TILE)
    y_rec, _ = _lru_backward(ab, bb, hf, gg, st_b, batch, LRU_TILE)

    xl = _out_proj(xl, y_attn, y_rec, w_out[1].astype(BF16), g1, tm)
    ngf = rowv(norm_ffn_g[1])
    wr = jnp.pad(router_w[0], ((0, 0), (0, LANES - N_EXPERTS)))
    wr_hi = wr.astype(BF16)
    wr_lo = (wr - wr_hi.astype(F32)).astype(BF16)
    br = jnp.pad(router_b[0], (0, LANES - N_EXPERTS)).reshape(1, LANES)
    hn, meta, counts = _router(xl, 1.0 + sc2, sh2, ngf, wr_hi, wr_lo, br, tm)
    pos, tile_expert, n_used, n_rows = _routing_plan(meta, counts, EXPERT_TILE)
    xs = _dispatch(pos, hn, n_rows, tm)
    ys = _experts(tile_expert, n_used, xs, moe_w1[0].astype(BF16), moe_w3[0].astype(BF16),
                  moe_w2[0].astype(BF16), EXPERT_TILE, tf)
    xl = _combine(pos, xl, meta, g2, ys, COMBINE_TILE)
    return xl.reshape(batch, n_lat, d)
```

```python
import functools
import math

import jax
import jax.numpy as jnp
from jax import lax
from jax.experimental import pallas as pl
from jax.experimental.pallas import tpu as pltpu

F32 = jnp.float32
BF16 = jnp.bfloat16

D_MODEL = 1024
HEAD_DIM = 64
GRID_W = 64
ROPE_THETA = 10000.0
EPS = 1e-6
CHUNK = 128
A_WIDTH = D_MODEL // 2
A_GROUPS = A_WIDTH // HEAD_DIM
B_HEADS = D_MODEL // (4 * HEAD_DIM)
B_WIDTH = B_HEADS * 2 * HEAD_DIM
E_IN = 2 * A_WIDTH + 3 * B_WIDTH
C_HEADS = (D_MODEL // 2) // HEAD_DIM
C_KV_HEADS = 2
C_GROUP = C_HEADS // C_KV_HEADS
C_WIDTH = C_HEADS * HEAD_DIM
C_KV_WIDTH = C_KV_HEADS * HEAD_DIM
D_WIDTH = D_MODEL // 2
D_BLOCKS = D_WIDTH // HEAD_DIM
CONV_W = 4
LRU_C = 8.0
O_IN = C_WIDTH + 2 * C_KV_WIDTH + 2 * D_WIDTH
FFN_DIM = 2816
N_EXPERTS = 8

LANES = 128
SUBLANES = 8
KV_BLOCK = 256
ATTN_TQ = 1024
ATTN_TK = 512
EXPERT_TILE = 512
COMBINE_TILE = 256
DEN_ROWS = 16
QUERY_SCALE =HEAD_DIM ** -0.5 * math.log2(math.e)
VMEM_LIMIT = 56 * 1024 * 1024


def _cparams(sem):
    return pltpu.CompilerParams(dimension_semantics=sem, vmem_limit_bytes=VMEM_LIMIT)


def _modnorm(x, g, scp, sh):
    y = x * lax.rsqrt(jnp.mean(x * x, axis=-1, keepdims=True) + EPS)
    return (y * g) * scp + sh


def _mod_kernel(c_ref, w_ref, b_ref, o_ref):
    c = c_ref[...]
    s = c * jax.nn.sigmoid(c)
    o_ref[0] = jnp.dot(s.astype(BF16), w_ref[0].astype(BF16), preferred_element_type=F32) + b_ref[0]


def _modulation(cc, w_mod, b_mod):
    depth, d, n = w_mod.shape
    tn = 1536
    return pl.pallas_call(
        _mod_kernel,
        grid=(depth, n // tn),
        in_specs=[
            pl.BlockSpec((SUBLANES, d), lambda l, j: (0, 0)),
            pl.BlockSpec((1, d, tn), lambda l, j: (l, 0, j)),
            pl.BlockSpec((1, 1, tn), lambda l, j: (l, 0, j)),
        ],
        out_specs=pl.BlockSpec((1, SUBLANES, tn), lambda l, j: (l, 0, j)),
        out_shape=jax.ShapeDtypeStruct((depth, SUBLANES, n), F32),
        compiler_params=_cparams(("arbitrary", "arbitrary")),
        name="modulation",
    )(cc, w_mod, b_mod.reshape(depth, 1, n))


def _head_norm_rope_t(xh, g, cos_t, sin_t):
    xn = xh * lax.rsqrt(jnp.mean(xh * xh, axis=0, keepdims=True) + EPS) * g
    sw = jnp.concatenate([xn[16:32], xn[0:16], xn[48:64], xn[32:48]], axis=0)
    return xn * cos_t + sw * sin_t


def _proj_even_kernel(x_ref, scp_ref, sh_ref, ng_ref, w_ref, ang_ref, qg_ref, kg_ref, cos_ref, sin_ref,
                      u_ref, vn_ref, qt_ref, k_ref, vt_ref, *, tm):
    h = _modnorm(x_ref[...], ng_ref[...], scp_ref[0], sh_ref[0])
    y = jnp.dot(h.astype(BF16), w_ref[...], preferred_element_type=F32)
    ua = y[:, 0:A_WIDTH]
    va = y[:, A_WIDTH:2 * A_WIDTH]
    q = y[:, 2 * A_WIDTH:2 * A_WIDTH + B_WIDTH]
    k = y[:, 2 * A_WIDTH + B_WIDTH:2 * A_WIDTH + 2 * B_WIDTH]
    v = y[:, 2 * A_WIDTH + 2 * B_WIDTH:]
    u_ref[...] = jax.nn.gelu(ua)
    gv = jax.nn.gelu(va)
    vn = gv * lax.rsqrt(jnp.mean(gv * gv, axis=-1, keepdims=True) + EPS) * ang_ref[...]
    vn_ref[...] = vn.astype(BF16)
    cos_t = cos_ref[...]
    sin_t = sin_ref[...]
    q_t = q.T
    k_t = k.T
    zeros = jnp.zeros((HEAD_DIM, tm), F32)
    k_parts = []
    for j in range(2 * B_HEADS):
        sl = slice(j * HEAD_DIM, (j + 1) * HEAD_DIM)
        qh = _head_norm_rope_t(q_t[sl], qg_ref[...], cos_t, sin_t) * QUERY_SCALE
        blk = jnp.concatenate([qh, zeros] if j % 2 == 0 else [zeros, qh], axis=0)
        qt_ref[j] = blk.astype(BF16)
        k_parts.append(_head_norm_rope_t(k_t[sl], kg_ref[...], cos_t, sin_t))
    k_ref[...] = jnp.concatenate(k_parts, axis=0).T.astype(BF16)
    v_t = v.T.astype(BF16)
    for c in range(tm // KV_BLOCK):
        vt_ref[c] = v_t[:, c * KV_BLOCK:(c + 1) * KV_BLOCK]


def _proj_odd_kernel(x_ref, scp_ref, sh_ref, ng_ref, w_ref, qg_ref, kg_ref, cos_ref, sin_ref,
                     qt_ref, k_ref, vt_ref, gg_ref, xr_ref, *, tm):
    h = _modnorm(x_ref[...], ng_ref[...], scp_ref[0], sh_ref[0])
    y = jnp.dot(h.astype(BF16), w_ref[...], preferred_element_type=F32)
    q = y[:, 0:C_WIDTH]
    k = y[:, C_WIDTH:C_WIDTH + C_KV_WIDTH]
    v = y[:, C_WIDTH + C_KV_WIDTH:C_WIDTH + 2 * C_KV_WIDTH]
    gate = y[:, C_WIDTH + 2 * C_KV_WIDTH:C_WIDTH + 2 * C_KV_WIDTH + D_WIDTH]
    xr = y[:, C_WIDTH + 2 * C_KV_WIDTH + D_WIDTH:]
    gg_ref[...] = jax.nn.gelu(gate)
    xr_ref[...] = xr
    cos_t = cos_ref[...]
    sin_t = sin_ref[...]
    q_t = q.T
    k_t = k.T
    zeros = jnp.zeros((HEAD_DIM, tm), F32)
    for j in range(C_HEADS):
        sl = slice(j * HEAD_DIM, (j + 1) * HEAD_DIM)
        qh = _head_norm_rope_t(q_t[sl], qg_ref[...], cos_t, sin_t) * QUERY_SCALE
        blk = jnp.concatenate([qh, zeros] if j // C_GROUP == 0 else [zeros, qh], axis=0)
        qt_ref[j] = blk.astype(BF16)
    k_parts = [_head_norm_rope_t(k_t[j * HEAD_DIM:(j + 1) * HEAD_DIM], kg_ref[...], cos_t, sin_t)
               for j in range(C_KV_HEADS)]
    k_ref[...] = jnp.concatenate(k_parts, axis=0).T.astype(BF16)
    v_t = v.T.astype(BF16)
    for c in range(tm // KV_BLOCK):
        vt_ref[c] = v_t[:, c * KV_BLOCK:(c + 1) * KV_BLOCK]


def _projection(even, x2, scp, sh, ng, w, gains, cos_t, sin_t, tm):
    r = x2.shape[0]
    nt = r // tm
    tiles_per_group = nt // scp.shape[0]
    pos_tiles = cos_t.shape[1] // tm
    n_in = w.shape[1]
    row = lambda i: (i, 0)
    grp = lambda i: (i // tiles_per_group, 0, 0)
    fixed = lambda i: (0, 0)
    pos = lambda i: (0, i % pos_tiles)
    in_specs = [
        pl.BlockSpec((tm, D_MODEL), row),
        pl.BlockSpec((1, 1, D_MODEL), grp),
        pl.BlockSpec((1, 1, D_MODEL), grp),
        pl.BlockSpec((1, D_MODEL), fixed),
        pl.BlockSpec((D_MODEL, n_in), fixed),
    ]
    in_specs += [pl.BlockSpec(g.shape, fixed) for g in gains]
    in_specs += [pl.BlockSpec((HEAD_DIM, tm), pos), pl.BlockSpec((HEAD_DIM, tm), pos)]
    if even:
        kw, vw, nh = B_WIDTH, B_WIDTH, 2 * B_HEADS
        body = functools.partial(_proj_even_kernel, tm=tm)
        out_shape = [
            jax.ShapeDtypeStruct((r, A_WIDTH), F32),
            jax.ShapeDtypeStruct((r, A_WIDTH), BF16),
            jax.ShapeDtypeStruct((nh, LANES, r), BF16),
            jax.ShapeDtypeStruct((r, kw), BF16),
            jax.ShapeDtypeStruct((r // KV_BLOCK, vw, KV_BLOCK), BF16),
        ]
        out_specs = [
            pl.BlockSpec((tm, A_WIDTH), row),
            pl.BlockSpec((tm, A_WIDTH), row),
            pl.BlockSpec((nh, LANES, tm), lambda i: (0, 0, i)),
            pl.BlockSpec((tm, kw), row),
            pl.BlockSpec((tm // KV_BLOCK, vw, KV_BLOCK), lambda i: (i, 0, 0)),
        ]
    else:
        kw, vw, nh = C_KV_WIDTH, C_KV_WIDTH, C_HEADS
        body = functools.partial(_proj_odd_kernel, tm=tm)
        out_shape = [
            jax.ShapeDtypeStruct((nh, LANES, r), BF16),
            jax.ShapeDtypeStruct((r, kw), BF16),
            jax.ShapeDtypeStruct((r // KV_BLOCK, vw, KV_BLOCK), BF16),
            jax.ShapeDtypeStruct((r, D_WIDTH), F32),
            jax.ShapeDtypeStruct((r, D_WIDTH), F32),
        ]
        out_specs = [
            pl.BlockSpec((nh, LANES, tm), lambda i: (0, 0, i)),
            pl.BlockSpec((tm, kw), row),
            pl.BlockSpec((tm // KV_BLOCK, vw, KV_BLOCK), lambda i: (i, 0, 0)),
            pl.BlockSpec((tm, D_WIDTH), row),
            pl.BlockSpec((tm, D_WIDTH), row),
        ]
    return pl.pallas_call(
        body, grid=(nt,), in_specs=in_specs, out_specs=out_specs, out_shape=out_shape,
        compiler_params=_cparams(("arbitrary",)),
        name="proj_even" if even else "proj_odd",
    )(x2, scp, sh, ng, w, *gains, cos_t, sin_t)


def _attn_kernel(*refs, mode, n_lat, tk, lam_init):
    qt_ref, kc_ref, vtc_ref = refs[:3]
    pos = 3
    if n_lat:
        kl_ref, vtl_ref = refs[3:5]
        pos = 5
    if mode == "diff":
        lamp_ref, sg_ref = refs[pos:pos + 2]
        pos += 2
    o_ref, acc_ref, s_ref, p_ref = refs[pos:pos + 4]
    tq = qt_ref.shape[-1]
    q = (qt_ref[0], qt_ref[1])
    acc_ref[...] = jnp.zeros(acc_ref.shape, F32)

    lc = kc_ref.shape[0]
    ones = jnp.ones((DEN_ROWS, KV_BLOCK), BF16)

    def scores(kb, slot):
        n = kb.shape[0]
        mb = []
        for c in range(2):
            s = jnp.dot(kb, q[c], preferred_element_type=F32)
            s_ref[slot, c, 0:n] = s
            mb.append(jnp.max(s, axis=0, keepdims=True))
        return tuple(mb)

    def softmax(slot, n, mb, m):
        out, alphas = [], []
        for c in range(2):
            m_new = jnp.maximum(m[c], mb[c])
            p_ref[slot, c, 0:n] = jnp.exp2(s_ref[slot, c, 0:n] - m_new).astype(BF16)
            alphas.append(jnp.exp2(m[c] - m_new))
            out.append(m_new)
        return tuple(out), tuple(alphas)

    def values(vt_blocks, slot, alphas):
        for c in range(2):
            upd = alphas[c] * acc_ref[c]
            for i, vtb in enumerate(vt_blocks):
                lhs = jnp.concatenate([vtb, ones], axis=0)
                upd = upd + jnp.dot(lhs, p_ref[slot, c, i * KV_BLOCK:(i + 1) * KV_BLOCK],
                                    preferred_element_type=F32)
            acc_ref[c] = upd

    per = tk // KV_BLOCK

    def k_lat(j):
        return kl_ref[pl.ds(pl.multiple_of(j * tk, tk), tk), :]

    def vt_lat(j):
        return [vtl_ref[j * per + i] for i in range(per)]

    vt_ctx = [vtc_ref[i] for i in range(lc // KV_BLOCK)]
    neg = jnp.full((1, tq), -jnp.inf, F32)
    m = (neg, neg)
    mb = scores(kc_ref[...], 0)
    if not n_lat:
        m, al = softmax(0, lc, mb, m)
        values(vt_ctx, 0, al)
    else:
        assert n_lat % 2 == 0
        mb_next = scores(k_lat(0), 1)
        m, al = softmax(0, lc, mb, m)
        mb = mb_next
        mb_next = scores(k_lat(1), 0)
        m, al_next = softmax(1, tk, mb, m)
        values(vt_ctx, 0, al)
        mb, al = mb_next, al_next

        def body(i, carry):
            m, mb, al = carry
            t = 2 * i + 2
            mb_next = scores(k_lat(t), 1)
            m, al_next = softmax(0, tk, mb, m)
            values(vt_lat(t - 2), 1, al)
            mb, al = mb_next, al_next
            mb_next = scores(k_lat(t + 1), 0)
            m, al_next = softmax(1, tk, mb, m)
            values(vt_lat(t - 1), 0, al)
            return m, mb_next, al_next

        m, mb, al = lax.fori_loop(0, (n_lat - 2) // 2, body, (m, mb, al))
        m, al_next = softmax(0, tk, mb, m)
        values(vt_lat(n_lat - 2), 1, al)
        values(vt_lat(n_lat - 1), 0, al_next)
    l0 = acc_ref[0, LANES:LANES + 1, :]
    l1 = acc_ref[1, LANES:LANES + 1, :]
    if mode == "diff":
        lp = lamp_ref[...]
        lam = (jnp.exp(jnp.sum(lp[0:1] * lp[1:2], axis=-1, keepdims=True))
               - jnp.exp(jnp.sum(lp[2:3] * lp[3:4], axis=-1, keepdims=True)) + lam_init)
        o = acc_ref[0, 0:LANES, :] * (1.0 / l0) - lam * (acc_ref[1, 0:LANES, :] * (1.0 / l1))
        o = o * lax.rsqrt(jnp.mean(o * o, axis=0, keepdims=True) + EPS) * sg_ref[...] * (1.0 - lam_init)
    else:
        row0 = pl.multiple_of((pl.program_id(1) // (C_GROUP // 2)) * HEAD_DIM, HEAD_DIM)
        o = jnp.concatenate([acc_ref[0, pl.ds(row0, HEAD_DIM), :] * (1.0 / l0),
                             acc_ref[1, pl.ds(row0, HEAD_DIM), :] * (1.0 / l1)], axis=0)
    o_ref[...] = o.T.astype(o_ref.dtype)


def _attention(mode, qt, kc, vtc, kl, vtl, extra, batch, tq, tk, lam_init=0.0):
    units = qt.shape[0] // 2
    rq = qt.shape[2]
    nq = rq // batch // tq
    lc = kc.shape[0] // batch
    col = (lambda u: u) if mode == "diff" else (lambda u: 0)
    in_specs = [
        pl.BlockSpec((2, LANES, tq), lambda b, u, i: (u, 0, b * nq + i)),
        pl.BlockSpec((lc, LANES), lambda b, u, i: (b, col(u))),
        pl.BlockSpec((lc // KV_BLOCK, LANES, KV_BLOCK), lambda b, u, i: (b, col(u), 0)),
    ]
    args = [qt, kc, vtc]
    n_lat = 0
    if kl is not None:
        ll = kl.shape[0] // batch
        n_lat = ll // tk
        in_specs += [
            pl.BlockSpec((ll, LANES), lambda b, u, i: (b, col(u))),
            pl.BlockSpec((ll // KV_BLOCK, LANES, KV_BLOCK), lambda b, u, i: (b, col(u), 0)),
        ]
        args += [kl, vtl]
    for e in extra:
        in_specs.append(pl.BlockSpec(e.shape, lambda b, u, i: (0, 0)))
        args.append(e)
    rows = max(tk, lc)
    return pl.pallas_call(
        functools.partial(_attn_kernel, mode=mode, n_lat=n_lat, tk=tk, lam_init=lam_init),
        grid=(batch, units, nq),
        in_specs=in_specs,
        out_specs=pl.BlockSpec((tq, LANES), lambda b, u, i: (b * nq + i, u)),
        out_shape=jax.ShapeDtypeStruct((rq, units * LANES), BF16),
        scratch_shapes=[pltpu.VMEM((2, LANES + DEN_ROWS, tq), F32),
                        pltpu.VMEM((2, 2, rows, tq), F32),
                        pltpu.VMEM((2, 2, rows, tq), BF16)],
        compiler_params=_cparams(("arbitrary", "arbitrary", "arbitrary")),
        name="attn_" + mode,
    )(*args)


def _gmlp_kernel(u_ref, vn_ref, ws_ref, bias_ref, o_ref, *, tm):
    lane = lax.broadcasted_iota(jnp.int32, (CHUNK, LANES), 1)
    for c in range(tm // CHUNK):
        rows = slice(c * CHUNK, (c + 1) * CHUNK)
        for j in range(A_GROUPS // 2):
            cols = slice(j * LANES, (j + 1) * LANES)
            rhs = vn_ref[rows, cols]
            lo = jnp.dot(ws_ref[2 * j], rhs, preferred_element_type=F32)
            hi = jnp.dot(ws_ref[2 * j + 1], rhs, preferred_element_type=F32)
            mixed = jnp.where(lane < HEAD_DIM, lo, hi) + bias_ref[:, cols]
            o_ref[rows, cols] = (u_ref[rows, cols] * mixed).astype(o_ref.dtype)


def _gmlp(u, vn, ws, bias_map, tm):
    r = u.shape[0]
    row = lambda i: (i, 0)
    return pl.pallas_call(
        functools.partial(_gmlp_kernel, tm=tm),
        grid=(r // tm,),
        in_specs=[
            pl.BlockSpec((tm, A_WIDTH), row),
            pl.BlockSpec((tm, A_WIDTH), row),
            pl.BlockSpec(ws.shape, lambda i: (0, 0, 0)),
            pl.BlockSpec(bias_map.shape, lambda i: (0, 0)),
        ],
        out_specs=pl.BlockSpec((tm, A_WIDTH), row),
        out_shape=jax.ShapeDtypeStruct((r, A_WIDTH), BF16),
        compiler_params=_cparams(("arbitrary",)),
        name="gmlp",
    )(u, vn, ws, bias_map)


def _out_kernel(x_ref, ya_ref, yb_ref, w_ref, g_ref, o_ref):
    half = ya_ref.shape[1]
    y = (jnp.dot(ya_ref[...], w_ref[0:half, :], preferred_element_type=F32)
         + jnp.dot(yb_ref[...], w_ref[half:, :], preferred_element_type=F32))
    o_ref[...] = x_ref[...] + g_ref[0] * y


def _out_proj(x2, ya, yb, w, g, tm):
    r = x2.shape[0]
    tiles_per_group = (r // tm) // g.shape[0]
    row = lambda i: (i, 0)
    return pl.pallas_call(
        _out_kernel,
        grid=(r // tm,),
        in_specs=[
            pl.BlockSpec((tm, D_MODEL), row),
            pl.BlockSpec((tm, ya.shape[1]), row),
            pl.BlockSpec((tm, yb.shape[1]), row),
            pl.BlockSpec(w.shape, lambda i: (0, 0)),
            pl.BlockSpec((1, 1, D_MODEL), lambda i: (i // tiles_per_group, 0, 0)),
        ],
        out_specs=pl.BlockSpec((tm, D_MODEL), row),
        out_shape=jax.ShapeDtypeStruct((r, D_MODEL), F32),
        compiler_params=_cparams(("arbitrary",)),
        name="out_proj",
    )(x2, ya, yb, w, g)


def _ffn_kernel(x_ref, scp_ref, sh_ref, g_ref, ng_ref, w1_ref, w3_ref, w2_ref, o_ref, h_ref, acc_ref):
    f = pl.program_id(1)

    @pl.when(f == 0)
    def _():
        h_ref[...] = _modnorm(x_ref[...], ng_ref[...], scp_ref[0], sh_ref[0]).astype(BF16)
        acc_ref[...] = jnp.zeros(acc_ref.shape, F32)

    h = h_ref[...]
    a = jnp.dot(h, w1_ref[...], preferred_element_type=F32)
    b = jnp.dot(h, w3_ref[...], preferred_element_type=F32)
    t = (a * jax.nn.sigmoid(a)) * b
    acc_ref[...] += jnp.dot(t.astype(BF16), w2_ref[...], preferred_element_type=F32)

    @pl.when(f == pl.num_programs(1) - 1)
    def _():
        o_ref[...] = x_ref[...] + g_ref[0] * acc_ref[...]


def _ffn(x2, scp, sh, g, ng, w1, w3, w2, tm, tf):
    r = x2.shape[0]
    fdim = w1.shape[1]
    tiles_per_group = (r // tm) // g.shape[0]
    row = lambda i, f: (i, 0)
    grp = lambda i, f: (i // tiles_per_group, 0, 0)
    return pl.pallas_call(
        _ffn_kernel,
        grid=(r // tm, fdim // tf),
        in_specs=[
            pl.BlockSpec((tm, D_MODEL), row),
            pl.BlockSpec((1, 1, D_MODEL), grp),
            pl.BlockSpec((1, 1, D_MODEL), grp),
            pl.BlockSpec((1, 1, D_MODEL), grp),
            pl.BlockSpec((1, D_MODEL), lambda i, f: (0, 0)),
            pl.BlockSpec((D_MODEL, tf), lambda i, f: (0, f)),
            pl.BlockSpec((D_MODEL, tf), lambda i, f: (0, f)),
            pl.BlockSpec((tf, D_MODEL), lambda i, f: (f, 0)),
        ],
        out_specs=pl.BlockSpec((tm, D_MODEL), row),
        out_shape=jax.ShapeDtypeStruct((r, D_MODEL), F32),
        scratch_shapes=[pltpu.VMEM((tm, D_MODEL), BF16), pltpu.VMEM((tm, D_MODEL), F32)],
        compiler_params=_cparams(("arbitrary", "arbitrary")),
        name="ffn",
    )(x2, scp, sh, g, ng, w1, w3, w2)


META_E1, META_E2, META_G1, META_G2, META_R1, META_R2 = range(6)


def _lane_col(x, lane, k):
    return jnp.sum(jnp.where(lane == k, x, 0.0), axis=-1, keepdims=True)


def _router_kernel(x_ref, scp_ref, sh_ref, ng_ref, whi_ref, wlo_ref, b_ref, hn_ref, meta_ref, cnt_ref, carry_ref):
    @pl.when(pl.program_id(0) == 0)
    def _():
        carry_ref[...] = jnp.zeros(carry_ref.shape, F32)

    h = _modnorm(x_ref[...], ng_ref[...], scp_ref[0], sh_ref[0])
    hn_ref[...] = h
    h_hi = h.astype(BF16)
    h_lo = (h - h_hi.astype(F32)).astype(BF16)
    logits = (jnp.dot(h_hi, whi_ref[...], preferred_element_type=F32)
              + jnp.dot(h_lo, whi_ref[...], preferred_element_type=F32)
              + jnp.dot(h_hi, wlo_ref[...], preferred_element_type=F32)) + b_ref[...]
    tm = logits.shape[0]
    lane = lax.broadcasted_iota(jnp.int32, logits.shape, 1).astype(F32)
    logits = jnp.where(lane < N_EXPERTS, logits, -jnp.inf)
    m1 = jnp.max(logits, axis=-1, keepdims=True)
    i1 = jnp.min(jnp.where(logits == m1, lane, float(LANES)), axis=-1, keepdims=True)
    rest = jnp.where(lane == i1, -jnp.inf, logits)
    m2 = jnp.max(rest, axis=-1, keepdims=True)
    i2 = jnp.min(jnp.where(rest == m2, lane, float(LANES)), axis=-1, keepdims=True)
    e2 = jnp.exp(m2 - m1)
    den = 1.0 + e2
    oh1 = lane == i1
    oh2 = lane == i2
    both = jnp.where(jnp.logical_or(oh1, oh2), 1.0, 0.0)
    tri = (lax.broadcasted_iota(jnp.int32, (tm, tm), 1) < lax.broadcasted_iota(jnp.int32, (tm, tm), 0))
    before = jnp.dot(jnp.where(tri, 1.0, 0.0).astype(BF16), both.astype(BF16),
                     preferred_element_type=F32) + carry_ref[0:1, :]
    r1 = jnp.sum(jnp.where(oh1, before, 0.0), axis=-1, keepdims=True)
    r2 = jnp.sum(jnp.where(oh2, before, 0.0), axis=-1, keepdims=True)
    meta = jnp.zeros(logits.shape, F32)
    for k, v in ((META_E1, i1), (META_E2, i2), (META_G1, 1.0 / den), (META_G2, e2 / den),
                 (META_R1, r1), (META_R2, r2)):
        meta = jnp.where(lane == k, v, meta)
    meta_ref[...] = meta
    carry_ref[...] = carry_ref[...] + jnp.sum(both, axis=0, keepdims=True)
    cnt_ref[...] = carry_ref[...]


def _router(x2, scp, sh, ng, w_hi, w_lo, b, tm):
    r = x2.shape[0]
    tiles_per_group = (r // tm) // scp.shape[0]
    row = lambda i: (i, 0)
    grp = lambda i: (i // tiles_per_group, 0, 0)
    fixed = lambda i: (0, 0)
    return pl.pallas_call(
        _router_kernel,
        grid=(r // tm,),
        in_specs=[
            pl.BlockSpec((tm, D_MODEL), row),
            pl.BlockSpec((1, 1, D_MODEL), grp),
            pl.BlockSpec((1, 1, D_MODEL), grp),
            pl.BlockSpec((1, D_MODEL), fixed),
            pl.BlockSpec(w_hi.shape, fixed),
            pl.BlockSpec(w_lo.shape, fixed),
            pl.BlockSpec(b.shape, fixed),
        ],
        out_specs=[
            pl.BlockSpec((tm, D_MODEL), row),
            pl.BlockSpec((tm, LANES), row),
            pl.BlockSpec((SUBLANES, LANES), fixed),
        ],
        out_shape=[
            jax.ShapeDtypeStruct((r, D_MODEL), F32),
            jax.ShapeDtypeStruct((r, LANES), F32),
            jax.ShapeDtypeStruct((SUBLANES, LANES), F32),
        ],
        scratch_shapes=[pltpu.VMEM((SUBLANES, LANES), F32)],
        compiler_params=_cparams(("arbitrary",)),
        name="router",
    )(x2, scp, sh, ng, w_hi, w_lo, b)


def _row_copy(src_ref, src_row, dst_ref, dst_row, sem):
    return pltpu.make_async_copy(src_ref.at[pl.ds(src_row, 1)], dst_ref.at[pl.ds(dst_row, 1)], sem)


def _dispatch_kernel(pos_ref, h_ref, init_ref, xs_ref, sem):
    del init_ref
    tm = h_ref.shape[0]
    base = pl.program_id(0) * (2 * tm)

    def issue(r, _):
        for k in range(2):
            _row_copy(h_ref, r, xs_ref, pos_ref[base + 2 * r + k], sem).start()
        return 0

    lax.fori_loop(0, tm, issue, 0, unroll=8)
    for k in range(2):
        pltpu.make_async_copy(h_ref, xs_ref.at[pl.ds(0, tm)], sem).wait()


def _dispatch(pos_flat, hn, n_rows, tm):
    r = hn.shape[0]
    return pl.pallas_call(
        _dispatch_kernel,
        grid_spec=pltpu.PrefetchScalarGridSpec(
            num_scalar_prefetch=1,
            grid=(r // tm,),
            in_specs=[pl.BlockSpec((tm, D_MODEL), lambda i, pos: (i, 0)),
                      pl.BlockSpec(memory_space=pl.ANY)],
            out_specs=pl.BlockSpec(memory_space=pl.ANY),
            scratch_shapes=[pltpu.SemaphoreType.DMA(())],
        ),
        out_shape=jax.ShapeDtypeStruct((n_rows, D_MODEL), F32),
        input_output_aliases={2: 0},
        compiler_params=_cparams(("arbitrary",)),
        name="dispatch",
    )(pos_flat, hn, jnp.zeros((n_rows, D_MODEL), F32))


def _expert_kernel(te_ref, nu_ref, xs_ref, w1_ref, w3_ref, w2_ref, o_ref, h_ref, acc_ref):
    del te_ref
    f = pl.program_id(1)

    @pl.when(pl.program_id(0) < nu_ref[0])
    def _():
        @pl.when(f == 0)
        def _():
            h_ref[...] = xs_ref[...].astype(BF16)
            acc_ref[...] = jnp.zeros(acc_ref.shape, F32)

        h = h_ref[...]
        a = jnp.dot(h, w1_ref[0], preferred_element_type=F32)
        b = jnp.dot(h, w3_ref[0], preferred_element_type=F32)
        t = (a * jax.nn.sigmoid(a)) * b
        acc_ref[...] += jnp.dot(t.astype(BF16), w2_ref[0], preferred_element_type=F32)

        @pl.when(f == pl.num_programs(1) - 1)
        def _():
            o_ref[...] = acc_ref[...]

    @pl.when(pl.program_id(0) >= nu_ref[0])
    def _():
        o_ref[...] = jnp.zeros(o_ref.shape, F32)


def _experts(tile_expert, n_used, xs, w1, w3, w2, tm, tf):
    n_tiles = xs.shape[0] // tm
    nf = w1.shape[2] // tf
    row = lambda i, f, te, nu: (jnp.minimum(i, nu[0] - 1), 0)
    fsel = lambda i, f, nu: jnp.where(i < nu[0], f, nf - 1)
    return pl.pallas_call(
        _expert_kernel,
        grid_spec=pltpu.PrefetchScalarGridSpec(
            num_scalar_prefetch=2,
            grid=(n_tiles, nf),
            in_specs=[
                pl.BlockSpec((tm, D_MODEL), row),
                pl.BlockSpec((1, D_MODEL, tf), lambda i, f, te, nu: (te[i], 0, fsel(i, f, nu))),
                pl.BlockSpec((1, D_MODEL, tf), lambda i, f, te, nu: (te[i], 0, fsel(i, f, nu))),
                pl.BlockSpec((1, tf, D_MODEL), lambda i, f, te, nu: (te[i], fsel(i, f, nu), 0)),
            ],
            out_specs=pl.BlockSpec((tm, D_MODEL), lambda i, f, te, nu: (i, 0)),
            scratch_shapes=[pltpu.VMEM((tm, D_MODEL), BF16), pltpu.VMEM((tm, D_MODEL), F32)],
        ),
        out_shape=jax.ShapeDtypeStruct(xs.shape, F32),
        compiler_params=_cparams(("arbitrary", "arbitrary")),
        name="experts",
    )(tile_expert, n_used, xs, w1, w3, w2)


def _combine_kernel(pos_ref, x_ref, meta_ref, g_ref, ys_ref, o_ref, buf_ref, sem):
    tm = x_ref.shape[0]
    i = pl.program_id(0)

    def gather(tile, slot):
        base = tile * (2 * tm)

        def issue(r, _):
            for k in range(2):
                _row_copy(ys_ref, pos_ref[base + 2 * r + k], buf_ref.at[slot, k], r, sem.at[slot]).start()
            return 0

        lax.fori_loop(0, tm, issue, 0, unroll=8)

    @pl.when(i == 0)
    def _():
        gather(0, 0)

    slot = i % 2

    @pl.when(i + 1 < pl.num_programs(0))
    def _():
        gather(i + 1, 1 - slot)

    for k in range(2):
        pltpu.make_async_copy(ys_ref.at[pl.ds(0, tm)], buf_ref.at[slot, k], sem.at[slot]).wait()
    meta = meta_ref[...]
    lane = lax.broadcasted_iota(jnp.int32, meta.shape, 1)
    y = _lane_col(meta, lane, META_G1) * buf_ref[slot, 0] + _lane_col(meta, lane, META_G2) * buf_ref[slot, 1]
    o_ref[...] = x_ref[...] + g_ref[0] * y


def _combine(pos_flat, x2, meta, g, ys, tm):
    r = x2.shape[0]
    tiles_per_group = (r // tm) // g.shape[0]
    return pl.pallas_call(
        _combine_kernel,
        grid_spec=pltpu.PrefetchScalarGridSpec(
            num_scalar_prefetch=1,
            grid=(r // tm,),
            in_specs=[
                pl.BlockSpec((tm, D_MODEL), lambda i, pos: (i, 0)),
                pl.BlockSpec((tm, LANES), lambda i, pos: (i, 0)),
                pl.BlockSpec((1, 1, D_MODEL), lambda i, pos: (i // tiles_per_group, 0, 0)),
                pl.BlockSpec(memory_space=pl.ANY),
            ],
            out_specs=pl.BlockSpec((tm, D_MODEL), lambda i, pos: (i, 0)),
            scratch_shapes=[pltpu.VMEM((2, 2, tm, D_MODEL), F32), pltpu.SemaphoreType.DMA((2,))],
        ),
        out_shape=jax.ShapeDtypeStruct((r, D_MODEL), F32),
        compiler_params=_cparams(("arbitrary",)),
        name="combine",
    )(pos_flat, x2, meta, g, ys)


def _routing_plan(meta, counts, tile):
    n = meta.shape[0]
    n_tiles = 2 * n // tile + N_EXPERTS
    e = meta[:, META_E1:META_E2 + 1].astype(jnp.int32)
    rank = meta[:, META_R1:META_R2 + 1].astype(jnp.int32)
    cnt = counts[0, :N_EXPERTS].astype(jnp.int32)
    tiles_e = (cnt + tile - 1) // tile
    ends = jnp.cumsum(tiles_e)
    starts = (ends - tiles_e) * tile
    pos = (starts[e] + rank).reshape(-1)
    n_used = ends[-1:]
    t_idx = jnp.minimum(jnp.arange(n_tiles, dtype=jnp.int32), n_used[0] - 1)
    tile_expert = jnp.sum((t_idx[:, None] >= ends[None, :]).astype(jnp.int32), axis=1)
    return pos, tile_expert.astype(jnp.int32), n_used.astype(jnp.int32), n_tiles * tile


LRU_TILE = 512
LRU_CHUNKS = D_WIDTH // LANES
LRU_PAD = SUBLANES


def _lru_pitch(tile):
    seg = tile // SUBLANES
    assert (seg // SUBLANES) % 2 == 0
    return seg, seg + LRU_PAD


def _tile_scans(af_ref, bf_ref, ab_ref, bb_ref, seg, pitch, carry_f):
    nc = LRU_CHUNKS

    def rows_f(t):
        return pl.ds(t, SUBLANES, stride=pitch)

    def rows_b(t):
        return pl.ds(seg - 1 - t, SUBLANES, stride=pitch)

    def ends(t, c):
        hf, pf, hb, pb = c
        rf, rb = rows_f(t), rows_b(t)
        nhf, npf, nhb, npb = [], [], [], []
        for cc in range(nc):
            a = af_ref[cc, rf, :]
            nhf.append(a * hf[cc] + bf_ref[cc, rf, :])
            npf.append(a * pf[cc])
            a = ab_ref[cc, rb, :]
            nhb.append(a * hb[cc] + bb_ref[cc, rb, :])
            npb.append(a * pb[cc])
        return tuple(nhf), tuple(npf), tuple(nhb), tuple(npb)

    z = tuple(jnp.zeros((SUBLANES, LANES), F32) for _ in range(nc))
    o = tuple(jnp.ones((SUBLANES, LANES), F32) for _ in range(nc))
    hf, pf, hb, pb = lax.fori_loop(0, seg, ends, (z, o, z, o), unroll=8)

    in_f, in_b, in_p, out_f, tile_p, tile_h = [], [], [], [], [], []
    for cc in range(nc):
        cin = carry_f[cc]
        rows = [None] * SUBLANES
        for s in range(SUBLANES):
            rows[s] = cin
            cin = hf[cc][s:s + 1] + pf[cc][s:s + 1] * cin
        in_f.append(jnp.concatenate(rows, axis=0))
        out_f.append(cin)
        cin = jnp.zeros((1, LANES), F32)
        pin = jnp.ones((1, LANES), F32)
        rows_h = [None] * SUBLANES
        rows_p = [None] * SUBLANES
        for s in range(SUBLANES - 1, -1, -1):
            rows_h[s] = cin
            rows_p[s] = pin
            cin = hb[cc][s:s + 1] + pb[cc][s:s + 1] * cin
            pin = pb[cc][s:s + 1] * pin
        in_b.append(jnp.concatenate(rows_h, axis=0))
        in_p.append(jnp.concatenate(rows_p, axis=0))
        tile_h.append(cin)
        tile_p.append(pin)

    def final(t, c):
        hf, hb, pb = c
        rf, rb = rows_f(t), rows_b(t)
        nhf, nhb, npb = [], [], []
        for cc in range(nc):
            h = af_ref[cc, rf, :] * hf[cc] + bf_ref[cc, rf, :]
            bf_ref[cc, rf, :] = h
            nhf.append(h)
            a = ab_ref[cc, rb, :]
            h = a * hb[cc] + bb_ref[cc, rb, :]
            p = a * pb[cc]
            bb_ref[cc, rb, :] = h
            ab_ref[cc, rb, :] = p
            nhb.append(h)
            npb.append(p)
        return tuple(nhf), tuple(nhb), tuple(npb)

    lax.fori_loop(0, seg, final, (tuple(in_f), tuple(in_b), tuple(in_p)), unroll=8)
    return out_f, tile_p, tile_h


def _lru_main_kernel(prev_ref, cur_ref, next_ref, gg_ref, cw_ref, cb_ref, w_ref, bias_ref, lam_ref, h0_ref,
                     s1_ref, s2_ref, agg_ref, hl_ref, ext_ref, af_ref, bf_ref, ab_ref, bb_ref, st_ref):
    i = pl.program_id(1)
    nt = pl.num_programs(1)
    t = cur_ref.shape[0]
    seg, pitch = _lru_pitch(t)
    pad = SUBLANES

    @pl.when(i == 0)
    def _():
        st_ref[...] = h0_ref[0]

    ext_ref[0:pad, :] = jnp.where(i == 0, 0.0, prev_ref[...])
    ext_ref[pad:pad + t, :] = cur_ref[...]
    ext_ref[pad + t:, :] = jnp.where(i == nt - 1, 0.0, next_ref[...])
    left = CONV_W // 2
    xd = cb_ref[...] + sum(cw_ref[j:j + 1, :] * ext_ref[pad + j - left:pad + j - left + t, :] for j in range(CONV_W))
    z = jnp.dot(xd.astype(BF16), w_ref[...], preferred_element_type=F32) + bias_ref[...]
    decay = LRU_C * jax.nn.softplus(-lam_ref[...])
    for d, (a_ref, b_ref) in enumerate(((af_ref, bf_ref), (ab_ref, bb_ref))):
        r = 0.5 * jnp.tanh(0.5 * z[:, (2 * d) * D_WIDTH:(2 * d + 1) * D_WIDTH]) + 0.5
        g = 0.5 * jnp.tanh(0.5 * z[:, (2 * d + 1) * D_WIDTH:(2 * d + 2) * D_WIDTH]) + 0.5
        nla = r * decay[:, d * D_WIDTH:(d + 1) * D_WIDTH]
        a = jnp.exp(-nla)
        m = jnp.tanh(nla) * (a * a + 1.0)
        b = jnp.where(m > 0.0, m * lax.rsqrt(m), 0.0) * (g * xd)
        for cc in range(LRU_CHUNKS):
            for s in range(SUBLANES):
                a_ref[cc, s * pitch:s * pitch + seg, :] = a[s * seg:(s + 1) * seg, cc * LANES:(cc + 1) * LANES]
                b_ref[cc, s * pitch:s * pitch + seg, :] = b[s * seg:(s + 1) * seg, cc * LANES:(cc + 1) * LANES]
    carry_f = [st_ref[0:1, cc * LANES:(cc + 1) * LANES] for cc in range(LRU_CHUNKS)]
    out_f, tile_p, tile_h = _tile_scans(af_ref, bf_ref, ab_ref, bb_ref, seg, pitch, carry_f)
    for cc in range(LRU_CHUNKS):
        cols = slice(cc * LANES, (cc + 1) * LANES)
        for s in range(SUBLANES):
            src = slice(s * pitch, s * pitch + seg)
            dst = slice(s * seg, (s + 1) * seg)
            gg = gg_ref[dst, cols]
            s1_ref[dst, cols] = (bf_ref[cc, src, :] + bb_ref[cc, src, :]) * gg
            s2_ref[dst, cols] = ab_ref[cc, src, :] * gg
    state = jnp.concatenate(out_f, axis=-1)
    st_ref[...] = jnp.broadcast_to(state, st_ref.shape)
    hl_ref[0] = jnp.broadcast_to(state, st_ref.shape)
    agg_ref[0] = jnp.concatenate([jnp.concatenate(tile_p, axis=-1), jnp.concatenate(tile_h, axis=-1),
                                  jnp.zeros((SUBLANES - 2, D_WIDTH), F32)], axis=0)


def _lru_fix_kernel(s1_ref, s2_ref, agg_ref, h0_ref, y_ref, st_ref):
    @pl.when(pl.program_id(1) == 0)
    def _():
        st_ref[...] = h0_ref[0]

    carry = st_ref[0:1, :]
    y_ref[...] = (s1_ref[...] + s2_ref[...] * carry).astype(y_ref.dtype)
    st_ref[...] = jnp.broadcast_to(agg_ref[0, 1:2, :] + agg_ref[0, 0:1, :] * carry, st_ref.shape)


def _lru_main(xr, gg, cw, cb, w_cat, b_cat, lam_cat, h0, batch, tile):
    r = xr.shape[0]
    nt = r // batch // tile
    sub = tile // SUBLANES
    _, pitch = _lru_pitch(tile)
    row = lambda b, i: (b * nt + i, 0)
    fixed = lambda b, i: (0, 0)
    state = pl.BlockSpec((1, SUBLANES, D_WIDTH), lambda b, i: (b, 0, 0))
    chunked = pltpu.VMEM((LRU_CHUNKS, SUBLANES * pitch, LANES), F32)
    return pl.pallas_call(
        _lru_main_kernel,
        grid=(batch, nt),
        in_specs=[
            pl.BlockSpec((SUBLANES, D_WIDTH), lambda b, i: ((b * nt + jnp.maximum(i, 1)) * sub - 1, 0)),
            pl.BlockSpec((tile, D_WIDTH), row),
            pl.BlockSpec((SUBLANES, D_WIDTH), lambda b, i: ((b * nt + jnp.minimum(i + 1, nt - 1)) * sub, 0)),
            pl.BlockSpec((tile, D_WIDTH), row),
            pl.BlockSpec(cw.shape, fixed),
            pl.BlockSpec(cb.shape, fixed),
            pl.BlockSpec(w_cat.shape, fixed),
            pl.BlockSpec(b_cat.shape, fixed),
            pl.BlockSpec(lam_cat.shape, fixed),
            state,
        ],
        out_specs=[
            pl.BlockSpec((tile, D_WIDTH), row),
            pl.BlockSpec((tile, D_WIDTH), row),
            pl.BlockSpec((1, SUBLANES, D_WIDTH), lambda b, i: (b * nt + i, 0, 0)),
            state,
        ],
        out_shape=[
            jax.ShapeDtypeStruct((r, D_WIDTH), F32),
            jax.ShapeDtypeStruct((r, D_WIDTH), F32),
            jax.ShapeDtypeStruct((batch * nt, SUBLANES, D_WIDTH), F32),
            jax.ShapeDtypeStruct((batch, SUBLANES, D_WIDTH), F32),
        ],
        scratch_shapes=[pltpu.VMEM((tile + 2 * SUBLANES, D_WIDTH), F32), chunked, chunked, chunked, chunked,
                        pltpu.VMEM((SUBLANES, D_WIDTH), F32)],
        compiler_params=_cparams(("arbitrary", "arbitrary")),
        name="lru_main",
    )(xr, xr, xr, gg, cw, cb, w_cat, b_cat, lam_cat, h0)


def _lru_fix(s1, s2, agg, h0, batch, tile):
    r = s1.shape[0]
    nt = r // batch // tile
    row = lambda b, i: (b * nt + (nt - 1 - i), 0)
    return pl.pallas_call(
        _lru_fix_kernel,
        grid=(batch, nt),
        in_specs=[
            pl.BlockSpec((tile, D_WIDTH), row),
            pl.BlockSpec((tile, D_WIDTH), row),
            pl.BlockSpec((1, SUBLANES, D_WIDTH), lambda b, i: (b * nt + (nt - 1 - i), 0, 0)),
            pl.BlockSpec((1, SUBLANES, D_WIDTH), lambda b, i: (b, 0, 0)),
        ],
        out_specs=pl.BlockSpec((tile, D_WIDTH), row),
        out_shape=jax.ShapeDtypeStruct((r, D_WIDTH), BF16),
        scratch_shapes=[pltpu.VMEM((SUBLANES, D_WIDTH), F32)],
        compiler_params=_cparams(("arbitrary", "arbitrary")),
        name="lru_fix",
    )(s1, s2, agg, h0)


def _rope_tables(n_lat):
    half = HEAD_DIM // 4
    freqs = ROPE_THETA ** (-jnp.arange(half, dtype=F32) / half)
    t = jnp.arange(n_lat, dtype=jnp.int32)
    row = (t // GRID_W).astype(F32)
    col = (t % GRID_W).astype(F32)
    ar = freqs[:, None] * row[None, :]
    ac = freqs[:, None] * col[None, :]
    cos_t = jnp.concatenate([jnp.cos(ar), jnp.cos(ar), jnp.cos(ac), jnp.cos(ac)], axis=0)
    sin_t = jnp.concatenate([-jnp.sin(ar), jnp.sin(ar), -jnp.sin(ac), jnp.sin(ac)], axis=0)
    return cos_t, sin_t


def _block_diag(w):
    nb, bs, _ = w.shape
    eye = jnp.eye(nb, dtype=w.dtype)
    return (eye[:, None, :, None] * w[:, :, None, :]).reshape(nb * bs, nb * bs)


def kernel(x, c, ctx, c_ctx, w_mod, b_mod, norm_mix_g, norm_ffn_g, w_out, e_w_in, a_norm_g, a_ws, a_bs, b_qnorm_g, b_knorm_g, b_lq1, b_lk1, b_lq2, b_lk2, b_subln_g, ffn_w1, ffn_w3, ffn_w2, o_w_in, c_qnorm_g, c_knorm_g, d_conv_w, d_conv_b, d_wa, d_ba, d_wx, d_bx, d_lambda, router_w, router_b, moe_w1, moe_w3, moe_w2):
    batch, n_lat, d = x.shape
    n_ctx = ctx.shape[1]
    depth = w_mod.shape[0]
    assert depth == 2 and d == D_MODEL and batch + 1 <= SUBLANES
    xl = x.reshape(batch * n_lat, d)
    xc = ctx.reshape(batch * n_ctx, d)

    cc = jnp.concatenate([c, c_ctx[None, :], jnp.zeros((SUBLANES - batch - 1, d), F32)], axis=0)
    mod = _modulation(cc, w_mod, b_mod)

    def mods(layer):
        parts = [mod[layer, :, k * d:(k + 1) * d] for k in range(6)]
        lat = [p[:batch, None, :] for p in parts]
        cx = [p[batch:batch + 1, None, :] for p in parts]
        return lat, cx

    cos_t, sin_t = _rope_tables(n_lat)
    cos_c = jnp.ones((HEAD_DIM, n_ctx), F32)
    sin_c = jnp.zeros((HEAD_DIM, n_ctx), F32)
    col = lambda v: v.reshape(-1, 1)
    rowv = lambda v: v.reshape(1, -1)
    tm = 512
    tm_c = 256

    (sh1, sc1, g1, sh2, sc2, g2), (sh1c, sc1c, g1c, sh2c, sc2c, g2c) = mods(0)
    lam_init = 0.8 - 0.6 * math.exp(-0.3 * 0)
    w_in = e_w_in[0].astype(BF16)
    gains = [rowv(a_norm_g[0]), col(b_qnorm_g[0]), col(b_knorm_g[0])]
    ng = rowv(norm_mix_g[0])
    u, vn, qt, k, vt = _projection(True, xl, 1.0 + sc1, sh1, ng, w_in, gains, cos_t, sin_t, tm)
    uc, vnc, qtc, kc, vtc = _projection(True, xc, 1.0 + sc1c, sh1c, ng, w_in, gains, cos_c, sin_c, tm_c)
    lamp = jnp.stack([b_lq1[0], b_lk1[0], b_lq2[0], b_lk2[0]], axis=0)
    extra = [lamp, col(b_subln_g[0])]
    yb = _attention("diff", qt, kc, vtc, k, vt, extra, batch, ATTN_TQ, ATTN_TK, lam_init)
    ybc = _attention("diff", qtc, kc, vtc, None, None, extra, batch, n_ctx, ATTN_TK, lam_init)
    ws = a_ws[0].astype(BF16)
    bias_map = jnp.repeat(a_bs[0].T, HEAD_DIM, axis=1)
    ya = _gmlp(u, vn, ws, bias_map, tm)
    yac = _gmlp(uc, vnc, ws, bias_map, tm_c)
    wo = w_out[0].astype(BF16)
    xl = _out_proj(xl, ya, yb, wo, g1, tm)
    xc = _out_proj(xc, yac, ybc, wo, g1c, tm_c)
    w1 = ffn_w1[0].astype(BF16)
    w3 = ffn_w3[0].astype(BF16)
    w2 = ffn_w2[0].astype(BF16)
    ngf = rowv(norm_ffn_g[0])
    tf = FFN_DIM // 2
    xl = _ffn(xl, 1.0 + sc2, sh2, g2, ngf, w1, w3, w2, tm, tf)
    xc = _ffn(xc, 1.0 + sc2c, sh2c, g2c, ngf, w1, w3, w2, tm_c, tf)

    (sh1, sc1, g1, sh2, sc2, g2), (sh1c, sc1c, _, _, _, _) = mods(1)
    w_in = o_w_in[0].astype(BF16)
    gains = [col(c_qnorm_g[0]), col(c_knorm_g[0])]
    ng = rowv(norm_mix_g[1])
    qt, k, vt, gg, xr = _projection(False, xl, 1.0 + sc1, sh1, ng, w_in, gains, cos_t, sin_t, tm)
    _, kc, vtc, _, xrc = _projection(False, xc, 1.0 + sc1c, sh1c, ng, w_in, gains, cos_c, sin_c, tm_c)
    y_attn = _attention("gqa", qt, kc, vtc, k, vt, [], batch, ATTN_TQ, ATTN_TK)

    w_cat = jnp.concatenate([_block_diag(d_wa[0, 0]), _block_diag(d_wx[0, 0]),
                             _block_diag(d_wa[0, 1]), _block_diag(d_wx[0, 1])], axis=1).astype(BF16)
    b_cat = jnp.concatenate([d_ba[0, 0], d_bx[0, 0], d_ba[0, 1], d_bx[0, 1]]).reshape(1, -1)
    lam_cat = d_lambda[0].reshape(1, -1)
    cw = d_conv_w[0]
    cb = rowv(d_conv_b[0])
    zero_state = jnp.zeros((batch, SUBLANES, D_WIDTH), F32)
    _, _, agg_c, st_f = _lru_main(xrc, xrc, cw, cb, w_cat, b_cat, lam_cat, zero_state, batch, n_ctx)
    st_b = jnp.broadcast_to(agg_c[:, 1:2, :], (batch, SUBLANES, D_WIDTH))
    s1, s2, agg, _ = _lru_main(xr, gg, cw, cb, w_cat, b_cat, lam_cat, st_f, batch, LRU_TILE)
    y_rec = _lru_fix(s1, s2, agg, st_b, batch, LRU_TILE)

    xl = _out_proj(xl, y_attn, y_rec, w_out[1].astype(BF16), g1, tm)
    ngf = rowv(norm_ffn_g[1])
    wr = jnp.pad(router_w[0], ((0, 0), (0, LANES - N_EXPERTS)))
    wr_hi = wr.astype(BF16)
    wr_lo = (wr - wr_hi.astype(F32)).astype(BF16)
    br = jnp.pad(router_b[0], (0, LANES - N_EXPERTS)).reshape(1, LANES)
    hn, meta, counts = _router(xl, 1.0 + sc2, sh2, ngf, wr_hi, wr_lo, br, tm)
    pos, tile_expert, n_used, n_rows = _routing_plan(meta, counts, EXPERT_TILE)
    xs = _dispatch(pos, hn, n_rows, tm)
    ys = _experts(tile_expert, n_used, xs, moe_w1[0].astype(BF16), moe_w3[0].astype(BF16),
                  moe_w2[0].astype(BF16), EXPERT_TILE, tf)
    xl = _combine(pos, xl, meta, g2, ys, COMBINE_TILE)
    return xl.reshape(batch, n_lat, d)
```

```python
import functools
import math

import jax
import jax.numpy as jnp
from jax import lax
from jax.experimental import pallas as pl
from jax.experimental.pallas import tpu as pltpu

F32 = jnp.float32
BF16 = jnp.bfloat16

D_MODEL = 1024
HEAD_DIM = 64
GRID_W = 64
ROPE_THETA = 10000.0
EPS = 1e-6
CHUNK = 128
A_WIDTH = D_MODEL // 2
A_GROUPS = A_WIDTH // HEAD_DIM
B_HEADS = D_MODEL // (4 * HEAD_DIM)
B_WIDTH = B_HEADS * 2 * HEAD_DIM
E_IN = 2 * A_WIDTH + 3 * B_WIDTH
C_HEADS = (D_MODEL // 2) // HEAD_DIM
C_KV_HEADS = 2
C_GROUP = C_HEADS // C_KV_HEADS
C_WIDTH = C_HEADS * HEAD_DIM
C_KV_WIDTH = C_KV_HEADS * HEAD_DIM
D_WIDTH = D_MODEL // 2
D_BLOCKS = D_WIDTH // HEAD_DIM
CONV_W = 4
LRU_C = 8.0
O_IN = C_WIDTH + 2 * C_KV_WIDTH + 2 * D_WIDTH
FFN_DIM = 2816
N_EXPERTS = 8

LANES = 128
SUBLANES = 8
KV_BLOCK = 256
ATTN_TQ = 2048
ATTN_TK = 512
EXPERT_TILE = 256
COMBINE_TILE = 256
DEN_ROWS = 16
QUERY_SCALE =HEAD_DIM ** -0.5 * math.log2(math.e)
VMEM_LIMIT = 56 * 1024 * 1024


def _cparams(sem):
    return pltpu.CompilerParams(dimension_semantics=sem, vmem_limit_bytes=VMEM_LIMIT)


def _modnorm(x, g, scp, sh):
    y = x * lax.rsqrt(jnp.mean(x * x, axis=-1, keepdims=True) + EPS)
    return (y * g) * scp + sh


def _mod_kernel(c_ref, w_ref, b_ref, o_ref):
    c = c_ref[...]
    s = c * jax.nn.sigmoid(c)
    o_ref[0] = jnp.dot(s.astype(BF16), w_ref[0].astype(BF16), preferred_element_type=F32) + b_ref[0]


def _modulation(cc, w_mod, b_mod):
    depth, d, n = w_mod.shape
    tn = 1536
    return pl.pallas_call(
        _mod_kernel,
        grid=(depth, n // tn),
        in_specs=[
            pl.BlockSpec((SUBLANES, d), lambda l, j: (0, 0)),
            pl.BlockSpec((1, d, tn), lambda l, j: (l, 0, j)),
            pl.BlockSpec((1, 1, tn), lambda l, j: (l, 0, j)),
        ],
        out_specs=pl.BlockSpec((1, SUBLANES, tn), lambda l, j: (l, 0, j)),
        out_shape=jax.ShapeDtypeStruct((depth, SUBLANES, n), F32),
        compiler_params=_cparams(("arbitrary", "arbitrary")),
        name="modulation",
    )(cc, w_mod, b_mod.reshape(depth, 1, n))


def _head_norm_rope_t(xh, g, cos_t, sin_t):
    xn = xh * lax.rsqrt(jnp.mean(xh * xh, axis=0, keepdims=True) + EPS) * g
    sw = jnp.concatenate([xn[16:32], xn[0:16], xn[48:64], xn[32:48]], axis=0)
    return xn * cos_t + sw * sin_t


def _proj_even_kernel(x_ref, scp_ref, sh_ref, ng_ref, w_ref, ang_ref, qg_ref, kg_ref, cos_ref, sin_ref,
                      u_ref, vn_ref, qt_ref, k_ref, vt_ref, *, tm):
    h = _modnorm(x_ref[...], ng_ref[...], scp_ref[0], sh_ref[0])
    y = jnp.dot(h.astype(BF16), w_ref[...], preferred_element_type=F32)
    ua = y[:, 0:A_WIDTH]
    va = y[:, A_WIDTH:2 * A_WIDTH]
    q = y[:, 2 * A_WIDTH:2 * A_WIDTH + B_WIDTH]
    k = y[:, 2 * A_WIDTH + B_WIDTH:2 * A_WIDTH + 2 * B_WIDTH]
    v = y[:, 2 * A_WIDTH + 2 * B_WIDTH:]
    u_ref[...] = jax.nn.gelu(ua)
    gv = jax.nn.gelu(va)
    vn = gv * lax.rsqrt(jnp.mean(gv * gv, axis=-1, keepdims=True) + EPS) * ang_ref[...]
    vn_ref[...] = vn.astype(BF16)
    cos_t = cos_ref[...]
    sin_t = sin_ref[...]
    q_t = q.T
    k_t = k.T
    zeros = jnp.zeros((HEAD_DIM, tm), F32)
    k_parts = []
    for j in range(2 * B_HEADS):
        sl = slice(j * HEAD_DIM, (j + 1) * HEAD_DIM)
        qh = _head_norm_rope_t(q_t[sl], qg_ref[...], cos_t, sin_t) * QUERY_SCALE
        blk = jnp.concatenate([qh, zeros] if j % 2 == 0 else [zeros, qh], axis=0)
        qt_ref[j] = blk.astype(BF16)
        k_parts.append(_head_norm_rope_t(k_t[sl], kg_ref[...], cos_t, sin_t))
    k_ref[...] = jnp.concatenate(k_parts, axis=0).T.astype(BF16)
    v_t = v.T.astype(BF16)
    for c in range(tm // KV_BLOCK):
        vt_ref[c] = v_t[:, c * KV_BLOCK:(c + 1) * KV_BLOCK]


def _proj_odd_kernel(x_ref, scp_ref, sh_ref, ng_ref, w_ref, qg_ref, kg_ref, cos_ref, sin_ref,
                     qt_ref, k_ref, vt_ref, gg_ref, xr_ref, *, tm):
    h = _modnorm(x_ref[...], ng_ref[...], scp_ref[0], sh_ref[0])
    y = jnp.dot(h.astype(BF16), w_ref[...], preferred_element_type=F32)
    q = y[:, 0:C_WIDTH]
    k = y[:, C_WIDTH:C_WIDTH + C_KV_WIDTH]
    v = y[:, C_WIDTH + C_KV_WIDTH:C_WIDTH + 2 * C_KV_WIDTH]
    gate = y[:, C_WIDTH + 2 * C_KV_WIDTH:C_WIDTH + 2 * C_KV_WIDTH + D_WIDTH]
    xr = y[:, C_WIDTH + 2 * C_KV_WIDTH + D_WIDTH:]
    gg_ref[...] = jax.nn.gelu(gate)
    xr_ref[...] = xr
    cos_t = cos_ref[...]
    sin_t = sin_ref[...]
    q_t = q.T
    k_t = k.T
    zeros = jnp.zeros((HEAD_DIM, tm), F32)
    for j in range(C_HEADS):
        sl = slice(j * HEAD_DIM, (j + 1) * HEAD_DIM)
        qh = _head_norm_rope_t(q_t[sl], qg_ref[...], cos_t, sin_t) * QUERY_SCALE
        blk = jnp.concatenate([qh, zeros] if j // C_GROUP == 0 else [zeros, qh], axis=0)
        qt_ref[j] = blk.astype(BF16)
    k_parts = [_head_norm_rope_t(k_t[j * HEAD_DIM:(j + 1) * HEAD_DIM], kg_ref[...], cos_t, sin_t)
               for j in range(C_KV_HEADS)]
    k_ref[...] = jnp.concatenate(k_parts, axis=0).T.astype(BF16)
    v_t = v.T.astype(BF16)
    for c in range(tm // KV_BLOCK):
        vt_ref[c] = v_t[:, c * KV_BLOCK:(c + 1) * KV_BLOCK]


def _projection(even, x2, scp, sh, ng, w, gains, cos_t, sin_t, tm):
    r = x2.shape[0]
    nt = r // tm
    tiles_per_group = nt // scp.shape[0]
    pos_tiles = cos_t.shape[1] // tm
    n_in = w.shape[1]
    row = lambda i: (i, 0)
    grp = lambda i: (i // tiles_per_group, 0, 0)
    fixed = lambda i: (0, 0)
    pos = lambda i: (0, i % pos_tiles)
    in_specs = [
        pl.BlockSpec((tm, D_MODEL), row),
        pl.BlockSpec((1, 1, D_MODEL), grp),
        pl.BlockSpec((1, 1, D_MODEL), grp),
        pl.BlockSpec((1, D_MODEL), fixed),
        pl.BlockSpec((D_MODEL, n_in), fixed),
    ]
    in_specs += [pl.BlockSpec(g.shape, fixed) for g in gains]
    in_specs += [pl.BlockSpec((HEAD_DIM, tm), pos), pl.BlockSpec((HEAD_DIM, tm), pos)]
    if even:
        kw, vw, nh = B_WIDTH, B_WIDTH, 2 * B_HEADS
        body = functools.partial(_proj_even_kernel, tm=tm)
        out_shape = [
            jax.ShapeDtypeStruct((r, A_WIDTH), F32),
            jax.ShapeDtypeStruct((r, A_WIDTH), BF16),
            jax.ShapeDtypeStruct((nh, LANES, r), BF16),
            jax.ShapeDtypeStruct((r, kw), BF16),
            jax.ShapeDtypeStruct((r // KV_BLOCK, vw, KV_BLOCK), BF16),
        ]
        out_specs = [
            pl.BlockSpec((tm, A_WIDTH), row),
            pl.BlockSpec((tm, A_WIDTH), row),
            pl.BlockSpec((nh, LANES, tm), lambda i: (0, 0, i)),
            pl.BlockSpec((tm, kw), row),
            pl.BlockSpec((tm // KV_BLOCK, vw, KV_BLOCK), lambda i: (i, 0, 0)),
        ]
    else:
        kw, vw, nh = C_KV_WIDTH, C_KV_WIDTH, C_HEADS
        body = functools.partial(_proj_odd_kernel, tm=tm)
        out_shape = [
            jax.ShapeDtypeStruct((nh, LANES, r), BF16),
            jax.ShapeDtypeStruct((r, kw), BF16),
            jax.ShapeDtypeStruct((r // KV_BLOCK, vw, KV_BLOCK), BF16),
            jax.ShapeDtypeStruct((r, D_WIDTH), F32),
            jax.ShapeDtypeStruct((r, D_WIDTH), F32),
        ]
        out_specs = [
            pl.BlockSpec((nh, LANES, tm), lambda i: (0, 0, i)),
            pl.BlockSpec((tm, kw), row),
            pl.BlockSpec((tm // KV_BLOCK, vw, KV_BLOCK), lambda i: (i, 0, 0)),
            pl.BlockSpec((tm, D_WIDTH), row),
            pl.BlockSpec((tm, D_WIDTH), row),
        ]
    return pl.pallas_call(
        body, grid=(nt,), in_specs=in_specs, out_specs=out_specs, out_shape=out_shape,
        compiler_params=_cparams(("arbitrary",)),
        name="proj_even" if even else "proj_odd",
    )(x2, scp, sh, ng, w, *gains, cos_t, sin_t)


def _attn_kernel(*refs, mode, n_lat, tk, lam_init):
    qt_ref, kc_ref, vtc_ref = refs[:3]
    pos = 3
    if n_lat:
        kl_ref, vtl_ref = refs[3:5]
        pos = 5
    if mode == "diff":
        lamp_ref, sg_ref = refs[pos:pos + 2]
        pos += 2
    o_ref, acc_ref, s_ref, p_ref = refs[pos:pos + 4]
    tq = qt_ref.shape[-1]
    q = (qt_ref[0], qt_ref[1])
    acc_ref[...] = jnp.zeros(acc_ref.shape, F32)

    lc = kc_ref.shape[0]
    ones = jnp.ones((DEN_ROWS, KV_BLOCK), BF16)

    def scores(kb, slot):
        n = kb.shape[0]
        mb = []
        for c in range(2):
            s = jnp.dot(kb, q[c], preferred_element_type=F32)
            s_ref[slot, c, 0:n] = s
            mb.append(jnp.max(s, axis=0, keepdims=True))
        return tuple(mb)

    def softmax(slot, n, mb, m):
        out, alphas = [], []
        for c in range(2):
            m_new = jnp.maximum(m[c], mb[c])
            p_ref[slot, c, 0:n] = jnp.exp2(s_ref[slot, c, 0:n] - m_new).astype(BF16)
            alphas.append(jnp.exp2(m[c] - m_new))
            out.append(m_new)
        return tuple(out), tuple(alphas)

    def values(vt_blocks, slot, alphas):
        for c in range(2):
            upd = alphas[c] * acc_ref[c]
            for i, vtb in enumerate(vt_blocks):
                lhs = jnp.concatenate([vtb, ones], axis=0)
                upd = upd + jnp.dot(lhs, p_ref[slot, c, i * KV_BLOCK:(i + 1) * KV_BLOCK],
                                    preferred_element_type=F32)
            acc_ref[c] = upd

    per = tk // KV_BLOCK

    def k_lat(j):
        return kl_ref[pl.ds(pl.multiple_of(j * tk, tk), tk), :]

    def vt_lat(j):
        return [vtl_ref[j * per + i] for i in range(per)]

    vt_ctx = [vtc_ref[i] for i in range(lc // KV_BLOCK)]
    neg = jnp.full((1, tq), -jnp.inf, F32)
    m = (neg, neg)
    mb = scores(kc_ref[...], 0)
    if not n_lat:
        m, al = softmax(0, lc, mb, m)
        values(vt_ctx, 0, al)
    else:
        assert n_lat % 2 == 0
        mb_next = scores(k_lat(0), 1)
        m, al = softmax(0, lc, mb, m)
        mb = mb_next
        mb_next = scores(k_lat(1), 0)
        m, al_next = softmax(1, tk, mb, m)
        values(vt_ctx, 0, al)
        mb, al = mb_next, al_next

        def body(i, carry):
            m, mb, al = carry
            t = 2 * i + 2
            mb_next = scores(k_lat(t), 1)
            m, al_next = softmax(0, tk, mb, m)
            values(vt_lat(t - 2), 1, al)
            mb, al = mb_next, al_next
            mb_next = scores(k_lat(t + 1), 0)
            m, al_next = softmax(1, tk, mb, m)
            values(vt_lat(t - 1), 0, al)
            return m, mb_next, al_next

        m, mb, al = lax.fori_loop(0, (n_lat - 2) // 2, body, (m, mb, al))
        m, al_next = softmax(0, tk, mb, m)
        values(vt_lat(n_lat - 2), 1, al)
        values(vt_lat(n_lat - 1), 0, al_next)
    vrows = vtc_ref.shape[1]
    l0 = acc_ref[0, vrows:vrows + 1, :]
    l1 = acc_ref[1, vrows:vrows + 1, :]
    if mode == "diff":
        lp = lamp_ref[...]
        lam = (jnp.exp(jnp.sum(lp[0:1] * lp[1:2], axis=-1, keepdims=True))
               - jnp.exp(jnp.sum(lp[2:3] * lp[3:4], axis=-1, keepdims=True)) + lam_init)
        o = acc_ref[0, 0:LANES, :] * (1.0 / l0) - lam * (acc_ref[1, 0:LANES, :] * (1.0 / l1))
        o = o * lax.rsqrt(jnp.mean(o * o, axis=0, keepdims=True) + EPS) * sg_ref[...] * (1.0 - lam_init)
    else:
        o = jnp.concatenate([acc_ref[0, 0:HEAD_DIM, :] * (1.0 / l0),
                             acc_ref[1, 0:HEAD_DIM, :] * (1.0 / l1)], axis=0)
    o_ref[...] = o.T.astype(o_ref.dtype)


def _attention(mode, qt, kc, vtc, kl, vtl, extra, batch, tq, tk, lam_init=0.0):
    units = qt.shape[0] // 2
    rq = qt.shape[2]
    nq = rq // batch // tq
    lc = kc.shape[0] // batch
    col = (lambda u: u) if mode == "diff" else (lambda u: 0)
    vrows = LANES if mode == "diff" else HEAD_DIM
    vcol = (lambda u: u) if mode == "diff" else (lambda u: u // (C_GROUP // 2))
    in_specs = [
        pl.BlockSpec((2, LANES, tq), lambda b, u, i: (u, 0, b * nq + i)),
        pl.BlockSpec((lc, LANES), lambda b, u, i: (b, col(u))),
        pl.BlockSpec((lc // KV_BLOCK, vrows, KV_BLOCK), lambda b, u, i: (b, vcol(u), 0)),
    ]
    args = [qt, kc, vtc]
    n_lat = 0
    if kl is not None:
        ll = kl.shape[0] // batch
        n_lat = ll // tk
        in_specs += [
            pl.BlockSpec((ll, LANES), lambda b, u, i: (b, col(u))),
            pl.BlockSpec((ll // KV_BLOCK, vrows, KV_BLOCK), lambda b, u, i: (b, vcol(u), 0)),
        ]
        args += [kl, vtl]
    for e in extra:
        in_specs.append(pl.BlockSpec(e.shape, lambda b, u, i: (0, 0)))
        args.append(e)
    rows = max(tk, lc)
    return pl.pallas_call(
        functools.partial(_attn_kernel, mode=mode, n_lat=n_lat, tk=tk, lam_init=lam_init),
        grid=(batch, units, nq),
        in_specs=in_specs,
        out_specs=pl.BlockSpec((tq, LANES), lambda b, u, i: (b * nq + i, u)),
        out_shape=jax.ShapeDtypeStruct((rq, units * LANES), BF16),
        scratch_shapes=[pltpu.VMEM((2, vrows + DEN_ROWS, tq), F32),
                        pltpu.VMEM((2, 2, rows, tq), F32),
                        pltpu.VMEM((2, 2, rows, tq), BF16)],
        compiler_params=_cparams(("arbitrary", "arbitrary", "arbitrary")),
        name="attn_" + mode,
    )(*args)


def _gmlp_kernel(u_ref, vn_ref, ws_ref, bias_ref, o_ref, *, tm):
    lane = lax.broadcasted_iota(jnp.int32, (CHUNK, LANES), 1)
    for c in range(tm // CHUNK):
        rows = slice(c * CHUNK, (c + 1) * CHUNK)
        for j in range(A_GROUPS // 2):
            cols = slice(j * LANES, (j + 1) * LANES)
            rhs = vn_ref[rows, cols]
            lo = jnp.dot(ws_ref[2 * j], rhs, preferred_element_type=F32)
            hi = jnp.dot(ws_ref[2 * j + 1], rhs, preferred_element_type=F32)
            mixed = jnp.where(lane < HEAD_DIM, lo, hi) + bias_ref[:, cols]
            o_ref[rows, cols] = (u_ref[rows, cols] * mixed).astype(o_ref.dtype)


def _gmlp(u, vn, ws, bias_map, tm):
    r = u.shape[0]
    row = lambda i: (i, 0)
    return pl.pallas_call(
        functools.partial(_gmlp_kernel, tm=tm),
        grid=(r // tm,),
        in_specs=[
            pl.BlockSpec((tm, A_WIDTH), row),
            pl.BlockSpec((tm, A_WIDTH), row),
            pl.BlockSpec(ws.shape, lambda i: (0, 0, 0)),
            pl.BlockSpec(bias_map.shape, lambda i: (0, 0)),
        ],
        out_specs=pl.BlockSpec((tm, A_WIDTH), row),
        out_shape=jax.ShapeDtypeStruct((r, A_WIDTH), BF16),
        compiler_params=_cparams(("arbitrary",)),
        name="gmlp",
    )(u, vn, ws, bias_map)


def _out_kernel(x_ref, ya_ref, yb_ref, w_ref, g_ref, o_ref):
    half = ya_ref.shape[1]
    y = (jnp.dot(ya_ref[...], w_ref[0:half, :], preferred_element_type=F32)
         + jnp.dot(yb_ref[...], w_ref[half:, :], preferred_element_type=F32))
    o_ref[...] = x_ref[...] + g_ref[0] * y


def _out_proj(x2, ya, yb, w, g, tm):
    r = x2.shape[0]
    tiles_per_group = (r // tm) // g.shape[0]
    row = lambda i: (i, 0)
    return pl.pallas_call(
        _out_kernel,
        grid=(r // tm,),
        in_specs=[
            pl.BlockSpec((tm, D_MODEL), row),
            pl.BlockSpec((tm, ya.shape[1]), row),
            pl.BlockSpec((tm, yb.shape[1]), row),
            pl.BlockSpec(w.shape, lambda i: (0, 0)),
            pl.BlockSpec((1, 1, D_MODEL), lambda i: (i // tiles_per_group, 0, 0)),
        ],
        out_specs=pl.BlockSpec((tm, D_MODEL), row),
        out_shape=jax.ShapeDtypeStruct((r, D_MODEL), F32),
        compiler_params=_cparams(("arbitrary",)),
        name="out_proj",
    )(x2, ya, yb, w, g)


def _swiglu(h, w1, w3, w2):
    a = jnp.dot(h, w1, preferred_element_type=F32)
    b = jnp.dot(h, w3, preferred_element_type=F32)
    t = (a * jax.nn.sigmoid(a)) * b
    return jnp.dot(t.astype(BF16), w2, preferred_element_type=F32)


def _ffn_kernel(x_ref, scp_ref, sh_ref, g_ref, ng_ref, w1_ref, w3_ref, w2_ref, o_ref):
    x = x_ref[...]
    h = _modnorm(x, ng_ref[...], scp_ref[0], sh_ref[0]).astype(BF16)
    o_ref[...] = x + g_ref[0] * _swiglu(h, w1_ref[...], w3_ref[...], w2_ref[...])


def _ffn(x2, scp, sh, g, ng, w1, w3, w2, tm):
    r = x2.shape[0]
    tiles_per_group = (r // tm) // g.shape[0]
    row = lambda i: (i, 0)
    grp = lambda i: (i // tiles_per_group, 0, 0)
    fixed = lambda i: (0, 0)
    resident = lambda w: pl.BlockSpec(w.shape, fixed, pipeline_mode=pl.Buffered(1))
    return pl.pallas_call(
        _ffn_kernel,
        grid=(r // tm,),
        in_specs=[
            pl.BlockSpec((tm, D_MODEL), row),
            pl.BlockSpec((1, 1, D_MODEL), grp),
            pl.BlockSpec((1, 1, D_MODEL), grp),
            pl.BlockSpec((1, 1, D_MODEL), grp),
            pl.BlockSpec((1, D_MODEL), fixed),
            resident(w1), resident(w3), resident(w2),
        ],
        out_specs=pl.BlockSpec((tm, D_MODEL), row),
        out_shape=jax.ShapeDtypeStruct((r, D_MODEL), F32),
        compiler_params=_cparams(("arbitrary",)),
        name="ffn",
    )(x2, scp, sh, g, ng, w1, w3, w2)


META_E1, META_E2, META_G1, META_G2, META_R1, META_R2 = range(6)


def _lane_col(x, lane, k):
    return jnp.sum(jnp.where(lane == k, x, 0.0), axis=-1, keepdims=True)


def _router_kernel(x_ref, scp_ref, sh_ref, ng_ref, whi_ref, wlo_ref, b_ref, hn_ref, meta_ref, cnt_ref, carry_ref):
    @pl.when(pl.program_id(0) == 0)
    def _():
        carry_ref[...] = jnp.zeros(carry_ref.shape, F32)

    h = _modnorm(x_ref[...], ng_ref[...], scp_ref[0], sh_ref[0])
    hn_ref[...] = h
    h_hi = h.astype(BF16)
    h_lo = (h - h_hi.astype(F32)).astype(BF16)
    logits = (jnp.dot(h_hi, whi_ref[...], preferred_element_type=F32)
              + jnp.dot(h_lo, whi_ref[...], preferred_element_type=F32)
              + jnp.dot(h_hi, wlo_ref[...], preferred_element_type=F32)) + b_ref[...]
    tm = logits.shape[0]
    lane = lax.broadcasted_iota(jnp.int32, logits.shape, 1).astype(F32)
    logits = jnp.where(lane < N_EXPERTS, logits, -jnp.inf)
    m1 = jnp.max(logits, axis=-1, keepdims=True)
    i1 = jnp.min(jnp.where(logits == m1, lane, float(LANES)), axis=-1, keepdims=True)
    rest = jnp.where(lane == i1, -jnp.inf, logits)
    m2 = jnp.max(rest, axis=-1, keepdims=True)
    i2 = jnp.min(jnp.where(rest == m2, lane, float(LANES)), axis=-1, keepdims=True)
    e2 = jnp.exp(m2 - m1)
    den = 1.0 + e2
    oh1 = lane == i1
    oh2 = lane == i2
    both = jnp.where(jnp.logical_or(oh1, oh2), 1.0, 0.0)
    tri = (lax.broadcasted_iota(jnp.int32, (tm, tm), 1) < lax.broadcasted_iota(jnp.int32, (tm, tm), 0))
    before = jnp.dot(jnp.where(tri, 1.0, 0.0).astype(BF16), both.astype(BF16),
                     preferred_element_type=F32) + carry_ref[0:1, :]
    r1 = jnp.sum(jnp.where(oh1, before, 0.0), axis=-1, keepdims=True)
    r2 = jnp.sum(jnp.where(oh2, before, 0.0), axis=-1, keepdims=True)
    meta = jnp.zeros(logits.shape, F32)
    for k, v in ((META_E1, i1), (META_E2, i2), (META_G1, 1.0 / den), (META_G2, e2 / den),
                 (META_R1, r1), (META_R2, r2)):
        meta = jnp.where(lane == k, v, meta)
    meta_ref[...] = meta
    carry_ref[...] = carry_ref[...] + jnp.sum(both, axis=0, keepdims=True)
    cnt_ref[...] = carry_ref[...]


def _router(x2, scp, sh, ng, w_hi, w_lo, b, tm):
    r = x2.shape[0]
    tiles_per_group = (r // tm) // scp.shape[0]
    row = lambda i: (i, 0)
    grp = lambda i: (i // tiles_per_group, 0, 0)
    fixed = lambda i: (0, 0)
    return pl.pallas_call(
        _router_kernel,
        grid=(r // tm,),
        in_specs=[
            pl.BlockSpec((tm, D_MODEL), row),
            pl.BlockSpec((1, 1, D_MODEL), grp),
            pl.BlockSpec((1, 1, D_MODEL), grp),
            pl.BlockSpec((1, D_MODEL), fixed),
            pl.BlockSpec(w_hi.shape, fixed),
            pl.BlockSpec(w_lo.shape, fixed),
            pl.BlockSpec(b.shape, fixed),
        ],
        out_specs=[
            pl.BlockSpec((tm, D_MODEL), row),
            pl.BlockSpec((tm, LANES), row),
            pl.BlockSpec((SUBLANES, LANES), fixed),
        ],
        out_shape=[
            jax.ShapeDtypeStruct((r, D_MODEL), F32),
            jax.ShapeDtypeStruct((r, LANES), F32),
            jax.ShapeDtypeStruct((SUBLANES, LANES), F32),
        ],
        scratch_shapes=[pltpu.VMEM((SUBLANES, LANES), F32)],
        compiler_params=_cparams(("arbitrary",)),
        name="router",
    )(x2, scp, sh, ng, w_hi, w_lo, b)


def _row_copy(src_ref, src_row, dst_ref, dst_row, sem):
    return pltpu.make_async_copy(src_ref.at[pl.ds(src_row, 1)], dst_ref.at[pl.ds(dst_row, 1)], sem)


def _dispatch_kernel(pos_ref, h_ref, init_ref, xs_ref, sem):
    del init_ref
    tm = h_ref.shape[0]
    base = pl.program_id(0) * (2 * tm)

    def issue(r, _):
        for k in range(2):
            _row_copy(h_ref, r, xs_ref, pos_ref[base + 2 * r + k], sem).start()
        return 0

    lax.fori_loop(0, tm, issue, 0, unroll=8)
    for k in range(2):
        pltpu.make_async_copy(h_ref, xs_ref.at[pl.ds(0, tm)], sem).wait()


def _dispatch(pos_flat, hn, n_rows, tm):
    r = hn.shape[0]
    return pl.pallas_call(
        _dispatch_kernel,
        grid_spec=pltpu.PrefetchScalarGridSpec(
            num_scalar_prefetch=1,
            grid=(r // tm,),
            in_specs=[pl.BlockSpec((tm, D_MODEL), lambda i, pos: (i, 0)),
                      pl.BlockSpec(memory_space=pl.ANY)],
            out_specs=pl.BlockSpec(memory_space=pl.ANY),
            scratch_shapes=[pltpu.SemaphoreType.DMA(())],
        ),
        out_shape=jax.ShapeDtypeStruct((n_rows, D_MODEL), F32),
        input_output_aliases={2: 0},
        compiler_params=_cparams(("arbitrary",)),
        name="dispatch",
    )(pos_flat, hn, jnp.zeros((n_rows, D_MODEL), F32))


def _expert_kernel(te_ref, nu_ref, xs_ref, w1_ref, w3_ref, w2_ref, o_ref):
    del te_ref

    @pl.when(pl.program_id(0) < nu_ref[0])
    def _():
        o_ref[...] = _swiglu(xs_ref[...].astype(BF16), w1_ref[0], w3_ref[0], w2_ref[0])

    @pl.when(pl.program_id(0) >= nu_ref[0])
    def _():
        o_ref[...] = jnp.zeros(o_ref.shape, F32)


def _experts(tile_expert, n_used, xs, w1, w3, w2, tm):
    n_tiles = xs.shape[0] // tm
    wspec = lambda w: pl.BlockSpec((1,) + w.shape[1:], lambda i, te, nu: (te[i], 0, 0))
    return pl.pallas_call(
        _expert_kernel,
        grid_spec=pltpu.PrefetchScalarGridSpec(
            num_scalar_prefetch=2,
            grid=(n_tiles,),
            in_specs=[
                pl.BlockSpec((tm, D_MODEL), lambda i, te, nu: (jnp.minimum(i, nu[0] - 1), 0)),
                wspec(w1), wspec(w3), wspec(w2),
            ],
            out_specs=pl.BlockSpec((tm, D_MODEL), lambda i, te, nu: (i, 0)),
        ),
        out_shape=jax.ShapeDtypeStruct(xs.shape, F32),
        compiler_params=_cparams(("arbitrary",)),
        name="experts",
    )(tile_expert, n_used, xs, w1, w3, w2)


def _combine_kernel(pos_ref, x_ref, meta_ref, g_ref, ys_ref, o_ref, buf_ref, sem):
    tm = x_ref.shape[0]
    i = pl.program_id(0)

    def gather(tile, slot):
        base = tile * (2 * tm)

        def issue(r, _):
            for k in range(2):
                _row_copy(ys_ref, pos_ref[base + 2 * r + k], buf_ref.at[slot, k], r, sem.at[slot]).start()
            return 0

        lax.fori_loop(0, tm, issue, 0, unroll=8)

    @pl.when(i == 0)
    def _():
        gather(0, 0)

    slot = i % 2

    @pl.when(i + 1 < pl.num_programs(0))
    def _():
        gather(i + 1, 1 - slot)

    for k in range(2):
        pltpu.make_async_copy(ys_ref.at[pl.ds(0, tm)], buf_ref.at[slot, k], sem.at[slot]).wait()
    meta = meta_ref[...]
    lane = lax.broadcasted_iota(jnp.int32, meta.shape, 1)
    y = _lane_col(meta, lane, META_G1) * buf_ref[slot, 0] + _lane_col(meta, lane, META_G2) * buf_ref[slot, 1]
    o_ref[...] = x_ref[...] + g_ref[0] * y


def _combine(pos_flat, x2, meta, g, ys, tm):
    r = x2.shape[0]
    tiles_per_group = (r // tm) // g.shape[0]
    return pl.pallas_call(
        _combine_kernel,
        grid_spec=pltpu.PrefetchScalarGridSpec(
            num_scalar_prefetch=1,
            grid=(r // tm,),
            in_specs=[
                pl.BlockSpec((tm, D_MODEL), lambda i, pos: (i, 0)),
                pl.BlockSpec((tm, LANES), lambda i, pos: (i, 0)),
                pl.BlockSpec((1, 1, D_MODEL), lambda i, pos: (i // tiles_per_group, 0, 0)),
                pl.BlockSpec(memory_space=pl.ANY),
            ],
            out_specs=pl.BlockSpec((tm, D_MODEL), lambda i, pos: (i, 0)),
            scratch_shapes=[pltpu.VMEM((2, 2, tm, D_MODEL), F32), pltpu.SemaphoreType.DMA((2,))],
        ),
        out_shape=jax.ShapeDtypeStruct((r, D_MODEL), F32),
        compiler_params=_cparams(("arbitrary",)),
        name="combine",
    )(pos_flat, x2, meta, g, ys)


def _routing_plan(meta, counts, tile):
    n = meta.shape[0]
    n_tiles = 2 * n // tile + N_EXPERTS
    e = meta[:, META_E1:META_E2 + 1].astype(jnp.int32)
    rank = meta[:, META_R1:META_R2 + 1].astype(jnp.int32)
    cnt = counts[0, :N_EXPERTS].astype(jnp.int32)
    tiles_e = (cnt + tile - 1) // tile
    ends = jnp.cumsum(tiles_e)
    starts = (ends - tiles_e) * tile
    pos = (starts[e] + rank).reshape(-1)
    n_used = ends[-1:]
    t_idx = jnp.minimum(jnp.arange(n_tiles, dtype=jnp.int32), n_used[0] - 1)
    tile_expert = jnp.sum((t_idx[:, None] >= ends[None, :]).astype(jnp.int32), axis=1)
    return pos, tile_expert.astype(jnp.int32), n_used.astype(jnp.int32), n_tiles * tile


LRU_TILE = 512
LRU_CHUNKS = D_WIDTH // LANES
LRU_PAD = SUBLANES


def _lru_pitch(tile):
    seg = tile // SUBLANES
    assert (seg // SUBLANES) % 2 == 0
    return seg, seg + LRU_PAD


def _tile_scans(af_ref, bf_ref, ab_ref, bb_ref, seg, pitch, carry_f):
    nc = LRU_CHUNKS

    def rows_f(t):
        return pl.ds(t, SUBLANES, stride=pitch)

    def rows_b(t):
        return pl.ds(seg - 1 - t, SUBLANES, stride=pitch)

    def ends(t, c):
        hf, pf, hb, pb = c
        rf, rb = rows_f(t), rows_b(t)
        nhf, npf, nhb, npb = [], [], [], []
        for cc in range(nc):
            a = af_ref[cc, rf, :]
            nhf.append(a * hf[cc] + bf_ref[cc, rf, :])
            npf.append(a * pf[cc])
            a = ab_ref[cc, rb, :]
            nhb.append(a * hb[cc] + bb_ref[cc, rb, :])
            npb.append(a * pb[cc])
        return tuple(nhf), tuple(npf), tuple(nhb), tuple(npb)

    z = tuple(jnp.zeros((SUBLANES, LANES), F32) for _ in range(nc))
    o = tuple(jnp.ones((SUBLANES, LANES), F32) for _ in range(nc))
    hf, pf, hb, pb = lax.fori_loop(0, seg, ends, (z, o, z, o), unroll=8)

    in_f, in_b, in_p, out_f, tile_p, tile_h = [], [], [], [], [], []
    for cc in range(nc):
        cin = carry_f[cc]
        rows = [None] * SUBLANES
        for s in range(SUBLANES):
            rows[s] = cin
            cin = hf[cc][s:s + 1] + pf[cc][s:s + 1] * cin
        in_f.append(jnp.concatenate(rows, axis=0))
        out_f.append(cin)
        cin = jnp.zeros((1, LANES), F32)
        pin = jnp.ones((1, LANES), F32)
        rows_h = [None] * SUBLANES
        rows_p = [None] * SUBLANES
        for s in range(SUBLANES - 1, -1, -1):
            rows_h[s] = cin
            rows_p[s] = pin
            cin = hb[cc][s:s + 1] + pb[cc][s:s + 1] * cin
            pin = pb[cc][s:s + 1] * pin
        in_b.append(jnp.concatenate(rows_h, axis=0))
        in_p.append(jnp.concatenate(rows_p, axis=0))
        tile_h.append(cin)
        tile_p.append(pin)

    def final(t, c):
        hf, hb, pb = c
        rf, rb = rows_f(t), rows_b(t)
        nhf, nhb, npb = [], [], []
        for cc in range(nc):
            h = af_ref[cc, rf, :] * hf[cc] + bf_ref[cc, rf, :]
            bf_ref[cc, rf, :] = h
            nhf.append(h)
            a = ab_ref[cc, rb, :]
            h = a * hb[cc] + bb_ref[cc, rb, :]
            p = a * pb[cc]
            bb_ref[cc, rb, :] = h
            ab_ref[cc, rb, :] = p
            nhb.append(h)
            npb.append(p)
        return tuple(nhf), tuple(nhb), tuple(npb)

    lax.fori_loop(0, seg, final, (tuple(in_f), tuple(in_b), tuple(in_p)), unroll=8)
    return out_f, tile_p, tile_h


def _lru_main_kernel(prev_ref, cur_ref, next_ref, gg_ref, cw_ref, cb_ref, w_ref, bias_ref, lam_ref, h0_ref,
                     s1_ref, s2_ref, agg_ref, hl_ref, ext_ref, af_ref, bf_ref, ab_ref, bb_ref, st_ref):
    i = pl.program_id(1)
    nt = pl.num_programs(1)
    t = cur_ref.shape[0]
    seg, pitch = _lru_pitch(t)
    pad = SUBLANES

    @pl.when(i == 0)
    def _():
        st_ref[...] = h0_ref[0]

    ext_ref[0:pad, :] = jnp.where(i == 0, 0.0, prev_ref[...])
    ext_ref[pad:pad + t, :] = cur_ref[...]
    ext_ref[pad + t:, :] = jnp.where(i == nt - 1, 0.0, next_ref[...])
    left = CONV_W // 2
    xd = cb_ref[...] + sum(cw_ref[j:j + 1, :] * ext_ref[pad + j - left:pad + j - left + t, :] for j in range(CONV_W))
    z = jnp.dot(xd.astype(BF16), w_ref[...], preferred_element_type=F32) + bias_ref[...]
    decay = LRU_C * jax.nn.softplus(-lam_ref[...])
    for d, (a_ref, b_ref) in enumerate(((af_ref, bf_ref), (ab_ref, bb_ref))):
        r = 0.5 * jnp.tanh(0.5 * z[:, (2 * d) * D_WIDTH:(2 * d + 1) * D_WIDTH]) + 0.5
        g = 0.5 * jnp.tanh(0.5 * z[:, (2 * d + 1) * D_WIDTH:(2 * d + 2) * D_WIDTH]) + 0.5
        nla = r * decay[:, d * D_WIDTH:(d + 1) * D_WIDTH]
        a = jnp.exp(-nla)
        m = jnp.tanh(nla) * (a * a + 1.0)
        b = jnp.where(m > 0.0, m * lax.rsqrt(m), 0.0) * (g * xd)
        for cc in range(LRU_CHUNKS):
            for s in range(SUBLANES):
                a_ref[cc, s * pitch:s * pitch + seg, :] = a[s * seg:(s + 1) * seg, cc * LANES:(cc + 1) * LANES]
                b_ref[cc, s * pitch:s * pitch + seg, :] = b[s * seg:(s + 1) * seg, cc * LANES:(cc + 1) * LANES]
    carry_f = [st_ref[0:1, cc * LANES:(cc + 1) * LANES] for cc in range(LRU_CHUNKS)]
    out_f, tile_p, tile_h = _tile_scans(af_ref, bf_ref, ab_ref, bb_ref, seg, pitch, carry_f)
    for cc in range(LRU_CHUNKS):
        cols = slice(cc * LANES, (cc + 1) * LANES)
        for s in range(SUBLANES):
            src = slice(s * pitch, s * pitch + seg)
            dst = slice(s * seg, (s + 1) * seg)
            gg = gg_ref[dst, cols]
            s1_ref[dst, cols] = (bf_ref[cc, src, :] + bb_ref[cc, src, :]) * gg
            s2_ref[dst, cols] = ab_ref[cc, src, :] * gg
    state = jnp.concatenate(out_f, axis=-1)
    st_ref[...] = jnp.broadcast_to(state, st_ref.shape)
    hl_ref[0] = jnp.broadcast_to(state, st_ref.shape)
    agg_ref[0] = jnp.concatenate([jnp.concatenate(tile_p, axis=-1), jnp.concatenate(tile_h, axis=-1),
                                  jnp.zeros((SUBLANES - 2, D_WIDTH), F32)], axis=0)


def _lru_fix_kernel(s1_ref, s2_ref, agg_ref, h0_ref, y_ref, st_ref):
    @pl.when(pl.program_id(1) == 0)
    def _():
        st_ref[...] = h0_ref[0]

    carry = st_ref[0:1, :]
    y_ref[...] = (s1_ref[...] + s2_ref[...] * carry).astype(y_ref.dtype)
    st_ref[...] = jnp.broadcast_to(agg_ref[0, 1:2, :] + agg_ref[0, 0:1, :] * carry, st_ref.shape)


def _lru_main(xr, gg, cw, cb, w_cat, b_cat, lam_cat, h0, batch, tile):
    r = xr.shape[0]
    nt = r // batch // tile
    sub = tile // SUBLANES
    _, pitch = _lru_pitch(tile)
    row = lambda b, i: (b * nt + i, 0)
    fixed = lambda b, i: (0, 0)
    state = pl.BlockSpec((1, SUBLANES, D_WIDTH), lambda b, i: (b, 0, 0))
    chunked = pltpu.VMEM((LRU_CHUNKS, SUBLANES * pitch, LANES), F32)
    return pl.pallas_call(
        _lru_main_kernel,
        grid=(batch, nt),
        in_specs=[
            pl.BlockSpec((SUBLANES, D_WIDTH), lambda b, i: ((b * nt + jnp.maximum(i, 1)) * sub - 1, 0)),
            pl.BlockSpec((tile, D_WIDTH), row),
            pl.BlockSpec((SUBLANES, D_WIDTH), lambda b, i: ((b * nt + jnp.minimum(i + 1, nt - 1)) * sub, 0)),
            pl.BlockSpec((tile, D_WIDTH), row),
            pl.BlockSpec(cw.shape, fixed),
            pl.BlockSpec(cb.shape, fixed),
            pl.BlockSpec(w_cat.shape, fixed),
            pl.BlockSpec(b_cat.shape, fixed),
            pl.BlockSpec(lam_cat.shape, fixed),
            state,
        ],
        out_specs=[
            pl.BlockSpec((tile, D_WIDTH), row),
            pl.BlockSpec((tile, D_WIDTH), row),
            pl.BlockSpec((1, SUBLANES, D_WIDTH), lambda b, i: (b * nt + i, 0, 0)),
            state,
        ],
        out_shape=[
            jax.ShapeDtypeStruct((r, D_WIDTH), F32),
            jax.ShapeDtypeStruct((r, D_WIDTH), F32),
            jax.ShapeDtypeStruct((batch * nt, SUBLANES, D_WIDTH), F32),
            jax.ShapeDtypeStruct((batch, SUBLANES, D_WIDTH), F32),
        ],
        scratch_shapes=[pltpu.VMEM((tile + 2 * SUBLANES, D_WIDTH), F32), chunked, chunked, chunked, chunked,
                        pltpu.VMEM((SUBLANES, D_WIDTH), F32)],
        compiler_params=_cparams(("arbitrary", "arbitrary")),
        name="lru_main",
    )(xr, xr, xr, gg, cw, cb, w_cat, b_cat, lam_cat, h0)


def _lru_fix(s1, s2, agg, h0, batch, tile):
    r = s1.shape[0]
    nt = r // batch // tile
    row = lambda b, i: (b * nt + (nt - 1 - i), 0)
    return pl.pallas_call(
        _lru_fix_kernel,
        grid=(batch, nt),
        in_specs=[
            pl.BlockSpec((tile, D_WIDTH), row),
            pl.BlockSpec((tile, D_WIDTH), row),
            pl.BlockSpec((1, SUBLANES, D_WIDTH), lambda b, i: (b * nt + (nt - 1 - i), 0, 0)),
            pl.BlockSpec((1, SUBLANES, D_WIDTH), lambda b, i: (b, 0, 0)),
        ],
        out_specs=pl.BlockSpec((tile, D_WIDTH), row),
        out_shape=jax.ShapeDtypeStruct((r, D_WIDTH), BF16),
        scratch_shapes=[pltpu.VMEM((SUBLANES, D_WIDTH), F32)],
        compiler_params=_cparams(("arbitrary", "arbitrary")),
        name="lru_fix",
    )(s1, s2, agg, h0)


def _rope_tables(n_lat):
    half = HEAD_DIM // 4
    freqs = ROPE_THETA ** (-jnp.arange(half, dtype=F32) / half)
    t = jnp.arange(n_lat, dtype=jnp.int32)
    row = (t // GRID_W).astype(F32)
    col = (t % GRID_W).astype(F32)
    ar = freqs[:, None] * row[None, :]
    ac = freqs[:, None] * col[None, :]
    cos_t = jnp.concatenate([jnp.cos(ar), jnp.cos(ar), jnp.cos(ac), jnp.cos(ac)], axis=0)
    sin_t = jnp.concatenate([-jnp.sin(ar), jnp.sin(ar), -jnp.sin(ac), jnp.sin(ac)], axis=0)
    return cos_t, sin_t


def _block_diag(w):
    nb, bs, _ = w.shape
    eye = jnp.eye(nb, dtype=w.dtype)
    return (eye[:, None, :, None] * w[:, :, None, :]).reshape(nb * bs, nb * bs)


def kernel(x, c, ctx, c_ctx, w_mod, b_mod, norm_mix_g, norm_ffn_g, w_out, e_w_in, a_norm_g, a_ws, a_bs, b_qnorm_g, b_knorm_g, b_lq1, b_lk1, b_lq2, b_lk2, b_subln_g, ffn_w1, ffn_w3, ffn_w2, o_w_in, c_qnorm_g, c_knorm_g, d_conv_w, d_conv_b, d_wa, d_ba, d_wx, d_bx, d_lambda, router_w, router_b, moe_w1, moe_w3, moe_w2):
    batch, n_lat, d = x.shape
    n_ctx = ctx.shape[1]
    depth = w_mod.shape[0]
    assert depth == 2 and d == D_MODEL and batch + 1 <= SUBLANES
    xl = x.reshape(batch * n_lat, d)
    xc = ctx.reshape(batch * n_ctx, d)

    cc = jnp.concatenate([c, c_ctx[None, :], jnp.zeros((SUBLANES - batch - 1, d), F32)], axis=0)
    mod = _modulation(cc, w_mod, b_mod)

    def mods(layer):
        parts = [mod[layer, :, k * d:(k + 1) * d] for k in range(6)]
        lat = [p[:batch, None, :] for p in parts]
        cx = [p[batch:batch + 1, None, :] for p in parts]
        return lat, cx

    cos_t, sin_t = _rope_tables(n_lat)
    cos_c = jnp.ones((HEAD_DIM, n_ctx), F32)
    sin_c = jnp.zeros((HEAD_DIM, n_ctx), F32)
    col = lambda v: v.reshape(-1, 1)
    rowv = lambda v: v.reshape(1, -1)
    tm = 512
    tm_c = 256

    (sh1, sc1, g1, sh2, sc2, g2), (sh1c, sc1c, g1c, sh2c, sc2c, g2c) = mods(0)
    lam_init = 0.8 - 0.6 * math.exp(-0.3 * 0)
    w_in = e_w_in[0].astype(BF16)
    gains = [rowv(a_norm_g[0]), col(b_qnorm_g[0]), col(b_knorm_g[0])]
    ng = rowv(norm_mix_g[0])
    u, vn, qt, k, vt = _projection(True, xl, 1.0 + sc1, sh1, ng, w_in, gains, cos_t, sin_t, tm)
    uc, vnc, qtc, kc, vtc = _projection(True, xc, 1.0 + sc1c, sh1c, ng, w_in, gains, cos_c, sin_c, tm_c)
    lamp = jnp.stack([b_lq1[0], b_lk1[0], b_lq2[0], b_lk2[0]], axis=0)
    extra = [lamp, col(b_subln_g[0])]
    yb = _attention("diff", qt, kc, vtc, k, vt, extra, batch, ATTN_TQ, ATTN_TK, lam_init)
    ybc = _attention("diff", qtc, kc, vtc, None, None, extra, batch, n_ctx, ATTN_TK, lam_init)
    ws = a_ws[0].astype(BF16)
    bias_map = jnp.repeat(a_bs[0].T, HEAD_DIM, axis=1)
    ya = _gmlp(u, vn, ws, bias_map, tm)
    yac = _gmlp(uc, vnc, ws, bias_map, tm_c)
    wo = w_out[0].astype(BF16)
    xl = _out_proj(xl, ya, yb, wo, g1, tm)
    xc = _out_proj(xc, yac, ybc, wo, g1c, tm_c)
    w1 = ffn_w1[0].astype(BF16)
    w3 = ffn_w3[0].astype(BF16)
    w2 = ffn_w2[0].astype(BF16)
    ngf = rowv(norm_ffn_g[0])
    xl = _ffn(xl, 1.0 + sc2, sh2, g2, ngf, w1, w3, w2, tm)
    xc = _ffn(xc, 1.0 + sc2c, sh2c, g2c, ngf, w1, w3, w2, tm_c)

    (sh1, sc1, g1, sh2, sc2, g2), (sh1c, sc1c, _, _, _, _) = mods(1)
    w_in = o_w_in[0].astype(BF16)
    gains = [col(c_qnorm_g[0]), col(c_knorm_g[0])]
    ng = rowv(norm_mix_g[1])
    qt, k, vt, gg, xr = _projection(False, xl, 1.0 + sc1, sh1, ng, w_in, gains, cos_t, sin_t, tm)
    _, kc, vtc, _, xrc = _projection(False, xc, 1.0 + sc1c, sh1c, ng, w_in, gains, cos_c, sin_c, tm_c)
    y_attn = _attention("gqa", qt, kc, vtc, k, vt, [], batch, ATTN_TQ, ATTN_TK)

    w_cat = jnp.concatenate([_block_diag(d_wa[0, 0]), _block_diag(d_wx[0, 0]),
                             _block_diag(d_wa[0, 1]), _block_diag(d_wx[0, 1])], axis=1).astype(BF16)
    b_cat = jnp.concatenate([d_ba[0, 0], d_bx[0, 0], d_ba[0, 1], d_bx[0, 1]]).reshape(1, -1)
    lam_cat = d_lambda[0].reshape(1, -1)
    cw = d_conv_w[0]
    cb = rowv(d_conv_b[0])
    zero_state = jnp.zeros((batch, SUBLANES, D_WIDTH), F32)
    _, _, agg_c, st_f = _lru_main(xrc, xrc, cw, cb, w_cat, b_cat, lam_cat, zero_state, batch, n_ctx)
    st_b = jnp.broadcast_to(agg_c[:, 1:2, :], (batch, SUBLANES, D_WIDTH))
    s1, s2, agg, _ = _lru_main(xr, gg, cw, cb, w_cat, b_cat, lam_cat, st_f, batch, LRU_TILE)
    y_rec = _lru_fix(s1, s2, agg, st_b, batch, LRU_TILE)

    xl = _out_proj(xl, y_attn, y_rec, w_out[1].astype(BF16), g1, tm)
    ngf = rowv(norm_ffn_g[1])
    wr = jnp.pad(router_w[0], ((0, 0), (0, LANES - N_EXPERTS)))
    wr_hi = wr.astype(BF16)
    wr_lo = (wr - wr_hi.astype(F32)).astype(BF16)
    br = jnp.pad(router_b[0], (0, LANES - N_EXPERTS)).reshape(1, LANES)
    hn, meta, counts = _router(xl, 1.0 + sc2, sh2, ngf, wr_hi, wr_lo, br, tm)
    pos, tile_expert, n_used, n_rows = _routing_plan(meta, counts, EXPERT_TILE)
    xs = _dispatch(pos, hn, n_rows, tm)
    ys = _experts(tile_expert, n_used, xs, moe_w1[0].astype(BF16), moe_w3[0].astype(BF16),
                  moe_w2[0].astype(BF16), EXPERT_TILE)
    xl = _combine(pos, xl, meta, g2, ys, COMBINE_TILE)
    return xl.reshape(batch, n_lat, d)
```

```python
import functools
import math

import jax
import jax.numpy as jnp
from jax import lax
from jax.experimental import pallas as pl
from jax.experimental.pallas import tpu as pltpu

F32 = jnp.float32
BF16 = jnp.bfloat16

D_MODEL = 1024
HEAD_DIM = 64
GRID_W = 64
ROPE_THETA = 10000.0
EPS = 1e-6
CHUNK = 128
A_WIDTH = D_MODEL // 2
A_GROUPS = A_WIDTH // HEAD_DIM
B_HEADS = D_MODEL // (4 * HEAD_DIM)
B_WIDTH = B_HEADS * 2 * HEAD_DIM
E_IN = 2 * A_WIDTH + 3 * B_WIDTH
C_HEADS = (D_MODEL // 2) // HEAD_DIM
C_KV_HEADS = 2
C_GROUP = C_HEADS // C_KV_HEADS
C_WIDTH = C_HEADS * HEAD_DIM
C_KV_WIDTH = C_KV_HEADS * HEAD_DIM
D_WIDTH = D_MODEL // 2
D_BLOCKS = D_WIDTH // HEAD_DIM
CONV_W = 4
LRU_C = 8.0
O_IN = C_WIDTH + 2 * C_KV_WIDTH + 2 * D_WIDTH
FFN_DIM = 2816
N_EXPERTS = 8

LANES = 128
SUBLANES = 8
KV_BLOCK = 256
ATTN_TQ = 2048
ATTN_TK = 512
EXPERT_TILE = 256
COMBINE_TILE = 256
DEN_ROWS = 16
QUERY_SCALE =HEAD_DIM ** -0.5 * math.log2(math.e)
VMEM_LIMIT = 56 * 1024 * 1024


def _cparams(sem):
    return pltpu.CompilerParams(dimension_semantics=sem, vmem_limit_bytes=VMEM_LIMIT)


def _side_cast_specs(w, n_steps, index):
    rows, cols = w.shape
    blk = (rows // n_steps, cols)
    return pl.BlockSpec(blk, index), pl.BlockSpec(blk, index), jax.ShapeDtypeStruct(w.shape, BF16)


def _modnorm(x, g, scp, sh):
    y = x * lax.rsqrt(jnp.mean(x * x, axis=-1, keepdims=True) + EPS)
    return (y * g) * scp + sh


def _mod_kernel(c_ref, w_ref, b_ref, o_ref):
    c = c_ref[...]
    s = c * jax.nn.sigmoid(c)
    o_ref[0] = jnp.dot(s.astype(BF16), w_ref[0].astype(BF16), preferred_element_type=F32) + b_ref[0]


def _modulation(cc, w_mod, b_mod):
    depth, d, n = w_mod.shape
    tn = 1536
    return pl.pallas_call(
        _mod_kernel,
        grid=(depth, n // tn),
        in_specs=[
            pl.BlockSpec((SUBLANES, d), lambda l, j: (0, 0)),
            pl.BlockSpec((1, d, tn), lambda l, j: (l, 0, j)),
            pl.BlockSpec((1, 1, tn), lambda l, j: (l, 0, j)),
        ],
        out_specs=pl.BlockSpec((1, SUBLANES, tn), lambda l, j: (l, 0, j)),
        out_shape=jax.ShapeDtypeStruct((depth, SUBLANES, n), F32),
        compiler_params=_cparams(("arbitrary", "arbitrary")),
        name="modulation",
    )(cc, w_mod, b_mod.reshape(depth, 1, n))


def _head_norm_rope_t(xh, g, cos_t, sin_t):
    xn = xh * lax.rsqrt(jnp.mean(xh * xh, axis=0, keepdims=True) + EPS) * g
    sw = jnp.concatenate([xn[16:32], xn[0:16], xn[48:64], xn[32:48]], axis=0)
    return xn * cos_t + sw * sin_t


def _proj_even_kernel(x_ref, scp_ref, sh_ref, ng_ref, w_ref, ang_ref, qg_ref, kg_ref, cos_ref, sin_ref, *rest, tm):
    u_ref, vn_ref, qt_ref, k_ref, vt_ref = rest[-5:]
    if len(rest) == 7:
        rest[1][...] = rest[0][...].astype(BF16)
    h = _modnorm(x_ref[...], ng_ref[...], scp_ref[0], sh_ref[0])
    y = jnp.dot(h.astype(BF16), w_ref[...], preferred_element_type=F32)
    ua = y[:, 0:A_WIDTH]
    va = y[:, A_WIDTH:2 * A_WIDTH]
    q = y[:, 2 * A_WIDTH:2 * A_WIDTH + B_WIDTH]
    k = y[:, 2 * A_WIDTH + B_WIDTH:2 * A_WIDTH + 2 * B_WIDTH]
    v = y[:, 2 * A_WIDTH + 2 * B_WIDTH:]
    u_ref[...] = jax.nn.gelu(ua)
    gv = jax.nn.gelu(va)
    vn = gv * lax.rsqrt(jnp.mean(gv * gv, axis=-1, keepdims=True) + EPS) * ang_ref[...]
    vn_ref[...] = vn.astype(BF16)
    cos_t = cos_ref[...]
    sin_t = sin_ref[...]
    q_t = q.T
    k_t = k.T
    zeros = jnp.zeros((HEAD_DIM, tm), F32)
    k_parts = []
    for j in range(2 * B_HEADS):
        sl = slice(j * HEAD_DIM, (j + 1) * HEAD_DIM)
        qh = _head_norm_rope_t(q_t[sl], qg_ref[...], cos_t, sin_t) * QUERY_SCALE
        blk = jnp.concatenate([qh, zeros] if j % 2 == 0 else [zeros, qh], axis=0)
        qt_ref[j] = blk.astype(BF16)
        k_parts.append(_head_norm_rope_t(k_t[sl], kg_ref[...], cos_t, sin_t))
    k_ref[...] = jnp.concatenate(k_parts, axis=0).T.astype(BF16)
    v_t = v.T.astype(BF16)
    for c in range(tm // KV_BLOCK):
        vt_ref[c] = v_t[:, c * KV_BLOCK:(c + 1) * KV_BLOCK]


def _proj_odd_kernel(x_ref, scp_ref, sh_ref, ng_ref, w_ref, qg_ref, kg_ref, cos_ref, sin_ref, *rest, tm):
    qt_ref, k_ref, vt_ref, gg_ref, xr_ref = rest[-5:]
    if len(rest) == 7:
        rest[1][...] = rest[0][...].astype(BF16)
    h = _modnorm(x_ref[...], ng_ref[...], scp_ref[0], sh_ref[0])
    y = jnp.dot(h.astype(BF16), w_ref[...], preferred_element_type=F32)
    q = y[:, 0:C_WIDTH]
    k = y[:, C_WIDTH:C_WIDTH + C_KV_WIDTH]
    v = y[:, C_WIDTH + C_KV_WIDTH:C_WIDTH + 2 * C_KV_WIDTH]
    gate = y[:, C_WIDTH + 2 * C_KV_WIDTH:C_WIDTH + 2 * C_KV_WIDTH + D_WIDTH]
    xr = y[:, C_WIDTH + 2 * C_KV_WIDTH + D_WIDTH:]
    gg_ref[...] = jax.nn.gelu(gate)
    xr_ref[...] = xr
    cos_t = cos_ref[...]
    sin_t = sin_ref[...]
    q_t = q.T
    k_t = k.T
    zeros = jnp.zeros((HEAD_DIM, tm), F32)
    for j in range(C_HEADS):
        sl = slice(j * HEAD_DIM, (j + 1) * HEAD_DIM)
        qh = _head_norm_rope_t(q_t[sl], qg_ref[...], cos_t, sin_t) * QUERY_SCALE
        blk = jnp.concatenate([qh, zeros] if j // C_GROUP == 0 else [zeros, qh], axis=0)
        qt_ref[j] = blk.astype(BF16)
    k_parts = [_head_norm_rope_t(k_t[j * HEAD_DIM:(j + 1) * HEAD_DIM], kg_ref[...], cos_t, sin_t)
               for j in range(C_KV_HEADS)]
    k_ref[...] = jnp.concatenate(k_parts, axis=0).T.astype(BF16)
    v_t = v.T.astype(BF16)
    for c in range(tm // KV_BLOCK):
        vt_ref[c] = v_t[:, c * KV_BLOCK:(c + 1) * KV_BLOCK]


def _projection(even, x2, scp, sh, ng, w, gains, cos_t, sin_t, tm, cast=None):
    r = x2.shape[0]
    nt = r // tm
    tiles_per_group = nt // scp.shape[0]
    pos_tiles = cos_t.shape[1] // tm
    n_in = w.shape[1]
    row = lambda i: (i, 0)
    grp = lambda i: (i // tiles_per_group, 0, 0)
    fixed = lambda i: (0, 0)
    pos = lambda i: (0, i % pos_tiles)
    in_specs = [
        pl.BlockSpec((tm, D_MODEL), row),
        pl.BlockSpec((1, 1, D_MODEL), grp),
        pl.BlockSpec((1, 1, D_MODEL), grp),
        pl.BlockSpec((1, D_MODEL), fixed),
        pl.BlockSpec((D_MODEL, n_in), fixed),
    ]
    in_specs += [pl.BlockSpec(g.shape, fixed) for g in gains]
    in_specs += [pl.BlockSpec((HEAD_DIM, tm), pos), pl.BlockSpec((HEAD_DIM, tm), pos)]
    if even:
        kw, vw, nh = B_WIDTH, B_WIDTH, 2 * B_HEADS
        body = functools.partial(_proj_even_kernel, tm=tm)
        out_shape = [
            jax.ShapeDtypeStruct((r, A_WIDTH), F32),
            jax.ShapeDtypeStruct((r, A_WIDTH), BF16),
            jax.ShapeDtypeStruct((nh, LANES, r), BF16),
            jax.ShapeDtypeStruct((r, kw), BF16),
            jax.ShapeDtypeStruct((r // KV_BLOCK, vw, KV_BLOCK), BF16),
        ]
        out_specs = [
            pl.BlockSpec((tm, A_WIDTH), row),
            pl.BlockSpec((tm, A_WIDTH), row),
            pl.BlockSpec((nh, LANES, tm), lambda i: (0, 0, i)),
            pl.BlockSpec((tm, kw), row),
            pl.BlockSpec((tm // KV_BLOCK, vw, KV_BLOCK), lambda i: (i, 0, 0)),
        ]
    else:
        kw, vw, nh = C_KV_WIDTH, C_KV_WIDTH, C_HEADS
        body = functools.partial(_proj_odd_kernel, tm=tm)
        out_shape = [
            jax.ShapeDtypeStruct((nh, LANES, r), BF16),
            jax.ShapeDtypeStruct((r, kw), BF16),
            jax.ShapeDtypeStruct((r // KV_BLOCK, vw, KV_BLOCK), BF16),
            jax.ShapeDtypeStruct((r, D_WIDTH), F32),
            jax.ShapeDtypeStruct((r, D_WIDTH), F32),
        ]
        out_specs = [
            pl.BlockSpec((nh, LANES, tm), lambda i: (0, 0, i)),
            pl.BlockSpec((tm, kw), row),
            pl.BlockSpec((tm // KV_BLOCK, vw, KV_BLOCK), lambda i: (i, 0, 0)),
            pl.BlockSpec((tm, D_WIDTH), row),
            pl.BlockSpec((tm, D_WIDTH), row),
        ]
    args = [x2, scp, sh, ng, w, *gains, cos_t, sin_t]
    if cast is not None:
        cin, cout, cshape = _side_cast_specs(cast, nt, row)
        in_specs.append(cin)
        out_specs.insert(0, cout)
        out_shape.insert(0, cshape)
        args.append(cast)
    return pl.pallas_call(
        body, grid=(nt,), in_specs=in_specs, out_specs=out_specs, out_shape=out_shape,
        compiler_params=_cparams(("arbitrary",)),
        name="proj_even" if even else "proj_odd",
    )(*args)


def _attn_kernel(*refs, mode, n_lat, tk, lam_init):
    qt_ref, kc_ref, vtc_ref = refs[:3]
    pos = 3
    if n_lat:
        kl_ref, vtl_ref = refs[3:5]
        pos = 5
    if mode == "diff":
        lamp_ref, sg_ref = refs[pos:pos + 2]
        pos += 2
    o_ref, acc_ref, s_ref, p_ref = refs[pos:pos + 4]
    tq = qt_ref.shape[-1]
    q = (qt_ref[0], qt_ref[1])
    acc_ref[...] = jnp.zeros(acc_ref.shape, F32)

    lc = kc_ref.shape[0]
    ones = jnp.ones((DEN_ROWS, KV_BLOCK), BF16)

    def scores(kb, slot):
        n = kb.shape[0]
        mb = []
        for c in range(2):
            s = jnp.dot(kb, q[c], preferred_element_type=F32)
            s_ref[slot, c, 0:n] = s
            mb.append(jnp.max(s, axis=0, keepdims=True))
        return tuple(mb)

    def softmax(slot, n, mb, m):
        out, alphas = [], []
        for c in range(2):
            m_new = jnp.maximum(m[c], mb[c])
            p_ref[slot, c, 0:n] = jnp.exp2(s_ref[slot, c, 0:n] - m_new).astype(BF16)
            alphas.append(jnp.exp2(m[c] - m_new))
            out.append(m_new)
        return tuple(out), tuple(alphas)

    def values(vt_blocks, slot, alphas):
        for c in range(2):
            upd = alphas[c] * acc_ref[c]
            for i, vtb in enumerate(vt_blocks):
                lhs = jnp.concatenate([vtb, ones], axis=0)
                upd = upd + jnp.dot(lhs, p_ref[slot, c, i * KV_BLOCK:(i + 1) * KV_BLOCK],
                                    preferred_element_type=F32)
            acc_ref[c] = upd

    per = tk // KV_BLOCK

    def k_lat(j):
        return kl_ref[pl.ds(pl.multiple_of(j * tk, tk), tk), :]

    def vt_lat(j):
        return [vtl_ref[j * per + i] for i in range(per)]

    vt_ctx = [vtc_ref[i] for i in range(lc // KV_BLOCK)]
    neg = jnp.full((1, tq), -jnp.inf, F32)
    m = (neg, neg)
    mb = scores(kc_ref[...], 0)
    if not n_lat:
        m, al = softmax(0, lc, mb, m)
        values(vt_ctx, 0, al)
    else:
        assert n_lat % 2 == 0
        mb_next = scores(k_lat(0), 1)
        m, al = softmax(0, lc, mb, m)
        mb = mb_next
        mb_next = scores(k_lat(1), 0)
        m, al_next = softmax(1, tk, mb, m)
        values(vt_ctx, 0, al)
        mb, al = mb_next, al_next

        def body(i, carry):
            m, mb, al = carry
            t = 2 * i + 2
            mb_next = scores(k_lat(t), 1)
            m, al_next = softmax(0, tk, mb, m)
            values(vt_lat(t - 2), 1, al)
            mb, al = mb_next, al_next
            mb_next = scores(k_lat(t + 1), 0)
            m, al_next = softmax(1, tk, mb, m)
            values(vt_lat(t - 1), 0, al)
            return m, mb_next, al_next

        m, mb, al = lax.fori_loop(0, (n_lat - 2) // 2, body, (m, mb, al))
        m, al_next = softmax(0, tk, mb, m)
        values(vt_lat(n_lat - 2), 1, al)
        values(vt_lat(n_lat - 1), 0, al_next)
    vrows = vtc_ref.shape[1]
    l0 = acc_ref[0, vrows:vrows + 1, :]
    l1 = acc_ref[1, vrows:vrows + 1, :]
    if mode == "diff":
        lp = lamp_ref[...]
        lam = (jnp.exp(jnp.sum(lp[0:1] * lp[1:2], axis=-1, keepdims=True))
               - jnp.exp(jnp.sum(lp[2:3] * lp[3:4], axis=-1, keepdims=True)) + lam_init)
        o = acc_ref[0, 0:LANES, :] * (1.0 / l0) - lam * (acc_ref[1, 0:LANES, :] * (1.0 / l1))
        o = o * lax.rsqrt(jnp.mean(o * o, axis=0, keepdims=True) + EPS) * sg_ref[...] * (1.0 - lam_init)
    else:
        o = jnp.concatenate([acc_ref[0, 0:HEAD_DIM, :] * (1.0 / l0),
                             acc_ref[1, 0:HEAD_DIM, :] * (1.0 / l1)], axis=0)
    o_ref[...] = o.T.astype(o_ref.dtype)


def _attention(mode, qt, kc, vtc, kl, vtl, extra, batch, tq, tk, lam_init=0.0):
    units = qt.shape[0] // 2
    rq = qt.shape[2]
    nq = rq // batch // tq
    lc = kc.shape[0] // batch
    col = (lambda u: u) if mode == "diff" else (lambda u: 0)
    vrows = LANES if mode == "diff" else HEAD_DIM
    vcol = (lambda u: u) if mode == "diff" else (lambda u: u // (C_GROUP // 2))
    in_specs = [
        pl.BlockSpec((2, LANES, tq), lambda b, u, i: (u, 0, b * nq + i)),
        pl.BlockSpec((lc, LANES), lambda b, u, i: (b, col(u))),
        pl.BlockSpec((lc // KV_BLOCK, vrows, KV_BLOCK), lambda b, u, i: (b, vcol(u), 0)),
    ]
    args = [qt, kc, vtc]
    n_lat = 0
    if kl is not None:
        ll = kl.shape[0] // batch
        n_lat = ll // tk
        in_specs += [
            pl.BlockSpec((ll, LANES), lambda b, u, i: (b, col(u))),
            pl.BlockSpec((ll // KV_BLOCK, vrows, KV_BLOCK), lambda b, u, i: (b, vcol(u), 0)),
        ]
        args += [kl, vtl]
    for e in extra:
        in_specs.append(pl.BlockSpec(e.shape, lambda b, u, i: (0, 0)))
        args.append(e)
    rows = max(tk, lc)
    return pl.pallas_call(
        functools.partial(_attn_kernel, mode=mode, n_lat=n_lat, tk=tk, lam_init=lam_init),
        grid=(batch, units, nq),
        in_specs=in_specs,
        out_specs=pl.BlockSpec((tq, LANES), lambda b, u, i: (b * nq + i, u)),
        out_shape=jax.ShapeDtypeStruct((rq, units * LANES), BF16),
        scratch_shapes=[pltpu.VMEM((2, vrows + DEN_ROWS, tq), F32),
                        pltpu.VMEM((2, 2, rows, tq), F32),
                        pltpu.VMEM((2, 2, rows, tq), BF16)],
        compiler_params=_cparams(("arbitrary", "arbitrary", "arbitrary")),
        name="attn_" + mode,
    )(*args)


def _gmlp_rows(u_ref, vn_ref, ws_ref, bias_ref, o_ref):
    lane = lax.broadcasted_iota(jnp.int32, (CHUNK, LANES), 1)
    for c in range(u_ref.shape[0] // CHUNK):
        rows = slice(c * CHUNK, (c + 1) * CHUNK)
        for j in range(A_GROUPS // 2):
            cols = slice(j * LANES, (j + 1) * LANES)
            rhs = vn_ref[rows, cols]
            lo = jnp.dot(ws_ref[2 * j], rhs, preferred_element_type=F32)
            hi = jnp.dot(ws_ref[2 * j + 1], rhs, preferred_element_type=F32)
            mixed = jnp.where(lane < HEAD_DIM, lo, hi) + bias_ref[:, cols]
            o_ref[rows, cols] = (u_ref[rows, cols] * mixed).astype(o_ref.dtype)


def _out_mix(x, ya, yb, w_ref, g):
    half = ya.shape[1]
    y = (jnp.dot(ya, w_ref[0:half, :], preferred_element_type=F32)
         + jnp.dot(yb, w_ref[half:, :], preferred_element_type=F32))
    return x + g * y


def _out_even_kernel(x_ref, u_ref, vn_ref, ws_ref, bias_ref, yb_ref, w_ref, g_ref, o_ref, ya_ref):
    _gmlp_rows(u_ref, vn_ref, ws_ref, bias_ref, ya_ref)
    o_ref[...] = _out_mix(x_ref[...], ya_ref[...], yb_ref[...], w_ref, g_ref[0])


def _out_proj_even(x2, u, vn, ws, bias_map, yb, w, g, tm):
    r = x2.shape[0]
    tiles_per_group = (r // tm) // g.shape[0]
    row = lambda i: (i, 0)
    return pl.pallas_call(
        _out_even_kernel,
        grid=(r // tm,),
        in_specs=[
            pl.BlockSpec((tm, D_MODEL), row),
            pl.BlockSpec((tm, A_WIDTH), row),
            pl.BlockSpec((tm, A_WIDTH), row),
            pl.BlockSpec(ws.shape, lambda i: (0, 0, 0)),
            pl.BlockSpec(bias_map.shape, lambda i: (0, 0)),
            pl.BlockSpec((tm, yb.shape[1]), row),
            pl.BlockSpec(w.shape, lambda i: (0, 0)),
            pl.BlockSpec((1, 1, D_MODEL), lambda i: (i // tiles_per_group, 0, 0)),
        ],
        out_specs=pl.BlockSpec((tm, D_MODEL), row),
        out_shape=jax.ShapeDtypeStruct((r, D_MODEL), F32),
        scratch_shapes=[pltpu.VMEM((tm, A_WIDTH), BF16)],
        compiler_params=_cparams(("arbitrary",)),
        name="out_proj_even",
    )(x2, u, vn, ws, bias_map, yb, w, g)


def _swiglu(h, w1, w3, w2):
    a = jnp.dot(h, w1, preferred_element_type=F32)
    b = jnp.dot(h, w3, preferred_element_type=F32)
    t = (a * jax.nn.sigmoid(a)) * b
    return jnp.dot(t.astype(BF16), w2, preferred_element_type=F32)


def _ffn_kernel(x_ref, scp_ref, sh_ref, g_ref, ng_ref, w1_ref, w3_ref, w2_ref, *rest):
    o_ref = rest[-1]
    if len(rest) == 3:
        rest[1][...] = rest[0][...].astype(BF16)
    x = x_ref[...]
    h = _modnorm(x, ng_ref[...], scp_ref[0], sh_ref[0]).astype(BF16)
    o_ref[...] = x + g_ref[0] * _swiglu(h, w1_ref[...], w3_ref[...], w2_ref[...])


def _ffn(x2, scp, sh, g, ng, w1, w3, w2, tm, cast=None):
    r = x2.shape[0]
    tiles_per_group = (r // tm) // g.shape[0]
    row = lambda i: (i, 0)
    grp = lambda i: (i // tiles_per_group, 0, 0)
    fixed = lambda i: (0, 0)
    resident = lambda w: pl.BlockSpec(w.shape, fixed, pipeline_mode=pl.Buffered(1))
    in_specs = [
        pl.BlockSpec((tm, D_MODEL), row),
        pl.BlockSpec((1, 1, D_MODEL), grp),
        pl.BlockSpec((1, 1, D_MODEL), grp),
        pl.BlockSpec((1, 1, D_MODEL), grp),
        pl.BlockSpec((1, D_MODEL), fixed),
        resident(w1), resident(w3), resident(w2),
    ]
    out_specs = [pl.BlockSpec((tm, D_MODEL), row)]
    out_shape = [jax.ShapeDtypeStruct((r, D_MODEL), F32)]
    args = [x2, scp, sh, g, ng, w1, w3, w2]
    if cast is not None:
        cin, cout, cshape = _side_cast_specs(cast, r // tm, row)
        in_specs.append(cin)
        out_specs.insert(0, cout)
        out_shape.insert(0, cshape)
        args.append(cast)
    out = pl.pallas_call(
        _ffn_kernel, grid=(r // tm,), in_specs=in_specs, out_specs=out_specs, out_shape=out_shape,
        compiler_params=_cparams(("arbitrary",)),
        name="ffn",
    )(*args)
    return out if cast is not None else out[0]


META_E1, META_E2, META_G1, META_G2, META_R1, META_R2 = range(6)


def _lane_col(x, lane, k):
    return jnp.sum(jnp.where(lane == k, x, 0.0), axis=-1, keepdims=True)


def _router_kernel(x_ref, ya_ref, yb_ref, wo_ref, g1_ref, scp_ref, sh_ref, ng_ref, whi_ref, wlo_ref, b_ref,
                   x1_ref, hn_ref, meta_ref, cnt_ref, carry_ref):
    @pl.when(pl.program_id(0) == 0)
    def _():
        carry_ref[...] = jnp.zeros(carry_ref.shape, F32)

    x1 = _out_mix(x_ref[...], ya_ref[...], yb_ref[...], wo_ref, g1_ref[0])
    x1_ref[...] = x1
    h = _modnorm(x1, ng_ref[...], scp_ref[0], sh_ref[0])
    hn_ref[...] = h
    h_hi = h.astype(BF16)
    h_lo = (h - h_hi.astype(F32)).astype(BF16)
    logits = (jnp.dot(h_hi, whi_ref[...], preferred_element_type=F32)
              + jnp.dot(h_lo, whi_ref[...], preferred_element_type=F32)
              + jnp.dot(h_hi, wlo_ref[...], preferred_element_type=F32)) + b_ref[...]
    tm = logits.shape[0]
    lane = lax.broadcasted_iota(jnp.int32, logits.shape, 1).astype(F32)
    logits = jnp.where(lane < N_EXPERTS, logits, -jnp.inf)
    m1 = jnp.max(logits, axis=-1, keepdims=True)
    i1 = jnp.min(jnp.where(logits == m1, lane, float(LANES)), axis=-1, keepdims=True)
    rest = jnp.where(lane == i1, -jnp.inf, logits)
    m2 = jnp.max(rest, axis=-1, keepdims=True)
    i2 = jnp.min(jnp.where(rest == m2, lane, float(LANES)), axis=-1, keepdims=True)
    e2 = jnp.exp(m2 - m1)
    den = 1.0 + e2
    oh1 = lane == i1
    oh2 = lane == i2
    both = jnp.where(jnp.logical_or(oh1, oh2), 1.0, 0.0)
    tri = (lax.broadcasted_iota(jnp.int32, (tm, tm), 1) < lax.broadcasted_iota(jnp.int32, (tm, tm), 0))
    before = jnp.dot(jnp.where(tri, 1.0, 0.0).astype(BF16), both.astype(BF16),
                     preferred_element_type=F32) + carry_ref[0:1, :]
    r1 = jnp.sum(jnp.where(oh1, before, 0.0), axis=-1, keepdims=True)
    r2 = jnp.sum(jnp.where(oh2, before, 0.0), axis=-1, keepdims=True)
    meta = jnp.zeros(logits.shape, F32)
    for k, v in ((META_E1, i1), (META_E2, i2), (META_G1, 1.0 / den), (META_G2, e2 / den),
                 (META_R1, r1), (META_R2, r2)):
        meta = jnp.where(lane == k, v, meta)
    meta_ref[...] = meta
    carry_ref[...] = carry_ref[...] + jnp.sum(both, axis=0, keepdims=True)
    cnt_ref[...] = carry_ref[...]


def _router(x2, ya, yb, w_out, g1, scp, sh, ng, w_hi, w_lo, b, tm):
    r = x2.shape[0]
    tiles_per_group = (r // tm) // scp.shape[0]
    row = lambda i: (i, 0)
    grp = lambda i: (i // tiles_per_group, 0, 0)
    fixed = lambda i: (0, 0)
    return pl.pallas_call(
        _router_kernel,
        grid=(r // tm,),
        in_specs=[
            pl.BlockSpec((tm, D_MODEL), row),
            pl.BlockSpec((tm, ya.shape[1]), row),
            pl.BlockSpec((tm, yb.shape[1]), row),
            pl.BlockSpec(w_out.shape, fixed),
            pl.BlockSpec((1, 1, D_MODEL), grp),
            pl.BlockSpec((1, 1, D_MODEL), grp),
            pl.BlockSpec((1, 1, D_MODEL), grp),
            pl.BlockSpec((1, D_MODEL), fixed),
            pl.BlockSpec(w_hi.shape, fixed),
            pl.BlockSpec(w_lo.shape, fixed),
            pl.BlockSpec(b.shape, fixed),
        ],
        out_specs=[
            pl.BlockSpec((tm, D_MODEL), row),
            pl.BlockSpec((tm, D_MODEL), row),
            pl.BlockSpec((tm, LANES), row),
            pl.BlockSpec((SUBLANES, LANES), fixed),
        ],
        out_shape=[
            jax.ShapeDtypeStruct((r, D_MODEL), F32),
            jax.ShapeDtypeStruct((r, D_MODEL), F32),
            jax.ShapeDtypeStruct((r, LANES), F32),
            jax.ShapeDtypeStruct((SUBLANES, LANES), F32),
        ],
        scratch_shapes=[pltpu.VMEM((SUBLANES, LANES), F32)],
        compiler_params=_cparams(("arbitrary",)),
        name="router",
    )(x2, ya, yb, w_out, g1, scp, sh, ng, w_hi, w_lo, b)


def _row_copy(src_ref, src_row, dst_ref, dst_row, sem):
    return pltpu.make_async_copy(src_ref.at[pl.ds(src_row, 1)], dst_ref.at[pl.ds(dst_row, 1)], sem)


def _dispatch_kernel(pos_ref, pb_ref, pn_ref, nu_ref, h_ref, xs_ref, zero_ref, sem, zsem):
    tm = h_ref.shape[0]
    tile = zero_ref.shape[0]
    n_tiles = xs_ref.shape[0] // tile
    base = pl.program_id(0) * (2 * tm)

    def zero_fill(op):
        for e in range(N_EXPERTS):
            first = pb_ref[e]
            head = (-first) & (SUBLANES - 1)
            for j in range(SUBLANES - 1):
                @pl.when(j < head)
                def _():
                    getattr(_row_copy(zero_ref, 0, xs_ref, first + j, zsem), op)()
            n = pn_ref[e] - head
            for bit in range(tile.bit_length() - 2, SUBLANES.bit_length() - 2, -1):
                size = 1 << bit
                off = pl.multiple_of(first + head + n - (n & (2 * size - 1)), SUBLANES)

                @pl.when((n & size) != 0)
                def _():
                    cp = pltpu.make_async_copy(zero_ref.at[pl.ds(0, size)], xs_ref.at[pl.ds(off, size)], zsem)
                    getattr(cp, op)()
        for j in range(N_EXPERTS):
            t = nu_ref[0] + j

            @pl.when(t < n_tiles)
            def _():
                cp = pltpu.make_async_copy(zero_ref, xs_ref.at[pl.ds(pl.multiple_of(t * tile, tile), tile)], zsem)
                getattr(cp, op)()

    @pl.when(pl.program_id(0) == 0)
    def _():
        zero_ref[...] = jnp.zeros(zero_ref.shape, F32)
        zero_fill("start")

    def issue(r, _):
        for k in range(2):
            _row_copy(h_ref, r, xs_ref, pos_ref[base + 2 * r + k], sem).start()
        return 0

    lax.fori_loop(0, tm, issue, 0, unroll=8)
    for k in range(2):
        pltpu.make_async_copy(h_ref, xs_ref.at[pl.ds(0, tm)], sem).wait()

    @pl.when(pl.program_id(0) == 0)
    def _():
        zero_fill("wait")


def _dispatch(pos_flat, pad_base, pad_n, n_used, hn, n_rows, tm, tile):
    r = hn.shape[0]
    return pl.pallas_call(
        _dispatch_kernel,
        grid_spec=pltpu.PrefetchScalarGridSpec(
            num_scalar_prefetch=4,
            grid=(r // tm,),
            in_specs=[pl.BlockSpec((tm, D_MODEL), lambda i, *_: (i, 0))],
            out_specs=pl.BlockSpec(memory_space=pl.ANY),
            scratch_shapes=[pltpu.VMEM((tile, D_MODEL), F32), pltpu.SemaphoreType.DMA(()),
                            pltpu.SemaphoreType.DMA(())],
        ),
        out_shape=jax.ShapeDtypeStruct((n_rows, D_MODEL), F32),
        compiler_params=_cparams(("arbitrary",)),
        name="dispatch",
    )(pos_flat, pad_base, pad_n, n_used, hn)


def _expert_kernel(te_ref, nu_ref, xs_ref, w1_ref, w3_ref, w2_ref, o_ref):
    del te_ref

    @pl.when(pl.program_id(0) < nu_ref[0])
    def _():
        o_ref[...] = _swiglu(xs_ref[...].astype(BF16), w1_ref[0], w3_ref[0], w2_ref[0])

    @pl.when(pl.program_id(0) >= nu_ref[0])
    def _():
        o_ref[...] = jnp.zeros(o_ref.shape, F32)


def _experts(tile_expert, n_used, xs, w1, w3, w2, tm):
    n_tiles = xs.shape[0] // tm
    wspec = lambda w: pl.BlockSpec((1,) + w.shape[1:], lambda i, te, nu: (te[i], 0, 0))
    return pl.pallas_call(
        _expert_kernel,
        grid_spec=pltpu.PrefetchScalarGridSpec(
            num_scalar_prefetch=2,
            grid=(n_tiles,),
            in_specs=[
                pl.BlockSpec((tm, D_MODEL), lambda i, te, nu: (jnp.minimum(i, nu[0] - 1), 0)),
                wspec(w1), wspec(w3), wspec(w2),
            ],
            out_specs=pl.BlockSpec((tm, D_MODEL), lambda i, te, nu: (i, 0)),
        ),
        out_shape=jax.ShapeDtypeStruct(xs.shape, F32),
        compiler_params=_cparams(("arbitrary",)),
        name="experts",
    )(tile_expert, n_used, xs, w1, w3, w2)


def _combine_kernel(pos_ref, x_ref, meta_ref, g_ref, ys_ref, o_ref, buf_ref, sem):
    tm = x_ref.shape[0]
    i = pl.program_id(0)

    def gather(tile, slot):
        base = tile * (2 * tm)

        def issue(r, _):
            for k in range(2):
                _row_copy(ys_ref, pos_ref[base + 2 * r + k], buf_ref.at[slot, k], r, sem.at[slot]).start()
            return 0

        lax.fori_loop(0, tm, issue, 0, unroll=8)

    @pl.when(i == 0)
    def _():
        gather(0, 0)

    slot = i % 2

    @pl.when(i + 1 < pl.num_programs(0))
    def _():
        gather(i + 1, 1 - slot)

    for k in range(2):
        pltpu.make_async_copy(ys_ref.at[pl.ds(0, tm)], buf_ref.at[slot, k], sem.at[slot]).wait()
    meta = meta_ref[...]
    lane = lax.broadcasted_iota(jnp.int32, meta.shape, 1)
    y = _lane_col(meta, lane, META_G1) * buf_ref[slot, 0] + _lane_col(meta, lane, META_G2) * buf_ref[slot, 1]
    o_ref[...] = x_ref[...] + g_ref[0] * y


def _combine(pos_flat, x2, meta, g, ys, tm):
    r = x2.shape[0]
    tiles_per_group = (r // tm) // g.shape[0]
    return pl.pallas_call(
        _combine_kernel,
        grid_spec=pltpu.PrefetchScalarGridSpec(
            num_scalar_prefetch=1,
            grid=(r // tm,),
            in_specs=[
                pl.BlockSpec((tm, D_MODEL), lambda i, pos: (i, 0)),
                pl.BlockSpec((tm, LANES), lambda i, pos: (i, 0)),
                pl.BlockSpec((1, 1, D_MODEL), lambda i, pos: (i // tiles_per_group, 0, 0)),
                pl.BlockSpec(memory_space=pl.ANY),
            ],
            out_specs=pl.BlockSpec((tm, D_MODEL), lambda i, pos: (i, 0)),
            scratch_shapes=[pltpu.VMEM((2, 2, tm, D_MODEL), F32), pltpu.SemaphoreType.DMA((2,))],
        ),
        out_shape=jax.ShapeDtypeStruct((r, D_MODEL), F32),
        compiler_params=_cparams(("arbitrary",)),
        name="combine",
    )(pos_flat, x2, meta, g, ys)


def _routing_plan(meta, counts, tile):
    n = meta.shape[0]
    n_tiles = 2 * n // tile + N_EXPERTS
    e = meta[:, META_E1:META_E2 + 1].astype(jnp.int32)
    rank = meta[:, META_R1:META_R2 + 1].astype(jnp.int32)
    cnt = counts[0, :N_EXPERTS].astype(jnp.int32)
    tiles_e = (cnt + tile - 1) // tile
    ends = jnp.cumsum(tiles_e)
    starts = (ends - tiles_e) * tile
    pos = (starts[e] + rank).reshape(-1)
    n_used = ends[-1:]
    t_idx = jnp.minimum(jnp.arange(n_tiles, dtype=jnp.int32), n_used[0] - 1)
    tile_expert = jnp.sum((t_idx[:, None] >= ends[None, :]).astype(jnp.int32), axis=1)
    return (pos, tile_expert.astype(jnp.int32), n_used.astype(jnp.int32), (starts + cnt).astype(jnp.int32),
            (tiles_e * tile - cnt).astype(jnp.int32), n_tiles * tile)


LRU_TILE = 512
LRU_CHUNKS = D_WIDTH // LANES
LRU_PAD = SUBLANES


def _lru_pitch(tile):
    seg = tile // SUBLANES
    assert (seg // SUBLANES) % 2 == 0
    return seg, seg + LRU_PAD


def _tile_scans(af_ref, bf_ref, ab_ref, bb_ref, seg, pitch, carry_f):
    nc = LRU_CHUNKS

    def rows_f(t):
        return pl.ds(t, SUBLANES, stride=pitch)

    def rows_b(t):
        return pl.ds(seg - 1 - t, SUBLANES, stride=pitch)

    def ends(t, c):
        hf, pf, hb, pb = c
        rf, rb = rows_f(t), rows_b(t)
        nhf, npf, nhb, npb = [], [], [], []
        for cc in range(nc):
            a = af_ref[cc, rf, :]
            nhf.append(a * hf[cc] + bf_ref[cc, rf, :])
            npf.append(a * pf[cc])
            a = ab_ref[cc, rb, :]
            nhb.append(a * hb[cc] + bb_ref[cc, rb, :])
            npb.append(a * pb[cc])
        return tuple(nhf), tuple(npf), tuple(nhb), tuple(npb)

    z = tuple(jnp.zeros((SUBLANES, LANES), F32) for _ in range(nc))
    o = tuple(jnp.ones((SUBLANES, LANES), F32) for _ in range(nc))
    hf, pf, hb, pb = lax.fori_loop(0, seg, ends, (z, o, z, o), unroll=8)

    in_f, in_b, in_p, out_f, tile_p, tile_h = [], [], [], [], [], []
    for cc in range(nc):
        cin = carry_f[cc]
        rows = [None] * SUBLANES
        for s in range(SUBLANES):
            rows[s] = cin
            cin = hf[cc][s:s + 1] + pf[cc][s:s + 1] * cin
        in_f.append(jnp.concatenate(rows, axis=0))
        out_f.append(cin)
        cin = jnp.zeros((1, LANES), F32)
        pin = jnp.ones((1, LANES), F32)
        rows_h = [None] * SUBLANES
        rows_p = [None] * SUBLANES
        for s in range(SUBLANES - 1, -1, -1):
            rows_h[s] = cin
            rows_p[s] = pin
            cin = hb[cc][s:s + 1] + pb[cc][s:s + 1] * cin
            pin = pb[cc][s:s + 1] * pin
        in_b.append(jnp.concatenate(rows_h, axis=0))
        in_p.append(jnp.concatenate(rows_p, axis=0))
        tile_h.append(cin)
        tile_p.append(pin)

    def final(t, c):
        hf, hb, pb = c
        rf, rb = rows_f(t), rows_b(t)
        nhf, nhb, npb = [], [], []
        for cc in range(nc):
            h = af_ref[cc, rf, :] * hf[cc] + bf_ref[cc, rf, :]
            bf_ref[cc, rf, :] = h
            nhf.append(h)
            a = ab_ref[cc, rb, :]
            h = a * hb[cc] + bb_ref[cc, rb, :]
            p = a * pb[cc]
            bb_ref[cc, rb, :] = h
            ab_ref[cc, rb, :] = p
            nhb.append(h)
            npb.append(p)
        return tuple(nhf), tuple(nhb), tuple(npb)

    lax.fori_loop(0, seg, final, (tuple(in_f), tuple(in_b), tuple(in_p)), unroll=8)
    return out_f, tile_p, tile_h


def _lru_main_kernel(prev_ref, cur_ref, next_ref, gg_ref, cw_ref, cb_ref, w_ref, bias_ref, lam_ref, h0_ref,
                     s1_ref, s2_ref, agg_ref, hl_ref, ext_ref, af_ref, bf_ref, ab_ref, bb_ref, st_ref):
    i = pl.program_id(1)
    nt = pl.num_programs(1)
    t = cur_ref.shape[0]
    seg, pitch = _lru_pitch(t)
    pad = SUBLANES

    @pl.when(i == 0)
    def _():
        st_ref[...] = h0_ref[0]

    ext_ref[0:pad, :] = jnp.where(i == 0, 0.0, prev_ref[...])
    ext_ref[pad:pad + t, :] = cur_ref[...]
    ext_ref[pad + t:, :] = jnp.where(i == nt - 1, 0.0, next_ref[...])
    left = CONV_W // 2
    xd = cb_ref[...] + sum(cw_ref[j:j + 1, :] * ext_ref[pad + j - left:pad + j - left + t, :] for j in range(CONV_W))
    z = jnp.dot(xd.astype(BF16), w_ref[...], preferred_element_type=F32) + bias_ref[...]
    decay = LRU_C * jax.nn.softplus(-lam_ref[...])
    for d, (a_ref, b_ref) in enumerate(((af_ref, bf_ref), (ab_ref, bb_ref))):
        r = 0.5 * jnp.tanh(0.5 * z[:, (2 * d) * D_WIDTH:(2 * d + 1) * D_WIDTH]) + 0.5
        g = 0.5 * jnp.tanh(0.5 * z[:, (2 * d + 1) * D_WIDTH:(2 * d + 2) * D_WIDTH]) + 0.5
        nla = r * decay[:, d * D_WIDTH:(d + 1) * D_WIDTH]
        a = jnp.exp(-nla)
        m = jnp.tanh(nla) * (a * a + 1.0)
        b = jnp.where(m > 0.0, m * lax.rsqrt(m), 0.0) * (g * xd)
        for cc in range(LRU_CHUNKS):
            for s in range(SUBLANES):
                a_ref[cc, s * pitch:s * pitch + seg, :] = a[s * seg:(s + 1) * seg, cc * LANES:(cc + 1) * LANES]
                b_ref[cc, s * pitch:s * pitch + seg, :] = b[s * seg:(s + 1) * seg, cc * LANES:(cc + 1) * LANES]
    carry_f = [st_ref[0:1, cc * LANES:(cc + 1) * LANES] for cc in range(LRU_CHUNKS)]
    out_f, tile_p, tile_h = _tile_scans(af_ref, bf_ref, ab_ref, bb_ref, seg, pitch, carry_f)
    for cc in range(LRU_CHUNKS):
        cols = slice(cc * LANES, (cc + 1) * LANES)
        for s in range(SUBLANES):
            src = slice(s * pitch, s * pitch + seg)
            dst = slice(s * seg, (s + 1) * seg)
            gg = gg_ref[dst, cols]
            s1_ref[dst, cols] = (bf_ref[cc, src, :] + bb_ref[cc, src, :]) * gg
            s2_ref[dst, cols] = ab_ref[cc, src, :] * gg
    state = jnp.concatenate(out_f, axis=-1)
    st_ref[...] = jnp.broadcast_to(state, st_ref.shape)
    hl_ref[0] = jnp.broadcast_to(state, st_ref.shape)
    agg_ref[0] = jnp.concatenate([jnp.concatenate(tile_p, axis=-1), jnp.concatenate(tile_h, axis=-1),
                                  jnp.zeros((SUBLANES - 2, D_WIDTH), F32)], axis=0)


def _lru_fix_kernel(s1_ref, s2_ref, agg_ref, h0_ref, y_ref, st_ref):
    @pl.when(pl.program_id(1) == 0)
    def _():
        st_ref[...] = h0_ref[0]

    carry = st_ref[0:1, :]
    y_ref[...] = (s1_ref[...] + s2_ref[...] * carry).astype(y_ref.dtype)
    st_ref[...] = jnp.broadcast_to(agg_ref[0, 1:2, :] + agg_ref[0, 0:1, :] * carry, st_ref.shape)


def _lru_main(xr, gg, cw, cb, w_cat, b_cat, lam_cat, h0, batch, tile):
    r = xr.shape[0]
    nt = r // batch // tile
    sub = tile // SUBLANES
    _, pitch = _lru_pitch(tile)
    row = lambda b, i: (b * nt + i, 0)
    fixed = lambda b, i: (0, 0)
    state = pl.BlockSpec((1, SUBLANES, D_WIDTH), lambda b, i: (b, 0, 0))
    chunked = pltpu.VMEM((LRU_CHUNKS, SUBLANES * pitch, LANES), F32)
    return pl.pallas_call(
        _lru_main_kernel,
        grid=(batch, nt),
        in_specs=[
            pl.BlockSpec((SUBLANES, D_WIDTH), lambda b, i: ((b * nt + jnp.maximum(i, 1)) * sub - 1, 0)),
            pl.BlockSpec((tile, D_WIDTH), row),
            pl.BlockSpec((SUBLANES, D_WIDTH), lambda b, i: ((b * nt + jnp.minimum(i + 1, nt - 1)) * sub, 0)),
            pl.BlockSpec((tile, D_WIDTH), row),
            pl.BlockSpec(cw.shape, fixed),
            pl.BlockSpec(cb.shape, fixed),
            pl.BlockSpec(w_cat.shape, fixed),
            pl.BlockSpec(b_cat.shape, fixed),
            pl.BlockSpec(lam_cat.shape, fixed),
            state,
        ],
        out_specs=[
            pl.BlockSpec((tile, D_WIDTH), row),
            pl.BlockSpec((tile, D_WIDTH), row),
            pl.BlockSpec((1, SUBLANES, D_WIDTH), lambda b, i: (b * nt + i, 0, 0)),
            state,
        ],
        out_shape=[
            jax.ShapeDtypeStruct((r, D_WIDTH), F32),
            jax.ShapeDtypeStruct((r, D_WIDTH), F32),
            jax.ShapeDtypeStruct((batch * nt, SUBLANES, D_WIDTH), F32),
            jax.ShapeDtypeStruct((batch, SUBLANES, D_WIDTH), F32),
        ],
        scratch_shapes=[pltpu.VMEM((tile + 2 * SUBLANES, D_WIDTH), F32), chunked, chunked, chunked, chunked,
                        pltpu.VMEM((SUBLANES, D_WIDTH), F32)],
        compiler_params=_cparams(("arbitrary", "arbitrary")),
        name="lru_main",
    )(xr, xr, xr, gg, cw, cb, w_cat, b_cat, lam_cat, h0)


def _lru_fix(s1, s2, agg, h0, batch, tile):
    r = s1.shape[0]
    nt = r // batch // tile
    row = lambda b, i: (b * nt + (nt - 1 - i), 0)
    return pl.pallas_call(
        _lru_fix_kernel,
        grid=(batch, nt),
        in_specs=[
            pl.BlockSpec((tile, D_WIDTH), row),
            pl.BlockSpec((tile, D_WIDTH), row),
            pl.BlockSpec((1, SUBLANES, D_WIDTH), lambda b, i: (b * nt + (nt - 1 - i), 0, 0)),
            pl.BlockSpec((1, SUBLANES, D_WIDTH), lambda b, i: (b, 0, 0)),
        ],
        out_specs=pl.BlockSpec((tile, D_WIDTH), row),
        out_shape=jax.ShapeDtypeStruct((r, D_WIDTH), BF16),
        scratch_shapes=[pltpu.VMEM((SUBLANES, D_WIDTH), F32)],
        compiler_params=_cparams(("arbitrary", "arbitrary")),
        name="lru_fix",
    )(s1, s2, agg, h0)


def _rope_tables(n_lat):
    half = HEAD_DIM // 4
    freqs = ROPE_THETA ** (-jnp.arange(half, dtype=F32) / half)
    t = jnp.arange(n_lat, dtype=jnp.int32)
    row = (t // GRID_W).astype(F32)
    col = (t % GRID_W).astype(F32)
    ar = freqs[:, None] * row[None, :]
    ac = freqs[:, None] * col[None, :]
    cos_t = jnp.concatenate([jnp.cos(ar), jnp.cos(ar), jnp.cos(ac), jnp.cos(ac)], axis=0)
    sin_t = jnp.concatenate([-jnp.sin(ar), jnp.sin(ar), -jnp.sin(ac), jnp.sin(ac)], axis=0)
    return cos_t, sin_t


def _block_diag(w):
    nb, bs, _ = w.shape
    eye = jnp.eye(nb, dtype=w.dtype)
    return (eye[:, None, :, None] * w[:, :, None, :]).reshape(nb * bs, nb * bs)


def kernel(x, c, ctx, c_ctx, w_mod, b_mod, norm_mix_g, norm_ffn_g, w_out, e_w_in, a_norm_g, a_ws, a_bs, b_qnorm_g, b_knorm_g, b_lq1, b_lk1, b_lq2, b_lk2, b_subln_g, ffn_w1, ffn_w3, ffn_w2, o_w_in, c_qnorm_g, c_knorm_g, d_conv_w, d_conv_b, d_wa, d_ba, d_wx, d_bx, d_lambda, router_w, router_b, moe_w1, moe_w3, moe_w2):
    batch, n_lat, d = x.shape
    n_ctx = ctx.shape[1]
    depth = w_mod.shape[0]
    assert depth == 2 and d == D_MODEL and batch + 1 <= SUBLANES
    xl = x.reshape(batch * n_lat, d)
    xc = ctx.reshape(batch * n_ctx, d)

    cc = jnp.concatenate([c, c_ctx[None, :], jnp.zeros((SUBLANES - batch - 1, d), F32)], axis=0)
    mod = _modulation(cc, w_mod, b_mod)

    def mods(layer):
        parts = [mod[layer, :, k * d:(k + 1) * d] for k in range(6)]
        lat = [p[:batch, None, :] for p in parts]
        cx = [p[batch:batch + 1, None, :] for p in parts]
        return lat, cx

    cos_t, sin_t = _rope_tables(n_lat)
    cos_c = jnp.ones((HEAD_DIM, n_ctx), F32)
    sin_c = jnp.zeros((HEAD_DIM, n_ctx), F32)
    col = lambda v: v.reshape(-1, 1)
    rowv = lambda v: v.reshape(1, -1)
    tm = 512
    tm_c = 256

    (sh1, sc1, g1, sh2, sc2, g2), (sh1c, sc1c, g1c, sh2c, sc2c, g2c) = mods(0)
    lam_init = 0.8 - 0.6 * math.exp(-0.3 * 0)
    w_in = e_w_in[0].astype(BF16)
    gains = [rowv(a_norm_g[0]), col(b_qnorm_g[0]), col(b_knorm_g[0])]
    ng = rowv(norm_mix_g[0])
    ne, _, fdim = moe_w1.shape[1:]
    m1, u, vn, qt, k, vt = _projection(True, xl, 1.0 + sc1, sh1, ng, w_in, gains, cos_t, sin_t, tm,
                                       cast=moe_w1[0].reshape(ne * d, fdim))
    uc, vnc, qtc, kc, vtc = _projection(True, xc, 1.0 + sc1c, sh1c, ng, w_in, gains, cos_c, sin_c, tm_c)
    lamp = jnp.stack([b_lq1[0], b_lk1[0], b_lq2[0], b_lk2[0]], axis=0)
    extra = [lamp, col(b_subln_g[0])]
    yb = _attention("diff", qt, kc, vtc, k, vt, extra, batch, ATTN_TQ, ATTN_TK, lam_init)
    ybc = _attention("diff", qtc, kc, vtc, None, None, extra, batch, n_ctx, ATTN_TK, lam_init)
    ws = a_ws[0].astype(BF16)
    bias_map = jnp.repeat(a_bs[0].T, HEAD_DIM, axis=1)
    wo = w_out[0].astype(BF16)
    xl = _out_proj_even(xl, u, vn, ws, bias_map, yb, wo, g1, tm)
    xc = _out_proj_even(xc, uc, vnc, ws, bias_map, ybc, wo, g1c, tm_c)
    w1 = ffn_w1[0].astype(BF16)
    w3 = ffn_w3[0].astype(BF16)
    w2 = ffn_w2[0].astype(BF16)
    ngf = rowv(norm_ffn_g[0])
    m3, xl = _ffn(xl, 1.0 + sc2, sh2, g2, ngf, w1, w3, w2, tm, cast=moe_w3[0].reshape(ne * d, fdim))
    xc = _ffn(xc, 1.0 + sc2c, sh2c, g2c, ngf, w1, w3, w2, tm_c)

    (sh1, sc1, g1, sh2, sc2, g2), (sh1c, sc1c, _, _, _, _) = mods(1)
    w_in = o_w_in[0].astype(BF16)
    gains = [col(c_qnorm_g[0]), col(c_knorm_g[0])]
    ng = rowv(norm_mix_g[1])
    m2, qt, k, vt, gg, xr = _projection(False, xl, 1.0 + sc1, sh1, ng, w_in, gains, cos_t, sin_t, tm,
                                        cast=moe_w2[0].reshape(ne * fdim, d))
    _, kc, vtc, _, xrc = _projection(False, xc, 1.0 + sc1c, sh1c, ng, w_in, gains, cos_c, sin_c, tm_c)
    y_attn = _attention("gqa", qt, kc, vtc, k, vt, [], batch, ATTN_TQ, ATTN_TK)

    w_cat = jnp.concatenate([_block_diag(d_wa[0, 0]), _block_diag(d_wx[0, 0]),
                             _block_diag(d_wa[0, 1]), _block_diag(d_wx[0, 1])], axis=1).astype(BF16)
    b_cat = jnp.concatenate([d_ba[0, 0], d_bx[0, 0], d_ba[0, 1], d_bx[0, 1]]).reshape(1, -1)
    lam_cat = d_lambda[0].reshape(1, -1)
    cw = d_conv_w[0]
    cb = rowv(d_conv_b[0])
    zero_state = jnp.zeros((batch, SUBLANES, D_WIDTH), F32)
    _, _, agg_c, st_f = _lru_main(xrc, xrc, cw, cb, w_cat, b_cat, lam_cat, zero_state, batch, n_ctx)
    st_b = jnp.broadcast_to(agg_c[:, 1:2, :], (batch, SUBLANES, D_WIDTH))
    s1, s2, agg, _ = _lru_main(xr, gg, cw, cb, w_cat, b_cat, lam_cat, st_f, batch, LRU_TILE)
    y_rec = _lru_fix(s1, s2, agg, st_b, batch, LRU_TILE)

    ngf = rowv(norm_ffn_g[1])
    wr = jnp.pad(router_w[0], ((0, 0), (0, LANES - N_EXPERTS)))
    wr_hi = wr.astype(BF16)
    wr_lo = (wr - wr_hi.astype(F32)).astype(BF16)
    br = jnp.pad(router_b[0], (0, LANES - N_EXPERTS)).reshape(1, LANES)
    xl, hn, meta, counts = _router(xl, y_attn, y_rec, w_out[1].astype(BF16), g1, 1.0 + sc2, sh2, ngf,
                                   wr_hi, wr_lo, br, tm)
    pos, tile_expert, n_used, pad_base, pad_n, n_rows = _routing_plan(meta, counts, EXPERT_TILE)
    xs = _dispatch(pos, pad_base, pad_n, n_used, hn, n_rows, tm, EXPERT_TILE)
    ys = _experts(tile_expert, n_used, xs, m1.reshape(ne, d, fdim), m3.reshape(ne, d, fdim),
                  m2.reshape(ne, fdim, d), EXPERT_TILE)
    xl = _combine(pos, xl, meta, g2, ys, COMBINE_TILE)
    return xl.reshape(batch, n_lat, d)
```

```python
import functools
import math

import jax
import jax.numpy as jnp
from jax import lax
from jax.experimental import pallas as pl
from jax.experimental.pallas import tpu as pltpu

F32 = jnp.float32
BF16 = jnp.bfloat16

D_MODEL = 1024
HEAD_DIM = 64
GRID_W = 64
ROPE_THETA = 10000.0
EPS = 1e-6
CHUNK = 128
A_WIDTH = D_MODEL // 2
A_GROUPS = A_WIDTH // HEAD_DIM
B_HEADS = D_MODEL // (4 * HEAD_DIM)
B_WIDTH = B_HEADS * 2 * HEAD_DIM
E_IN = 2 * A_WIDTH + 3 * B_WIDTH
C_HEADS = (D_MODEL // 2) // HEAD_DIM
C_KV_HEADS = 2
C_GROUP = C_HEADS // C_KV_HEADS
C_WIDTH = C_HEADS * HEAD_DIM
C_KV_WIDTH = C_KV_HEADS * HEAD_DIM
D_WIDTH = D_MODEL // 2
D_BLOCKS = D_WIDTH // HEAD_DIM
CONV_W = 4
LRU_C = 8.0
O_IN = C_WIDTH + 2 * C_KV_WIDTH + 2 * D_WIDTH
FFN_DIM = 2816
N_EXPERTS = 8

LANES = 128
SUBLANES = 8
KV_BLOCK = 256
ATTN_TQ = 2048
ATTN_TK = 512
EXPERT_TILE = 256
COMBINE_TILE = 256
FIXED_REFERENCE_MAX = 40.0
DEN_ROWS = 16
QUERY_SCALE =HEAD_DIM ** -0.5 * math.log2(math.e)
VMEM_LIMIT = 56 * 1024 * 1024


def _cparams(sem):
    return pltpu.CompilerParams(dimension_semantics=sem, vmem_limit_bytes=VMEM_LIMIT)


def _side_cast_specs(w, n_steps, index):
    rows, cols = w.shape
    blk = (rows // n_steps, cols)
    return pl.BlockSpec(blk, index), pl.BlockSpec(blk, index), jax.ShapeDtypeStruct(w.shape, BF16)


def _modnorm(x, g, scp, sh):
    y = x * lax.rsqrt(jnp.mean(x * x, axis=-1, keepdims=True) + EPS)
    return (y * g) * scp + sh


def _mod_kernel(c_ref, w_ref, b_ref, o_ref):
    c = c_ref[...]
    s = c * jax.nn.sigmoid(c)
    o_ref[0] = jnp.dot(s.astype(BF16), w_ref[0].astype(BF16), preferred_element_type=F32) + b_ref[0]


def _modulation(cc, w_mod, b_mod):
    depth, d, n = w_mod.shape
    tn = 1536
    return pl.pallas_call(
        _mod_kernel,
        grid=(depth, n // tn),
        in_specs=[
            pl.BlockSpec((SUBLANES, d), lambda l, j: (0, 0)),
            pl.BlockSpec((1, d, tn), lambda l, j: (l, 0, j)),
            pl.BlockSpec((1, 1, tn), lambda l, j: (l, 0, j)),
        ],
        out_specs=pl.BlockSpec((1, SUBLANES, tn), lambda l, j: (l, 0, j)),
        out_shape=jax.ShapeDtypeStruct((depth, SUBLANES, n), F32),
        compiler_params=_cparams(("arbitrary", "arbitrary")),
        name="modulation",
    )(cc, w_mod, b_mod.reshape(depth, 1, n))


def _head_norm_rope_t(xh, g, cos_t, sin_t):
    xn = xh * lax.rsqrt(jnp.mean(xh * xh, axis=0, keepdims=True) + EPS) * g
    sw = jnp.concatenate([xn[16:32], xn[0:16], xn[48:64], xn[32:48]], axis=0)
    return xn * cos_t + sw * sin_t


def _proj_even_kernel(x_ref, scp_ref, sh_ref, ng_ref, w_ref, ang_ref, qg_ref, kg_ref, cos_ref, sin_ref, *rest, tm):
    u_ref, vn_ref, qt_ref, k_ref, vt_ref = rest[-5:]
    if len(rest) == 7:
        rest[1][...] = rest[0][...].astype(BF16)
    h = _modnorm(x_ref[...], ng_ref[...], scp_ref[0], sh_ref[0])
    y = jnp.dot(h.astype(BF16), w_ref[...], preferred_element_type=F32)
    ua = y[:, 0:A_WIDTH]
    va = y[:, A_WIDTH:2 * A_WIDTH]
    q = y[:, 2 * A_WIDTH:2 * A_WIDTH + B_WIDTH]
    k = y[:, 2 * A_WIDTH + B_WIDTH:2 * A_WIDTH + 2 * B_WIDTH]
    v = y[:, 2 * A_WIDTH + 2 * B_WIDTH:]
    u_ref[...] = jax.nn.gelu(ua)
    gv = jax.nn.gelu(va)
    vn = gv * lax.rsqrt(jnp.mean(gv * gv, axis=-1, keepdims=True) + EPS) * ang_ref[...]
    vn_ref[...] = vn.astype(BF16)
    cos_t = cos_ref[...]
    sin_t = sin_ref[...]
    q_t = q.T
    k_t = k.T
    zeros = jnp.zeros((HEAD_DIM, tm), F32)
    k_parts = []
    for j in range(2 * B_HEADS):
        sl = slice(j * HEAD_DIM, (j + 1) * HEAD_DIM)
        qh = _head_norm_rope_t(q_t[sl], qg_ref[...], cos_t, sin_t) * QUERY_SCALE
        blk = jnp.concatenate([qh, zeros] if j % 2 == 0 else [zeros, qh], axis=0)
        qt_ref[j] = blk.astype(BF16)
        k_parts.append(_head_norm_rope_t(k_t[sl], kg_ref[...], cos_t, sin_t))
    k_ref[...] = jnp.concatenate(k_parts, axis=0).T.astype(BF16)
    v_t = v.T.astype(BF16)
    for c in range(tm // KV_BLOCK):
        vt_ref[c] = v_t[:, c * KV_BLOCK:(c + 1) * KV_BLOCK]


def _proj_odd_kernel(x_ref, scp_ref, sh_ref, ng_ref, w_ref, qg_ref, kg_ref, cos_ref, sin_ref, *rest, tm):
    qt_ref, k_ref, vt_ref, gg_ref, xr_ref = rest[-5:]
    if len(rest) == 7:
        rest[1][...] = rest[0][...].astype(BF16)
    h = _modnorm(x_ref[...], ng_ref[...], scp_ref[0], sh_ref[0])
    y = jnp.dot(h.astype(BF16), w_ref[...], preferred_element_type=F32)
    q = y[:, 0:C_WIDTH]
    k = y[:, C_WIDTH:C_WIDTH + C_KV_WIDTH]
    v = y[:, C_WIDTH + C_KV_WIDTH:C_WIDTH + 2 * C_KV_WIDTH]
    gate = y[:, C_WIDTH + 2 * C_KV_WIDTH:C_WIDTH + 2 * C_KV_WIDTH + D_WIDTH]
    xr = y[:, C_WIDTH + 2 * C_KV_WIDTH + D_WIDTH:]
    gg_ref[...] = jax.nn.gelu(gate)
    xr_ref[...] = xr
    cos_t = cos_ref[...]
    sin_t = sin_ref[...]
    q_t = q.T
    k_t = k.T
    zeros = jnp.zeros((HEAD_DIM, tm), F32)
    for j in range(C_HEADS):
        sl = slice(j * HEAD_DIM, (j + 1) * HEAD_DIM)
        qh = _head_norm_rope_t(q_t[sl], qg_ref[...], cos_t, sin_t) * QUERY_SCALE
        blk = jnp.concatenate([qh, zeros] if j // C_GROUP == 0 else [zeros, qh], axis=0)
        qt_ref[j] = blk.astype(BF16)
    k_parts = [_head_norm_rope_t(k_t[j * HEAD_DIM:(j + 1) * HEAD_DIM], kg_ref[...], cos_t, sin_t)
               for j in range(C_KV_HEADS)]
    k_ref[...] = jnp.concatenate(k_parts, axis=0).T.astype(BF16)
    v_t = v.T.astype(BF16)
    for c in range(tm // KV_BLOCK):
        vt_ref[c] = v_t[:, c * KV_BLOCK:(c + 1) * KV_BLOCK]


def _projection(even, x2, scp, sh, ng, w, gains, cos_t, sin_t, tm, cast=None):
    r = x2.shape[0]
    nt = r // tm
    tiles_per_group = nt // scp.shape[0]
    pos_tiles = cos_t.shape[1] // tm
    n_in = w.shape[1]
    row = lambda i: (i, 0)
    grp = lambda i: (i // tiles_per_group, 0, 0)
    fixed = lambda i: (0, 0)
    pos = lambda i: (0, i % pos_tiles)
    in_specs = [
        pl.BlockSpec((tm, D_MODEL), row),
        pl.BlockSpec((1, 1, D_MODEL), grp),
        pl.BlockSpec((1, 1, D_MODEL), grp),
        pl.BlockSpec((1, D_MODEL), fixed),
        pl.BlockSpec((D_MODEL, n_in), fixed),
    ]
    in_specs += [pl.BlockSpec(g.shape, fixed) for g in gains]
    in_specs += [pl.BlockSpec((HEAD_DIM, tm), pos), pl.BlockSpec((HEAD_DIM, tm), pos)]
    if even:
        kw, vw, nh = B_WIDTH, B_WIDTH, 2 * B_HEADS
        body = functools.partial(_proj_even_kernel, tm=tm)
        out_shape = [
            jax.ShapeDtypeStruct((r, A_WIDTH), F32),
            jax.ShapeDtypeStruct((r, A_WIDTH), BF16),
            jax.ShapeDtypeStruct((nh, LANES, r), BF16),
            jax.ShapeDtypeStruct((r, kw), BF16),
            jax.ShapeDtypeStruct((r // KV_BLOCK, vw, KV_BLOCK), BF16),
        ]
        out_specs = [
            pl.BlockSpec((tm, A_WIDTH), row),
            pl.BlockSpec((tm, A_WIDTH), row),
            pl.BlockSpec((nh, LANES, tm), lambda i: (0, 0, i)),
            pl.BlockSpec((tm, kw), row),
            pl.BlockSpec((tm // KV_BLOCK, vw, KV_BLOCK), lambda i: (i, 0, 0)),
        ]
    else:
        kw, vw, nh = C_KV_WIDTH, C_KV_WIDTH, C_HEADS
        body = functools.partial(_proj_odd_kernel, tm=tm)
        out_shape = [
            jax.ShapeDtypeStruct((nh, LANES, r), BF16),
            jax.ShapeDtypeStruct((r, kw), BF16),
            jax.ShapeDtypeStruct((r // KV_BLOCK, vw, KV_BLOCK), BF16),
            jax.ShapeDtypeStruct((r, D_WIDTH), F32),
            jax.ShapeDtypeStruct((r, D_WIDTH), F32),
        ]
        out_specs = [
            pl.BlockSpec((nh, LANES, tm), lambda i: (0, 0, i)),
            pl.BlockSpec((tm, kw), row),
            pl.BlockSpec((tm // KV_BLOCK, vw, KV_BLOCK), lambda i: (i, 0, 0)),
            pl.BlockSpec((tm, D_WIDTH), row),
            pl.BlockSpec((tm, D_WIDTH), row),
        ]
    args = [x2, scp, sh, ng, w, *gains, cos_t, sin_t]
    if cast is not None:
        cin, cout, cshape = _side_cast_specs(cast, nt, row)
        in_specs.append(cin)
        out_specs.insert(0, cout)
        out_shape.insert(0, cshape)
        args.append(cast)
    return pl.pallas_call(
        body, grid=(nt,), in_specs=in_specs, out_specs=out_specs, out_shape=out_shape,
        compiler_params=_cparams(("arbitrary",)),
        name="proj_even" if even else "proj_odd",
    )(*args)


def _running_maximum(q, kc_ref, s_ref, p_ref, acc_ref, ones, k_lat, vt_lat, vt_ctx, n_lat, tk):
    def scores(kb, slot):
        n = kb.shape[0]
        mb = []
        for c in range(2):
            s = jnp.dot(kb, q[c], preferred_element_type=F32)
            s_ref[slot, c, 0:n] = s
            mb.append(jnp.max(s, axis=0, keepdims=True))
        return tuple(mb)

    def softmax(slot, n, mb, m):
        out, alphas = [], []
        for c in range(2):
            m_new = jnp.maximum(m[c], mb[c])
            p_ref[slot, c, 0:n] = jnp.exp2(s_ref[slot, c, 0:n] - m_new).astype(BF16)
            alphas.append(jnp.exp2(m[c] - m_new))
            out.append(m_new)
        return tuple(out), tuple(alphas)

    def values(vt_blocks, slot, alphas):
        for c in range(2):
            upd = alphas[c] * acc_ref[c]
            for i, vtb in enumerate(vt_blocks):
                lhs = jnp.concatenate([vtb, ones], axis=0)
                upd = upd + jnp.dot(lhs, p_ref[slot, c, i * KV_BLOCK:(i + 1) * KV_BLOCK],
                                    preferred_element_type=F32)
            acc_ref[c] = upd

    lc = kc_ref.shape[0]
    tq = q[0].shape[-1]
    neg = jnp.full((1, tq), -jnp.inf, F32)
    m = (neg, neg)
    mb = scores(kc_ref[...], 0)
    if not n_lat:
        m, al = softmax(0, lc, mb, m)
        values(vt_ctx, 0, al)
    else:
        assert n_lat % 2 == 0
        mb_next = scores(k_lat(0), 1)
        m, al = softmax(0, lc, mb, m)
        mb = mb_next
        mb_next = scores(k_lat(1), 0)
        m, al_next = softmax(1, tk, mb, m)
        values(vt_ctx, 0, al)
        mb, al = mb_next, al_next

        def body(i, carry):
            m, mb, al = carry
            t = 2 * i + 2
            mb_next = scores(k_lat(t), 1)
            m, al_next = softmax(0, tk, mb, m)
            values(vt_lat(t - 2), 1, al)
            mb, al = mb_next, al_next
            mb_next = scores(k_lat(t + 1), 0)
            m, al_next = softmax(1, tk, mb, m)
            values(vt_lat(t - 1), 0, al)
            return m, mb_next, al_next

        m, mb, al = lax.fori_loop(0, (n_lat - 2) // 2, body, (m, mb, al))
        m, al_next = softmax(0, tk, mb, m)
        values(vt_lat(n_lat - 2), 1, al)
        values(vt_lat(n_lat - 1), 0, al_next)


def _attn_kernel(par_ref, *refs, mode, n_lat, tk, lam_init):
    qt_ref, kc_ref, vtc_ref = refs[:3]
    pos = 3
    if n_lat:
        kl_ref, vtl_ref = refs[3:5]
        pos = 5
    if mode == "diff":
        lamp_ref, sg_ref = refs[pos:pos + 2]
        pos += 2
    o_ref, acc_ref, s_ref, p_ref = refs[pos:pos + 4]
    tq = qt_ref.shape[-1]
    q = (qt_ref[0], qt_ref[1])
    acc_ref[...] = jnp.zeros(acc_ref.shape, F32)

    lc = kc_ref.shape[0]
    ones = jnp.ones((DEN_ROWS, KV_BLOCK), BF16)
    per = tk // KV_BLOCK

    def k_lat(j):
        return kl_ref[pl.ds(pl.multiple_of(j * tk, tk), tk), :]

    def vt_lat(j):
        return [vtl_ref[j * per + i] for i in range(per)]

    vt_ctx = [vtc_ref[i] for i in range(lc // KV_BLOCK)]

    def fixed_reference():
        bound = par_ref[0]

        def numer(kb, slot, base):
            n = kb.shape[0]
            for c in range(2):
                s = jnp.dot(kb, q[c], preferred_element_type=F32)
                p_ref[slot, c, base:base + n] = jnp.exp2(s - bound).astype(BF16)

        def accum(vt_blocks, slot, n):
            lhs = jnp.concatenate([jnp.concatenate(vt_blocks, axis=1), jnp.ones((DEN_ROWS, n), BF16)], axis=0)
            for c in range(2):
                acc_ref[c] += jnp.dot(lhs, p_ref[slot, c, 0:n], preferred_element_type=F32)

        def numer_pair(j, slot):
            numer(k_lat(2 * j), slot, 0)
            numer(k_lat(2 * j + 1), slot, tk)

        def accum_pair(j, slot):
            accum(vt_lat(2 * j) + vt_lat(2 * j + 1), slot, 2 * tk)

        numer(kc_ref[...], 1, 0)
        if not n_lat:
            accum(vt_ctx, 1, lc)
            return
        n_pairs = n_lat // 2
        assert n_lat % 4 == 0
        numer_pair(0, 0)
        accum(vt_ctx, 1, lc)

        def body(j, _):
            numer_pair(2 * j + 1, 1)
            accum_pair(2 * j, 0)
            numer_pair(2 * j + 2, 0)
            accum_pair(2 * j + 1, 1)
            return 0

        lax.fori_loop(0, n_pairs // 2 - 1, body, 0)
        numer_pair(n_pairs - 1, 1)
        accum_pair(n_pairs - 2, 0)
        accum_pair(n_pairs - 1, 1)

    def running_maximum():
        _running_maximum(q, kc_ref, s_ref, p_ref, acc_ref, ones, k_lat, vt_lat, vt_ctx, n_lat, tk)

    use_fixed = par_ref[1] > 0.5
    pl.when(use_fixed)(fixed_reference)
    pl.when(jnp.logical_not(use_fixed))(running_maximum)

    vrows = vtc_ref.shape[1]
    l0 = acc_ref[0, vrows:vrows + 1, :]
    l1 = acc_ref[1, vrows:vrows + 1, :]
    if mode == "diff":
        lp = lamp_ref[...]
        lam = (jnp.exp(jnp.sum(lp[0:1] * lp[1:2], axis=-1, keepdims=True))
               - jnp.exp(jnp.sum(lp[2:3] * lp[3:4], axis=-1, keepdims=True)) + lam_init)
        o = acc_ref[0, 0:LANES, :] * (1.0 / l0) - lam * (acc_ref[1, 0:LANES, :] * (1.0 / l1))
        o = o * lax.rsqrt(jnp.mean(o * o, axis=0, keepdims=True) + EPS) * sg_ref[...] * (1.0 - lam_init)
    else:
        o = jnp.concatenate([acc_ref[0, 0:HEAD_DIM, :] * (1.0 / l0),
                             acc_ref[1, 0:HEAD_DIM, :] * (1.0 / l1)], axis=0)
    o_ref[...] = o.T.astype(o_ref.dtype)


def _attention(mode, par, qt, kc, vtc, kl, vtl, extra, batch, tq, tk, lam_init=0.0):
    units = qt.shape[0] // 2
    rq = qt.shape[2]
    nq = rq // batch // tq
    lc = kc.shape[0] // batch
    col = (lambda u: u) if mode == "diff" else (lambda u: 0)
    vrows = LANES if mode == "diff" else HEAD_DIM
    vcol = (lambda u: u) if mode == "diff" else (lambda u: u // (C_GROUP // 2))
    in_specs = [
        pl.BlockSpec(memory_space=pltpu.SMEM),
        pl.BlockSpec((2, LANES, tq), lambda b, u, i: (u, 0, b * nq + i)),
        pl.BlockSpec((lc, LANES), lambda b, u, i: (b, col(u))),
        pl.BlockSpec((lc // KV_BLOCK, vrows, KV_BLOCK), lambda b, u, i: (b, vcol(u), 0)),
    ]
    args = [par, qt, kc, vtc]
    n_lat = 0
    if kl is not None:
        ll = kl.shape[0] // batch
        n_lat = ll // tk
        in_specs += [
            pl.BlockSpec((ll, LANES), lambda b, u, i: (b, col(u))),
            pl.BlockSpec((ll // KV_BLOCK, vrows, KV_BLOCK), lambda b, u, i: (b, vcol(u), 0)),
        ]
        args += [kl, vtl]
    for e in extra:
        in_specs.append(pl.BlockSpec(e.shape, lambda b, u, i: (0, 0)))
        args.append(e)
    rows = max(tk, lc)
    return pl.pallas_call(
        functools.partial(_attn_kernel, mode=mode, n_lat=n_lat, tk=tk, lam_init=lam_init),
        grid=(batch, units, nq),
        in_specs=in_specs,
        out_specs=pl.BlockSpec((tq, LANES), lambda b, u, i: (b * nq + i, u)),
        out_shape=jax.ShapeDtypeStruct((rq, units * LANES), BF16),
        scratch_shapes=[pltpu.VMEM((2, vrows + DEN_ROWS, tq), F32),
                        pltpu.VMEM((2, 2, rows, tq), F32),
                        pltpu.VMEM((2, 2, max(2 * tk, lc), tq), BF16)],
        compiler_params=_cparams(("arbitrary", "arbitrary", "arbitrary")),
        name="attn_" + mode,
    )(*args)


def _gmlp_rows(u_ref, vn_ref, ws_ref, bias_ref, o_ref):
    lane = lax.broadcasted_iota(jnp.int32, (CHUNK, LANES), 1)
    for c in range(u_ref.shape[0] // CHUNK):
        rows = slice(c * CHUNK, (c + 1) * CHUNK)
        for j in range(A_GROUPS // 2):
            cols = slice(j * LANES, (j + 1) * LANES)
            rhs = vn_ref[rows, cols]
            lo = jnp.dot(ws_ref[2 * j], rhs, preferred_element_type=F32)
            hi = jnp.dot(ws_ref[2 * j + 1], rhs, preferred_element_type=F32)
            mixed = jnp.where(lane < HEAD_DIM, lo, hi) + bias_ref[:, cols]
            o_ref[rows, cols] = (u_ref[rows, cols] * mixed).astype(o_ref.dtype)


def _out_mix(x, ya, yb, w_ref, g):
    half = ya.shape[1]
    y = (jnp.dot(ya, w_ref[0:half, :], preferred_element_type=F32)
         + jnp.dot(yb, w_ref[half:, :], preferred_element_type=F32))
    return x + g * y


def _out_even_kernel(x_ref, u_ref, vn_ref, ws_ref, bias_ref, yb_ref, w_ref, g_ref, o_ref, ya_ref):
    _gmlp_rows(u_ref, vn_ref, ws_ref, bias_ref, ya_ref)
    o_ref[...] = _out_mix(x_ref[...], ya_ref[...], yb_ref[...], w_ref, g_ref[0])


def _out_proj_even(x2, u, vn, ws, bias_map, yb, w, g, tm):
    r = x2.shape[0]
    tiles_per_group = (r // tm) // g.shape[0]
    row = lambda i: (i, 0)
    return pl.pallas_call(
        _out_even_kernel,
        grid=(r // tm,),
        in_specs=[
            pl.BlockSpec((tm, D_MODEL), row),
            pl.BlockSpec((tm, A_WIDTH), row),
            pl.BlockSpec((tm, A_WIDTH), row),
            pl.BlockSpec(ws.shape, lambda i: (0, 0, 0)),
            pl.BlockSpec(bias_map.shape, lambda i: (0, 0)),
            pl.BlockSpec((tm, yb.shape[1]), row),
            pl.BlockSpec(w.shape, lambda i: (0, 0)),
            pl.BlockSpec((1, 1, D_MODEL), lambda i: (i // tiles_per_group, 0, 0)),
        ],
        out_specs=pl.BlockSpec((tm, D_MODEL), row),
        out_shape=jax.ShapeDtypeStruct((r, D_MODEL), F32),
        scratch_shapes=[pltpu.VMEM((tm, A_WIDTH), BF16)],
        compiler_params=_cparams(("arbitrary",)),
        name="out_proj_even",
    )(x2, u, vn, ws, bias_map, yb, w, g)


def _swiglu(h, w1, w3, w2):
    a = jnp.dot(h, w1, preferred_element_type=F32)
    b = jnp.dot(h, w3, preferred_element_type=F32)
    t = (a * jax.nn.sigmoid(a)) * b
    return jnp.dot(t.astype(BF16), w2, preferred_element_type=F32)


def _ffn_kernel(x_ref, scp_ref, sh_ref, g_ref, ng_ref, w1_ref, w3_ref, w2_ref, *rest):
    o_ref = rest[-1]
    if len(rest) == 3:
        rest[1][...] = rest[0][...].astype(BF16)
    x = x_ref[...]
    h = _modnorm(x, ng_ref[...], scp_ref[0], sh_ref[0]).astype(BF16)
    o_ref[...] = x + g_ref[0] * _swiglu(h, w1_ref[...], w3_ref[...], w2_ref[...])


def _ffn(x2, scp, sh, g, ng, w1, w3, w2, tm, cast=None):
    r = x2.shape[0]
    tiles_per_group = (r // tm) // g.shape[0]
    row = lambda i: (i, 0)
    grp = lambda i: (i // tiles_per_group, 0, 0)
    fixed = lambda i: (0, 0)
    resident = lambda w: pl.BlockSpec(w.shape, fixed, pipeline_mode=pl.Buffered(1))
    in_specs = [
        pl.BlockSpec((tm, D_MODEL), row),
        pl.BlockSpec((1, 1, D_MODEL), grp),
        pl.BlockSpec((1, 1, D_MODEL), grp),
        pl.BlockSpec((1, 1, D_MODEL), grp),
        pl.BlockSpec((1, D_MODEL), fixed),
        resident(w1), resident(w3), resident(w2),
    ]
    out_specs = [pl.BlockSpec((tm, D_MODEL), row)]
    out_shape = [jax.ShapeDtypeStruct((r, D_MODEL), F32)]
    args = [x2, scp, sh, g, ng, w1, w3, w2]
    if cast is not None:
        cin, cout, cshape = _side_cast_specs(cast, r // tm, row)
        in_specs.append(cin)
        out_specs.insert(0, cout)
        out_shape.insert(0, cshape)
        args.append(cast)
    out = pl.pallas_call(
        _ffn_kernel, grid=(r // tm,), in_specs=in_specs, out_specs=out_specs, out_shape=out_shape,
        compiler_params=_cparams(("arbitrary",)),
        name="ffn",
    )(*args)
    return out if cast is not None else out[0]


META_E1, META_E2, META_G1, META_G2, META_R1, META_R2 = range(6)


def _lane_col(x, lane, k):
    return jnp.sum(jnp.where(lane == k, x, 0.0), axis=-1, keepdims=True)


def _router_kernel(x_ref, ya_ref, yb_ref, wo_ref, g1_ref, scp_ref, sh_ref, ng_ref, whi_ref, wlo_ref, b_ref,
                   x1_ref, hn_ref, meta_ref, cnt_ref, carry_ref):
    @pl.when(pl.program_id(0) == 0)
    def _():
        carry_ref[...] = jnp.zeros(carry_ref.shape, F32)

    x1 = _out_mix(x_ref[...], ya_ref[...], yb_ref[...], wo_ref, g1_ref[0])
    x1_ref[...] = x1
    h = _modnorm(x1, ng_ref[...], scp_ref[0], sh_ref[0])
    hn_ref[...] = h
    h_hi = h.astype(BF16)
    h_lo = (h - h_hi.astype(F32)).astype(BF16)
    logits = (jnp.dot(h_hi, whi_ref[...], preferred_element_type=F32)
              + jnp.dot(h_lo, whi_ref[...], preferred_element_type=F32)
              + jnp.dot(h_hi, wlo_ref[...], preferred_element_type=F32)) + b_ref[...]
    tm = logits.shape[0]
    lane = lax.broadcasted_iota(jnp.int32, logits.shape, 1).astype(F32)
    logits = jnp.where(lane < N_EXPERTS, logits, -jnp.inf)
    m1 = jnp.max(logits, axis=-1, keepdims=True)
    i1 = jnp.min(jnp.where(logits == m1, lane, float(LANES)), axis=-1, keepdims=True)
    rest = jnp.where(lane == i1, -jnp.inf, logits)
    m2 = jnp.max(rest, axis=-1, keepdims=True)
    i2 = jnp.min(jnp.where(rest == m2, lane, float(LANES)), axis=-1, keepdims=True)
    e2 = jnp.exp(m2 - m1)
    den = 1.0 + e2
    oh1 = lane == i1
    oh2 = lane == i2
    both = jnp.where(jnp.logical_or(oh1, oh2), 1.0, 0.0)
    tri = (lax.broadcasted_iota(jnp.int32, (tm, tm), 1) < lax.broadcasted_iota(jnp.int32, (tm, tm), 0))
    before = jnp.dot(jnp.where(tri, 1.0, 0.0).astype(BF16), both.astype(BF16),
                     preferred_element_type=F32) + carry_ref[0:1, :]
    r1 = jnp.sum(jnp.where(oh1, before, 0.0), axis=-1, keepdims=True)
    r2 = jnp.sum(jnp.where(oh2, before, 0.0), axis=-1, keepdims=True)
    meta = jnp.zeros(logits.shape, F32)
    for k, v in ((META_E1, i1), (META_E2, i2), (META_G1, 1.0 / den), (META_G2, e2 / den),
                 (META_R1, r1), (META_R2, r2)):
        meta = jnp.where(lane == k, v, meta)
    meta_ref[...] = meta
    carry_ref[...] = carry_ref[...] + jnp.sum(both, axis=0, keepdims=True)
    cnt_ref[...] = carry_ref[...]


def _router(x2, ya, yb, w_out, g1, scp, sh, ng, w_hi, w_lo, b, tm):
    r = x2.shape[0]
    tiles_per_group = (r // tm) // scp.shape[0]
    row = lambda i: (i, 0)
    grp = lambda i: (i // tiles_per_group, 0, 0)
    fixed = lambda i: (0, 0)
    return pl.pallas_call(
        _router_kernel,
        grid=(r // tm,),
        in_specs=[
            pl.BlockSpec((tm, D_MODEL), row),
            pl.BlockSpec((tm, ya.shape[1]), row),
            pl.BlockSpec((tm, yb.shape[1]), row),
            pl.BlockSpec(w_out.shape, fixed),
            pl.BlockSpec((1, 1, D_MODEL), grp),
            pl.BlockSpec((1, 1, D_MODEL), grp),
            pl.BlockSpec((1, 1, D_MODEL), grp),
            pl.BlockSpec((1, D_MODEL), fixed),
            pl.BlockSpec(w_hi.shape, fixed),
            pl.BlockSpec(w_lo.shape, fixed),
            pl.BlockSpec(b.shape, fixed),
        ],
        out_specs=[
            pl.BlockSpec((tm, D_MODEL), row),
            pl.BlockSpec((tm, D_MODEL), row),
            pl.BlockSpec((tm, LANES), row),
            pl.BlockSpec((SUBLANES, LANES), fixed),
        ],
        out_shape=[
            jax.ShapeDtypeStruct((r, D_MODEL), F32),
            jax.ShapeDtypeStruct((r, D_MODEL), F32),
            jax.ShapeDtypeStruct((r, LANES), F32),
            jax.ShapeDtypeStruct((SUBLANES, LANES), F32),
        ],
        scratch_shapes=[pltpu.VMEM((SUBLANES, LANES), F32)],
        compiler_params=_cparams(("arbitrary",)),
        name="router",
    )(x2, ya, yb, w_out, g1, scp, sh, ng, w_hi, w_lo, b)


def _row_copy(src_ref, src_row, dst_ref, dst_row, sem):
    return pltpu.make_async_copy(src_ref.at[pl.ds(src_row, 1)], dst_ref.at[pl.ds(dst_row, 1)], sem)


def _dispatch_kernel(pos_ref, pb_ref, pn_ref, nu_ref, h_ref, xs_ref, zero_ref, sem, zsem):
    tm = h_ref.shape[0]
    tile = zero_ref.shape[0]
    n_tiles = xs_ref.shape[0] // tile
    base = pl.program_id(0) * (2 * tm)

    def zero_fill(op):
        for e in range(N_EXPERTS):
            first = pb_ref[e]
            head = (-first) & (SUBLANES - 1)
            for j in range(SUBLANES - 1):
                @pl.when(j < head)
                def _():
                    getattr(_row_copy(zero_ref, 0, xs_ref, first + j, zsem), op)()
            n = pn_ref[e] - head
            for bit in range(tile.bit_length() - 2, SUBLANES.bit_length() - 2, -1):
                size = 1 << bit
                off = pl.multiple_of(first + head + n - (n & (2 * size - 1)), SUBLANES)

                @pl.when((n & size) != 0)
                def _():
                    cp = pltpu.make_async_copy(zero_ref.at[pl.ds(0, size)], xs_ref.at[pl.ds(off, size)], zsem)
                    getattr(cp, op)()
        for j in range(N_EXPERTS):
            t = nu_ref[0] + j

            @pl.when(t < n_tiles)
            def _():
                cp = pltpu.make_async_copy(zero_ref, xs_ref.at[pl.ds(pl.multiple_of(t * tile, tile), tile)], zsem)
                getattr(cp, op)()

    @pl.when(pl.program_id(0) == 0)
    def _():
        zero_ref[...] = jnp.zeros(zero_ref.shape, F32)
        zero_fill("start")

    def issue(r, _):
        for k in range(2):
            _row_copy(h_ref, r, xs_ref, pos_ref[base + 2 * r + k], sem).start()
        return 0

    lax.fori_loop(0, tm, issue, 0, unroll=8)
    for k in range(2):
        pltpu.make_async_copy(h_ref, xs_ref.at[pl.ds(0, tm)], sem).wait()

    @pl.when(pl.program_id(0) == 0)
    def _():
        zero_fill("wait")


def _dispatch(pos_flat, pad_base, pad_n, n_used, hn, n_rows, tm, tile):
    r = hn.shape[0]
    return pl.pallas_call(
        _dispatch_kernel,
        grid_spec=pltpu.PrefetchScalarGridSpec(
            num_scalar_prefetch=4,
            grid=(r // tm,),
            in_specs=[pl.BlockSpec((tm, D_MODEL), lambda i, *_: (i, 0))],
            out_specs=pl.BlockSpec(memory_space=pl.ANY),
            scratch_shapes=[pltpu.VMEM((tile, D_MODEL), F32), pltpu.SemaphoreType.DMA(()),
                            pltpu.SemaphoreType.DMA(())],
        ),
        out_shape=jax.ShapeDtypeStruct((n_rows, D_MODEL), F32),
        compiler_params=_cparams(("arbitrary",)),
        name="dispatch",
    )(pos_flat, pad_base, pad_n, n_used, hn)


def _expert_kernel(te_ref, nu_ref, xs_ref, w1_ref, w3_ref, w2_ref, o_ref):
    del te_ref

    @pl.when(pl.program_id(0) < nu_ref[0])
    def _():
        o_ref[...] = _swiglu(xs_ref[...].astype(BF16), w1_ref[0], w3_ref[0], w2_ref[0])

    @pl.when(pl.program_id(0) >= nu_ref[0])
    def _():
        o_ref[...] = jnp.zeros(o_ref.shape, F32)


def _experts(tile_expert, n_used, xs, w1, w3, w2, tm):
    n_tiles = xs.shape[0] // tm
    wspec = lambda w: pl.BlockSpec((1,) + w.shape[1:], lambda i, te, nu: (te[i], 0, 0))
    return pl.pallas_call(
        _expert_kernel,
        grid_spec=pltpu.PrefetchScalarGridSpec(
            num_scalar_prefetch=2,
            grid=(n_tiles,),
            in_specs=[
                pl.BlockSpec((tm, D_MODEL), lambda i, te, nu: (jnp.minimum(i, nu[0] - 1), 0)),
                wspec(w1), wspec(w3), wspec(w2),
            ],
            out_specs=pl.BlockSpec((tm, D_MODEL), lambda i, te, nu: (i, 0)),
        ),
        out_shape=jax.ShapeDtypeStruct(xs.shape, F32),
        compiler_params=_cparams(("arbitrary",)),
        name="experts",
    )(tile_expert, n_used, xs, w1, w3, w2)


def _combine_kernel(pos_ref, x_ref, meta_ref, g_ref, ys_ref, o_ref, buf_ref, sem):
    tm = x_ref.shape[0]
    i = pl.program_id(0)

    def gather(tile, slot):
        base = tile * (2 * tm)

        def issue(r, _):
            for k in range(2):
                _row_copy(ys_ref, pos_ref[base + 2 * r + k], buf_ref.at[slot, k], r, sem.at[slot]).start()
            return 0

        lax.fori_loop(0, tm, issue, 0, unroll=8)

    @pl.when(i == 0)
    def _():
        gather(0, 0)

    slot = i % 2

    @pl.when(i + 1 < pl.num_programs(0))
    def _():
        gather(i + 1, 1 - slot)

    for k in range(2):
        pltpu.make_async_copy(ys_ref.at[pl.ds(0, tm)], buf_ref.at[slot, k], sem.at[slot]).wait()
    meta = meta_ref[...]
    lane = lax.broadcasted_iota(jnp.int32, meta.shape, 1)
    y = _lane_col(meta, lane, META_G1) * buf_ref[slot, 0] + _lane_col(meta, lane, META_G2) * buf_ref[slot, 1]
    o_ref[...] = x_ref[...] + g_ref[0] * y


def _combine(pos_flat, x2, meta, g, ys, tm):
    r = x2.shape[0]
    tiles_per_group = (r // tm) // g.shape[0]
    return pl.pallas_call(
        _combine_kernel,
        grid_spec=pltpu.PrefetchScalarGridSpec(
            num_scalar_prefetch=1,
            grid=(r // tm,),
            in_specs=[
                pl.BlockSpec((tm, D_MODEL), lambda i, pos: (i, 0)),
                pl.BlockSpec((tm, LANES), lambda i, pos: (i, 0)),
                pl.BlockSpec((1, 1, D_MODEL), lambda i, pos: (i // tiles_per_group, 0, 0)),
                pl.BlockSpec(memory_space=pl.ANY),
            ],
            out_specs=pl.BlockSpec((tm, D_MODEL), lambda i, pos: (i, 0)),
            scratch_shapes=[pltpu.VMEM((2, 2, tm, D_MODEL), F32), pltpu.SemaphoreType.DMA((2,))],
        ),
        out_shape=jax.ShapeDtypeStruct((r, D_MODEL), F32),
        compiler_params=_cparams(("arbitrary",)),
        name="combine",
    )(pos_flat, x2, meta, g, ys)


def _routing_plan(meta, counts, tile):
    n = meta.shape[0]
    n_tiles = 2 * n // tile + N_EXPERTS
    e = meta[:, META_E1:META_E2 + 1].astype(jnp.int32)
    rank = meta[:, META_R1:META_R2 + 1].astype(jnp.int32)
    cnt = counts[0, :N_EXPERTS].astype(jnp.int32)
    tiles_e = (cnt + tile - 1) // tile
    ends = jnp.cumsum(tiles_e)
    starts = (ends - tiles_e) * tile
    pos = (starts[e] + rank).reshape(-1)
    n_used = ends[-1:]
    t_idx = jnp.minimum(jnp.arange(n_tiles, dtype=jnp.int32), n_used[0] - 1)
    tile_expert = jnp.sum((t_idx[:, None] >= ends[None, :]).astype(jnp.int32), axis=1)
    return (pos, tile_expert.astype(jnp.int32), n_used.astype(jnp.int32), (starts + cnt).astype(jnp.int32),
            (tiles_e * tile - cnt).astype(jnp.int32), n_tiles * tile)


LRU_TILE = 512
LRU_CHUNKS = D_WIDTH // LANES
LRU_PAD = SUBLANES


def _lru_pitch(tile):
    seg = tile // SUBLANES
    assert (seg // SUBLANES) % 2 == 0
    return seg, seg + LRU_PAD


def _tile_scans(af_ref, bf_ref, ab_ref, bb_ref, seg, pitch, carry_f):
    nc = LRU_CHUNKS

    def rows_f(t):
        return pl.ds(t, SUBLANES, stride=pitch)

    def rows_b(t):
        return pl.ds(seg - 1 - t, SUBLANES, stride=pitch)

    def ends(t, c):
        hf, pf, hb, pb = c
        rf, rb = rows_f(t), rows_b(t)
        nhf, npf, nhb, npb = [], [], [], []
        for cc in range(nc):
            a = af_ref[cc, rf, :]
            nhf.append(a * hf[cc] + bf_ref[cc, rf, :])
            npf.append(a * pf[cc])
            a = ab_ref[cc, rb, :]
            nhb.append(a * hb[cc] + bb_ref[cc, rb, :])
            npb.append(a * pb[cc])
        return tuple(nhf), tuple(npf), tuple(nhb), tuple(npb)

    z = tuple(jnp.zeros((SUBLANES, LANES), F32) for _ in range(nc))
    o = tuple(jnp.ones((SUBLANES, LANES), F32) for _ in range(nc))
    hf, pf, hb, pb = lax.fori_loop(0, seg, ends, (z, o, z, o), unroll=8)

    in_f, in_b, in_p, out_f, tile_p, tile_h = [], [], [], [], [], []
    for cc in range(nc):
        cin = carry_f[cc]
        rows = [None] * SUBLANES
        for s in range(SUBLANES):
            rows[s] = cin
            cin = hf[cc][s:s + 1] + pf[cc][s:s + 1] * cin
        in_f.append(jnp.concatenate(rows, axis=0))
        out_f.append(cin)
        cin = jnp.zeros((1, LANES), F32)
        pin = jnp.ones((1, LANES), F32)
        rows_h = [None] * SUBLANES
        rows_p = [None] * SUBLANES
        for s in range(SUBLANES - 1, -1, -1):
            rows_h[s] = cin
            rows_p[s] = pin
            cin = hb[cc][s:s + 1] + pb[cc][s:s + 1] * cin
            pin = pb[cc][s:s + 1] * pin
        in_b.append(jnp.concatenate(rows_h, axis=0))
        in_p.append(jnp.concatenate(rows_p, axis=0))
        tile_h.append(cin)
        tile_p.append(pin)

    def final(t, c):
        hf, hb, pb = c
        rf, rb = rows_f(t), rows_b(t)
        nhf, nhb, npb = [], [], []
        for cc in range(nc):
            h = af_ref[cc, rf, :] * hf[cc] + bf_ref[cc, rf, :]
            bf_ref[cc, rf, :] = h
            nhf.append(h)
            a = ab_ref[cc, rb, :]
            h = a * hb[cc] + bb_ref[cc, rb, :]
            p = a * pb[cc]
            bb_ref[cc, rb, :] = h
            ab_ref[cc, rb, :] = p
            nhb.append(h)
            npb.append(p)
        return tuple(nhf), tuple(nhb), tuple(npb)

    lax.fori_loop(0, seg, final, (tuple(in_f), tuple(in_b), tuple(in_p)), unroll=8)
    return out_f, tile_p, tile_h


def _lru_main_kernel(prev_ref, cur_ref, next_ref, gg_ref, cw_ref, cb_ref, w_ref, bias_ref, lam_ref, h0_ref,
                     s1_ref, s2_ref, agg_ref, hl_ref, ext_ref, af_ref, bf_ref, ab_ref, bb_ref, st_ref):
    i = pl.program_id(1)
    nt = pl.num_programs(1)
    t = cur_ref.shape[0]
    seg, pitch = _lru_pitch(t)
    pad = SUBLANES

    @pl.when(i == 0)
    def _():
        st_ref[...] = h0_ref[0]

    ext_ref[0:pad, :] = jnp.where(i == 0, 0.0, prev_ref[...])
    ext_ref[pad:pad + t, :] = cur_ref[...]
    ext_ref[pad + t:, :] = jnp.where(i == nt - 1, 0.0, next_ref[...])
    left = CONV_W // 2
    xd = cb_ref[...] + sum(cw_ref[j:j + 1, :] * ext_ref[pad + j - left:pad + j - left + t, :] for j in range(CONV_W))
    z = jnp.dot(xd.astype(BF16), w_ref[...], preferred_element_type=F32) + bias_ref[...]
    decay = LRU_C * jax.nn.softplus(-lam_ref[...])
    for d, (a_ref, b_ref) in enumerate(((af_ref, bf_ref), (ab_ref, bb_ref))):
        r = 0.5 * jnp.tanh(0.5 * z[:, (2 * d) * D_WIDTH:(2 * d + 1) * D_WIDTH]) + 0.5
        g = 0.5 * jnp.tanh(0.5 * z[:, (2 * d + 1) * D_WIDTH:(2 * d + 2) * D_WIDTH]) + 0.5
        nla = r * decay[:, d * D_WIDTH:(d + 1) * D_WIDTH]
        a = jnp.exp(-nla)
        m = jnp.tanh(nla) * (a * a + 1.0)
        b = jnp.where(m > 0.0, m * lax.rsqrt(m), 0.0) * (g * xd)
        for cc in range(LRU_CHUNKS):
            for s in range(SUBLANES):
                a_ref[cc, s * pitch:s * pitch + seg, :] = a[s * seg:(s + 1) * seg, cc * LANES:(cc + 1) * LANES]
                b_ref[cc, s * pitch:s * pitch + seg, :] = b[s * seg:(s + 1) * seg, cc * LANES:(cc + 1) * LANES]
    carry_f = [st_ref[0:1, cc * LANES:(cc + 1) * LANES] for cc in range(LRU_CHUNKS)]
    out_f, tile_p, tile_h = _tile_scans(af_ref, bf_ref, ab_ref, bb_ref, seg, pitch, carry_f)
    for cc in range(LRU_CHUNKS):
        cols = slice(cc * LANES, (cc + 1) * LANES)
        for s in range(SUBLANES):
            src = slice(s * pitch, s * pitch + seg)
            dst = slice(s * seg, (s + 1) * seg)
            gg = gg_ref[dst, cols]
            s1_ref[dst, cols] = (bf_ref[cc, src, :] + bb_ref[cc, src, :]) * gg
            s2_ref[dst, cols] = ab_ref[cc, src, :] * gg
    state = jnp.concatenate(out_f, axis=-1)
    st_ref[...] = jnp.broadcast_to(state, st_ref.shape)
    hl_ref[0] = jnp.broadcast_to(state, st_ref.shape)
    agg_ref[0] = jnp.concatenate([jnp.concatenate(tile_p, axis=-1), jnp.concatenate(tile_h, axis=-1),
                                  jnp.zeros((SUBLANES - 2, D_WIDTH), F32)], axis=0)


def _lru_fix_kernel(s1_ref, s2_ref, agg_ref, h0_ref, y_ref, st_ref):
    @pl.when(pl.program_id(1) == 0)
    def _():
        st_ref[...] = h0_ref[0]

    carry = st_ref[0:1, :]
    y_ref[...] = (s1_ref[...] + s2_ref[...] * carry).astype(y_ref.dtype)
    st_ref[...] = jnp.broadcast_to(agg_ref[0, 1:2, :] + agg_ref[0, 0:1, :] * carry, st_ref.shape)


def _lru_main(xr, gg, cw, cb, w_cat, b_cat, lam_cat, h0, batch, tile):
    r = xr.shape[0]
    nt = r // batch // tile
    sub = tile // SUBLANES
    _, pitch = _lru_pitch(tile)
    row = lambda b, i: (b * nt + i, 0)
    fixed = lambda b, i: (0, 0)
    state = pl.BlockSpec((1, SUBLANES, D_WIDTH), lambda b, i: (b, 0, 0))
    chunked = pltpu.VMEM((LRU_CHUNKS, SUBLANES * pitch, LANES), F32)
    return pl.pallas_call(
        _lru_main_kernel,
        grid=(batch, nt),
        in_specs=[
            pl.BlockSpec((SUBLANES, D_WIDTH), lambda b, i: ((b * nt + jnp.maximum(i, 1)) * sub - 1, 0)),
            pl.BlockSpec((tile, D_WIDTH), row),
            pl.BlockSpec((SUBLANES, D_WIDTH), lambda b, i: ((b * nt + jnp.minimum(i + 1, nt - 1)) * sub, 0)),
            pl.BlockSpec((tile, D_WIDTH), row),
            pl.BlockSpec(cw.shape, fixed),
            pl.BlockSpec(cb.shape, fixed),
            pl.BlockSpec(w_cat.shape, fixed),
            pl.BlockSpec(b_cat.shape, fixed),
            pl.BlockSpec(lam_cat.shape, fixed),
            state,
        ],
        out_specs=[
            pl.BlockSpec((tile, D_WIDTH), row),
            pl.BlockSpec((tile, D_WIDTH), row),
            pl.BlockSpec((1, SUBLANES, D_WIDTH), lambda b, i: (b * nt + i, 0, 0)),
            state,
        ],
        out_shape=[
            jax.ShapeDtypeStruct((r, D_WIDTH), F32),
            jax.ShapeDtypeStruct((r, D_WIDTH), F32),
            jax.ShapeDtypeStruct((batch * nt, SUBLANES, D_WIDTH), F32),
            jax.ShapeDtypeStruct((batch, SUBLANES, D_WIDTH), F32),
        ],
        scratch_shapes=[pltpu.VMEM((tile + 2 * SUBLANES, D_WIDTH), F32), chunked, chunked, chunked, chunked,
                        pltpu.VMEM((SUBLANES, D_WIDTH), F32)],
        compiler_params=_cparams(("arbitrary", "arbitrary")),
        name="lru_main",
    )(xr, xr, xr, gg, cw, cb, w_cat, b_cat, lam_cat, h0)


def _lru_fix(s1, s2, agg, h0, batch, tile):
    r = s1.shape[0]
    nt = r // batch // tile
    row = lambda b, i: (b * nt + (nt - 1 - i), 0)
    return pl.pallas_call(
        _lru_fix_kernel,
        grid=(batch, nt),
        in_specs=[
            pl.BlockSpec((tile, D_WIDTH), row),
            pl.BlockSpec((tile, D_WIDTH), row),
            pl.BlockSpec((1, SUBLANES, D_WIDTH), lambda b, i: (b * nt + (nt - 1 - i), 0, 0)),
            pl.BlockSpec((1, SUBLANES, D_WIDTH), lambda b, i: (b, 0, 0)),
        ],
        out_specs=pl.BlockSpec((tile, D_WIDTH), row),
        out_shape=jax.ShapeDtypeStruct((r, D_WIDTH), BF16),
        scratch_shapes=[pltpu.VMEM((SUBLANES, D_WIDTH), F32)],
        compiler_params=_cparams(("arbitrary", "arbitrary")),
        name="lru_fix",
    )(s1, s2, agg, h0)


def _score_bound(q_gain, k_gain):
    bound = 1.02 * QUERY_SCALE * HEAD_DIM * jnp.max(jnp.abs(q_gain)) * jnp.max(jnp.abs(k_gain))
    return jnp.stack([bound, (bound <= FIXED_REFERENCE_MAX).astype(F32)]).astype(F32)


def _rope_tables(n_lat):
    half = HEAD_DIM // 4
    freqs = ROPE_THETA ** (-jnp.arange(half, dtype=F32) / half)
    t = jnp.arange(n_lat, dtype=jnp.int32)
    row = (t // GRID_W).astype(F32)
    col = (t % GRID_W).astype(F32)
    ar = freqs[:, None] * row[None, :]
    ac = freqs[:, None] * col[None, :]
    cos_t = jnp.concatenate([jnp.cos(ar), jnp.cos(ar), jnp.cos(ac), jnp.cos(ac)], axis=0)
    sin_t = jnp.concatenate([-jnp.sin(ar), jnp.sin(ar), -jnp.sin(ac), jnp.sin(ac)], axis=0)
    return cos_t, sin_t


def _block_diag(w):
    nb, bs, _ = w.shape
    eye = jnp.eye(nb, dtype=w.dtype)
    return (eye[:, None, :, None] * w[:, :, None, :]).reshape(nb * bs, nb * bs)


def kernel(x, c, ctx, c_ctx, w_mod, b_mod, norm_mix_g, norm_ffn_g, w_out, e_w_in, a_norm_g, a_ws, a_bs, b_qnorm_g, b_knorm_g, b_lq1, b_lk1, b_lq2, b_lk2, b_subln_g, ffn_w1, ffn_w3, ffn_w2, o_w_in, c_qnorm_g, c_knorm_g, d_conv_w, d_conv_b, d_wa, d_ba, d_wx, d_bx, d_lambda, router_w, router_b, moe_w1, moe_w3, moe_w2):
    batch, n_lat, d = x.shape
    n_ctx = ctx.shape[1]
    depth = w_mod.shape[0]
    assert depth == 2 and d == D_MODEL and batch + 1 <= SUBLANES
    xl = x.reshape(batch * n_lat, d)
    xc = ctx.reshape(batch * n_ctx, d)

    cc = jnp.concatenate([c, c_ctx[None, :], jnp.zeros((SUBLANES - batch - 1, d), F32)], axis=0)
    mod = _modulation(cc, w_mod, b_mod)

    def mods(layer):
        parts = [mod[layer, :, k * d:(k + 1) * d] for k in range(6)]
        lat = [p[:batch, None, :] for p in parts]
        cx = [p[batch:batch + 1, None, :] for p in parts]
        return lat, cx

    cos_t, sin_t = _rope_tables(n_lat)
    cos_c = jnp.ones((HEAD_DIM, n_ctx), F32)
    sin_c = jnp.zeros((HEAD_DIM, n_ctx), F32)
    col = lambda v: v.reshape(-1, 1)
    rowv = lambda v: v.reshape(1, -1)
    tm = 512
    tm_c = 256

    (sh1, sc1, g1, sh2, sc2, g2), (sh1c, sc1c, g1c, sh2c, sc2c, g2c) = mods(0)
    lam_init = 0.8 - 0.6 * math.exp(-0.3 * 0)
    w_in = e_w_in[0].astype(BF16)
    gains = [rowv(a_norm_g[0]), col(b_qnorm_g[0]), col(b_knorm_g[0])]
    ng = rowv(norm_mix_g[0])
    ne, _, fdim = moe_w1.shape[1:]
    m1, u, vn, qt, k, vt = _projection(True, xl, 1.0 + sc1, sh1, ng, w_in, gains, cos_t, sin_t, tm,
                                       cast=moe_w1[0].reshape(ne * d, fdim))
    uc, vnc, qtc, kc, vtc = _projection(True, xc, 1.0 + sc1c, sh1c, ng, w_in, gains, cos_c, sin_c, tm_c)
    lamp = jnp.stack([b_lq1[0], b_lk1[0], b_lq2[0], b_lk2[0]], axis=0)
    extra = [lamp, col(b_subln_g[0])]
    par = _score_bound(b_qnorm_g[0], b_knorm_g[0])
    yb = _attention("diff", par, qt, kc, vtc, k, vt, extra, batch, ATTN_TQ, ATTN_TK, lam_init)
    ybc = _attention("diff", par, qtc, kc, vtc, None, None, extra, batch, n_ctx, ATTN_TK, lam_init)
    ws = a_ws[0].astype(BF16)
    bias_map = jnp.repeat(a_bs[0].T, HEAD_DIM, axis=1)
    wo = w_out[0].astype(BF16)
    xl = _out_proj_even(xl, u, vn, ws, bias_map, yb, wo, g1, tm)
    xc = _out_proj_even(xc, uc, vnc, ws, bias_map, ybc, wo, g1c, tm_c)
    w1 = ffn_w1[0].astype(BF16)
    w3 = ffn_w3[0].astype(BF16)
    w2 = ffn_w2[0].astype(BF16)
    ngf = rowv(norm_ffn_g[0])
    m3, xl = _ffn(xl, 1.0 + sc2, sh2, g2, ngf, w1, w3, w2, tm, cast=moe_w3[0].reshape(ne * d, fdim))
    xc = _ffn(xc, 1.0 + sc2c, sh2c, g2c, ngf, w1, w3, w2, tm_c)

    (sh1, sc1, g1, sh2, sc2, g2), (sh1c, sc1c, _, _, _, _) = mods(1)
    w_in = o_w_in[0].astype(BF16)
    gains = [col(c_qnorm_g[0]), col(c_knorm_g[0])]
    ng = rowv(norm_mix_g[1])
    m2, qt, k, vt, gg, xr = _projection(False, xl, 1.0 + sc1, sh1, ng, w_in, gains, cos_t, sin_t, tm,
                                        cast=moe_w2[0].reshape(ne * fdim, d))
    _, kc, vtc, _, xrc = _projection(False, xc, 1.0 + sc1c, sh1c, ng, w_in, gains, cos_c, sin_c, tm_c)
    y_attn = _attention("gqa", _score_bound(c_qnorm_g[0], c_knorm_g[0]), qt, kc, vtc, k, vt, [], batch,
                        ATTN_TQ, ATTN_TK)

    w_cat = jnp.concatenate([_block_diag(d_wa[0, 0]), _block_diag(d_wx[0, 0]),
                             _block_diag(d_wa[0, 1]), _block_diag(d_wx[0, 1])], axis=1).astype(BF16)
    b_cat = jnp.concatenate([d_ba[0, 0], d_bx[0, 0], d_ba[0, 1], d_bx[0, 1]]).reshape(1, -1)
    lam_cat = d_lambda[0].reshape(1, -1)
    cw = d_conv_w[0]
    cb = rowv(d_conv_b[0])
    zero_state = jnp.zeros((batch, SUBLANES, D_WIDTH), F32)
    _, _, agg_c, st_f = _lru_main(xrc, xrc, cw, cb, w_cat, b_cat, lam_cat, zero_state, batch, n_ctx)
    st_b = jnp.broadcast_to(agg_c[:, 1:2, :], (batch, SUBLANES, D_WIDTH))
    s1, s2, agg, _ = _lru_main(xr, gg, cw, cb, w_cat, b_cat, lam_cat, st_f, batch, LRU_TILE)
    y_rec = _lru_fix(s1, s2, agg, st_b, batch, LRU_TILE)

    ngf = rowv(norm_ffn_g[1])
    wr = jnp.pad(router_w[0], ((0, 0), (0, LANES - N_EXPERTS)))
    wr_hi = wr.astype(BF16)
    wr_lo = (wr - wr_hi.astype(F32)).astype(BF16)
    br = jnp.pad(router_b[0], (0, LANES - N_EXPERTS)).reshape(1, LANES)
    xl, hn, meta, counts = _router(xl, y_attn, y_rec, w_out[1].astype(BF16), g1, 1.0 + sc2, sh2, ngf,
                                   wr_hi, wr_lo, br, tm)
    pos, tile_expert, n_used, pad_base, pad_n, n_rows = _routing_plan(meta, counts, EXPERT_TILE)
    xs = _dispatch(pos, pad_base, pad_n, n_used, hn, n_rows, tm, EXPERT_TILE)
    ys = _experts(tile_expert, n_used, xs, m1.reshape(ne, d, fdim), m3.reshape(ne, d, fdim),
                  m2.reshape(ne, fdim, d), EXPERT_TILE)
    xl = _combine(pos, xl, meta, g2, ys, COMBINE_TILE)
    return xl.reshape(batch, n_lat, d)
```

```python
import functools
import math

import jax
import jax.numpy as jnp
from jax import lax
from jax.experimental import pallas as pl
from jax.experimental.pallas import tpu as pltpu

F32 = jnp.float32
BF16 = jnp.bfloat16

D_MODEL = 1024
HEAD_DIM = 64
GRID_W = 64
ROPE_THETA = 10000.0
EPS = 1e-6
CHUNK = 128
A_WIDTH = D_MODEL // 2
A_GROUPS = A_WIDTH // HEAD_DIM
B_HEADS = D_MODEL // (4 * HEAD_DIM)
B_WIDTH = B_HEADS * 2 * HEAD_DIM
E_IN = 2 * A_WIDTH + 3 * B_WIDTH
C_HEADS = (D_MODEL // 2) // HEAD_DIM
C_KV_HEADS = 2
C_GROUP = C_HEADS // C_KV_HEADS
C_WIDTH = C_HEADS * HEAD_DIM
C_KV_WIDTH = C_KV_HEADS * HEAD_DIM
D_WIDTH = D_MODEL // 2
D_BLOCKS = D_WIDTH // HEAD_DIM
CONV_W = 4
LRU_C = 8.0
O_IN = C_WIDTH + 2 * C_KV_WIDTH + 2 * D_WIDTH
FFN_DIM = 2816
N_EXPERTS = 8

LANES = 128
SUBLANES = 8
KV_BLOCK = 256
ATTN_TQ = 2048
ATTN_TK = 512
EXPERT_TILE = 256
COMBINE_TILE = 256
FIXED_REFERENCE_MAX = 40.0
DEN_ROWS = 16
QUERY_SCALE =HEAD_DIM ** -0.5 * math.log2(math.e)
VMEM_LIMIT = 56 * 1024 * 1024


def _cparams(sem):
    return pltpu.CompilerParams(dimension_semantics=sem, vmem_limit_bytes=VMEM_LIMIT)


def _side_cast_specs(w, n_steps, index):
    rows, cols = w.shape
    blk = (rows // n_steps, cols)
    return pl.BlockSpec(blk, index), pl.BlockSpec(blk, index), jax.ShapeDtypeStruct(w.shape, BF16)


def _modnorm(x, g, scp, sh):
    y = x * lax.rsqrt(jnp.mean(x * x, axis=-1, keepdims=True) + EPS)
    return (y * g) * scp + sh


def _mod_kernel(c_ref, w_ref, b_ref, o_ref):
    c = c_ref[...]
    s = c * jax.nn.sigmoid(c)
    o_ref[0] = jnp.dot(s.astype(BF16), w_ref[0].astype(BF16), preferred_element_type=F32) + b_ref[0]


def _modulation(cc, w_mod, b_mod):
    depth, d, n = w_mod.shape
    tn = 1536
    return pl.pallas_call(
        _mod_kernel,
        grid=(depth, n // tn),
        in_specs=[
            pl.BlockSpec((SUBLANES, d), lambda l, j: (0, 0)),
            pl.BlockSpec((1, d, tn), lambda l, j: (l, 0, j)),
            pl.BlockSpec((1, 1, tn), lambda l, j: (l, 0, j)),
        ],
        out_specs=pl.BlockSpec((1, SUBLANES, tn), lambda l, j: (l, 0, j)),
        out_shape=jax.ShapeDtypeStruct((depth, SUBLANES, n), F32),
        compiler_params=_cparams(("arbitrary", "arbitrary")),
        name="modulation",
    )(cc, w_mod, b_mod.reshape(depth, 1, n))


def _head_norm_rope_t(xh, g, cos_t, sin_t):
    xn = xh * lax.rsqrt(jnp.mean(xh * xh, axis=0, keepdims=True) + EPS) * g
    sw = jnp.concatenate([xn[16:32], xn[0:16], xn[48:64], xn[32:48]], axis=0)
    return xn * cos_t + sw * sin_t


def _proj_even_kernel(x_ref, scp_ref, sh_ref, ng_ref, w_ref, ang_ref, qg_ref, kg_ref, cos_ref, sin_ref,
                      u_ref, vn_ref, qt_ref, k_ref, vt_ref, *, tm):
    h = _modnorm(x_ref[...], ng_ref[...], scp_ref[0], sh_ref[0])
    y = jnp.dot(h.astype(BF16), w_ref[...], preferred_element_type=F32)
    ua = y[:, 0:A_WIDTH]
    va = y[:, A_WIDTH:2 * A_WIDTH]
    q = y[:, 2 * A_WIDTH:2 * A_WIDTH + B_WIDTH]
    k = y[:, 2 * A_WIDTH + B_WIDTH:2 * A_WIDTH + 2 * B_WIDTH]
    v = y[:, 2 * A_WIDTH + 2 * B_WIDTH:]
    u_ref[...] = jax.nn.gelu(ua)
    gv = jax.nn.gelu(va)
    vn = gv * lax.rsqrt(jnp.mean(gv * gv, axis=-1, keepdims=True) + EPS) * ang_ref[...]
    vn_ref[...] = vn.astype(BF16)
    cos_t = cos_ref[...]
    sin_t = sin_ref[...]
    q_t = q.T
    k_t = k.T
    zeros = jnp.zeros((HEAD_DIM, tm), F32)
    k_parts = []
    for j in range(2 * B_HEADS):
        sl = slice(j * HEAD_DIM, (j + 1) * HEAD_DIM)
        qh = _head_norm_rope_t(q_t[sl], qg_ref[...], cos_t, sin_t) * QUERY_SCALE
        blk = jnp.concatenate([qh, zeros] if j % 2 == 0 else [zeros, qh], axis=0)
        qt_ref[j] = blk.astype(BF16)
        k_parts.append(_head_norm_rope_t(k_t[sl], kg_ref[...], cos_t, sin_t))
    k_ref[...] = jnp.concatenate(k_parts, axis=0).T.astype(BF16)
    v_t = v.T.astype(BF16)
    for c in range(tm // KV_BLOCK):
        vt_ref[c] = v_t[:, c * KV_BLOCK:(c + 1) * KV_BLOCK]


def _proj_odd_kernel(x_ref, scp_ref, sh_ref, ng_ref, w_ref, qg_ref, kg_ref, cos_ref, sin_ref,
                     qt_ref, k_ref, vt_ref, gg_ref, xr_ref, *, tm):
    h = _modnorm(x_ref[...], ng_ref[...], scp_ref[0], sh_ref[0])
    y = jnp.dot(h.astype(BF16), w_ref[...], preferred_element_type=F32)
    q = y[:, 0:C_WIDTH]
    k = y[:, C_WIDTH:C_WIDTH + C_KV_WIDTH]
    v = y[:, C_WIDTH + C_KV_WIDTH:C_WIDTH + 2 * C_KV_WIDTH]
    gate = y[:, C_WIDTH + 2 * C_KV_WIDTH:C_WIDTH + 2 * C_KV_WIDTH + D_WIDTH]
    xr = y[:, C_WIDTH + 2 * C_KV_WIDTH + D_WIDTH:]
    gg_ref[...] = jax.nn.gelu(gate)
    xr_ref[...] = xr
    cos_t = cos_ref[...]
    sin_t = sin_ref[...]
    q_t = q.T
    k_t = k.T
    zeros = jnp.zeros((HEAD_DIM, tm), F32)
    for j in range(C_HEADS):
        sl = slice(j * HEAD_DIM, (j + 1) * HEAD_DIM)
        qh = _head_norm_rope_t(q_t[sl], qg_ref[...], cos_t, sin_t) * QUERY_SCALE
        blk = jnp.concatenate([qh, zeros] if j // C_GROUP == 0 else [zeros, qh], axis=0)
        qt_ref[j] = blk.astype(BF16)
    k_parts = [_head_norm_rope_t(k_t[j * HEAD_DIM:(j + 1) * HEAD_DIM], kg_ref[...], cos_t, sin_t)
               for j in range(C_KV_HEADS)]
    k_ref[...] = jnp.concatenate(k_parts, axis=0).T.astype(BF16)
    v_t = v.T.astype(BF16)
    for c in range(tm // KV_BLOCK):
        vt_ref[c] = v_t[:, c * KV_BLOCK:(c + 1) * KV_BLOCK]


def _projection(even, x2, scp, sh, ng, w, gains, cos_t, sin_t, tm):
    r = x2.shape[0]
    nt = r // tm
    tiles_per_group = nt // scp.shape[0]
    pos_tiles = cos_t.shape[1] // tm
    n_in = w.shape[1]
    row = lambda i: (i, 0)
    grp = lambda i: (i // tiles_per_group, 0, 0)
    fixed = lambda i: (0, 0)
    pos = lambda i: (0, i % pos_tiles)
    in_specs = [
        pl.BlockSpec((tm, D_MODEL), row),
        pl.BlockSpec((1, 1, D_MODEL), grp),
        pl.BlockSpec((1, 1, D_MODEL), grp),
        pl.BlockSpec((1, D_MODEL), fixed),
        pl.BlockSpec((D_MODEL, n_in), fixed),
    ]
    in_specs += [pl.BlockSpec(g.shape, fixed) for g in gains]
    in_specs += [pl.BlockSpec((HEAD_DIM, tm), pos), pl.BlockSpec((HEAD_DIM, tm), pos)]
    if even:
        kw, vw, nh = B_WIDTH, B_WIDTH, 2 * B_HEADS
        body = functools.partial(_proj_even_kernel, tm=tm)
        out_shape = [
            jax.ShapeDtypeStruct((r, A_WIDTH), F32),
            jax.ShapeDtypeStruct((r, A_WIDTH), BF16),
            jax.ShapeDtypeStruct((nh, LANES, r), BF16),
            jax.ShapeDtypeStruct((r, kw), BF16),
            jax.ShapeDtypeStruct((r // KV_BLOCK, vw, KV_BLOCK), BF16),
        ]
        out_specs = [
            pl.BlockSpec((tm, A_WIDTH), row),
            pl.BlockSpec((tm, A_WIDTH), row),
            pl.BlockSpec((nh, LANES, tm), lambda i: (0, 0, i)),
            pl.BlockSpec((tm, kw), row),
            pl.BlockSpec((tm // KV_BLOCK, vw, KV_BLOCK), lambda i: (i, 0, 0)),
        ]
    else:
        kw, vw, nh = C_KV_WIDTH, C_KV_WIDTH, C_HEADS
        body = functools.partial(_proj_odd_kernel, tm=tm)
        out_shape = [
            jax.ShapeDtypeStruct((nh, LANES, r), BF16),
            jax.ShapeDtypeStruct((r, kw), BF16),
            jax.ShapeDtypeStruct((r // KV_BLOCK, vw, KV_BLOCK), BF16),
            jax.ShapeDtypeStruct((r, D_WIDTH), F32),
            jax.ShapeDtypeStruct((r, D_WIDTH), F32),
        ]
        out_specs = [
            pl.BlockSpec((nh, LANES, tm), lambda i: (0, 0, i)),
            pl.BlockSpec((tm, kw), row),
            pl.BlockSpec((tm // KV_BLOCK, vw, KV_BLOCK), lambda i: (i, 0, 0)),
            pl.BlockSpec((tm, D_WIDTH), row),
            pl.BlockSpec((tm, D_WIDTH), row),
        ]
    return pl.pallas_call(
        body, grid=(nt,), in_specs=in_specs, out_specs=out_specs, out_shape=out_shape,
        compiler_params=_cparams(("arbitrary",)),
        name="proj_even" if even else "proj_odd",
    )(x2, scp, sh, ng, w, *gains, cos_t, sin_t)


def _running_maximum(q, kc_ref, s_ref, p_ref, acc_ref, ones, k_lat, vt_lat, vt_ctx, n_lat, tk):
    def scores(kb, slot):
        n = kb.shape[0]
        mb = []
        for c in range(2):
            s = jnp.dot(kb, q[c], preferred_element_type=F32)
            s_ref[slot, c, 0:n] = s
            mb.append(jnp.max(s, axis=0, keepdims=True))
        return tuple(mb)

    def softmax(slot, n, mb, m):
        out, alphas = [], []
        for c in range(2):
            m_new = jnp.maximum(m[c], mb[c])
            p_ref[slot, c, 0:n] = jnp.exp2(s_ref[slot, c, 0:n] - m_new).astype(BF16)
            alphas.append(jnp.exp2(m[c] - m_new))
            out.append(m_new)
        return tuple(out), tuple(alphas)

    def values(vt_blocks, slot, alphas):
        for c in range(2):
            upd = alphas[c] * acc_ref[c]
            for i, vtb in enumerate(vt_blocks):
                lhs = jnp.concatenate([vtb, ones], axis=0)
                upd = upd + jnp.dot(lhs, p_ref[slot, c, i * KV_BLOCK:(i + 1) * KV_BLOCK],
                                    preferred_element_type=F32)
            acc_ref[c] = upd

    lc = kc_ref.shape[0]
    tq = q[0].shape[-1]
    neg = jnp.full((1, tq), -jnp.inf, F32)
    m = (neg, neg)
    mb = scores(kc_ref[...], 0)
    if not n_lat:
        m, al = softmax(0, lc, mb, m)
        values(vt_ctx, 0, al)
    else:
        assert n_lat % 2 == 0
        mb_next = scores(k_lat(0), 1)
        m, al = softmax(0, lc, mb, m)
        mb = mb_next
        mb_next = scores(k_lat(1), 0)
        m, al_next = softmax(1, tk, mb, m)
        values(vt_ctx, 0, al)
        mb, al = mb_next, al_next

        def body(i, carry):
            m, mb, al = carry
            t = 2 * i + 2
            mb_next = scores(k_lat(t), 1)
            m, al_next = softmax(0, tk, mb, m)
            values(vt_lat(t - 2), 1, al)
            mb, al = mb_next, al_next
            mb_next = scores(k_lat(t + 1), 0)
            m, al_next = softmax(1, tk, mb, m)
            values(vt_lat(t - 1), 0, al)
            return m, mb_next, al_next

        m, mb, al = lax.fori_loop(0, (n_lat - 2) // 2, body, (m, mb, al))
        m, al_next = softmax(0, tk, mb, m)
        values(vt_lat(n_lat - 2), 1, al)
        values(vt_lat(n_lat - 1), 0, al_next)


def _attn_kernel(par_ref, *refs, mode, n_lat, tk, lam_init):
    qt_ref, kc_ref, vtc_ref = refs[:3]
    pos = 3
    if n_lat:
        kl_ref, vtl_ref = refs[3:5]
        pos = 5
    if mode == "diff":
        lamp_ref, sg_ref = refs[pos:pos + 2]
        pos += 2
    o_ref, acc_ref, s_ref, p_ref = refs[pos:pos + 4]
    tq = qt_ref.shape[-1]
    q = (qt_ref[0], qt_ref[1])
    acc_ref[...] = jnp.zeros(acc_ref.shape, F32)

    lc = kc_ref.shape[0]
    ones = jnp.ones((DEN_ROWS, KV_BLOCK), BF16)
    per = tk // KV_BLOCK

    def k_lat(j):
        return kl_ref[pl.ds(pl.multiple_of(j * tk, tk), tk), :]

    def vt_lat(j):
        return [vtl_ref[j * per + i] for i in range(per)]

    vt_ctx = [vtc_ref[i] for i in range(lc // KV_BLOCK)]

    def fixed_reference():
        bound = par_ref[0]

        def numer(kb, slot, base):
            n = kb.shape[0]
            for c in range(2):
                s = jnp.dot(kb, q[c], preferred_element_type=F32)
                p_ref[slot, c, base:base + n] = jnp.exp2(s - bound).astype(BF16)

        def accum(vt_blocks, slot, n):
            lhs = jnp.concatenate([jnp.concatenate(vt_blocks, axis=1), jnp.ones((DEN_ROWS, n), BF16)], axis=0)
            for c in range(2):
                acc_ref[c] += jnp.dot(lhs, p_ref[slot, c, 0:n], preferred_element_type=F32)

        def numer_pair(j, slot):
            numer(k_lat(2 * j), slot, 0)
            numer(k_lat(2 * j + 1), slot, tk)

        def accum_pair(j, slot):
            accum(vt_lat(2 * j) + vt_lat(2 * j + 1), slot, 2 * tk)

        numer(kc_ref[...], 1, 0)
        if not n_lat:
            accum(vt_ctx, 1, lc)
            return
        n_pairs = n_lat // 2
        assert n_lat % 4 == 0
        numer_pair(0, 0)
        accum(vt_ctx, 1, lc)

        def body(j, _):
            numer_pair(2 * j + 1, 1)
            accum_pair(2 * j, 0)
            numer_pair(2 * j + 2, 0)
            accum_pair(2 * j + 1, 1)
            return 0

        lax.fori_loop(0, n_pairs // 2 - 1, body, 0)
        numer_pair(n_pairs - 1, 1)
        accum_pair(n_pairs - 2, 0)
        accum_pair(n_pairs - 1, 1)

    def running_maximum():
        _running_maximum(q, kc_ref, s_ref, p_ref, acc_ref, ones, k_lat, vt_lat, vt_ctx, n_lat, tk)

    use_fixed = par_ref[1] > 0.5
    pl.when(use_fixed)(fixed_reference)
    pl.when(jnp.logical_not(use_fixed))(running_maximum)

    vrows = vtc_ref.shape[1]
    l0 = acc_ref[0, vrows:vrows + 1, :]
    l1 = acc_ref[1, vrows:vrows + 1, :]
    if mode == "diff":
        lp = lamp_ref[...]
        lam = (jnp.exp(jnp.sum(lp[0:1] * lp[1:2], axis=-1, keepdims=True))
               - jnp.exp(jnp.sum(lp[2:3] * lp[3:4], axis=-1, keepdims=True)) + lam_init)
        o = acc_ref[0, 0:LANES, :] * (1.0 / l0) - lam * (acc_ref[1, 0:LANES, :] * (1.0 / l1))
        o = o * lax.rsqrt(jnp.mean(o * o, axis=0, keepdims=True) + EPS) * sg_ref[...] * (1.0 - lam_init)
    else:
        o = jnp.concatenate([acc_ref[0, 0:HEAD_DIM, :] * (1.0 / l0),
                             acc_ref[1, 0:HEAD_DIM, :] * (1.0 / l1)], axis=0)
    o_ref[...] = o.T.astype(o_ref.dtype)


def _attention(mode, par, qt, kc, vtc, kl, vtl, extra, batch, tq, tk, lam_init=0.0):
    units = qt.shape[0] // 2
    rq = qt.shape[2]
    nq = rq // batch // tq
    lc = kc.shape[0] // batch
    col = (lambda u: u) if mode == "diff" else (lambda u: 0)
    vrows = LANES if mode == "diff" else HEAD_DIM
    vcol = (lambda u: u) if mode == "diff" else (lambda u: u // (C_GROUP // 2))
    in_specs = [
        pl.BlockSpec(memory_space=pltpu.SMEM),
        pl.BlockSpec((2, LANES, tq), lambda b, u, i: (u, 0, b * nq + i)),
        pl.BlockSpec((lc, LANES), lambda b, u, i: (b, col(u))),
        pl.BlockSpec((lc // KV_BLOCK, vrows, KV_BLOCK), lambda b, u, i: (b, vcol(u), 0)),
    ]
    args = [par, qt, kc, vtc]
    n_lat = 0
    if kl is not None:
        ll = kl.shape[0] // batch
        n_lat = ll // tk
        in_specs += [
            pl.BlockSpec((ll, LANES), lambda b, u, i: (b, col(u))),
            pl.BlockSpec((ll // KV_BLOCK, vrows, KV_BLOCK), lambda b, u, i: (b, vcol(u), 0)),
        ]
        args += [kl, vtl]
    for e in extra:
        in_specs.append(pl.BlockSpec(e.shape, lambda b, u, i: (0, 0)))
        args.append(e)
    rows = max(tk, lc)
    return pl.pallas_call(
        functools.partial(_attn_kernel, mode=mode, n_lat=n_lat, tk=tk, lam_init=lam_init),
        grid=(batch, units, nq),
        in_specs=in_specs,
        out_specs=pl.BlockSpec((tq, LANES), lambda b, u, i: (b * nq + i, u)),
        out_shape=jax.ShapeDtypeStruct((rq, units * LANES), BF16),
        scratch_shapes=[pltpu.VMEM((2, vrows + DEN_ROWS, tq), F32),
                        pltpu.VMEM((2, 2, rows, tq), F32),
                        pltpu.VMEM((2, 2, max(2 * tk, lc), tq), BF16)],
        compiler_params=_cparams(("arbitrary", "arbitrary", "arbitrary")),
        name="attn_" + mode,
    )(*args)


def _gmlp_rows(u_ref, vn_ref, ws_ref, bias_ref, o_ref):
    lane = lax.broadcasted_iota(jnp.int32, (CHUNK, LANES), 1)
    for c in range(u_ref.shape[0] // CHUNK):
        rows = slice(c * CHUNK, (c + 1) * CHUNK)
        for j in range(A_GROUPS // 2):
            cols = slice(j * LANES, (j + 1) * LANES)
            rhs = vn_ref[rows, cols]
            lo = jnp.dot(ws_ref[2 * j], rhs, preferred_element_type=F32)
            hi = jnp.dot(ws_ref[2 * j + 1], rhs, preferred_element_type=F32)
            mixed = jnp.where(lane < HEAD_DIM, lo, hi) + bias_ref[:, cols]
            o_ref[rows, cols] = (u_ref[rows, cols] * mixed).astype(o_ref.dtype)


def _out_mix(x, ya, yb, w_ref, g):
    half = ya.shape[1]
    y = (jnp.dot(ya, w_ref[0:half, :], preferred_element_type=F32)
         + jnp.dot(yb, w_ref[half:, :], preferred_element_type=F32))
    return x + g * y


def _out_even_kernel(x_ref, u_ref, vn_ref, ws_ref, bias_ref, yb_ref, w_ref, g_ref, o_ref, ya_ref):
    _gmlp_rows(u_ref, vn_ref, ws_ref, bias_ref, ya_ref)
    o_ref[...] = _out_mix(x_ref[...], ya_ref[...], yb_ref[...], w_ref, g_ref[0])


def _out_proj_even(x2, u, vn, ws, bias_map, yb, w, g, tm):
    r = x2.shape[0]
    tiles_per_group = (r // tm) // g.shape[0]
    row = lambda i: (i, 0)
    return pl.pallas_call(
        _out_even_kernel,
        grid=(r // tm,),
        in_specs=[
            pl.BlockSpec((tm, D_MODEL), row),
            pl.BlockSpec((tm, A_WIDTH), row),
            pl.BlockSpec((tm, A_WIDTH), row),
            pl.BlockSpec(ws.shape, lambda i: (0, 0, 0)),
            pl.BlockSpec(bias_map.shape, lambda i: (0, 0)),
            pl.BlockSpec((tm, yb.shape[1]), row),
            pl.BlockSpec(w.shape, lambda i: (0, 0)),
            pl.BlockSpec((1, 1, D_MODEL), lambda i: (i // tiles_per_group, 0, 0)),
        ],
        out_specs=pl.BlockSpec((tm, D_MODEL), row),
        out_shape=jax.ShapeDtypeStruct((r, D_MODEL), F32),
        scratch_shapes=[pltpu.VMEM((tm, A_WIDTH), BF16)],
        compiler_params=_cparams(("arbitrary",)),
        name="out_proj_even",
    )(x2, u, vn, ws, bias_map, yb, w, g)


def _swiglu(h, w1, w3, w2):
    a = jnp.dot(h, w1, preferred_element_type=F32)
    b = jnp.dot(h, w3, preferred_element_type=F32)
    t = (a * jax.nn.sigmoid(a)) * b
    return jnp.dot(t.astype(BF16), w2, preferred_element_type=F32)


def _ffn_kernel(x_ref, scp_ref, sh_ref, g_ref, ng_ref, w1_ref, w3_ref, w2_ref, *rest):
    o_ref = rest[-1]
    n_cast = len(rest) // 2
    for src, dst in zip(rest[:n_cast], rest[n_cast:2 * n_cast]):
        dst[...] = src[...].astype(BF16)
    x = x_ref[...]
    h = _modnorm(x, ng_ref[...], scp_ref[0], sh_ref[0]).astype(BF16)
    o_ref[...] = x + g_ref[0] * _swiglu(h, w1_ref[...], w3_ref[...], w2_ref[...])


def _ffn(x2, scp, sh, g, ng, w1, w3, w2, tm, casts=()):
    r = x2.shape[0]
    tiles_per_group = (r // tm) // g.shape[0]
    row = lambda i: (i, 0)
    grp = lambda i: (i // tiles_per_group, 0, 0)
    fixed = lambda i: (0, 0)
    resident = lambda w: pl.BlockSpec(w.shape, fixed, pipeline_mode=pl.Buffered(1))
    in_specs = [
        pl.BlockSpec((tm, D_MODEL), row),
        pl.BlockSpec((1, 1, D_MODEL), grp),
        pl.BlockSpec((1, 1, D_MODEL), grp),
        pl.BlockSpec((1, 1, D_MODEL), grp),
        pl.BlockSpec((1, D_MODEL), fixed),
        resident(w1), resident(w3), resident(w2),
    ]
    out_specs = [pl.BlockSpec((tm, D_MODEL), row)]
    out_shape = [jax.ShapeDtypeStruct((r, D_MODEL), F32)]
    args = [x2, scp, sh, g, ng, w1, w3, w2]
    for n, c in enumerate(casts):
        cin, cout, cshape = _side_cast_specs(c, r // tm, row)
        in_specs.append(cin)
        out_specs.insert(n, cout)
        out_shape.insert(n, cshape)
        args.append(c)
    out = pl.pallas_call(
        _ffn_kernel, grid=(r // tm,), in_specs=in_specs, out_specs=out_specs, out_shape=out_shape,
        compiler_params=_cparams(("arbitrary",)),
        name="ffn",
    )(*args)
    return out if casts else out[0]


META_E1, META_E2, META_G1, META_G2, META_R1, META_R2 = range(6)


def _lane_col(x, lane, k):
    return jnp.sum(jnp.where(lane == k, x, 0.0), axis=-1, keepdims=True)


def _router_kernel(x_ref, ya_ref, s1_ref, s2_ref, agg_ref, hb_ref, wo_ref, g1_ref, scp_ref, sh_ref, ng_ref,
                   whi_ref, wlo_ref, b_ref, x1_ref, hn_ref, meta_ref, cnt_ref, carry_ref, rev_ref):
    n_tiles = agg_ref.shape[0]
    batch = hb_ref.shape[0]

    @pl.when(pl.program_id(0) == 0)
    def _():
        carry_ref[...] = jnp.zeros(carry_ref.shape, F32)
        per = n_tiles // batch
        for b in range(batch):
            c = hb_ref[b, 0:1, :]
            for i in range(per - 1, -1, -1):
                t = b * per + i
                rev_ref[t] = c
                c = agg_ref[t, 1:2, :] + agg_ref[t, 0:1, :] * c

    yb = (s1_ref[...] + s2_ref[...] * rev_ref[pl.program_id(0)]).astype(BF16)
    x1 = _out_mix(x_ref[...], ya_ref[...], yb, wo_ref, g1_ref[0])
    x1_ref[...] = x1
    h = _modnorm(x1, ng_ref[...], scp_ref[0], sh_ref[0])
    hn_ref[...] = h
    h_hi = h.astype(BF16)
    h_lo = (h - h_hi.astype(F32)).astype(BF16)
    logits = (jnp.dot(h_hi, whi_ref[...], preferred_element_type=F32)
              + jnp.dot(h_lo, whi_ref[...], preferred_element_type=F32)
              + jnp.dot(h_hi, wlo_ref[...], preferred_element_type=F32)) + b_ref[...]
    tm = logits.shape[0]
    lane = lax.broadcasted_iota(jnp.int32, logits.shape, 1).astype(F32)
    logits = jnp.where(lane < N_EXPERTS, logits, -jnp.inf)
    m1 = jnp.max(logits, axis=-1, keepdims=True)
    i1 = jnp.min(jnp.where(logits == m1, lane, float(LANES)), axis=-1, keepdims=True)
    rest = jnp.where(lane == i1, -jnp.inf, logits)
    m2 = jnp.max(rest, axis=-1, keepdims=True)
    i2 = jnp.min(jnp.where(rest == m2, lane, float(LANES)), axis=-1, keepdims=True)
    e2 = jnp.exp(m2 - m1)
    den = 1.0 + e2
    oh1 = lane == i1
    oh2 = lane == i2
    both = jnp.where(jnp.logical_or(oh1, oh2), 1.0, 0.0)
    tri = (lax.broadcasted_iota(jnp.int32, (tm, tm), 1) < lax.broadcasted_iota(jnp.int32, (tm, tm), 0))
    before = jnp.dot(jnp.where(tri, 1.0, 0.0).astype(BF16), both.astype(BF16),
                     preferred_element_type=F32) + carry_ref[0:1, :]
    r1 = jnp.sum(jnp.where(oh1, before, 0.0), axis=-1, keepdims=True)
    r2 = jnp.sum(jnp.where(oh2, before, 0.0), axis=-1, keepdims=True)
    meta = jnp.zeros(logits.shape, F32)
    for k, v in ((META_E1, i1), (META_E2, i2), (META_G1, 1.0 / den), (META_G2, e2 / den),
                 (META_R1, r1), (META_R2, r2)):
        meta = jnp.where(lane == k, v, meta)
    meta_ref[...] = meta
    carry_ref[...] = carry_ref[...] + jnp.sum(both, axis=0, keepdims=True)
    cnt_ref[...] = carry_ref[...]


def _router(x2, ya, s1, s2, agg, h_rev, w_out, g1, scp, sh, ng, w_hi, w_lo, b, tm):
    r = x2.shape[0]
    assert agg.shape[0] == r // tm
    tiles_per_group = (r // tm) // scp.shape[0]
    row = lambda i: (i, 0)
    grp = lambda i: (i // tiles_per_group, 0, 0)
    fixed = lambda i: (0, 0)
    return pl.pallas_call(
        _router_kernel,
        grid=(r // tm,),
        in_specs=[
            pl.BlockSpec((tm, D_MODEL), row),
            pl.BlockSpec((tm, ya.shape[1]), row),
            pl.BlockSpec((tm, D_WIDTH), row),
            pl.BlockSpec((tm, D_WIDTH), row),
            pl.BlockSpec(agg.shape, lambda i: (0, 0, 0)),
            pl.BlockSpec(h_rev.shape, lambda i: (0, 0, 0)),
            pl.BlockSpec(w_out.shape, fixed),
            pl.BlockSpec((1, 1, D_MODEL), grp),
            pl.BlockSpec((1, 1, D_MODEL), grp),
            pl.BlockSpec((1, 1, D_MODEL), grp),
            pl.BlockSpec((1, D_MODEL), fixed),
            pl.BlockSpec(w_hi.shape, fixed),
            pl.BlockSpec(w_lo.shape, fixed),
            pl.BlockSpec(b.shape, fixed),
        ],
        out_specs=[
            pl.BlockSpec((tm, D_MODEL), row),
            pl.BlockSpec((tm, D_MODEL), row),
            pl.BlockSpec((tm, LANES), row),
            pl.BlockSpec((SUBLANES, LANES), fixed),
        ],
        out_shape=[
            jax.ShapeDtypeStruct((r, D_MODEL), F32),
            jax.ShapeDtypeStruct((r, D_MODEL), F32),
            jax.ShapeDtypeStruct((r, LANES), F32),
            jax.ShapeDtypeStruct((SUBLANES, LANES), F32),
        ],
        scratch_shapes=[pltpu.VMEM((SUBLANES, LANES), F32), pltpu.VMEM((agg.shape[0], 1, D_WIDTH), F32)],
        compiler_params=_cparams(("arbitrary",)),
        name="router",
    )(x2, ya, s1, s2, agg, h_rev, w_out, g1, scp, sh, ng, w_hi, w_lo, b)


def _row_copy(src_ref, src_row, dst_ref, dst_row, sem):
    return pltpu.make_async_copy(src_ref.at[pl.ds(src_row, 1)], dst_ref.at[pl.ds(dst_row, 1)], sem)


def _dispatch_kernel(pos_ref, pb_ref, pn_ref, nu_ref, h_ref, xs_ref, zero_ref, sem, zsem):
    tm = h_ref.shape[0]
    tile = zero_ref.shape[0]
    n_tiles = xs_ref.shape[0] // tile
    base = pl.program_id(0) * (2 * tm)

    def zero_fill(op):
        for e in range(N_EXPERTS):
            first = pb_ref[e]
            head = (-first) & (SUBLANES - 1)
            for j in range(SUBLANES - 1):
                @pl.when(j < head)
                def _():
                    getattr(_row_copy(zero_ref, 0, xs_ref, first + j, zsem), op)()
            n = pn_ref[e] - head
            for bit in range(tile.bit_length() - 2, SUBLANES.bit_length() - 2, -1):
                size = 1 << bit
                off = pl.multiple_of(first + head + n - (n & (2 * size - 1)), SUBLANES)

                @pl.when((n & size) != 0)
                def _():
                    cp = pltpu.make_async_copy(zero_ref.at[pl.ds(0, size)], xs_ref.at[pl.ds(off, size)], zsem)
                    getattr(cp, op)()
        for j in range(N_EXPERTS):
            t = nu_ref[0] + j

            @pl.when(t < n_tiles)
            def _():
                cp = pltpu.make_async_copy(zero_ref, xs_ref.at[pl.ds(pl.multiple_of(t * tile, tile), tile)], zsem)
                getattr(cp, op)()

    @pl.when(pl.program_id(0) == 0)
    def _():
        zero_ref[...] = jnp.zeros(zero_ref.shape, F32)
        zero_fill("start")

    def issue(r, _):
        for k in range(2):
            _row_copy(h_ref, r, xs_ref, pos_ref[base + 2 * r + k], sem).start()
        return 0

    lax.fori_loop(0, tm, issue, 0, unroll=8)
    for k in range(2):
        pltpu.make_async_copy(h_ref, xs_ref.at[pl.ds(0, tm)], sem).wait()

    @pl.when(pl.program_id(0) == 0)
    def _():
        zero_fill("wait")


def _dispatch(pos_flat, pad_base, pad_n, n_used, hn, n_rows, tm, tile):
    r = hn.shape[0]
    return pl.pallas_call(
        _dispatch_kernel,
        grid_spec=pltpu.PrefetchScalarGridSpec(
            num_scalar_prefetch=4,
            grid=(r // tm,),
            in_specs=[pl.BlockSpec((tm, D_MODEL), lambda i, *_: (i, 0))],
            out_specs=pl.BlockSpec(memory_space=pl.ANY),
            scratch_shapes=[pltpu.VMEM((tile, D_MODEL), F32), pltpu.SemaphoreType.DMA(()),
                            pltpu.SemaphoreType.DMA(())],
        ),
        out_shape=jax.ShapeDtypeStruct((n_rows, D_MODEL), F32),
        compiler_params=_cparams(("arbitrary",)),
        name="dispatch",
    )(pos_flat, pad_base, pad_n, n_used, hn)


def _expert_kernel(te_ref, nu_ref, xs_ref, w1_ref, w3_ref, w2_ref, o_ref):
    del te_ref

    @pl.when(pl.program_id(0) < nu_ref[0])
    def _():
        o_ref[...] = _swiglu(xs_ref[...].astype(BF16), w1_ref[0], w3_ref[0], w2_ref[0])

    @pl.when(pl.program_id(0) >= nu_ref[0])
    def _():
        o_ref[...] = jnp.zeros(o_ref.shape, F32)


def _experts(tile_expert, n_used, xs, w1, w3, w2, tm):
    n_tiles = xs.shape[0] // tm
    wspec = lambda w: pl.BlockSpec((1,) + w.shape[1:], lambda i, te, nu: (te[i], 0, 0))
    return pl.pallas_call(
        _expert_kernel,
        grid_spec=pltpu.PrefetchScalarGridSpec(
            num_scalar_prefetch=2,
            grid=(n_tiles,),
            in_specs=[
                pl.BlockSpec((tm, D_MODEL), lambda i, te, nu: (jnp.minimum(i, nu[0] - 1), 0)),
                wspec(w1), wspec(w3), wspec(w2),
            ],
            out_specs=pl.BlockSpec((tm, D_MODEL), lambda i, te, nu: (i, 0)),
        ),
        out_shape=jax.ShapeDtypeStruct(xs.shape, F32),
        compiler_params=_cparams(("arbitrary",)),
        name="experts",
    )(tile_expert, n_used, xs, w1, w3, w2)


def _combine_kernel(pos_ref, x_ref, meta_ref, g_ref, ys_ref, o_ref, buf_ref, sem):
    tm = x_ref.shape[0]
    i = pl.program_id(0)

    def gather(tile, slot):
        base = tile * (2 * tm)

        def issue(r, _):
            for k in range(2):
                _row_copy(ys_ref, pos_ref[base + 2 * r + k], buf_ref.at[slot, k], r, sem.at[slot]).start()
            return 0

        lax.fori_loop(0, tm, issue, 0, unroll=8)

    @pl.when(i == 0)
    def _():
        gather(0, 0)

    slot = i % 2

    @pl.when(i + 1 < pl.num_programs(0))
    def _():
        gather(i + 1, 1 - slot)

    for k in range(2):
        pltpu.make_async_copy(ys_ref.at[pl.ds(0, tm)], buf_ref.at[slot, k], sem.at[slot]).wait()
    meta = meta_ref[...]
    lane = lax.broadcasted_iota(jnp.int32, meta.shape, 1)
    y = _lane_col(meta, lane, META_G1) * buf_ref[slot, 0] + _lane_col(meta, lane, META_G2) * buf_ref[slot, 1]
    o_ref[...] = x_ref[...] + g_ref[0] * y


def _combine(pos_flat, x2, meta, g, ys, tm):
    r = x2.shape[0]
    tiles_per_group = (r // tm) // g.shape[0]
    return pl.pallas_call(
        _combine_kernel,
        grid_spec=pltpu.PrefetchScalarGridSpec(
            num_scalar_prefetch=1,
            grid=(r // tm,),
            in_specs=[
                pl.BlockSpec((tm, D_MODEL), lambda i, pos: (i, 0)),
                pl.BlockSpec((tm, LANES), lambda i, pos: (i, 0)),
                pl.BlockSpec((1, 1, D_MODEL), lambda i, pos: (i // tiles_per_group, 0, 0)),
                pl.BlockSpec(memory_space=pl.ANY),
            ],
            out_specs=pl.BlockSpec((tm, D_MODEL), lambda i, pos: (i, 0)),
            scratch_shapes=[pltpu.VMEM((2, 2, tm, D_MODEL), F32), pltpu.SemaphoreType.DMA((2,))],
        ),
        out_shape=jax.ShapeDtypeStruct((r, D_MODEL), F32),
        compiler_params=_cparams(("arbitrary",)),
        name="combine",
    )(pos_flat, x2, meta, g, ys)


def _routing_plan(meta, counts, tile):
    n = meta.shape[0]
    n_tiles = 2 * n // tile + N_EXPERTS
    e = meta[:, META_E1:META_E2 + 1].astype(jnp.int32)
    rank = meta[:, META_R1:META_R2 + 1].astype(jnp.int32)
    cnt = counts[0, :N_EXPERTS].astype(jnp.int32)
    tiles_e = (cnt + tile - 1) // tile
    ends = jnp.cumsum(tiles_e)
    starts = (ends - tiles_e) * tile
    pos = (starts[e] + rank).reshape(-1)
    n_used = ends[-1:]
    t_idx = jnp.minimum(jnp.arange(n_tiles, dtype=jnp.int32), n_used[0] - 1)
    tile_expert = jnp.sum((t_idx[:, None] >= ends[None, :]).astype(jnp.int32), axis=1)
    return (pos, tile_expert.astype(jnp.int32), n_used.astype(jnp.int32), (starts + cnt).astype(jnp.int32),
            (tiles_e * tile - cnt).astype(jnp.int32), n_tiles * tile)


LRU_TILE = 512
LRU_CHUNKS = D_WIDTH // LANES
LRU_PAD = SUBLANES


def _lru_pitch(tile):
    seg = tile // SUBLANES
    assert (seg // SUBLANES) % 2 == 0
    return seg, seg + LRU_PAD


def _tile_scans(af_ref, bf_ref, ab_ref, bb_ref, seg, pitch, carry_f):
    nc = LRU_CHUNKS

    def rows_f(t):
        return pl.ds(t, SUBLANES, stride=pitch)

    def rows_b(t):
        return pl.ds(seg - 1 - t, SUBLANES, stride=pitch)

    def ends(t, c):
        hf, pf, hb, pb = c
        rf, rb = rows_f(t), rows_b(t)
        nhf, npf, nhb, npb = [], [], [], []
        for cc in range(nc):
            a = af_ref[cc, rf, :]
            nhf.append(a * hf[cc] + bf_ref[cc, rf, :])
            npf.append(a * pf[cc])
            a = ab_ref[cc, rb, :]
            nhb.append(a * hb[cc] + bb_ref[cc, rb, :])
            npb.append(a * pb[cc])
        return tuple(nhf), tuple(npf), tuple(nhb), tuple(npb)

    z = tuple(jnp.zeros((SUBLANES, LANES), F32) for _ in range(nc))
    o = tuple(jnp.ones((SUBLANES, LANES), F32) for _ in range(nc))
    hf, pf, hb, pb = lax.fori_loop(0, seg, ends, (z, o, z, o), unroll=8)

    in_f, in_b, in_p, out_f, tile_p, tile_h = [], [], [], [], [], []
    for cc in range(nc):
        cin = carry_f[cc]
        rows = [None] * SUBLANES
        for s in range(SUBLANES):
            rows[s] = cin
            cin = hf[cc][s:s + 1] + pf[cc][s:s + 1] * cin
        in_f.append(jnp.concatenate(rows, axis=0))
        out_f.append(cin)
        cin = jnp.zeros((1, LANES), F32)
        pin = jnp.ones((1, LANES), F32)
        rows_h = [None] * SUBLANES
        rows_p = [None] * SUBLANES
        for s in range(SUBLANES - 1, -1, -1):
            rows_h[s] = cin
            rows_p[s] = pin
            cin = hb[cc][s:s + 1] + pb[cc][s:s + 1] * cin
            pin = pb[cc][s:s + 1] * pin
        in_b.append(jnp.concatenate(rows_h, axis=0))
        in_p.append(jnp.concatenate(rows_p, axis=0))
        tile_h.append(cin)
        tile_p.append(pin)

    def final(t, c):
        hf, hb, pb = c
        rf, rb = rows_f(t), rows_b(t)
        nhf, nhb, npb = [], [], []
        for cc in range(nc):
            h = af_ref[cc, rf, :] * hf[cc] + bf_ref[cc, rf, :]
            bf_ref[cc, rf, :] = h
            nhf.append(h)
            a = ab_ref[cc, rb, :]
            h = a * hb[cc] + bb_ref[cc, rb, :]
            p = a * pb[cc]
            bb_ref[cc, rb, :] = h
            ab_ref[cc, rb, :] = p
            nhb.append(h)
            npb.append(p)
        return tuple(nhf), tuple(nhb), tuple(npb)

    lax.fori_loop(0, seg, final, (tuple(in_f), tuple(in_b), tuple(in_p)), unroll=8)
    return out_f, tile_p, tile_h


def _lru_main_kernel(prev_ref, cur_ref, next_ref, gg_ref, cw_ref, cb_ref, w_ref, bias_ref, lam_ref, h0_ref, *rest):
    s1_ref, s2_ref, agg_ref, hl_ref, ext_ref, af_ref, bf_ref, ab_ref, bb_ref, st_ref = rest[-10:]
    if len(rest) == 12:
        rest[1][...] = rest[0][...].astype(BF16)
    i = pl.program_id(1)
    nt = pl.num_programs(1)
    t = cur_ref.shape[0]
    seg, pitch = _lru_pitch(t)
    pad = SUBLANES

    @pl.when(i == 0)
    def _():
        st_ref[...] = h0_ref[0]

    ext_ref[0:pad, :] = jnp.where(i == 0, 0.0, prev_ref[...])
    ext_ref[pad:pad + t, :] = cur_ref[...]
    ext_ref[pad + t:, :] = jnp.where(i == nt - 1, 0.0, next_ref[...])
    left = CONV_W // 2
    xd = cb_ref[...] + sum(cw_ref[j:j + 1, :] * ext_ref[pad + j - left:pad + j - left + t, :] for j in range(CONV_W))
    z = jnp.dot(xd.astype(BF16), w_ref[...], preferred_element_type=F32) + bias_ref[...]
    decay = LRU_C * jax.nn.softplus(-lam_ref[...])
    for d, (a_ref, b_ref) in enumerate(((af_ref, bf_ref), (ab_ref, bb_ref))):
        r = 0.5 * jnp.tanh(0.5 * z[:, (2 * d) * D_WIDTH:(2 * d + 1) * D_WIDTH]) + 0.5
        g = 0.5 * jnp.tanh(0.5 * z[:, (2 * d + 1) * D_WIDTH:(2 * d + 2) * D_WIDTH]) + 0.5
        nla = r * decay[:, d * D_WIDTH:(d + 1) * D_WIDTH]
        a = jnp.exp(-nla)
        m = jnp.tanh(nla) * (a * a + 1.0)
        b = jnp.where(m > 0.0, m * lax.rsqrt(m), 0.0) * (g * xd)
        for cc in range(LRU_CHUNKS):
            for s in range(SUBLANES):
                a_ref[cc, s * pitch:s * pitch + seg, :] = a[s * seg:(s + 1) * seg, cc * LANES:(cc + 1) * LANES]
                b_ref[cc, s * pitch:s * pitch + seg, :] = b[s * seg:(s + 1) * seg, cc * LANES:(cc + 1) * LANES]
    carry_f = [st_ref[0:1, cc * LANES:(cc + 1) * LANES] for cc in range(LRU_CHUNKS)]
    out_f, tile_p, tile_h = _tile_scans(af_ref, bf_ref, ab_ref, bb_ref, seg, pitch, carry_f)
    for cc in range(LRU_CHUNKS):
        cols = slice(cc * LANES, (cc + 1) * LANES)
        for s in range(SUBLANES):
            src = slice(s * pitch, s * pitch + seg)
            dst = slice(s * seg, (s + 1) * seg)
            gg = gg_ref[dst, cols]
            s1_ref[dst, cols] = (bf_ref[cc, src, :] + bb_ref[cc, src, :]) * gg
            s2_ref[dst, cols] = ab_ref[cc, src, :] * gg
    state = jnp.concatenate(out_f, axis=-1)
    st_ref[...] = jnp.broadcast_to(state, st_ref.shape)
    hl_ref[0] = jnp.broadcast_to(state, st_ref.shape)
    agg_ref[0] = jnp.concatenate([jnp.concatenate(tile_p, axis=-1), jnp.concatenate(tile_h, axis=-1),
                                  jnp.zeros((SUBLANES - 2, D_WIDTH), F32)], axis=0)


def _lru_main(xr, gg, cw, cb, w_cat, b_cat, lam_cat, h0, batch, tile, cast=None):
    r = xr.shape[0]
    nt = r // batch // tile
    sub = tile // SUBLANES
    _, pitch = _lru_pitch(tile)
    row = lambda b, i: (b * nt + i, 0)
    fixed = lambda b, i: (0, 0)
    state = pl.BlockSpec((1, SUBLANES, D_WIDTH), lambda b, i: (b, 0, 0))
    chunked = pltpu.VMEM((LRU_CHUNKS, SUBLANES * pitch, LANES), F32)
    in_specs = [
        pl.BlockSpec((SUBLANES, D_WIDTH), lambda b, i: ((b * nt + jnp.maximum(i, 1)) * sub - 1, 0)),
        pl.BlockSpec((tile, D_WIDTH), row),
        pl.BlockSpec((SUBLANES, D_WIDTH), lambda b, i: ((b * nt + jnp.minimum(i + 1, nt - 1)) * sub, 0)),
        pl.BlockSpec((tile, D_WIDTH), row),
        pl.BlockSpec(cw.shape, fixed),
        pl.BlockSpec(cb.shape, fixed),
        pl.BlockSpec(w_cat.shape, fixed),
        pl.BlockSpec(b_cat.shape, fixed),
        pl.BlockSpec(lam_cat.shape, fixed),
        state,
    ]
    out_specs = [
        pl.BlockSpec((tile, D_WIDTH), row),
        pl.BlockSpec((tile, D_WIDTH), row),
        pl.BlockSpec((1, SUBLANES, D_WIDTH), lambda b, i: (b * nt + i, 0, 0)),
        state,
    ]
    out_shape = [
        jax.ShapeDtypeStruct((r, D_WIDTH), F32),
        jax.ShapeDtypeStruct((r, D_WIDTH), F32),
        jax.ShapeDtypeStruct((batch * nt, SUBLANES, D_WIDTH), F32),
        jax.ShapeDtypeStruct((batch, SUBLANES, D_WIDTH), F32),
    ]
    args = [xr, xr, xr, gg, cw, cb, w_cat, b_cat, lam_cat, h0]
    if cast is not None:
        cin, cout, cshape = _side_cast_specs(cast, batch * nt, row)
        in_specs.append(cin)
        out_specs.insert(0, cout)
        out_shape.insert(0, cshape)
        args.append(cast)
    return pl.pallas_call(
        _lru_main_kernel,
        grid=(batch, nt),
        in_specs=in_specs,
        out_specs=out_specs,
        out_shape=out_shape,
        scratch_shapes=[pltpu.VMEM((tile + 2 * SUBLANES, D_WIDTH), F32), chunked, chunked, chunked, chunked,
                        pltpu.VMEM((SUBLANES, D_WIDTH), F32)],
        compiler_params=_cparams(("arbitrary", "arbitrary")),
        name="lru_main",
    )(*args)


def _score_bound(q_gain, k_gain):
    bound = 1.02 * QUERY_SCALE * HEAD_DIM * jnp.max(jnp.abs(q_gain)) * jnp.max(jnp.abs(k_gain))
    return jnp.stack([bound, (bound <= FIXED_REFERENCE_MAX).astype(F32)]).astype(F32)


def _rope_tables(n_lat):
    half = HEAD_DIM // 4
    freqs = ROPE_THETA ** (-jnp.arange(half, dtype=F32) / half)
    t = jnp.arange(n_lat, dtype=jnp.int32)
    row = (t // GRID_W).astype(F32)
    col = (t % GRID_W).astype(F32)
    ar = freqs[:, None] * row[None, :]
    ac = freqs[:, None] * col[None, :]
    cos_t = jnp.concatenate([jnp.cos(ar), jnp.cos(ar), jnp.cos(ac), jnp.cos(ac)], axis=0)
    sin_t = jnp.concatenate([-jnp.sin(ar), jnp.sin(ar), -jnp.sin(ac), jnp.sin(ac)], axis=0)
    return cos_t, sin_t


def _block_diag(w):
    nb, bs, _ = w.shape
    eye = jnp.eye(nb, dtype=w.dtype)
    return (eye[:, None, :, None] * w[:, :, None, :]).reshape(nb * bs, nb * bs)


def kernel(x, c, ctx, c_ctx, w_mod, b_mod, norm_mix_g, norm_ffn_g, w_out, e_w_in, a_norm_g, a_ws, a_bs, b_qnorm_g, b_knorm_g, b_lq1, b_lk1, b_lq2, b_lk2, b_subln_g, ffn_w1, ffn_w3, ffn_w2, o_w_in, c_qnorm_g, c_knorm_g, d_conv_w, d_conv_b, d_wa, d_ba, d_wx, d_bx, d_lambda, router_w, router_b, moe_w1, moe_w3, moe_w2):
    batch, n_lat, d = x.shape
    n_ctx = ctx.shape[1]
    depth = w_mod.shape[0]
    assert depth == 2 and d == D_MODEL and batch + 1 <= SUBLANES
    xl = x.reshape(batch * n_lat, d)
    xc = ctx.reshape(batch * n_ctx, d)

    cc = jnp.concatenate([c, c_ctx[None, :], jnp.zeros((SUBLANES - batch - 1, d), F32)], axis=0)
    mod = _modulation(cc, w_mod, b_mod)

    def mods(layer):
        parts = [mod[layer, :, k * d:(k + 1) * d] for k in range(6)]
        lat = [p[:batch, None, :] for p in parts]
        cx = [p[batch:batch + 1, None, :] for p in parts]
        return lat, cx

    cos_t, sin_t = _rope_tables(n_lat)
    cos_c = jnp.ones((HEAD_DIM, n_ctx), F32)
    sin_c = jnp.zeros((HEAD_DIM, n_ctx), F32)
    col = lambda v: v.reshape(-1, 1)
    rowv = lambda v: v.reshape(1, -1)
    tm = 512
    tm_c = 256

    (sh1, sc1, g1, sh2, sc2, g2), (sh1c, sc1c, g1c, sh2c, sc2c, g2c) = mods(0)
    lam_init = 0.8 - 0.6 * math.exp(-0.3 * 0)
    w_in = e_w_in[0].astype(BF16)
    gains = [rowv(a_norm_g[0]), col(b_qnorm_g[0]), col(b_knorm_g[0])]
    ng = rowv(norm_mix_g[0])
    u, vn, qt, k, vt = _projection(True, xl, 1.0 + sc1, sh1, ng, w_in, gains, cos_t, sin_t, tm)
    uc, vnc, qtc, kc, vtc = _projection(True, xc, 1.0 + sc1c, sh1c, ng, w_in, gains, cos_c, sin_c, tm_c)
    lamp = jnp.stack([b_lq1[0], b_lk1[0], b_lq2[0], b_lk2[0]], axis=0)
    extra = [lamp, col(b_subln_g[0])]
    par = _score_bound(b_qnorm_g[0], b_knorm_g[0])
    yb = _attention("diff", par, qt, kc, vtc, k, vt, extra, batch, ATTN_TQ, ATTN_TK, lam_init)
    ybc = _attention("diff", par, qtc, kc, vtc, None, None, extra, batch, n_ctx, ATTN_TK, lam_init)
    ws = a_ws[0].astype(BF16)
    bias_map = jnp.repeat(a_bs[0].T, HEAD_DIM, axis=1)
    wo = w_out[0].astype(BF16)
    xl = _out_proj_even(xl, u, vn, ws, bias_map, yb, wo, g1, tm)
    xc = _out_proj_even(xc, uc, vnc, ws, bias_map, ybc, wo, g1c, tm_c)
    w1 = ffn_w1[0].astype(BF16)
    w3 = ffn_w3[0].astype(BF16)
    w2 = ffn_w2[0].astype(BF16)
    ngf = rowv(norm_ffn_g[0])
    ne, _, fdim = moe_w1.shape[1:]
    m1, m3, xl = _ffn(xl, 1.0 + sc2, sh2, g2, ngf, w1, w3, w2, tm,
                      casts=(moe_w1[0].reshape(ne * d, fdim), moe_w3[0].reshape(ne * d, fdim)))
    xc = _ffn(xc, 1.0 + sc2c, sh2c, g2c, ngf, w1, w3, w2, tm_c)

    (sh1, sc1, g1, sh2, sc2, g2), (sh1c, sc1c, _, _, _, _) = mods(1)
    w_in = o_w_in[0].astype(BF16)
    gains = [col(c_qnorm_g[0]), col(c_knorm_g[0])]
    ng = rowv(norm_mix_g[1])
    qt, k, vt, gg, xr = _projection(False, xl, 1.0 + sc1, sh1, ng, w_in, gains, cos_t, sin_t, tm)
    _, kc, vtc, _, xrc = _projection(False, xc, 1.0 + sc1c, sh1c, ng, w_in, gains, cos_c, sin_c, tm_c)
    y_attn = _attention("gqa", _score_bound(c_qnorm_g[0], c_knorm_g[0]), qt, kc, vtc, k, vt, [], batch,
                        ATTN_TQ, ATTN_TK)

    w_cat = jnp.concatenate([_block_diag(d_wa[0, 0]), _block_diag(d_wx[0, 0]),
                             _block_diag(d_wa[0, 1]), _block_diag(d_wx[0, 1])], axis=1).astype(BF16)
    b_cat = jnp.concatenate([d_ba[0, 0], d_bx[0, 0], d_ba[0, 1], d_bx[0, 1]]).reshape(1, -1)
    lam_cat = d_lambda[0].reshape(1, -1)
    cw = d_conv_w[0]
    cb = rowv(d_conv_b[0])
    zero_state = jnp.zeros((batch, SUBLANES, D_WIDTH), F32)
    _, _, agg_c, st_f = _lru_main(xrc, xrc, cw, cb, w_cat, b_cat, lam_cat, zero_state, batch, n_ctx)
    st_b = jnp.broadcast_to(agg_c[:, 1:2, :], (batch, SUBLANES, D_WIDTH))
    m2, s1, s2, agg, _ = _lru_main(xr, gg, cw, cb, w_cat, b_cat, lam_cat, st_f, batch, LRU_TILE,
                                   cast=moe_w2[0].reshape(ne * fdim, d))

    ngf = rowv(norm_ffn_g[1])
    wr = jnp.pad(router_w[0], ((0, 0), (0, LANES - N_EXPERTS)))
    wr_hi = wr.astype(BF16)
    wr_lo = (wr - wr_hi.astype(F32)).astype(BF16)
    br = jnp.pad(router_b[0], (0, LANES - N_EXPERTS)).reshape(1, LANES)
    xl, hn, meta, counts = _router(xl, y_attn, s1, s2, agg, st_b, w_out[1].astype(BF16), g1, 1.0 + sc2, sh2, ngf,
                                   wr_hi, wr_lo, br, LRU_TILE)
    pos, tile_expert, n_used, pad_base, pad_n, n_rows = _routing_plan(meta, counts, EXPERT_TILE)
    xs = _dispatch(pos, pad_base, pad_n, n_used, hn, n_rows, tm, EXPERT_TILE)
    ys = _experts(tile_expert, n_used, xs, m1.reshape(ne, d, fdim), m3.reshape(ne, d, fdim),
                  m2.reshape(ne, fdim, d), EXPERT_TILE)
    xl = _combine(pos, xl, meta, g2, ys, COMBINE_TILE)
    return xl.reshape(batch, n_lat, d)
```

```python
import functools
import math

import jax
import jax.numpy as jnp
from jax import lax
from jax.experimental import pallas as pl
from jax.experimental.pallas import tpu as pltpu

F32 = jnp.float32
BF16 = jnp.bfloat16

D_MODEL = 1024
HEAD_DIM = 64
GRID_W = 64
ROPE_THETA = 10000.0
EPS = 1e-6
CHUNK = 128
A_WIDTH = D_MODEL // 2
A_GROUPS = A_WIDTH // HEAD_DIM
B_HEADS = D_MODEL // (4 * HEAD_DIM)
B_WIDTH = B_HEADS * 2 * HEAD_DIM
E_IN = 2 * A_WIDTH + 3 * B_WIDTH
C_HEADS = (D_MODEL // 2) // HEAD_DIM
C_KV_HEADS = 2
C_GROUP = C_HEADS // C_KV_HEADS
C_WIDTH = C_HEADS * HEAD_DIM
C_KV_WIDTH = C_KV_HEADS * HEAD_DIM
D_WIDTH = D_MODEL // 2
CONV_W = 4
LRU_C = 8.0
O_IN = C_WIDTH + 2 * C_KV_WIDTH + 2 * D_WIDTH
N_EXPERTS = 8

LANES = 128
SUBLANES = 8
KV_BLOCK = 256
ATTN_TQ = 2048
ATTN_TK = 512
EXPERT_TILE = 256
COMBINE_TILE = 512
FIXED_REFERENCE_MAX = 40.0
DEN_ROWS = 16
QUERY_SCALE =HEAD_DIM ** -0.5 * math.log2(math.e)
VMEM_LIMIT = 56 * 1024 * 1024


def _cparams(sem):
    return pltpu.CompilerParams(dimension_semantics=sem, vmem_limit_bytes=VMEM_LIMIT)


def _side_cast_specs(w, n_steps, index):
    rows, cols = w.shape
    blk = (rows // n_steps, cols)
    return pl.BlockSpec(blk, index), pl.BlockSpec(blk, index), jax.ShapeDtypeStruct(w.shape, BF16)


def _modnorm(x, g, scp, sh):
    y = x * lax.rsqrt(jnp.mean(x * x, axis=-1, keepdims=True) + EPS)
    return (y * g) * scp + sh


def _mod_kernel(c_ref, w_ref, b_ref, o_ref):
    c = c_ref[...]
    s = c * jax.nn.sigmoid(c)
    o_ref[0] = jnp.dot(s.astype(BF16), w_ref[0].astype(BF16), preferred_element_type=F32) + b_ref[0]


def _modulation(cc, w_mod, b_mod):
    depth, d, n = w_mod.shape
    tn = 1536
    return pl.pallas_call(
        _mod_kernel,
        grid=(depth, n // tn),
        in_specs=[
            pl.BlockSpec((SUBLANES, d), lambda l, j: (0, 0)),
            pl.BlockSpec((1, d, tn), lambda l, j: (l, 0, j)),
            pl.BlockSpec((1, 1, tn), lambda l, j: (l, 0, j)),
        ],
        out_specs=pl.BlockSpec((1, SUBLANES, tn), lambda l, j: (l, 0, j)),
        out_shape=jax.ShapeDtypeStruct((depth, SUBLANES, n), F32),
        compiler_params=_cparams(("arbitrary", "arbitrary")),
        name="modulation",
    )(cc, w_mod, b_mod.reshape(depth, 1, n))


def _head_norm_rope_t(xh, g, cos_t, sin_t):
    xn = xh * lax.rsqrt(jnp.mean(xh * xh, axis=0, keepdims=True) + EPS) * g
    sw = jnp.concatenate([xn[16:32], xn[0:16], xn[48:64], xn[32:48]], axis=0)
    return xn * cos_t + sw * sin_t


def _proj_even_kernel(x_ref, scp_ref, sh_ref, ng_ref, w_ref, ang_ref, qg_ref, kg_ref, cos_ref, sin_ref,
                      u_ref, vn_ref, qt_ref, k_ref, vt_ref, *, tm):
    h = _modnorm(x_ref[...], ng_ref[...], scp_ref[0], sh_ref[0])
    y = jnp.dot(h.astype(BF16), w_ref[...], preferred_element_type=F32)
    ua = y[:, 0:A_WIDTH]
    va = y[:, A_WIDTH:2 * A_WIDTH]
    q = y[:, 2 * A_WIDTH:2 * A_WIDTH + B_WIDTH]
    k = y[:, 2 * A_WIDTH + B_WIDTH:2 * A_WIDTH + 2 * B_WIDTH]
    v = y[:, 2 * A_WIDTH + 2 * B_WIDTH:]
    u_ref[...] = jax.nn.gelu(ua)
    gv = jax.nn.gelu(va)
    vn = gv * lax.rsqrt(jnp.mean(gv * gv, axis=-1, keepdims=True) + EPS) * ang_ref[...]
    vn_ref[...] = vn.astype(BF16)
    cos_t = cos_ref[...]
    sin_t = sin_ref[...]
    q_t = q.T
    k_t = k.T
    zeros = jnp.zeros((HEAD_DIM, tm), F32)
    k_parts = []
    for j in range(2 * B_HEADS):
        sl = slice(j * HEAD_DIM, (j + 1) * HEAD_DIM)
        qh = _head_norm_rope_t(q_t[sl], qg_ref[...], cos_t, sin_t) * QUERY_SCALE
        blk = jnp.concatenate([qh, zeros] if j % 2 == 0 else [zeros, qh], axis=0)
        qt_ref[j] = blk.astype(BF16)
        k_parts.append(_head_norm_rope_t(k_t[sl], kg_ref[...], cos_t, sin_t))
    k_ref[...] = jnp.concatenate(k_parts, axis=0).T.astype(BF16)
    v_t = v.T.astype(BF16)
    for c in range(tm // KV_BLOCK):
        vt_ref[c] = v_t[:, c * KV_BLOCK:(c + 1) * KV_BLOCK]


def _proj_odd_kernel(x_ref, scp_ref, sh_ref, ng_ref, w_ref, qg_ref, kg_ref, cos_ref, sin_ref,
                     qt_ref, k_ref, vt_ref, gg_ref, xr_ref, *, tm):
    h = _modnorm(x_ref[...], ng_ref[...], scp_ref[0], sh_ref[0])
    y = jnp.dot(h.astype(BF16), w_ref[...], preferred_element_type=F32)
    q = y[:, 0:C_WIDTH]
    k = y[:, C_WIDTH:C_WIDTH + C_KV_WIDTH]
    v = y[:, C_WIDTH + C_KV_WIDTH:C_WIDTH + 2 * C_KV_WIDTH]
    gate = y[:, C_WIDTH + 2 * C_KV_WIDTH:C_WIDTH + 2 * C_KV_WIDTH + D_WIDTH]
    xr = y[:, C_WIDTH + 2 * C_KV_WIDTH + D_WIDTH:]
    gg_ref[...] = jax.nn.gelu(gate)
    xr_ref[...] = xr
    cos_t = cos_ref[...]
    sin_t = sin_ref[...]
    q_t = q.T
    k_t = k.T
    zeros = jnp.zeros((HEAD_DIM, tm), F32)
    for j in range(C_HEADS):
        sl = slice(j * HEAD_DIM, (j + 1) * HEAD_DIM)
        qh = _head_norm_rope_t(q_t[sl], qg_ref[...], cos_t, sin_t) * QUERY_SCALE
        blk = jnp.concatenate([qh, zeros] if j // C_GROUP == 0 else [zeros, qh], axis=0)
        qt_ref[j] = blk.astype(BF16)
    k_parts = [_head_norm_rope_t(k_t[j * HEAD_DIM:(j + 1) * HEAD_DIM], kg_ref[...], cos_t, sin_t)
               for j in range(C_KV_HEADS)]
    k_ref[...] = jnp.concatenate(k_parts, axis=0).T.astype(BF16)
    v_t = v.T.astype(BF16)
    for c in range(tm // KV_BLOCK):
        vt_ref[c] = v_t[:, c * KV_BLOCK:(c + 1) * KV_BLOCK]


def _projection(even, x2, scp, sh, ng, w, gains, cos_t, sin_t, tm):
    r = x2.shape[0]
    nt = r // tm
    tiles_per_group = nt // scp.shape[0]
    pos_tiles = cos_t.shape[1] // tm
    n_in = w.shape[1]
    row = lambda i: (i, 0)
    grp = lambda i: (i // tiles_per_group, 0, 0)
    fixed = lambda i: (0, 0)
    pos = lambda i: (0, i % pos_tiles)
    in_specs = [
        pl.BlockSpec((tm, D_MODEL), row),
        pl.BlockSpec((1, 1, D_MODEL), grp),
        pl.BlockSpec((1, 1, D_MODEL), grp),
        pl.BlockSpec((1, D_MODEL), fixed),
        pl.BlockSpec((D_MODEL, n_in), fixed),
    ]
    in_specs += [pl.BlockSpec(g.shape, fixed) for g in gains]
    in_specs += [pl.BlockSpec((HEAD_DIM, tm), pos), pl.BlockSpec((HEAD_DIM, tm), pos)]
    if even:
        kw, vw, nh = B_WIDTH, B_WIDTH, 2 * B_HEADS
        body = functools.partial(_proj_even_kernel, tm=tm)
        out_shape = [
            jax.ShapeDtypeStruct((r, A_WIDTH), F32),
            jax.ShapeDtypeStruct((r, A_WIDTH), BF16),
            jax.ShapeDtypeStruct((nh, LANES, r), BF16),
            jax.ShapeDtypeStruct((r, kw), BF16),
            jax.ShapeDtypeStruct((r // KV_BLOCK, vw, KV_BLOCK), BF16),
        ]
        out_specs = [
            pl.BlockSpec((tm, A_WIDTH), row),
            pl.BlockSpec((tm, A_WIDTH), row),
            pl.BlockSpec((nh, LANES, tm), lambda i: (0, 0, i)),
            pl.BlockSpec((tm, kw), row),
            pl.BlockSpec((tm // KV_BLOCK, vw, KV_BLOCK), lambda i: (i, 0, 0)),
        ]
    else:
        kw, vw, nh = C_KV_WIDTH, C_KV_WIDTH, C_HEADS
        body = functools.partial(_proj_odd_kernel, tm=tm)
        out_shape = [
            jax.ShapeDtypeStruct((nh, LANES, r), BF16),
            jax.ShapeDtypeStruct((r, kw), BF16),
            jax.ShapeDtypeStruct((r // KV_BLOCK, vw, KV_BLOCK), BF16),
            jax.ShapeDtypeStruct((r, D_WIDTH), F32),
            jax.ShapeDtypeStruct((r, D_WIDTH), F32),
        ]
        out_specs = [
            pl.BlockSpec((nh, LANES, tm), lambda i: (0, 0, i)),
            pl.BlockSpec((tm, kw), row),
            pl.BlockSpec((tm // KV_BLOCK, vw, KV_BLOCK), lambda i: (i, 0, 0)),
            pl.BlockSpec((tm, D_WIDTH), row),
            pl.BlockSpec((tm, D_WIDTH), row),
        ]
    return pl.pallas_call(
        body, grid=(nt,), in_specs=in_specs, out_specs=out_specs, out_shape=out_shape,
        compiler_params=_cparams(("arbitrary",)),
        name="proj_even" if even else "proj_odd",
    )(x2, scp, sh, ng, w, *gains, cos_t, sin_t)


def _running_maximum(q, kc_ref, s_ref, p_ref, acc_ref, ones, k_lat, vt_lat, vt_ctx, n_lat, tk):
    def scores(kb, slot):
        n = kb.shape[0]
        mb = []
        for c in range(2):
            s = jnp.dot(kb, q[c], preferred_element_type=F32)
            s_ref[slot, c, 0:n] = s
            mb.append(jnp.max(s, axis=0, keepdims=True))
        return tuple(mb)

    def softmax(slot, n, mb, m):
        out, alphas = [], []
        for c in range(2):
            m_new = jnp.maximum(m[c], mb[c])
            p_ref[slot, c, 0:n] = jnp.exp2(s_ref[slot, c, 0:n] - m_new).astype(BF16)
            alphas.append(jnp.exp2(m[c] - m_new))
            out.append(m_new)
        return tuple(out), tuple(alphas)

    def values(vt_blocks, slot, alphas):
        for c in range(2):
            upd = alphas[c] * acc_ref[c]
            for i, vtb in enumerate(vt_blocks):
                lhs = jnp.concatenate([vtb, ones], axis=0)
                upd = upd + jnp.dot(lhs, p_ref[slot, c, i * KV_BLOCK:(i + 1) * KV_BLOCK],
                                    preferred_element_type=F32)
            acc_ref[c] = upd

    lc = kc_ref.shape[0]
    tq = q[0].shape[-1]
    neg = jnp.full((1, tq), -jnp.inf, F32)
    m = (neg, neg)
    mb = scores(kc_ref[...], 0)
    if not n_lat:
        m, al = softmax(0, lc, mb, m)
        values(vt_ctx, 0, al)
    else:
        assert n_lat % 2 == 0
        mb_next = scores(k_lat(0), 1)
        m, al = softmax(0, lc, mb, m)
        mb = mb_next
        mb_next = scores(k_lat(1), 0)
        m, al_next = softmax(1, tk, mb, m)
        values(vt_ctx, 0, al)
        mb, al = mb_next, al_next

        def body(i, carry):
            m, mb, al = carry
            t = 2 * i + 2
            mb_next = scores(k_lat(t), 1)
            m, al_next = softmax(0, tk, mb, m)
            values(vt_lat(t - 2), 1, al)
            mb, al = mb_next, al_next
            mb_next = scores(k_lat(t + 1), 0)
            m, al_next = softmax(1, tk, mb, m)
            values(vt_lat(t - 1), 0, al)
            return m, mb_next, al_next

        m, mb, al = lax.fori_loop(0, (n_lat - 2) // 2, body, (m, mb, al))
        m, al_next = softmax(0, tk, mb, m)
        values(vt_lat(n_lat - 2), 1, al)
        values(vt_lat(n_lat - 1), 0, al_next)


def _attn_kernel(par_ref, *refs, mode, n_lat, tk, lam_init):
    qt_ref, kc_ref, vtc_ref = refs[:3]
    pos = 3
    if n_lat:
        kl_ref, vtl_ref = refs[3:5]
        pos = 5
    if mode == "diff":
        lamp_ref, sg_ref = refs[pos:pos + 2]
        pos += 2
    o_ref, acc_ref, s_ref, p_ref = refs[pos:pos + 4]
    tq = qt_ref.shape[-1]
    q = (qt_ref[0], qt_ref[1])
    acc_ref[...] = jnp.zeros(acc_ref.shape, F32)

    lc = kc_ref.shape[0]
    ones = jnp.ones((DEN_ROWS, KV_BLOCK), BF16)
    per = tk // KV_BLOCK

    def k_lat(j):
        return kl_ref[pl.ds(pl.multiple_of(j * tk, tk), tk), :]

    def vt_lat(j):
        return [vtl_ref[j * per + i] for i in range(per)]

    vt_ctx = [vtc_ref[i] for i in range(lc // KV_BLOCK)]

    def fixed_reference():
        bound = par_ref[0]

        def numer(kb, slot, base):
            n = kb.shape[0]
            for c in range(2):
                s = jnp.dot(kb, q[c], preferred_element_type=F32)
                p_ref[slot, c, base:base + n] = jnp.exp2(s - bound).astype(BF16)

        def accum(vt_blocks, slot, n):
            lhs = jnp.concatenate([jnp.concatenate(vt_blocks, axis=1), jnp.ones((DEN_ROWS, n), BF16)], axis=0)
            for c in range(2):
                acc_ref[c] += jnp.dot(lhs, p_ref[slot, c, 0:n], preferred_element_type=F32)

        def numer_pair(j, slot):
            numer(k_lat(2 * j), slot, 0)
            numer(k_lat(2 * j + 1), slot, tk)

        def accum_pair(j, slot):
            accum(vt_lat(2 * j) + vt_lat(2 * j + 1), slot, 2 * tk)

        numer(kc_ref[...], 1, 0)
        if not n_lat:
            accum(vt_ctx, 1, lc)
            return
        n_pairs = n_lat // 2
        assert n_lat % 4 == 0
        numer_pair(0, 0)
        accum(vt_ctx, 1, lc)

        def body(j, _):
            numer_pair(2 * j + 1, 1)
            accum_pair(2 * j, 0)
            numer_pair(2 * j + 2, 0)
            accum_pair(2 * j + 1, 1)
            return 0

        lax.fori_loop(0, n_pairs // 2 - 1, body, 0)
        numer_pair(n_pairs - 1, 1)
        accum_pair(n_pairs - 2, 0)
        accum_pair(n_pairs - 1, 1)

    def running_maximum():
        _running_maximum(q, kc_ref, s_ref, p_ref, acc_ref, ones, k_lat, vt_lat, vt_ctx, n_lat, tk)

    use_fixed = par_ref[1] > 0.5
    pl.when(use_fixed)(fixed_reference)
    pl.when(jnp.logical_not(use_fixed))(running_maximum)

    vrows = vtc_ref.shape[1]
    l0 = acc_ref[0, vrows:vrows + 1, :]
    l1 = acc_ref[1, vrows:vrows + 1, :]
    if mode == "diff":
        lp = lamp_ref[...]
        lam = (jnp.exp(jnp.sum(lp[0:1] * lp[1:2], axis=-1, keepdims=True))
               - jnp.exp(jnp.sum(lp[2:3] * lp[3:4], axis=-1, keepdims=True)) + lam_init)
        o = acc_ref[0, 0:LANES, :] * (1.0 / l0) - lam * (acc_ref[1, 0:LANES, :] * (1.0 / l1))
        o = o * lax.rsqrt(jnp.mean(o * o, axis=0, keepdims=True) + EPS) * sg_ref[...] * (1.0 - lam_init)
    else:
        o = jnp.concatenate([acc_ref[0, 0:HEAD_DIM, :] * (1.0 / l0),
                             acc_ref[1, 0:HEAD_DIM, :] * (1.0 / l1)], axis=0)
    o_ref[...] = o.T.astype(o_ref.dtype)


def _attention(mode, par, qt, kc, vtc, kl, vtl, extra, batch, tq, tk, lam_init=0.0):
    units = qt.shape[0] // 2
    rq = qt.shape[2]
    nq = rq // batch // tq
    lc = kc.shape[0] // batch
    col = (lambda u: u) if mode == "diff" else (lambda u: 0)
    vrows = LANES if mode == "diff" else HEAD_DIM
    vcol = (lambda u: u) if mode == "diff" else (lambda u: u // (C_GROUP // 2))
    in_specs = [
        pl.BlockSpec(memory_space=pltpu.SMEM),
        pl.BlockSpec((2, LANES, tq), lambda b, u, i: (u, 0, b * nq + i)),
        pl.BlockSpec((lc, LANES), lambda b, u, i: (b, col(u))),
        pl.BlockSpec((lc // KV_BLOCK, vrows, KV_BLOCK), lambda b, u, i: (b, vcol(u), 0)),
    ]
    args = [par, qt, kc, vtc]
    n_lat = 0
    if kl is not None:
        ll = kl.shape[0] // batch
        n_lat = ll // tk
        in_specs += [
            pl.BlockSpec((ll, LANES), lambda b, u, i: (b, col(u))),
            pl.BlockSpec((ll // KV_BLOCK, vrows, KV_BLOCK), lambda b, u, i: (b, vcol(u), 0)),
        ]
        args += [kl, vtl]
    for e in extra:
        in_specs.append(pl.BlockSpec(e.shape, lambda b, u, i: (0, 0)))
        args.append(e)
    rows = max(tk, lc)
    return pl.pallas_call(
        functools.partial(_attn_kernel, mode=mode, n_lat=n_lat, tk=tk, lam_init=lam_init),
        grid=(batch, units, nq),
        in_specs=in_specs,
        out_specs=pl.BlockSpec((tq, LANES), lambda b, u, i: (b * nq + i, u)),
        out_shape=jax.ShapeDtypeStruct((rq, units * LANES), BF16),
        scratch_shapes=[pltpu.VMEM((2, vrows + DEN_ROWS, tq), F32),
                        pltpu.VMEM((2, 2, rows, tq), F32),
                        pltpu.VMEM((2, 2, max(2 * tk, lc), tq), BF16)],
        compiler_params=_cparams(("arbitrary", "arbitrary", "arbitrary")),
        name="attn_" + mode,
    )(*args)


def _gmlp_rows(u_ref, vn_ref, ws_ref, bias_ref, o_ref):
    lane = lax.broadcasted_iota(jnp.int32, (CHUNK, LANES), 1)
    for c in range(u_ref.shape[0] // CHUNK):
        rows = slice(c * CHUNK, (c + 1) * CHUNK)
        for j in range(A_GROUPS // 2):
            cols = slice(j * LANES, (j + 1) * LANES)
            rhs = vn_ref[rows, cols]
            lo = jnp.dot(ws_ref[2 * j], rhs, preferred_element_type=F32)
            hi = jnp.dot(ws_ref[2 * j + 1], rhs, preferred_element_type=F32)
            mixed = jnp.where(lane < HEAD_DIM, lo, hi) + bias_ref[:, cols]
            o_ref[rows, cols] = (u_ref[rows, cols] * mixed).astype(o_ref.dtype)


def _out_mix(x, ya, yb, w_ref, g):
    half = ya.shape[1]
    y = (jnp.dot(ya, w_ref[0:half, :], preferred_element_type=F32)
         + jnp.dot(yb, w_ref[half:, :], preferred_element_type=F32))
    return x + g * y


def _out_even_kernel(x_ref, u_ref, vn_ref, ws_ref, bias_ref, yb_ref, w_ref, g_ref, o_ref, ya_ref):
    _gmlp_rows(u_ref, vn_ref, ws_ref, bias_ref, ya_ref)
    o_ref[...] = _out_mix(x_ref[...], ya_ref[...], yb_ref[...], w_ref, g_ref[0])


def _out_proj_even(x2, u, vn, ws, bias_map, yb, w, g, tm):
    r = x2.shape[0]
    tiles_per_group = (r // tm) // g.shape[0]
    row = lambda i: (i, 0)
    return pl.pallas_call(
        _out_even_kernel,
        grid=(r // tm,),
        in_specs=[
            pl.BlockSpec((tm, D_MODEL), row),
            pl.BlockSpec((tm, A_WIDTH), row),
            pl.BlockSpec((tm, A_WIDTH), row),
            pl.BlockSpec(ws.shape, lambda i: (0, 0, 0)),
            pl.BlockSpec(bias_map.shape, lambda i: (0, 0)),
            pl.BlockSpec((tm, yb.shape[1]), row),
            pl.BlockSpec(w.shape, lambda i: (0, 0)),
            pl.BlockSpec((1, 1, D_MODEL), lambda i: (i // tiles_per_group, 0, 0)),
        ],
        out_specs=pl.BlockSpec((tm, D_MODEL), row),
        out_shape=jax.ShapeDtypeStruct((r, D_MODEL), F32),
        scratch_shapes=[pltpu.VMEM((tm, A_WIDTH), BF16)],
        compiler_params=_cparams(("arbitrary",)),
        name="out_proj_even",
    )(x2, u, vn, ws, bias_map, yb, w, g)


def _swiglu(h, w1, w3, w2):
    a = jnp.dot(h, w1, preferred_element_type=F32)
    b = jnp.dot(h, w3, preferred_element_type=F32)
    t = (a * jax.nn.sigmoid(a)) * b
    return jnp.dot(t.astype(BF16), w2, preferred_element_type=F32)


def _ffn_kernel(x_ref, scp_ref, sh_ref, g_ref, ng_ref, w1_ref, w3_ref, w2_ref, *rest):
    o_ref = rest[-1]
    n_cast = len(rest) // 2
    for src, dst in zip(rest[:n_cast], rest[n_cast:2 * n_cast]):
        dst[...] = src[...].astype(BF16)
    x = x_ref[...]
    h = _modnorm(x, ng_ref[...], scp_ref[0], sh_ref[0]).astype(BF16)
    o_ref[...] = x + g_ref[0] * _swiglu(h, w1_ref[...], w3_ref[...], w2_ref[...])


def _ffn(x2, scp, sh, g, ng, w1, w3, w2, tm, casts=()):
    r = x2.shape[0]
    tiles_per_group = (r // tm) // g.shape[0]
    row = lambda i: (i, 0)
    grp = lambda i: (i // tiles_per_group, 0, 0)
    fixed = lambda i: (0, 0)
    resident = lambda w: pl.BlockSpec(w.shape, fixed, pipeline_mode=pl.Buffered(1))
    in_specs = [
        pl.BlockSpec((tm, D_MODEL), row),
        pl.BlockSpec((1, 1, D_MODEL), grp),
        pl.BlockSpec((1, 1, D_MODEL), grp),
        pl.BlockSpec((1, 1, D_MODEL), grp),
        pl.BlockSpec((1, D_MODEL), fixed),
        resident(w1), resident(w3), resident(w2),
    ]
    out_specs = [pl.BlockSpec((tm, D_MODEL), row)]
    out_shape = [jax.ShapeDtypeStruct((r, D_MODEL), F32)]
    args = [x2, scp, sh, g, ng, w1, w3, w2]
    for n, c in enumerate(casts):
        cin, cout, cshape = _side_cast_specs(c, r // tm, row)
        in_specs.append(cin)
        out_specs.insert(n, cout)
        out_shape.insert(n, cshape)
        args.append(c)
    out = pl.pallas_call(
        _ffn_kernel, grid=(r // tm,), in_specs=in_specs, out_specs=out_specs, out_shape=out_shape,
        compiler_params=_cparams(("arbitrary",)),
        name="ffn",
    )(*args)
    return out if casts else out[0]


META_E1, META_E2, META_G1, META_G2, META_R1, META_R2 = range(6)


def _lane_col(x, lane, k):
    return jnp.sum(jnp.where(lane == k, x, 0.0), axis=-1, keepdims=True)


def _router_kernel(x_ref, ya_ref, s1_ref, s2_ref, agg_ref, hb_ref, wo_ref, g1_ref, scp_ref, sh_ref, ng_ref,
                   whi_ref, wlo_ref, b_ref, x1_ref, hn_ref, meta_ref, cnt_ref, carry_ref, rev_ref):
    n_tiles = agg_ref.shape[0]
    batch = hb_ref.shape[0]

    @pl.when(pl.program_id(0) == 0)
    def _():
        carry_ref[...] = jnp.zeros(carry_ref.shape, F32)
        per = n_tiles // batch
        for b in range(batch):
            c = hb_ref[b, 0:1, :]
            for i in range(per - 1, -1, -1):
                t = b * per + i
                rev_ref[t] = c
                c = agg_ref[t, 1:2, :] + agg_ref[t, 0:1, :] * c

    yb = (s1_ref[...] + s2_ref[...] * rev_ref[pl.program_id(0)]).astype(BF16)
    x1 = _out_mix(x_ref[...], ya_ref[...], yb, wo_ref, g1_ref[0])
    x1_ref[...] = x1
    h = _modnorm(x1, ng_ref[...], scp_ref[0], sh_ref[0])
    hn_ref[...] = h
    h_hi = h.astype(BF16)
    h_lo = (h - h_hi.astype(F32)).astype(BF16)
    logits = (jnp.dot(h_hi, whi_ref[...], preferred_element_type=F32)
              + jnp.dot(h_lo, whi_ref[...], preferred_element_type=F32)
              + jnp.dot(h_hi, wlo_ref[...], preferred_element_type=F32)) + b_ref[...]
    tm = logits.shape[0]
    lane = lax.broadcasted_iota(jnp.int32, logits.shape, 1).astype(F32)
    logits = jnp.where(lane < N_EXPERTS, logits, -jnp.inf)
    m1 = jnp.max(logits, axis=-1, keepdims=True)
    i1 = jnp.min(jnp.where(logits == m1, lane, float(LANES)), axis=-1, keepdims=True)
    rest = jnp.where(lane == i1, -jnp.inf, logits)
    m2 = jnp.max(rest, axis=-1, keepdims=True)
    i2 = jnp.min(jnp.where(rest == m2, lane, float(LANES)), axis=-1, keepdims=True)
    e2 = jnp.exp(m2 - m1)
    den = 1.0 + e2
    oh1 = lane == i1
    oh2 = lane == i2
    both = jnp.where(jnp.logical_or(oh1, oh2), 1.0, 0.0)
    tri = (lax.broadcasted_iota(jnp.int32, (tm, tm), 1) < lax.broadcasted_iota(jnp.int32, (tm, tm), 0))
    before = jnp.dot(jnp.where(tri, 1.0, 0.0).astype(BF16), both.astype(BF16),
                     preferred_element_type=F32) + carry_ref[0:1, :]
    r1 = jnp.sum(jnp.where(oh1, before, 0.0), axis=-1, keepdims=True)
    r2 = jnp.sum(jnp.where(oh2, before, 0.0), axis=-1, keepdims=True)
    meta = jnp.zeros(logits.shape, F32)
    for k, v in ((META_E1, i1), (META_E2, i2), (META_G1, 1.0 / den), (META_G2, e2 / den),
                 (META_R1, r1), (META_R2, r2)):
        meta = jnp.where(lane == k, v, meta)
    meta_ref[...] = meta
    carry_ref[...] = carry_ref[...] + jnp.sum(both, axis=0, keepdims=True)
    cnt_ref[...] = carry_ref[...]


def _router(x2, ya, s1, s2, agg, h_rev, w_out, g1, scp, sh, ng, w_hi, w_lo, b, tm):
    r = x2.shape[0]
    assert agg.shape[0] == r // tm
    tiles_per_group = (r // tm) // scp.shape[0]
    row = lambda i: (i, 0)
    grp = lambda i: (i // tiles_per_group, 0, 0)
    fixed = lambda i: (0, 0)
    return pl.pallas_call(
        _router_kernel,
        grid=(r // tm,),
        in_specs=[
            pl.BlockSpec((tm, D_MODEL), row),
            pl.BlockSpec((tm, ya.shape[1]), row),
            pl.BlockSpec((tm, D_WIDTH), row),
            pl.BlockSpec((tm, D_WIDTH), row),
            pl.BlockSpec(agg.shape, lambda i: (0, 0, 0)),
            pl.BlockSpec(h_rev.shape, lambda i: (0, 0, 0)),
            pl.BlockSpec(w_out.shape, fixed),
            pl.BlockSpec((1, 1, D_MODEL), grp),
            pl.BlockSpec((1, 1, D_MODEL), grp),
            pl.BlockSpec((1, 1, D_MODEL), grp),
            pl.BlockSpec((1, D_MODEL), fixed),
            pl.BlockSpec(w_hi.shape, fixed),
            pl.BlockSpec(w_lo.shape, fixed),
            pl.BlockSpec(b.shape, fixed),
        ],
        out_specs=[
            pl.BlockSpec((tm, D_MODEL), row),
            pl.BlockSpec((tm, D_MODEL), row),
            pl.BlockSpec((tm, LANES), row),
            pl.BlockSpec((SUBLANES, LANES), fixed),
        ],
        out_shape=[
            jax.ShapeDtypeStruct((r, D_MODEL), F32),
            jax.ShapeDtypeStruct((r, D_MODEL), F32),
            jax.ShapeDtypeStruct((r, LANES), F32),
            jax.ShapeDtypeStruct((SUBLANES, LANES), F32),
        ],
        scratch_shapes=[pltpu.VMEM((SUBLANES, LANES), F32), pltpu.VMEM((agg.shape[0], 1, D_WIDTH), F32)],
        compiler_params=_cparams(("arbitrary",)),
        name="router",
    )(x2, ya, s1, s2, agg, h_rev, w_out, g1, scp, sh, ng, w_hi, w_lo, b)


def _row_copy(src_ref, src_row, dst_ref, dst_row, sem):
    return pltpu.make_async_copy(src_ref.at[pl.ds(src_row, 1)], dst_ref.at[pl.ds(dst_row, 1)], sem)


def _dispatch_kernel(pos_ref, pb_ref, pn_ref, nu_ref, h_ref, xs_ref, zero_ref, sem, zsem):
    tm = h_ref.shape[0]
    tile = zero_ref.shape[0]
    n_tiles = xs_ref.shape[0] // tile
    base = pl.program_id(0) * (2 * tm)

    def zero_fill(op):
        for e in range(N_EXPERTS):
            first = pb_ref[e]
            head = (-first) & (SUBLANES - 1)
            for j in range(SUBLANES - 1):
                @pl.when(j < head)
                def _():
                    getattr(_row_copy(zero_ref, 0, xs_ref, first + j, zsem), op)()
            n = pn_ref[e] - head
            for bit in range(tile.bit_length() - 2, SUBLANES.bit_length() - 2, -1):
                size = 1 << bit
                off = pl.multiple_of(first + head + n - (n & (2 * size - 1)), SUBLANES)

                @pl.when((n & size) != 0)
                def _():
                    cp = pltpu.make_async_copy(zero_ref.at[pl.ds(0, size)], xs_ref.at[pl.ds(off, size)], zsem)
                    getattr(cp, op)()
        for j in range(N_EXPERTS):
            t = nu_ref[0] + j

            @pl.when(t < n_tiles)
            def _():
                cp = pltpu.make_async_copy(zero_ref, xs_ref.at[pl.ds(pl.multiple_of(t * tile, tile), tile)], zsem)
                getattr(cp, op)()

    @pl.when(pl.program_id(0) == 0)
    def _():
        zero_ref[...] = jnp.zeros(zero_ref.shape, F32)
        zero_fill("start")

    def issue(r, _):
        for k in range(2):
            _row_copy(h_ref, r, xs_ref, pos_ref[base + 2 * r + k], sem).start()
        return 0

    lax.fori_loop(0, tm, issue, 0, unroll=8)
    for k in range(2):
        pltpu.make_async_copy(h_ref, xs_ref.at[pl.ds(0, tm)], sem).wait()

    @pl.when(pl.program_id(0) == 0)
    def _():
        zero_fill("wait")


def _dispatch(pos_flat, pad_base, pad_n, n_used, hn, n_rows, tm, tile):
    r = hn.shape[0]
    return pl.pallas_call(
        _dispatch_kernel,
        grid_spec=pltpu.PrefetchScalarGridSpec(
            num_scalar_prefetch=4,
            grid=(r // tm,),
            in_specs=[pl.BlockSpec((tm, D_MODEL), lambda i, *_: (i, 0))],
            out_specs=pl.BlockSpec(memory_space=pl.ANY),
            scratch_shapes=[pltpu.VMEM((tile, D_MODEL), F32), pltpu.SemaphoreType.DMA(()),
                            pltpu.SemaphoreType.DMA(())],
        ),
        out_shape=jax.ShapeDtypeStruct((n_rows, D_MODEL), F32),
        compiler_params=_cparams(("arbitrary",)),
        name="dispatch",
    )(pos_flat, pad_base, pad_n, n_used, hn)


def _expert_kernel(te_ref, nu_ref, xs_ref, w1_ref, w3_ref, w2_ref, o_ref):
    del te_ref

    @pl.when(pl.program_id(0) < nu_ref[0])
    def _():
        o_ref[...] = _swiglu(xs_ref[...].astype(BF16), w1_ref[0], w3_ref[0], w2_ref[0])

    @pl.when(pl.program_id(0) >= nu_ref[0])
    def _():
        o_ref[...] = jnp.zeros(o_ref.shape, F32)


def _experts(tile_expert, n_used, xs, w1, w3, w2, tm):
    n_tiles = xs.shape[0] // tm
    wspec = lambda w: pl.BlockSpec((1,) + w.shape[1:], lambda i, te, nu: (te[i], 0, 0))
    return pl.pallas_call(
        _expert_kernel,
        grid_spec=pltpu.PrefetchScalarGridSpec(
            num_scalar_prefetch=2,
            grid=(n_tiles,),
            in_specs=[
                pl.BlockSpec((tm, D_MODEL), lambda i, te, nu: (jnp.minimum(i, nu[0] - 1), 0)),
                wspec(w1), wspec(w3), wspec(w2),
            ],
            out_specs=pl.BlockSpec((tm, D_MODEL), lambda i, te, nu: (i, 0)),
        ),
        out_shape=jax.ShapeDtypeStruct(xs.shape, F32),
        compiler_params=_cparams(("arbitrary",)),
        name="experts",
    )(tile_expert, n_used, xs, w1, w3, w2)


def _combine_kernel(pos_ref, x_ref, meta_ref, g_ref, ys_ref, o_ref, buf_ref, sem):
    tm = x_ref.shape[0]
    i = pl.program_id(0)

    def gather(tile, slot):
        base = tile * (2 * tm)

        def issue(r, _):
            for k in range(2):
                _row_copy(ys_ref, pos_ref[base + 2 * r + k], buf_ref.at[slot, k], r, sem.at[slot]).start()
            return 0

        lax.fori_loop(0, tm, issue, 0, unroll=8)

    @pl.when(i == 0)
    def _():
        gather(0, 0)

    slot = i % 2

    @pl.when(i + 1 < pl.num_programs(0))
    def _():
        gather(i + 1, 1 - slot)

    for k in range(2):
        pltpu.make_async_copy(ys_ref.at[pl.ds(0, tm)], buf_ref.at[slot, k], sem.at[slot]).wait()
    meta = meta_ref[...]
    lane = lax.broadcasted_iota(jnp.int32, meta.shape, 1)
    y = _lane_col(meta, lane, META_G1) * buf_ref[slot, 0] + _lane_col(meta, lane, META_G2) * buf_ref[slot, 1]
    o_ref[...] = x_ref[...] + g_ref[0] * y


def _combine(pos_flat, x2, meta, g, ys, tm):
    r = x2.shape[0]
    tiles_per_group = (r // tm) // g.shape[0]
    return pl.pallas_call(
        _combine_kernel,
        grid_spec=pltpu.PrefetchScalarGridSpec(
            num_scalar_prefetch=1,
            grid=(r // tm,),
            in_specs=[
                pl.BlockSpec((tm, D_MODEL), lambda i, pos: (i, 0)),
                pl.BlockSpec((tm, LANES), lambda i, pos: (i, 0)),
                pl.BlockSpec((1, 1, D_MODEL), lambda i, pos: (i // tiles_per_group, 0, 0)),
                pl.BlockSpec(memory_space=pl.ANY),
            ],
            out_specs=pl.BlockSpec((tm, D_MODEL), lambda i, pos: (i, 0)),
            scratch_shapes=[pltpu.VMEM((2, 2, tm, D_MODEL), F32), pltpu.SemaphoreType.DMA((2,))],
        ),
        out_shape=jax.ShapeDtypeStruct((r, D_MODEL), F32),
        compiler_params=_cparams(("arbitrary",)),
        name="combine",
    )(pos_flat, x2, meta, g, ys)


def _routing_plan(meta, counts, tile):
    n = meta.shape[0]
    n_tiles = 2 * n // tile + N_EXPERTS
    e = meta[:, META_E1:META_E2 + 1].astype(jnp.int32)
    rank = meta[:, META_R1:META_R2 + 1].astype(jnp.int32)
    cnt = counts[0, :N_EXPERTS].astype(jnp.int32)
    tiles_e = (cnt + tile - 1) // tile
    ends = jnp.cumsum(tiles_e)
    starts = (ends - tiles_e) * tile
    pos = (starts[e] + rank).reshape(-1)
    n_used = ends[-1:]
    t_idx = jnp.minimum(jnp.arange(n_tiles, dtype=jnp.int32), n_used[0] - 1)
    tile_expert = jnp.sum((t_idx[:, None] >= ends[None, :]).astype(jnp.int32), axis=1)
    return (pos, tile_expert.astype(jnp.int32), n_used.astype(jnp.int32), (starts + cnt).astype(jnp.int32),
            (tiles_e * tile - cnt).astype(jnp.int32), n_tiles * tile)


LRU_TILE = 512
LRU_CHUNKS = D_WIDTH // LANES
LRU_PAD = SUBLANES


def _lru_pitch(tile):
    seg = tile // SUBLANES
    assert (seg // SUBLANES) % 2 == 0
    return seg, seg + LRU_PAD


def _tile_scans(af_ref, bf_ref, ab_ref, bb_ref, seg, pitch, carry_f):
    nc = LRU_CHUNKS

    def rows_f(t):
        return pl.ds(t, SUBLANES, stride=pitch)

    def rows_b(t):
        return pl.ds(seg - 1 - t, SUBLANES, stride=pitch)

    def ends(t, c):
        hf, pf, hb, pb = c
        rf, rb = rows_f(t), rows_b(t)
        nhf, npf, nhb, npb = [], [], [], []
        for cc in range(nc):
            a = af_ref[cc, rf, :]
            nhf.append(a * hf[cc] + bf_ref[cc, rf, :])
            npf.append(a * pf[cc])
            a = ab_ref[cc, rb, :]
            nhb.append(a * hb[cc] + bb_ref[cc, rb, :])
            npb.append(a * pb[cc])
        return tuple(nhf), tuple(npf), tuple(nhb), tuple(npb)

    z = tuple(jnp.zeros((SUBLANES, LANES), F32) for _ in range(nc))
    o = tuple(jnp.ones((SUBLANES, LANES), F32) for _ in range(nc))
    hf, pf, hb, pb = lax.fori_loop(0, seg, ends, (z, o, z, o), unroll=8)

    in_f, in_b, in_p, out_f, tile_p, tile_h = [], [], [], [], [], []
    for cc in range(nc):
        cin = carry_f[cc]
        rows = [None] * SUBLANES
        for s in range(SUBLANES):
            rows[s] = cin
            cin = hf[cc][s:s + 1] + pf[cc][s:s + 1] * cin
        in_f.append(jnp.concatenate(rows, axis=0))
        out_f.append(cin)
        cin = jnp.zeros((1, LANES), F32)
        pin = jnp.ones((1, LANES), F32)
        rows_h = [None] * SUBLANES
        rows_p = [None] * SUBLANES
        for s in range(SUBLANES - 1, -1, -1):
            rows_h[s] = cin
            rows_p[s] = pin
            cin = hb[cc][s:s + 1] + pb[cc][s:s + 1] * cin
            pin = pb[cc][s:s + 1] * pin
        in_b.append(jnp.concatenate(rows_h, axis=0))
        in_p.append(jnp.concatenate(rows_p, axis=0))
        tile_h.append(cin)
        tile_p.append(pin)

    def final(t, c):
        hf, hb, pb = c
        rf, rb = rows_f(t), rows_b(t)
        nhf, nhb, npb = [], [], []
        for cc in range(nc):
            h = af_ref[cc, rf, :] * hf[cc] + bf_ref[cc, rf, :]
            bf_ref[cc, rf, :] = h
            nhf.append(h)
            a = ab_ref[cc, rb, :]
            h = a * hb[cc] + bb_ref[cc, rb, :]
            p = a * pb[cc]
            bb_ref[cc, rb, :] = h
            ab_ref[cc, rb, :] = p
            nhb.append(h)
            npb.append(p)
        return tuple(nhf), tuple(nhb), tuple(npb)

    lax.fori_loop(0, seg, final, (tuple(in_f), tuple(in_b), tuple(in_p)), unroll=8)
    return out_f, tile_p, tile_h


def _lru_main_kernel(prev_ref, cur_ref, next_ref, gg_ref, cw_ref, cb_ref, w_ref, bias_ref, lam_ref, h0_ref, *rest):
    s1_ref, s2_ref, agg_ref, hl_ref, ext_ref, af_ref, bf_ref, ab_ref, bb_ref, st_ref = rest[-10:]
    if len(rest) == 12:
        rest[1][...] = rest[0][...].astype(BF16)
    i = pl.program_id(1)
    nt = pl.num_programs(1)
    t = cur_ref.shape[0]
    seg, pitch = _lru_pitch(t)
    pad = SUBLANES

    @pl.when(i == 0)
    def _():
        st_ref[...] = h0_ref[0]

    ext_ref[0:pad, :] = jnp.where(i == 0, 0.0, prev_ref[...])
    ext_ref[pad:pad + t, :] = cur_ref[...]
    ext_ref[pad + t:, :] = jnp.where(i == nt - 1, 0.0, next_ref[...])
    left = CONV_W // 2
    xd = cb_ref[...] + sum(cw_ref[j:j + 1, :] * ext_ref[pad + j - left:pad + j - left + t, :] for j in range(CONV_W))
    z = jnp.dot(xd.astype(BF16), w_ref[...], preferred_element_type=F32) + bias_ref[...]
    decay = LRU_C * jax.nn.softplus(-lam_ref[...])
    for d, (a_ref, b_ref) in enumerate(((af_ref, bf_ref), (ab_ref, bb_ref))):
        r = 0.5 * jnp.tanh(0.5 * z[:, (2 * d) * D_WIDTH:(2 * d + 1) * D_WIDTH]) + 0.5
        g = 0.5 * jnp.tanh(0.5 * z[:, (2 * d + 1) * D_WIDTH:(2 * d + 2) * D_WIDTH]) + 0.5
        nla = r * decay[:, d * D_WIDTH:(d + 1) * D_WIDTH]
        a = jnp.exp(-nla)
        m = jnp.tanh(nla) * (a * a + 1.0)
        b = jnp.where(m > 0.0, m * lax.rsqrt(m), 0.0) * (g * xd)
        for cc in range(LRU_CHUNKS):
            for s in range(SUBLANES):
                a_ref[cc, s * pitch:s * pitch + seg, :] = a[s * seg:(s + 1) * seg, cc * LANES:(cc + 1) * LANES]
                b_ref[cc, s * pitch:s * pitch + seg, :] = b[s * seg:(s + 1) * seg, cc * LANES:(cc + 1) * LANES]
    carry_f = [st_ref[0:1, cc * LANES:(cc + 1) * LANES] for cc in range(LRU_CHUNKS)]
    out_f, tile_p, tile_h = _tile_scans(af_ref, bf_ref, ab_ref, bb_ref, seg, pitch, carry_f)
    for cc in range(LRU_CHUNKS):
        cols = slice(cc * LANES, (cc + 1) * LANES)
        for s in range(SUBLANES):
            src = slice(s * pitch, s * pitch + seg)
            dst = slice(s * seg, (s + 1) * seg)
            gg = gg_ref[dst, cols]
            s1_ref[dst, cols] = (bf_ref[cc, src, :] + bb_ref[cc, src, :]) * gg
            s2_ref[dst, cols] = ab_ref[cc, src, :] * gg
    state = jnp.concatenate(out_f, axis=-1)
    st_ref[...] = jnp.broadcast_to(state, st_ref.shape)
    hl_ref[0] = jnp.broadcast_to(state, st_ref.shape)
    agg_ref[0] = jnp.concatenate([jnp.concatenate(tile_p, axis=-1), jnp.concatenate(tile_h, axis=-1),
                                  jnp.zeros((SUBLANES - 2, D_WIDTH), F32)], axis=0)


def _lru_main(xr, gg, cw, cb, w_cat, b_cat, lam_cat, h0, batch, tile, cast=None):
    r = xr.shape[0]
    nt = r // batch // tile
    sub = tile // SUBLANES
    _, pitch = _lru_pitch(tile)
    row = lambda b, i: (b * nt + i, 0)
    fixed = lambda b, i: (0, 0)
    state = pl.BlockSpec((1, SUBLANES, D_WIDTH), lambda b, i: (b, 0, 0))
    chunked = pltpu.VMEM((LRU_CHUNKS, SUBLANES * pitch, LANES), F32)
    in_specs = [
        pl.BlockSpec((SUBLANES, D_WIDTH), lambda b, i: ((b * nt + jnp.maximum(i, 1)) * sub - 1, 0)),
        pl.BlockSpec((tile, D_WIDTH), row),
        pl.BlockSpec((SUBLANES, D_WIDTH), lambda b, i: ((b * nt + jnp.minimum(i + 1, nt - 1)) * sub, 0)),
        pl.BlockSpec((tile, D_WIDTH), row),
        pl.BlockSpec(cw.shape, fixed),
        pl.BlockSpec(cb.shape, fixed),
        pl.BlockSpec(w_cat.shape, fixed),
        pl.BlockSpec(b_cat.shape, fixed),
        pl.BlockSpec(lam_cat.shape, fixed),
        state,
    ]
    out_specs = [
        pl.BlockSpec((tile, D_WIDTH), row),
        pl.BlockSpec((tile, D_WIDTH), row),
        pl.BlockSpec((1, SUBLANES, D_WIDTH), lambda b, i: (b * nt + i, 0, 0)),
        state,
    ]
    out_shape = [
        jax.ShapeDtypeStruct((r, D_WIDTH), F32),
        jax.ShapeDtypeStruct((r, D_WIDTH), F32),
        jax.ShapeDtypeStruct((batch * nt, SUBLANES, D_WIDTH), F32),
        jax.ShapeDtypeStruct((batch, SUBLANES, D_WIDTH), F32),
    ]
    args = [xr, xr, xr, gg, cw, cb, w_cat, b_cat, lam_cat, h0]
    if cast is not None:
        cin, cout, cshape = _side_cast_specs(cast, batch * nt, row)
        in_specs.append(cin)
        out_specs.insert(0, cout)
        out_shape.insert(0, cshape)
        args.append(cast)
    return pl.pallas_call(
        _lru_main_kernel,
        grid=(batch, nt),
        in_specs=in_specs,
        out_specs=out_specs,
        out_shape=out_shape,
        scratch_shapes=[pltpu.VMEM((tile + 2 * SUBLANES, D_WIDTH), F32), chunked, chunked, chunked, chunked,
                        pltpu.VMEM((SUBLANES, D_WIDTH), F32)],
        compiler_params=_cparams(("arbitrary", "arbitrary")),
        name="lru_main",
    )(*args)


def _score_bound(q_gain, k_gain):
    bound = 1.02 * QUERY_SCALE * HEAD_DIM * jnp.max(jnp.abs(q_gain)) * jnp.max(jnp.abs(k_gain))
    return jnp.stack([bound, (bound <= FIXED_REFERENCE_MAX).astype(F32)]).astype(F32)


def _rope_tables(n_lat):
    half = HEAD_DIM // 4
    freqs = ROPE_THETA ** (-jnp.arange(half, dtype=F32) / half)
    t = jnp.arange(n_lat, dtype=jnp.int32)
    row = (t // GRID_W).astype(F32)
    col = (t % GRID_W).astype(F32)
    ar = freqs[:, None] * row[None, :]
    ac = freqs[:, None] * col[None, :]
    cos_t = jnp.concatenate([jnp.cos(ar), jnp.cos(ar), jnp.cos(ac), jnp.cos(ac)], axis=0)
    sin_t = jnp.concatenate([-jnp.sin(ar), jnp.sin(ar), -jnp.sin(ac), jnp.sin(ac)], axis=0)
    return cos_t, sin_t


def _block_diag(w):
    nb, bs, _ = w.shape
    eye = jnp.eye(nb, dtype=w.dtype)
    return (eye[:, None, :, None] * w[:, :, None, :]).reshape(nb * bs, nb * bs)


def kernel(x, c, ctx, c_ctx, w_mod, b_mod, norm_mix_g, norm_ffn_g, w_out, e_w_in, a_norm_g, a_ws, a_bs, b_qnorm_g, b_knorm_g, b_lq1, b_lk1, b_lq2, b_lk2, b_subln_g, ffn_w1, ffn_w3, ffn_w2, o_w_in, c_qnorm_g, c_knorm_g, d_conv_w, d_conv_b, d_wa, d_ba, d_wx, d_bx, d_lambda, router_w, router_b, moe_w1, moe_w3, moe_w2):
    batch, n_lat, d = x.shape
    n_ctx = ctx.shape[1]
    depth = w_mod.shape[0]
    assert depth == 2 and d == D_MODEL and batch + 1 <= SUBLANES
    xl = x.reshape(batch * n_lat, d)
    xc = ctx.reshape(batch * n_ctx, d)

    cc = jnp.concatenate([c, c_ctx[None, :], jnp.zeros((SUBLANES - batch - 1, d), F32)], axis=0)
    mod = _modulation(cc, w_mod, b_mod)

    def mods(layer):
        parts = [mod[layer, :, k * d:(k + 1) * d] for k in range(6)]
        lat = [p[:batch, None, :] for p in parts]
        cx = [p[batch:batch + 1, None, :] for p in parts]
        return lat, cx

    cos_t, sin_t = _rope_tables(n_lat)
    cos_c = jnp.ones((HEAD_DIM, n_ctx), F32)
    sin_c = jnp.zeros((HEAD_DIM, n_ctx), F32)
    col = lambda v: v.reshape(-1, 1)
    rowv = lambda v: v.reshape(1, -1)
    tm = 512
    tm_c = 256

    (sh1, sc1, g1, sh2, sc2, g2), (sh1c, sc1c, g1c, sh2c, sc2c, g2c) = mods(0)
    lam_init = 0.8 - 0.6 * math.exp(-0.3 * 0)
    w_in = e_w_in[0].astype(BF16)
    gains = [rowv(a_norm_g[0]), col(b_qnorm_g[0]), col(b_knorm_g[0])]
    ng = rowv(norm_mix_g[0])
    u, vn, qt, k, vt = _projection(True, xl, 1.0 + sc1, sh1, ng, w_in, gains, cos_t, sin_t, tm)
    uc, vnc, qtc, kc, vtc = _projection(True, xc, 1.0 + sc1c, sh1c, ng, w_in, gains, cos_c, sin_c, tm_c)
    lamp = jnp.stack([b_lq1[0], b_lk1[0], b_lq2[0], b_lk2[0]], axis=0)
    extra = [lamp, col(b_subln_g[0])]
    par = _score_bound(b_qnorm_g[0], b_knorm_g[0])
    yb = _attention("diff", par, qt, kc, vtc, k, vt, extra, batch, ATTN_TQ, ATTN_TK, lam_init)
    ybc = _attention("diff", par, qtc, kc, vtc, None, None, extra, batch, n_ctx, ATTN_TK, lam_init)
    ws = a_ws[0].astype(BF16)
    bias_map = jnp.repeat(a_bs[0].T, HEAD_DIM, axis=1)
    wo = w_out[0].astype(BF16)
    xl = _out_proj_even(xl, u, vn, ws, bias_map, yb, wo, g1, tm)
    xc = _out_proj_even(xc, uc, vnc, ws, bias_map, ybc, wo, g1c, tm_c)
    w1 = ffn_w1[0].astype(BF16)
    w3 = ffn_w3[0].astype(BF16)
    w2 = ffn_w2[0].astype(BF16)
    ngf = rowv(norm_ffn_g[0])
    ne, _, fdim = moe_w1.shape[1:]
    m1, m3, xl = _ffn(xl, 1.0 + sc2, sh2, g2, ngf, w1, w3, w2, tm,
                      casts=(moe_w1[0].reshape(ne * d, fdim), moe_w3[0].reshape(ne * d, fdim)))
    xc = _ffn(xc, 1.0 + sc2c, sh2c, g2c, ngf, w1, w3, w2, tm_c)

    (sh1, sc1, g1, sh2, sc2, g2), (sh1c, sc1c, _, _, _, _) = mods(1)
    w_in = o_w_in[0].astype(BF16)
    gains = [col(c_qnorm_g[0]), col(c_knorm_g[0])]
    ng = rowv(norm_mix_g[1])
    qt, k, vt, gg, xr = _projection(False, xl, 1.0 + sc1, sh1, ng, w_in, gains, cos_t, sin_t, tm)
    _, kc, vtc, _, xrc = _projection(False, xc, 1.0 + sc1c, sh1c, ng, w_in, gains, cos_c, sin_c, tm_c)
    y_attn = _attention("gqa", _score_bound(c_qnorm_g[0], c_knorm_g[0]), qt, kc, vtc, k, vt, [], batch,
                        ATTN_TQ, ATTN_TK)

    w_cat = jnp.concatenate([_block_diag(d_wa[0, 0]), _block_diag(d_wx[0, 0]),
                             _block_diag(d_wa[0, 1]), _block_diag(d_wx[0, 1])], axis=1).astype(BF16)
    b_cat = jnp.concatenate([d_ba[0, 0], d_bx[0, 0], d_ba[0, 1], d_bx[0, 1]]).reshape(1, -1)
    lam_cat = d_lambda[0].reshape(1, -1)
    cw = d_conv_w[0]
    cb = rowv(d_conv_b[0])
    zero_state = jnp.zeros((batch, SUBLANES, D_WIDTH), F32)
    _, _, agg_c, st_f = _lru_main(xrc, xrc, cw, cb, w_cat, b_cat, lam_cat, zero_state, batch, n_ctx)
    st_b = jnp.broadcast_to(agg_c[:, 1:2, :], (batch, SUBLANES, D_WIDTH))
    m2, s1, s2, agg, _ = _lru_main(xr, gg, cw, cb, w_cat, b_cat, lam_cat, st_f, batch, LRU_TILE,
                                   cast=moe_w2[0].reshape(ne * fdim, d))

    ngf = rowv(norm_ffn_g[1])
    wr = jnp.pad(router_w[0], ((0, 0), (0, LANES - N_EXPERTS)))
    wr_hi = wr.astype(BF16)
    wr_lo = (wr - wr_hi.astype(F32)).astype(BF16)
    br = jnp.pad(router_b[0], (0, LANES - N_EXPERTS)).reshape(1, LANES)
    xl, hn, meta, counts = _router(xl, y_attn, s1, s2, agg, st_b, w_out[1].astype(BF16), g1, 1.0 + sc2, sh2, ngf,
                                   wr_hi, wr_lo, br, LRU_TILE)
    pos, tile_expert, n_used, pad_base, pad_n, n_rows = _routing_plan(meta, counts, EXPERT_TILE)
    xs = _dispatch(pos, pad_base, pad_n, n_used, hn, n_rows, tm, EXPERT_TILE)
    ys = _experts(tile_expert, n_used, xs, m1.reshape(ne, d, fdim), m3.reshape(ne, d, fdim),
                  m2.reshape(ne, fdim, d), EXPERT_TILE)
    xl = _combine(pos, xl, meta, g2, ys, COMBINE_TILE)
    return xl.reshape(batch, n_lat, d)
```

```python
import functools
import math

import jax
import jax.numpy as jnp
from jax import lax
from jax.experimental import pallas as pl
from jax.experimental.pallas import tpu as pltpu

F32 = jnp.float32
BF16 = jnp.bfloat16

D_MODEL = 1024
HEAD_DIM = 64
GRID_W = 64
ROPE_THETA = 10000.0
EPS = 1e-6
CHUNK = 128
A_WIDTH = D_MODEL // 2
A_GROUPS = A_WIDTH // HEAD_DIM
B_HEADS = D_MODEL // (4 * HEAD_DIM)
B_WIDTH = B_HEADS * 2 * HEAD_DIM
E_IN = 2 * A_WIDTH + 3 * B_WIDTH
C_HEADS = (D_MODEL // 2) // HEAD_DIM
C_KV_HEADS = 2
C_GROUP = C_HEADS // C_KV_HEADS
C_WIDTH = C_HEADS * HEAD_DIM
C_KV_WIDTH = C_KV_HEADS * HEAD_DIM
D_WIDTH = D_MODEL // 2
CONV_W = 4
LRU_C = 8.0
O_IN = C_WIDTH + 2 * C_KV_WIDTH + 2 * D_WIDTH
N_EXPERTS = 8

LANES = 128
SUBLANES = 8
KV_BLOCK = 256
ATTN_TQ = 2048
ATTN_TK = 512
EXPERT_TILE = 256
COMBINE_TILE = 512
FIXED_REFERENCE_MAX = 40.0
DEN_ROWS = 16
QUERY_SCALE =HEAD_DIM ** -0.5 * math.log2(math.e)
VMEM_LIMIT = 56 * 1024 * 1024


def _cparams(sem):
    return pltpu.CompilerParams(dimension_semantics=sem, vmem_limit_bytes=VMEM_LIMIT)


def _side_cast_specs(w, n_steps, index):
    rows, cols = w.shape
    blk = (rows // n_steps, cols)
    return pl.BlockSpec(blk, index), pl.BlockSpec(blk, index), jax.ShapeDtypeStruct(w.shape, BF16)


def _modnorm(x, g, scp, sh):
    y = x * lax.rsqrt(jnp.mean(x * x, axis=-1, keepdims=True) + EPS)
    return (y * g) * scp + sh


def _mod_kernel(c_ref, w_ref, b_ref, o_ref):
    c = c_ref[...]
    s = c * jax.nn.sigmoid(c)
    o_ref[0] = jnp.dot(s.astype(BF16), w_ref[0].astype(BF16), preferred_element_type=F32) + b_ref[0]


def _modulation(cc, w_mod, b_mod):
    depth, d, n = w_mod.shape
    tn = 1536
    return pl.pallas_call(
        _mod_kernel,
        grid=(depth, n // tn),
        in_specs=[
            pl.BlockSpec((SUBLANES, d), lambda l, j: (0, 0)),
            pl.BlockSpec((1, d, tn), lambda l, j: (l, 0, j)),
            pl.BlockSpec((1, 1, tn), lambda l, j: (l, 0, j)),
        ],
        out_specs=pl.BlockSpec((1, SUBLANES, tn), lambda l, j: (l, 0, j)),
        out_shape=jax.ShapeDtypeStruct((depth, SUBLANES, n), F32),
        compiler_params=_cparams(("arbitrary", "arbitrary")),
        name="modulation",
    )(cc, w_mod, b_mod.reshape(depth, 1, n))


def _head_norm_rope_t(xh, g, cos_t, sin_t):
    xn = xh * lax.rsqrt(jnp.mean(xh * xh, axis=0, keepdims=True) + EPS) * g
    sw = jnp.concatenate([xn[16:32], xn[0:16], xn[48:64], xn[32:48]], axis=0)
    return xn * cos_t + sw * sin_t


def _proj_even_kernel(x_ref, scp_ref, sh_ref, ng_ref, w_ref, ang_ref, qg_ref, kg_ref, cos_ref, sin_ref,
                      u_ref, vn_ref, qt_ref, k_ref, vt_ref, *, tm):
    h = _modnorm(x_ref[...], ng_ref[...], scp_ref[0], sh_ref[0])
    y = jnp.dot(h.astype(BF16), w_ref[...], preferred_element_type=F32)
    ua = y[:, 0:A_WIDTH]
    va = y[:, A_WIDTH:2 * A_WIDTH]
    q = y[:, 2 * A_WIDTH:2 * A_WIDTH + B_WIDTH]
    k = y[:, 2 * A_WIDTH + B_WIDTH:2 * A_WIDTH + 2 * B_WIDTH]
    v = y[:, 2 * A_WIDTH + 2 * B_WIDTH:]
    u_ref[...] = jax.nn.gelu(ua)
    gv = jax.nn.gelu(va)
    vn = gv * lax.rsqrt(jnp.mean(gv * gv, axis=-1, keepdims=True) + EPS) * ang_ref[...]
    vn_ref[...] = vn.astype(BF16)
    cos_t = cos_ref[...]
    sin_t = sin_ref[...]
    q_t = q.T
    k_t = k.T
    zeros = jnp.zeros((HEAD_DIM, tm), F32)
    k_parts = []
    for j in range(2 * B_HEADS):
        sl = slice(j * HEAD_DIM, (j + 1) * HEAD_DIM)
        qh = _head_norm_rope_t(q_t[sl], qg_ref[...], cos_t, sin_t) * QUERY_SCALE
        blk = jnp.concatenate([qh, zeros] if j % 2 == 0 else [zeros, qh], axis=0)
        qt_ref[j] = blk.astype(BF16)
        k_parts.append(_head_norm_rope_t(k_t[sl], kg_ref[...], cos_t, sin_t))
    k_ref[...] = jnp.concatenate(k_parts, axis=0).T.astype(BF16)
    v_t = v.T.astype(BF16)
    for c in range(tm // KV_BLOCK):
        vt_ref[c] = v_t[:, c * KV_BLOCK:(c + 1) * KV_BLOCK]


def _proj_odd_kernel(x_ref, scp_ref, sh_ref, ng_ref, w_ref, qg_ref, kg_ref, cos_ref, sin_ref,
                     qt_ref, k_ref, vt_ref, gg_ref, xr_ref, *, tm):
    h = _modnorm(x_ref[...], ng_ref[...], scp_ref[0], sh_ref[0])
    y = jnp.dot(h.astype(BF16), w_ref[...], preferred_element_type=F32)
    q = y[:, 0:C_WIDTH]
    k = y[:, C_WIDTH:C_WIDTH + C_KV_WIDTH]
    v = y[:, C_WIDTH + C_KV_WIDTH:C_WIDTH + 2 * C_KV_WIDTH]
    gate = y[:, C_WIDTH + 2 * C_KV_WIDTH:C_WIDTH + 2 * C_KV_WIDTH + D_WIDTH]
    xr = y[:, C_WIDTH + 2 * C_KV_WIDTH + D_WIDTH:]
    gg_ref[...] = jax.nn.gelu(gate)
    xr_ref[...] = xr
    cos_t = cos_ref[...]
    sin_t = sin_ref[...]
    q_t = q.T
    k_t = k.T
    zeros = jnp.zeros((HEAD_DIM, tm), F32)
    for j in range(C_HEADS):
        sl = slice(j * HEAD_DIM, (j + 1) * HEAD_DIM)
        qh = _head_norm_rope_t(q_t[sl], qg_ref[...], cos_t, sin_t) * QUERY_SCALE
        blk = jnp.concatenate([qh, zeros] if j // C_GROUP == 0 else [zeros, qh], axis=0)
        qt_ref[j] = blk.astype(BF16)
    k_parts = [_head_norm_rope_t(k_t[j * HEAD_DIM:(j + 1) * HEAD_DIM], kg_ref[...], cos_t, sin_t)
               for j in range(C_KV_HEADS)]
    k_ref[...] = jnp.concatenate(k_parts, axis=0).T.astype(BF16)
    v_t = v.T.astype(BF16)
    for c in range(tm // KV_BLOCK):
        vt_ref[c] = v_t[:, c * KV_BLOCK:(c + 1) * KV_BLOCK]


def _projection(even, x2, scp, sh, ng, w, gains, cos_t, sin_t, tm):
    r = x2.shape[0]
    nt = r // tm
    tiles_per_group = nt // scp.shape[0]
    pos_tiles = cos_t.shape[1] // tm
    n_in = w.shape[1]
    row = lambda i: (i, 0)
    grp = lambda i: (i // tiles_per_group, 0, 0)
    fixed = lambda i: (0, 0)
    pos = lambda i: (0, i % pos_tiles)
    in_specs = [
        pl.BlockSpec((tm, D_MODEL), row),
        pl.BlockSpec((1, 1, D_MODEL), grp),
        pl.BlockSpec((1, 1, D_MODEL), grp),
        pl.BlockSpec((1, D_MODEL), fixed),
        pl.BlockSpec((D_MODEL, n_in), fixed),
    ]
    in_specs += [pl.BlockSpec(g.shape, fixed) for g in gains]
    in_specs += [pl.BlockSpec((HEAD_DIM, tm), pos), pl.BlockSpec((HEAD_DIM, tm), pos)]
    if even:
        kw, vw, nh = B_WIDTH, B_WIDTH, 2 * B_HEADS
        body = functools.partial(_proj_even_kernel, tm=tm)
        out_shape = [
            jax.ShapeDtypeStruct((r, A_WIDTH), F32),
            jax.ShapeDtypeStruct((r, A_WIDTH), BF16),
            jax.ShapeDtypeStruct((nh, LANES, r), BF16),
            jax.ShapeDtypeStruct((r, kw), BF16),
            jax.ShapeDtypeStruct((r // KV_BLOCK, vw, KV_BLOCK), BF16),
        ]
        out_specs = [
            pl.BlockSpec((tm, A_WIDTH), row),
            pl.BlockSpec((tm, A_WIDTH), row),
            pl.BlockSpec((nh, LANES, tm), lambda i: (0, 0, i)),
            pl.BlockSpec((tm, kw), row),
            pl.BlockSpec((tm // KV_BLOCK, vw, KV_BLOCK), lambda i: (i, 0, 0)),
        ]
    else:
        kw, vw, nh = C_KV_WIDTH, C_KV_WIDTH, C_HEADS
        body = functools.partial(_proj_odd_kernel, tm=tm)
        out_shape = [
            jax.ShapeDtypeStruct((nh, LANES, r), BF16),
            jax.ShapeDtypeStruct((r, kw), BF16),
            jax.ShapeDtypeStruct((r // KV_BLOCK, vw, KV_BLOCK), BF16),
            jax.ShapeDtypeStruct((r, D_WIDTH), F32),
            jax.ShapeDtypeStruct((r, D_WIDTH), F32),
        ]
        out_specs = [
            pl.BlockSpec((nh, LANES, tm), lambda i: (0, 0, i)),
            pl.BlockSpec((tm, kw), row),
            pl.BlockSpec((tm // KV_BLOCK, vw, KV_BLOCK), lambda i: (i, 0, 0)),
            pl.BlockSpec((tm, D_WIDTH), row),
            pl.BlockSpec((tm, D_WIDTH), row),
        ]
    return pl.pallas_call(
        body, grid=(nt,), in_specs=in_specs, out_specs=out_specs, out_shape=out_shape,
        compiler_params=_cparams(("arbitrary",)),
        name="proj_even" if even else "proj_odd",
    )(x2, scp, sh, ng, w, *gains, cos_t, sin_t)


def _running_maximum(q, kc_ref, s_ref, p_ref, acc_ref, ones, k_lat, vt_lat, vt_ctx, n_lat, tk):
    def scores(kb, slot):
        n = kb.shape[0]
        mb = []
        for c in range(2):
            s = jnp.dot(kb, q[c], preferred_element_type=F32)
            s_ref[slot, c, 0:n] = s
            mb.append(jnp.max(s, axis=0, keepdims=True))
        return tuple(mb)

    def softmax(slot, n, mb, m):
        out, alphas = [], []
        for c in range(2):
            m_new = jnp.maximum(m[c], mb[c])
            p_ref[slot, c, 0:n] = jnp.exp2(s_ref[slot, c, 0:n] - m_new).astype(BF16)
            alphas.append(jnp.exp2(m[c] - m_new))
            out.append(m_new)
        return tuple(out), tuple(alphas)

    def values(vt_blocks, slot, alphas):
        for c in range(2):
            upd = alphas[c] * acc_ref[c]
            for i, vtb in enumerate(vt_blocks):
                lhs = jnp.concatenate([vtb, ones], axis=0)
                upd = upd + jnp.dot(lhs, p_ref[slot, c, i * KV_BLOCK:(i + 1) * KV_BLOCK],
                                    preferred_element_type=F32)
            acc_ref[c] = upd

    lc = kc_ref.shape[0]
    tq = q[0].shape[-1]
    neg = jnp.full((1, tq), -jnp.inf, F32)
    m = (neg, neg)
    mb = scores(kc_ref[...], 0)
    if not n_lat:
        m, al = softmax(0, lc, mb, m)
        values(vt_ctx, 0, al)
    else:
        assert n_lat % 2 == 0
        mb_next = scores(k_lat(0), 1)
        m, al = softmax(0, lc, mb, m)
        mb = mb_next
        mb_next = scores(k_lat(1), 0)
        m, al_next = softmax(1, tk, mb, m)
        values(vt_ctx, 0, al)
        mb, al = mb_next, al_next

        def body(i, carry):
            m, mb, al = carry
            t = 2 * i + 2
            mb_next = scores(k_lat(t), 1)
            m, al_next = softmax(0, tk, mb, m)
            values(vt_lat(t - 2), 1, al)
            mb, al = mb_next, al_next
            mb_next = scores(k_lat(t + 1), 0)
            m, al_next = softmax(1, tk, mb, m)
            values(vt_lat(t - 1), 0, al)
            return m, mb_next, al_next

        m, mb, al = lax.fori_loop(0, (n_lat - 2) // 2, body, (m, mb, al))
        m, al_next = softmax(0, tk, mb, m)
        values(vt_lat(n_lat - 2), 1, al)
        values(vt_lat(n_lat - 1), 0, al_next)


def _attn_kernel(par_ref, *refs, mode, n_lat, tk, lam_init):
    qt_ref, kc_ref, vtc_ref = refs[:3]
    pos = 3
    if n_lat:
        kl_ref, vtl_ref = refs[3:5]
        pos = 5
    if mode == "diff":
        lamp_ref, sg_ref = refs[pos:pos + 2]
        pos += 2
    o_ref, acc_ref, s_ref, p_ref = refs[pos:pos + 4]
    tq = qt_ref.shape[-1]
    q = (qt_ref[0], qt_ref[1])
    acc_ref[...] = jnp.zeros(acc_ref.shape, F32)

    lc = kc_ref.shape[0]
    ones = jnp.ones((DEN_ROWS, KV_BLOCK), BF16)
    per = tk // KV_BLOCK

    def k_lat(j):
        return kl_ref[pl.ds(pl.multiple_of(j * tk, tk), tk), :]

    def vt_lat(j):
        return [vtl_ref[j * per + i] for i in range(per)]

    vt_ctx = [vtc_ref[i] for i in range(lc // KV_BLOCK)]

    def fixed_reference():
        bound = par_ref[0]
        vrows = vtc_ref.shape[1]
        den_on_mxu = vrows + DEN_ROWS <= LANES

        def numer(kb, slot, base):
            n = kb.shape[0]
            for c in range(2):
                s = jnp.dot(kb, q[c], preferred_element_type=F32)
                p = jnp.exp2(s - bound)
                p_ref[slot, c, base:base + n] = p.astype(BF16)
                if not den_on_mxu:
                    part = jnp.sum(p.reshape(n // SUBLANES, SUBLANES, tq), axis=0)
                    acc_ref[c, vrows:vrows + SUBLANES, :] += part

        def accum(vt_blocks, slot, n):
            lhs = jnp.concatenate(vt_blocks, axis=1)
            if den_on_mxu:
                lhs = jnp.concatenate([lhs, jnp.ones((DEN_ROWS, n), BF16)], axis=0)
            rows = lhs.shape[0]
            for c in range(2):
                acc_ref[c, 0:rows, :] += jnp.dot(lhs, p_ref[slot, c, 0:n], preferred_element_type=F32)

        def finish():
            if not den_on_mxu:
                for c in range(2):
                    total = jnp.sum(acc_ref[c, vrows:vrows + SUBLANES, :], axis=0, keepdims=True)
                    acc_ref[c, vrows:vrows + SUBLANES, :] = jnp.broadcast_to(total, (SUBLANES, tq))

        def numer_pair(j, slot):
            numer(k_lat(2 * j), slot, 0)
            numer(k_lat(2 * j + 1), slot, tk)

        def accum_pair(j, slot):
            accum(vt_lat(2 * j) + vt_lat(2 * j + 1), slot, 2 * tk)

        numer(kc_ref[...], 1, 0)
        if not n_lat:
            accum(vt_ctx, 1, lc)
            finish()
            return
        n_pairs = n_lat // 2
        assert n_lat % 4 == 0
        numer_pair(0, 0)
        accum(vt_ctx, 1, lc)

        def body(j, _):
            numer_pair(2 * j + 1, 1)
            accum_pair(2 * j, 0)
            numer_pair(2 * j + 2, 0)
            accum_pair(2 * j + 1, 1)
            return 0

        lax.fori_loop(0, n_pairs // 2 - 1, body, 0)
        numer_pair(n_pairs - 1, 1)
        accum_pair(n_pairs - 2, 0)
        accum_pair(n_pairs - 1, 1)
        finish()

    def running_maximum():
        _running_maximum(q, kc_ref, s_ref, p_ref, acc_ref, ones, k_lat, vt_lat, vt_ctx, n_lat, tk)

    use_fixed = par_ref[1] > 0.5
    pl.when(use_fixed)(fixed_reference)
    pl.when(jnp.logical_not(use_fixed))(running_maximum)

    vrows = vtc_ref.shape[1]
    l0 = acc_ref[0, vrows:vrows + 1, :]
    l1 = acc_ref[1, vrows:vrows + 1, :]
    if mode == "diff":
        lp = lamp_ref[...]
        lam = (jnp.exp(jnp.sum(lp[0:1] * lp[1:2], axis=-1, keepdims=True))
               - jnp.exp(jnp.sum(lp[2:3] * lp[3:4], axis=-1, keepdims=True)) + lam_init)
        o = acc_ref[0, 0:LANES, :] * (1.0 / l0) - lam * (acc_ref[1, 0:LANES, :] * (1.0 / l1))
        o = o * lax.rsqrt(jnp.mean(o * o, axis=0, keepdims=True) + EPS) * sg_ref[...] * (1.0 - lam_init)
    else:
        o = jnp.concatenate([acc_ref[0, 0:HEAD_DIM, :] * (1.0 / l0),
                             acc_ref[1, 0:HEAD_DIM, :] * (1.0 / l1)], axis=0)
    o_ref[...] = o.T.astype(o_ref.dtype)


def _attention(mode, par, qt, kc, vtc, kl, vtl, extra, batch, tq, tk, lam_init=0.0):
    units = qt.shape[0] // 2
    rq = qt.shape[2]
    nq = rq // batch // tq
    lc = kc.shape[0] // batch
    col = (lambda u: u) if mode == "diff" else (lambda u: 0)
    vrows = LANES if mode == "diff" else HEAD_DIM
    vcol = (lambda u: u) if mode == "diff" else (lambda u: u // (C_GROUP // 2))
    in_specs = [
        pl.BlockSpec(memory_space=pltpu.SMEM),
        pl.BlockSpec((2, LANES, tq), lambda b, u, i: (u, 0, b * nq + i)),
        pl.BlockSpec((lc, LANES), lambda b, u, i: (b, col(u))),
        pl.BlockSpec((lc // KV_BLOCK, vrows, KV_BLOCK), lambda b, u, i: (b, vcol(u), 0)),
    ]
    args = [par, qt, kc, vtc]
    n_lat = 0
    if kl is not None:
        ll = kl.shape[0] // batch
        n_lat = ll // tk
        in_specs += [
            pl.BlockSpec((ll, LANES), lambda b, u, i: (b, col(u))),
            pl.BlockSpec((ll // KV_BLOCK, vrows, KV_BLOCK), lambda b, u, i: (b, vcol(u), 0)),
        ]
        args += [kl, vtl]
    for e in extra:
        in_specs.append(pl.BlockSpec(e.shape, lambda b, u, i: (0, 0)))
        args.append(e)
    rows = max(tk, lc)
    return pl.pallas_call(
        functools.partial(_attn_kernel, mode=mode, n_lat=n_lat, tk=tk, lam_init=lam_init),
        grid=(batch, units, nq),
        in_specs=in_specs,
        out_specs=pl.BlockSpec((tq, LANES), lambda b, u, i: (b * nq + i, u)),
        out_shape=jax.ShapeDtypeStruct((rq, units * LANES), BF16),
        scratch_shapes=[pltpu.VMEM((2, vrows + DEN_ROWS, tq), F32),
                        pltpu.VMEM((2, 2, rows, tq), F32),
                        pltpu.VMEM((2, 2, max(2 * tk, lc), tq), BF16)],
        compiler_params=_cparams(("arbitrary", "arbitrary", "arbitrary")),
        name="attn_" + mode,
    )(*args)


def _gmlp_rows(u_ref, vn_ref, ws_ref, bias_ref, o_ref):
    lane = lax.broadcasted_iota(jnp.int32, (CHUNK, LANES), 1)
    for c in range(u_ref.shape[0] // CHUNK):
        rows = slice(c * CHUNK, (c + 1) * CHUNK)
        for j in range(A_GROUPS // 2):
            cols = slice(j * LANES, (j + 1) * LANES)
            rhs = vn_ref[rows, cols]
            lo = jnp.dot(ws_ref[2 * j], rhs, preferred_element_type=F32)
            hi = jnp.dot(ws_ref[2 * j + 1], rhs, preferred_element_type=F32)
            mixed = jnp.where(lane < HEAD_DIM, lo, hi) + bias_ref[:, cols]
            o_ref[rows, cols] = (u_ref[rows, cols] * mixed).astype(o_ref.dtype)


def _out_mix(x, ya, yb, w_ref, g):
    half = ya.shape[1]
    y = (jnp.dot(ya, w_ref[0:half, :], preferred_element_type=F32)
         + jnp.dot(yb, w_ref[half:, :], preferred_element_type=F32))
    return x + g * y


def _out_even_kernel(x_ref, u_ref, vn_ref, ws_ref, bias_ref, yb_ref, w_ref, g_ref, o_ref, ya_ref):
    _gmlp_rows(u_ref, vn_ref, ws_ref, bias_ref, ya_ref)
    o_ref[...] = _out_mix(x_ref[...], ya_ref[...], yb_ref[...], w_ref, g_ref[0])


def _out_proj_even(x2, u, vn, ws, bias_map, yb, w, g, tm):
    r = x2.shape[0]
    tiles_per_group = (r // tm) // g.shape[0]
    row = lambda i: (i, 0)
    return pl.pallas_call(
        _out_even_kernel,
        grid=(r // tm,),
        in_specs=[
            pl.BlockSpec((tm, D_MODEL), row),
            pl.BlockSpec((tm, A_WIDTH), row),
            pl.BlockSpec((tm, A_WIDTH), row),
            pl.BlockSpec(ws.shape, lambda i: (0, 0, 0)),
            pl.BlockSpec(bias_map.shape, lambda i: (0, 0)),
            pl.BlockSpec((tm, yb.shape[1]), row),
            pl.BlockSpec(w.shape, lambda i: (0, 0)),
            pl.BlockSpec((1, 1, D_MODEL), lambda i: (i // tiles_per_group, 0, 0)),
        ],
        out_specs=pl.BlockSpec((tm, D_MODEL), row),
        out_shape=jax.ShapeDtypeStruct((r, D_MODEL), F32),
        scratch_shapes=[pltpu.VMEM((tm, A_WIDTH), BF16)],
        compiler_params=_cparams(("arbitrary",)),
        name="out_proj_even",
    )(x2, u, vn, ws, bias_map, yb, w, g)


def _swiglu(h, w1, w3, w2):
    a = jnp.dot(h, w1, preferred_element_type=F32)
    b = jnp.dot(h, w3, preferred_element_type=F32)
    t = (a * jax.nn.sigmoid(a)) * b
    return jnp.dot(t.astype(BF16), w2, preferred_element_type=F32)


def _ffn_kernel(x_ref, scp_ref, sh_ref, g_ref, ng_ref, w1_ref, w3_ref, w2_ref, *rest):
    o_ref = rest[-1]
    n_cast = len(rest) // 2
    for src, dst in zip(rest[:n_cast], rest[n_cast:2 * n_cast]):
        dst[...] = src[...].astype(BF16)
    x = x_ref[...]
    h = _modnorm(x, ng_ref[...], scp_ref[0], sh_ref[0]).astype(BF16)
    o_ref[...] = x + g_ref[0] * _swiglu(h, w1_ref[...], w3_ref[...], w2_ref[...])


def _ffn(x2, scp, sh, g, ng, w1, w3, w2, tm, casts=()):
    r = x2.shape[0]
    tiles_per_group = (r // tm) // g.shape[0]
    row = lambda i: (i, 0)
    grp = lambda i: (i // tiles_per_group, 0, 0)
    fixed = lambda i: (0, 0)
    resident = lambda w: pl.BlockSpec(w.shape, fixed, pipeline_mode=pl.Buffered(1))
    in_specs = [
        pl.BlockSpec((tm, D_MODEL), row),
        pl.BlockSpec((1, 1, D_MODEL), grp),
        pl.BlockSpec((1, 1, D_MODEL), grp),
        pl.BlockSpec((1, 1, D_MODEL), grp),
        pl.BlockSpec((1, D_MODEL), fixed),
        resident(w1), resident(w3), resident(w2),
    ]
    out_specs = [pl.BlockSpec((tm, D_MODEL), row)]
    out_shape = [jax.ShapeDtypeStruct((r, D_MODEL), F32)]
    args = [x2, scp, sh, g, ng, w1, w3, w2]
    for n, c in enumerate(casts):
        cin, cout, cshape = _side_cast_specs(c, r // tm, row)
        in_specs.append(cin)
        out_specs.insert(n, cout)
        out_shape.insert(n, cshape)
        args.append(c)
    out = pl.pallas_call(
        _ffn_kernel, grid=(r // tm,), in_specs=in_specs, out_specs=out_specs, out_shape=out_shape,
        compiler_params=_cparams(("arbitrary",)),
        name="ffn",
    )(*args)
    return out if casts else out[0]


META_E1, META_E2, META_G1, META_G2, META_R1, META_R2 = range(6)


def _lane_col(x, lane, k):
    return jnp.sum(jnp.where(lane == k, x, 0.0), axis=-1, keepdims=True)


def _router_kernel(x_ref, ya_ref, s1_ref, s2_ref, agg_ref, hb_ref, wo_ref, g1_ref, scp_ref, sh_ref, ng_ref,
                   whi_ref, wlo_ref, b_ref, x1_ref, hn_ref, meta_ref, cnt_ref, carry_ref, rev_ref):
    n_tiles = agg_ref.shape[0]
    batch = hb_ref.shape[0]

    @pl.when(pl.program_id(0) == 0)
    def _():
        carry_ref[...] = jnp.zeros(carry_ref.shape, F32)
        per = n_tiles // batch
        for b in range(batch):
            c = hb_ref[b, 0:1, :]
            for i in range(per - 1, -1, -1):
                t = b * per + i
                rev_ref[t] = c
                c = agg_ref[t, 1:2, :] + agg_ref[t, 0:1, :] * c

    yb = (s1_ref[...] + s2_ref[...] * rev_ref[pl.program_id(0)]).astype(BF16)
    x1 = _out_mix(x_ref[...], ya_ref[...], yb, wo_ref, g1_ref[0])
    x1_ref[...] = x1
    h = _modnorm(x1, ng_ref[...], scp_ref[0], sh_ref[0])
    hn_ref[...] = h
    h_hi = h.astype(BF16)
    h_lo = (h - h_hi.astype(F32)).astype(BF16)
    logits = (jnp.dot(h_hi, whi_ref[...], preferred_element_type=F32)
              + jnp.dot(h_lo, whi_ref[...], preferred_element_type=F32)
              + jnp.dot(h_hi, wlo_ref[...], preferred_element_type=F32)) + b_ref[...]
    tm = logits.shape[0]
    lane = lax.broadcasted_iota(jnp.int32, logits.shape, 1).astype(F32)
    logits = jnp.where(lane < N_EXPERTS, logits, -jnp.inf)
    m1 = jnp.max(logits, axis=-1, keepdims=True)
    i1 = jnp.min(jnp.where(logits == m1, lane, float(LANES)), axis=-1, keepdims=True)
    rest = jnp.where(lane == i1, -jnp.inf, logits)
    m2 = jnp.max(rest, axis=-1, keepdims=True)
    i2 = jnp.min(jnp.where(rest == m2, lane, float(LANES)), axis=-1, keepdims=True)
    e2 = jnp.exp(m2 - m1)
    den = 1.0 + e2
    oh1 = lane == i1
    oh2 = lane == i2
    both = jnp.where(jnp.logical_or(oh1, oh2), 1.0, 0.0)
    tri = (lax.broadcasted_iota(jnp.int32, (tm, tm), 1) < lax.broadcasted_iota(jnp.int32, (tm, tm), 0))
    before = jnp.dot(jnp.where(tri, 1.0, 0.0).astype(BF16), both.astype(BF16),
                     preferred_element_type=F32) + carry_ref[0:1, :]
    r1 = jnp.sum(jnp.where(oh1, before, 0.0), axis=-1, keepdims=True)
    r2 = jnp.sum(jnp.where(oh2, before, 0.0), axis=-1, keepdims=True)
    meta = jnp.zeros(logits.shape, F32)
    for k, v in ((META_E1, i1), (META_E2, i2), (META_G1, 1.0 / den), (META_G2, e2 / den),
                 (META_R1, r1), (META_R2, r2)):
        meta = jnp.where(lane == k, v, meta)
    meta_ref[...] = meta
    carry_ref[...] = carry_ref[...] + jnp.sum(both, axis=0, keepdims=True)
    cnt_ref[...] = carry_ref[...]


def _router(x2, ya, s1, s2, agg, h_rev, w_out, g1, scp, sh, ng, w_hi, w_lo, b, tm):
    r = x2.shape[0]
    assert agg.shape[0] == r // tm
    tiles_per_group = (r // tm) // scp.shape[0]
    row = lambda i: (i, 0)
    grp = lambda i: (i // tiles_per_group, 0, 0)
    fixed = lambda i: (0, 0)
    return pl.pallas_call(
        _router_kernel,
        grid=(r // tm,),
        in_specs=[
            pl.BlockSpec((tm, D_MODEL), row),
            pl.BlockSpec((tm, ya.shape[1]), row),
            pl.BlockSpec((tm, D_WIDTH), row),
            pl.BlockSpec((tm, D_WIDTH), row),
            pl.BlockSpec(agg.shape, lambda i: (0, 0, 0)),
            pl.BlockSpec(h_rev.shape, lambda i: (0, 0, 0)),
            pl.BlockSpec(w_out.shape, fixed),
            pl.BlockSpec((1, 1, D_MODEL), grp),
            pl.BlockSpec((1, 1, D_MODEL), grp),
            pl.BlockSpec((1, 1, D_MODEL), grp),
            pl.BlockSpec((1, D_MODEL), fixed),
            pl.BlockSpec(w_hi.shape, fixed),
            pl.BlockSpec(w_lo.shape, fixed),
            pl.BlockSpec(b.shape, fixed),
        ],
        out_specs=[
            pl.BlockSpec((tm, D_MODEL), row),
            pl.BlockSpec((tm, D_MODEL), row),
            pl.BlockSpec((tm, LANES), row),
            pl.BlockSpec((SUBLANES, LANES), fixed),
        ],
        out_shape=[
            jax.ShapeDtypeStruct((r, D_MODEL), F32),
            jax.ShapeDtypeStruct((r, D_MODEL), F32),
            jax.ShapeDtypeStruct((r, LANES), F32),
            jax.ShapeDtypeStruct((SUBLANES, LANES), F32),
        ],
        scratch_shapes=[pltpu.VMEM((SUBLANES, LANES), F32), pltpu.VMEM((agg.shape[0], 1, D_WIDTH), F32)],
        compiler_params=_cparams(("arbitrary",)),
        name="router",
    )(x2, ya, s1, s2, agg, h_rev, w_out, g1, scp, sh, ng, w_hi, w_lo, b)


def _row_copy(src_ref, src_row, dst_ref, dst_row, sem):
    return pltpu.make_async_copy(src_ref.at[pl.ds(src_row, 1)], dst_ref.at[pl.ds(dst_row, 1)], sem)


def _dispatch_kernel(pos_ref, pb_ref, pn_ref, nu_ref, h_ref, xs_ref, zero_ref, sem, zsem):
    tm = h_ref.shape[0]
    tile = zero_ref.shape[0]
    n_tiles = xs_ref.shape[0] // tile
    base = pl.program_id(0) * (2 * tm)

    def zero_fill(op):
        for e in range(N_EXPERTS):
            first = pb_ref[e]
            head = (-first) & (SUBLANES - 1)
            for j in range(SUBLANES - 1):
                @pl.when(j < head)
                def _():
                    getattr(_row_copy(zero_ref, 0, xs_ref, first + j, zsem), op)()
            n = pn_ref[e] - head
            for bit in range(tile.bit_length() - 2, SUBLANES.bit_length() - 2, -1):
                size = 1 << bit
                off = pl.multiple_of(first + head + n - (n & (2 * size - 1)), SUBLANES)

                @pl.when((n & size) != 0)
                def _():
                    cp = pltpu.make_async_copy(zero_ref.at[pl.ds(0, size)], xs_ref.at[pl.ds(off, size)], zsem)
                    getattr(cp, op)()
        for j in range(N_EXPERTS):
            t = nu_ref[0] + j

            @pl.when(t < n_tiles)
            def _():
                cp = pltpu.make_async_copy(zero_ref, xs_ref.at[pl.ds(pl.multiple_of(t * tile, tile), tile)], zsem)
                getattr(cp, op)()

    @pl.when(pl.program_id(0) == 0)
    def _():
        zero_ref[...] = jnp.zeros(zero_ref.shape, F32)
        zero_fill("start")

    def issue(r, _):
        for k in range(2):
            _row_copy(h_ref, r, xs_ref, pos_ref[base + 2 * r + k], sem).start()
        return 0

    lax.fori_loop(0, tm, issue, 0, unroll=8)
    for k in range(2):
        pltpu.make_async_copy(h_ref, xs_ref.at[pl.ds(0, tm)], sem).wait()

    @pl.when(pl.program_id(0) == 0)
    def _():
        zero_fill("wait")


def _dispatch(pos_flat, pad_base, pad_n, n_used, hn, n_rows, tm, tile):
    r = hn.shape[0]
    return pl.pallas_call(
        _dispatch_kernel,
        grid_spec=pltpu.PrefetchScalarGridSpec(
            num_scalar_prefetch=4,
            grid=(r // tm,),
            in_specs=[pl.BlockSpec((tm, D_MODEL), lambda i, *_: (i, 0))],
            out_specs=pl.BlockSpec(memory_space=pl.ANY),
            scratch_shapes=[pltpu.VMEM((tile, D_MODEL), F32), pltpu.SemaphoreType.DMA(()),
                            pltpu.SemaphoreType.DMA(())],
        ),
        out_shape=jax.ShapeDtypeStruct((n_rows, D_MODEL), F32),
        compiler_params=_cparams(("arbitrary",)),
        name="dispatch",
    )(pos_flat, pad_base, pad_n, n_used, hn)


def _expert_kernel(te_ref, nu_ref, xs_ref, w1_ref, w3_ref, w2_ref, o_ref):
    del te_ref

    @pl.when(pl.program_id(0) < nu_ref[0])
    def _():
        o_ref[...] = _swiglu(xs_ref[...].astype(BF16), w1_ref[0], w3_ref[0], w2_ref[0])

    @pl.when(pl.program_id(0) >= nu_ref[0])
    def _():
        o_ref[...] = jnp.zeros(o_ref.shape, F32)


def _experts(tile_expert, n_used, xs, w1, w3, w2, tm):
    n_tiles = xs.shape[0] // tm
    wspec = lambda w: pl.BlockSpec((1,) + w.shape[1:], lambda i, te, nu: (te[i], 0, 0))
    return pl.pallas_call(
        _expert_kernel,
        grid_spec=pltpu.PrefetchScalarGridSpec(
            num_scalar_prefetch=2,
            grid=(n_tiles,),
            in_specs=[
                pl.BlockSpec((tm, D_MODEL), lambda i, te, nu: (jnp.minimum(i, nu[0] - 1), 0)),
                wspec(w1), wspec(w3), wspec(w2),
            ],
            out_specs=pl.BlockSpec((tm, D_MODEL), lambda i, te, nu: (i, 0)),
        ),
        out_shape=jax.ShapeDtypeStruct(xs.shape, F32),
        compiler_params=_cparams(("arbitrary",)),
        name="experts",
    )(tile_expert, n_used, xs, w1, w3, w2)


def _combine_kernel(pos_ref, x_ref, meta_ref, g_ref, ys_ref, o_ref, buf_ref, sem):
    tm = x_ref.shape[0]
    i = pl.program_id(0)

    def gather(tile, slot):
        base = tile * (2 * tm)

        def issue(r, _):
            for k in range(2):
                _row_copy(ys_ref, pos_ref[base + 2 * r + k], buf_ref.at[slot, k], r, sem.at[slot]).start()
            return 0

        lax.fori_loop(0, tm, issue, 0, unroll=8)

    @pl.when(i == 0)
    def _():
        gather(0, 0)

    slot = i % 2

    @pl.when(i + 1 < pl.num_programs(0))
    def _():
        gather(i + 1, 1 - slot)

    for k in range(2):
        pltpu.make_async_copy(ys_ref.at[pl.ds(0, tm)], buf_ref.at[slot, k], sem.at[slot]).wait()
    meta = meta_ref[...]
    lane = lax.broadcasted_iota(jnp.int32, meta.shape, 1)
    y = _lane_col(meta, lane, META_G1) * buf_ref[slot, 0] + _lane_col(meta, lane, META_G2) * buf_ref[slot, 1]
    o_ref[...] = x_ref[...] + g_ref[0] * y


def _combine(pos_flat, x2, meta, g, ys, tm):
    r = x2.shape[0]
    tiles_per_group = (r // tm) // g.shape[0]
    return pl.pallas_call(
        _combine_kernel,
        grid_spec=pltpu.PrefetchScalarGridSpec(
            num_scalar_prefetch=1,
            grid=(r // tm,),
            in_specs=[
                pl.BlockSpec((tm, D_MODEL), lambda i, pos: (i, 0)),
                pl.BlockSpec((tm, LANES), lambda i, pos: (i, 0)),
                pl.BlockSpec((1, 1, D_MODEL), lambda i, pos: (i // tiles_per_group, 0, 0)),
                pl.BlockSpec(memory_space=pl.ANY),
            ],
            out_specs=pl.BlockSpec((tm, D_MODEL), lambda i, pos: (i, 0)),
            scratch_shapes=[pltpu.VMEM((2, 2, tm, D_MODEL), F32), pltpu.SemaphoreType.DMA((2,))],
        ),
        out_shape=jax.ShapeDtypeStruct((r, D_MODEL), F32),
        compiler_params=_cparams(("arbitrary",)),
        name="combine",
    )(pos_flat, x2, meta, g, ys)


def _routing_plan(meta, counts, tile):
    n = meta.shape[0]
    n_tiles = 2 * n // tile + N_EXPERTS
    e = meta[:, META_E1:META_E2 + 1].astype(jnp.int32)
    rank = meta[:, META_R1:META_R2 + 1].astype(jnp.int32)
    cnt = counts[0, :N_EXPERTS].astype(jnp.int32)
    tiles_e = (cnt + tile - 1) // tile
    ends = jnp.cumsum(tiles_e)
    starts = (ends - tiles_e) * tile
    pos = (starts[e] + rank).reshape(-1)
    n_used = ends[-1:]
    t_idx = jnp.minimum(jnp.arange(n_tiles, dtype=jnp.int32), n_used[0] - 1)
    tile_expert = jnp.sum((t_idx[:, None] >= ends[None, :]).astype(jnp.int32), axis=1)
    return (pos, tile_expert.astype(jnp.int32), n_used.astype(jnp.int32), (starts + cnt).astype(jnp.int32),
            (tiles_e * tile - cnt).astype(jnp.int32), n_tiles * tile)


LRU_TILE = 512
LRU_CHUNKS = D_WIDTH // LANES
LRU_PAD = SUBLANES


def _lru_pitch(tile):
    seg = tile // SUBLANES
    assert (seg // SUBLANES) % 2 == 0
    return seg, seg + LRU_PAD


def _tile_scans(af_ref, bf_ref, ab_ref, bb_ref, seg, pitch, carry_f):
    nc = LRU_CHUNKS

    def rows_f(t):
        return pl.ds(t, SUBLANES, stride=pitch)

    def rows_b(t):
        return pl.ds(seg - 1 - t, SUBLANES, stride=pitch)

    def ends(t, c):
        hf, pf, hb, pb = c
        rf, rb = rows_f(t), rows_b(t)
        nhf, npf, nhb, npb = [], [], [], []
        for cc in range(nc):
            a = af_ref[cc, rf, :]
            nhf.append(a * hf[cc] + bf_ref[cc, rf, :])
            npf.append(a * pf[cc])
            a = ab_ref[cc, rb, :]
            nhb.append(a * hb[cc] + bb_ref[cc, rb, :])
            npb.append(a * pb[cc])
        return tuple(nhf), tuple(npf), tuple(nhb), tuple(npb)

    z = tuple(jnp.zeros((SUBLANES, LANES), F32) for _ in range(nc))
    o = tuple(jnp.ones((SUBLANES, LANES), F32) for _ in range(nc))
    hf, pf, hb, pb = lax.fori_loop(0, seg, ends, (z, o, z, o), unroll=8)

    in_f, in_b, in_p, out_f, tile_p, tile_h = [], [], [], [], [], []
    for cc in range(nc):
        cin = carry_f[cc]
        rows = [None] * SUBLANES
        for s in range(SUBLANES):
            rows[s] = cin
            cin = hf[cc][s:s + 1] + pf[cc][s:s + 1] * cin
        in_f.append(jnp.concatenate(rows, axis=0))
        out_f.append(cin)
        cin = jnp.zeros((1, LANES), F32)
        pin = jnp.ones((1, LANES), F32)
        rows_h = [None] * SUBLANES
        rows_p = [None] * SUBLANES
        for s in range(SUBLANES - 1, -1, -1):
            rows_h[s] = cin
            rows_p[s] = pin
            cin = hb[cc][s:s + 1] + pb[cc][s:s + 1] * cin
            pin = pb[cc][s:s + 1] * pin
        in_b.append(jnp.concatenate(rows_h, axis=0))
        in_p.append(jnp.concatenate(rows_p, axis=0))
        tile_h.append(cin)
        tile_p.append(pin)

    def final(t, c):
        hf, hb, pb = c
        rf, rb = rows_f(t), rows_b(t)
        nhf, nhb, npb = [], [], []
        for cc in range(nc):
            h = af_ref[cc, rf, :] * hf[cc] + bf_ref[cc, rf, :]
            bf_ref[cc, rf, :] = h
            nhf.append(h)
            a = ab_ref[cc, rb, :]
            h = a * hb[cc] + bb_ref[cc, rb, :]
            p = a * pb[cc]
            bb_ref[cc, rb, :] = h
            ab_ref[cc, rb, :] = p
            nhb.append(h)
            npb.append(p)
        return tuple(nhf), tuple(nhb), tuple(npb)

    lax.fori_loop(0, seg, final, (tuple(in_f), tuple(in_b), tuple(in_p)), unroll=8)
    return out_f, tile_p, tile_h


def _lru_main_kernel(prev_ref, cur_ref, next_ref, gg_ref, cw_ref, cb_ref, w_ref, bias_ref, lam_ref, h0_ref, *rest):
    s1_ref, s2_ref, agg_ref, hl_ref, ext_ref, af_ref, bf_ref, ab_ref, bb_ref, st_ref = rest[-10:]
    if len(rest) == 12:
        rest[1][...] = rest[0][...].astype(BF16)
    i = pl.program_id(1)
    nt = pl.num_programs(1)
    t = cur_ref.shape[0]
    seg, pitch = _lru_pitch(t)
    pad = SUBLANES

    @pl.when(i == 0)
    def _():
        st_ref[...] = h0_ref[0]

    ext_ref[0:pad, :] = jnp.where(i == 0, 0.0, prev_ref[...])
    ext_ref[pad:pad + t, :] = cur_ref[...]
    ext_ref[pad + t:, :] = jnp.where(i == nt - 1, 0.0, next_ref[...])
    left = CONV_W // 2
    xd = cb_ref[...] + sum(cw_ref[j:j + 1, :] * ext_ref[pad + j - left:pad + j - left + t, :] for j in range(CONV_W))
    z = jnp.dot(xd.astype(BF16), w_ref[...], preferred_element_type=F32) + bias_ref[...]
    decay = LRU_C * jax.nn.softplus(-lam_ref[...])
    for d, (a_ref, b_ref) in enumerate(((af_ref, bf_ref), (ab_ref, bb_ref))):
        r = 0.5 * jnp.tanh(0.5 * z[:, (2 * d) * D_WIDTH:(2 * d + 1) * D_WIDTH]) + 0.5
        g = 0.5 * jnp.tanh(0.5 * z[:, (2 * d + 1) * D_WIDTH:(2 * d + 2) * D_WIDTH]) + 0.5
        nla = r * decay[:, d * D_WIDTH:(d + 1) * D_WIDTH]
        a = jnp.exp(-nla)
        m = jnp.tanh(nla) * (a * a + 1.0)
        b = jnp.where(m > 0.0, m * lax.rsqrt(m), 0.0) * (g * xd)
        for cc in range(LRU_CHUNKS):
            for s in range(SUBLANES):
                a_ref[cc, s * pitch:s * pitch + seg, :] = a[s * seg:(s + 1) * seg, cc * LANES:(cc + 1) * LANES]
                b_ref[cc, s * pitch:s * pitch + seg, :] = b[s * seg:(s + 1) * seg, cc * LANES:(cc + 1) * LANES]
    carry_f = [st_ref[0:1, cc * LANES:(cc + 1) * LANES] for cc in range(LRU_CHUNKS)]
    out_f, tile_p, tile_h = _tile_scans(af_ref, bf_ref, ab_ref, bb_ref, seg, pitch, carry_f)
    for cc in range(LRU_CHUNKS):
        cols = slice(cc * LANES, (cc + 1) * LANES)
        for s in range(SUBLANES):
            src = slice(s * pitch, s * pitch + seg)
            dst = slice(s * seg, (s + 1) * seg)
            gg = gg_ref[dst, cols]
            s1_ref[dst, cols] = (bf_ref[cc, src, :] + bb_ref[cc, src, :]) * gg
            s2_ref[dst, cols] = ab_ref[cc, src, :] * gg
    state = jnp.concatenate(out_f, axis=-1)
    st_ref[...] = jnp.broadcast_to(state, st_ref.shape)
    hl_ref[0] = jnp.broadcast_to(state, st_ref.shape)
    agg_ref[0] = jnp.concatenate([jnp.concatenate(tile_p, axis=-1), jnp.concatenate(tile_h, axis=-1),
                                  jnp.zeros((SUBLANES - 2, D_WIDTH), F32)], axis=0)


def _lru_main(xr, gg, cw, cb, w_cat, b_cat, lam_cat, h0, batch, tile, cast=None):
    r = xr.shape[0]
    nt = r // batch // tile
    sub = tile // SUBLANES
    _, pitch = _lru_pitch(tile)
    row = lambda b, i: (b * nt + i, 0)
    fixed = lambda b, i: (0, 0)
    state = pl.BlockSpec((1, SUBLANES, D_WIDTH), lambda b, i: (b, 0, 0))
    chunked = pltpu.VMEM((LRU_CHUNKS, SUBLANES * pitch, LANES), F32)
    in_specs = [
        pl.BlockSpec((SUBLANES, D_WIDTH), lambda b, i: ((b * nt + jnp.maximum(i, 1)) * sub - 1, 0)),
        pl.BlockSpec((tile, D_WIDTH), row),
        pl.BlockSpec((SUBLANES, D_WIDTH), lambda b, i: ((b * nt + jnp.minimum(i + 1, nt - 1)) * sub, 0)),
        pl.BlockSpec((tile, D_WIDTH), row),
        pl.BlockSpec(cw.shape, fixed),
        pl.BlockSpec(cb.shape, fixed),
        pl.BlockSpec(w_cat.shape, fixed),
        pl.BlockSpec(b_cat.shape, fixed),
        pl.BlockSpec(lam_cat.shape, fixed),
        state,
    ]
    out_specs = [
        pl.BlockSpec((tile, D_WIDTH), row),
        pl.BlockSpec((tile, D_WIDTH), row),
        pl.BlockSpec((1, SUBLANES, D_WIDTH), lambda b, i: (b * nt + i, 0, 0)),
        state,
    ]
    out_shape = [
        jax.ShapeDtypeStruct((r, D_WIDTH), F32),
        jax.ShapeDtypeStruct((r, D_WIDTH), F32),
        jax.ShapeDtypeStruct((batch * nt, SUBLANES, D_WIDTH), F32),
        jax.ShapeDtypeStruct((batch, SUBLANES, D_WIDTH), F32),
    ]
    args = [xr, xr, xr, gg, cw, cb, w_cat, b_cat, lam_cat, h0]
    if cast is not None:
        cin, cout, cshape = _side_cast_specs(cast, batch * nt, row)
        in_specs.append(cin)
        out_specs.insert(0, cout)
        out_shape.insert(0, cshape)
        args.append(cast)
    return pl.pallas_call(
        _lru_main_kernel,
        grid=(batch, nt),
        in_specs=in_specs,
        out_specs=out_specs,
        out_shape=out_shape,
        scratch_shapes=[pltpu.VMEM((tile + 2 * SUBLANES, D_WIDTH), F32), chunked, chunked, chunked, chunked,
                        pltpu.VMEM((SUBLANES, D_WIDTH), F32)],
        compiler_params=_cparams(("arbitrary", "arbitrary")),
        name="lru_main",
    )(*args)


def _score_bound(q_gain, k_gain):
    bound = 1.02 * QUERY_SCALE * HEAD_DIM * jnp.max(jnp.abs(q_gain)) * jnp.max(jnp.abs(k_gain))
    return jnp.stack([bound, (bound <= FIXED_REFERENCE_MAX).astype(F32)]).astype(F32)


def _rope_tables(n_lat):
    half = HEAD_DIM // 4
    freqs = ROPE_THETA ** (-jnp.arange(half, dtype=F32) / half)
    t = jnp.arange(n_lat, dtype=jnp.int32)
    row = (t // GRID_W).astype(F32)
    col = (t % GRID_W).astype(F32)
    ar = freqs[:, None] * row[None, :]
    ac = freqs[:, None] * col[None, :]
    cos_t = jnp.concatenate([jnp.cos(ar), jnp.cos(ar), jnp.cos(ac), jnp.cos(ac)], axis=0)
    sin_t = jnp.concatenate([-jnp.sin(ar), jnp.sin(ar), -jnp.sin(ac), jnp.sin(ac)], axis=0)
    return cos_t, sin_t


def _block_diag(w):
    nb, bs, _ = w.shape
    eye = jnp.eye(nb, dtype=w.dtype)
    return (eye[:, None, :, None] * w[:, :, None, :]).reshape(nb * bs, nb * bs)


def kernel(x, c, ctx, c_ctx, w_mod, b_mod, norm_mix_g, norm_ffn_g, w_out, e_w_in, a_norm_g, a_ws, a_bs, b_qnorm_g, b_knorm_g, b_lq1, b_lk1, b_lq2, b_lk2, b_subln_g, ffn_w1, ffn_w3, ffn_w2, o_w_in, c_qnorm_g, c_knorm_g, d_conv_w, d_conv_b, d_wa, d_ba, d_wx, d_bx, d_lambda, router_w, router_b, moe_w1, moe_w3, moe_w2):
    batch, n_lat, d = x.shape
    n_ctx = ctx.shape[1]
    depth = w_mod.shape[0]
    assert depth == 2 and d == D_MODEL and batch + 1 <= SUBLANES
    xl = x.reshape(batch * n_lat, d)
    xc = ctx.reshape(batch * n_ctx, d)

    cc = jnp.concatenate([c, c_ctx[None, :], jnp.zeros((SUBLANES - batch - 1, d), F32)], axis=0)
    mod = _modulation(cc, w_mod, b_mod)

    def mods(layer):
        parts = [mod[layer, :, k * d:(k + 1) * d] for k in range(6)]
        lat = [p[:batch, None, :] for p in parts]
        cx = [p[batch:batch + 1, None, :] for p in parts]
        return lat, cx

    cos_t, sin_t = _rope_tables(n_lat)
    cos_c = jnp.ones((HEAD_DIM, n_ctx), F32)
    sin_c = jnp.zeros((HEAD_DIM, n_ctx), F32)
    col = lambda v: v.reshape(-1, 1)
    rowv = lambda v: v.reshape(1, -1)
    tm = 512
    tm_c = 256

    (sh1, sc1, g1, sh2, sc2, g2), (sh1c, sc1c, g1c, sh2c, sc2c, g2c) = mods(0)
    lam_init = 0.8 - 0.6 * math.exp(-0.3 * 0)
    w_in = e_w_in[0].astype(BF16)
    gains = [rowv(a_norm_g[0]), col(b_qnorm_g[0]), col(b_knorm_g[0])]
    ng = rowv(norm_mix_g[0])
    u, vn, qt, k, vt = _projection(True, xl, 1.0 + sc1, sh1, ng, w_in, gains, cos_t, sin_t, tm)
    uc, vnc, qtc, kc, vtc = _projection(True, xc, 1.0 + sc1c, sh1c, ng, w_in, gains, cos_c, sin_c, tm_c)
    lamp = jnp.stack([b_lq1[0], b_lk1[0], b_lq2[0], b_lk2[0]], axis=0)
    extra = [lamp, col(b_subln_g[0])]
    par = _score_bound(b_qnorm_g[0], b_knorm_g[0])
    yb = _attention("diff", par, qt, kc, vtc, k, vt, extra, batch, ATTN_TQ, ATTN_TK, lam_init)
    ybc = _attention("diff", par, qtc, kc, vtc, None, None, extra, batch, n_ctx, ATTN_TK, lam_init)
    ws = a_ws[0].astype(BF16)
    bias_map = jnp.repeat(a_bs[0].T, HEAD_DIM, axis=1)
    wo = w_out[0].astype(BF16)
    xl = _out_proj_even(xl, u, vn, ws, bias_map, yb, wo, g1, tm)
    xc = _out_proj_even(xc, uc, vnc, ws, bias_map, ybc, wo, g1c, tm_c)
    w1 = ffn_w1[0].astype(BF16)
    w3 = ffn_w3[0].astype(BF16)
    w2 = ffn_w2[0].astype(BF16)
    ngf = rowv(norm_ffn_g[0])
    ne, _, fdim = moe_w1.shape[1:]
    m1, m3, xl = _ffn(xl, 1.0 + sc2, sh2, g2, ngf, w1, w3, w2, tm,
                      casts=(moe_w1[0].reshape(ne * d, fdim), moe_w3[0].reshape(ne * d, fdim)))
    xc = _ffn(xc, 1.0 + sc2c, sh2c, g2c, ngf, w1, w3, w2, tm_c)

    (sh1, sc1, g1, sh2, sc2, g2), (sh1c, sc1c, _, _, _, _) = mods(1)
    w_in = o_w_in[0].astype(BF16)
    gains = [col(c_qnorm_g[0]), col(c_knorm_g[0])]
    ng = rowv(norm_mix_g[1])
    qt, k, vt, gg, xr = _projection(False, xl, 1.0 + sc1, sh1, ng, w_in, gains, cos_t, sin_t, tm)
    _, kc, vtc, _, xrc = _projection(False, xc, 1.0 + sc1c, sh1c, ng, w_in, gains, cos_c, sin_c, tm_c)
    y_attn = _attention("gqa", _score_bound(c_qnorm_g[0], c_knorm_g[0]), qt, kc, vtc, k, vt, [], batch,
                        ATTN_TQ, ATTN_TK)

    w_cat = jnp.concatenate([_block_diag(d_wa[0, 0]), _block_diag(d_wx[0, 0]),
                             _block_diag(d_wa[0, 1]), _block_diag(d_wx[0, 1])], axis=1).astype(BF16)
    b_cat = jnp.concatenate([d_ba[0, 0], d_bx[0, 0], d_ba[0, 1], d_bx[0, 1]]).reshape(1, -1)
    lam_cat = d_lambda[0].reshape(1, -1)
    cw = d_conv_w[0]
    cb = rowv(d_conv_b[0])
    zero_state = jnp.zeros((batch, SUBLANES, D_WIDTH), F32)
    _, _, agg_c, st_f = _lru_main(xrc, xrc, cw, cb, w_cat, b_cat, lam_cat, zero_state, batch, n_ctx)
    st_b = jnp.broadcast_to(agg_c[:, 1:2, :], (batch, SUBLANES, D_WIDTH))
    m2, s1, s2, agg, _ = _lru_main(xr, gg, cw, cb, w_cat, b_cat, lam_cat, st_f, batch, LRU_TILE,
                                   cast=moe_w2[0].reshape(ne * fdim, d))

    ngf = rowv(norm_ffn_g[1])
    wr = jnp.pad(router_w[0], ((0, 0), (0, LANES - N_EXPERTS)))
    wr_hi = wr.astype(BF16)
    wr_lo = (wr - wr_hi.astype(F32)).astype(BF16)
    br = jnp.pad(router_b[0], (0, LANES - N_EXPERTS)).reshape(1, LANES)
    xl, hn, meta, counts = _router(xl, y_attn, s1, s2, agg, st_b, w_out[1].astype(BF16), g1, 1.0 + sc2, sh2, ngf,
                                   wr_hi, wr_lo, br, LRU_TILE)
    pos, tile_expert, n_used, pad_base, pad_n, n_rows = _routing_plan(meta, counts, EXPERT_TILE)
    xs = _dispatch(pos, pad_base, pad_n, n_used, hn, n_rows, tm, EXPERT_TILE)
    ys = _experts(tile_expert, n_used, xs, m1.reshape(ne, d, fdim), m3.reshape(ne, d, fdim),
                  m2.reshape(ne, fdim, d), EXPERT_TILE)
    xl = _combine(pos, xl, meta, g2, ys, COMBINE_TILE)
    return xl.reshape(batch, n_lat, d)
```

```python
import functools
import math

import jax
import jax.numpy as jnp
from jax import lax
from jax.experimental import pallas as pl
from jax.experimental.pallas import tpu as pltpu

F32 = jnp.float32
BF16 = jnp.bfloat16

D_MODEL = 1024
HEAD_DIM = 64
GRID_W = 64
ROPE_THETA = 10000.0
EPS = 1e-6
CHUNK = 128
A_WIDTH = D_MODEL // 2
A_GROUPS = A_WIDTH // HEAD_DIM
B_HEADS = D_MODEL // (4 * HEAD_DIM)
B_WIDTH = B_HEADS * 2 * HEAD_DIM
E_IN = 2 * A_WIDTH + 3 * B_WIDTH
C_HEADS = (D_MODEL // 2) // HEAD_DIM
C_KV_HEADS = 2
C_GROUP = C_HEADS // C_KV_HEADS
C_WIDTH = C_HEADS * HEAD_DIM
C_KV_WIDTH = C_KV_HEADS * HEAD_DIM
D_WIDTH = D_MODEL // 2
CONV_W = 4
LRU_C = 8.0
O_IN = C_WIDTH + 2 * C_KV_WIDTH + 2 * D_WIDTH
N_EXPERTS = 8

LANES = 128
SUBLANES = 8
KV_BLOCK = 256
ATTN_TQ = 2048
ATTN_TK = 512
EXPERT_TILE = 256
COMBINE_TILE = 512
FIXED_REFERENCE_MAX = 40.0
DEN_ROWS = 16
QUERY_SCALE =HEAD_DIM ** -0.5 * math.log2(math.e)
VMEM_LIMIT = 56 * 1024 * 1024


def _cparams(sem):
    return pltpu.CompilerParams(dimension_semantics=sem, vmem_limit_bytes=VMEM_LIMIT)


def _side_cast_specs(w, n_steps, index):
    rows, cols = w.shape
    blk = (rows // n_steps, cols)
    return pl.BlockSpec(blk, index), pl.BlockSpec(blk, index), jax.ShapeDtypeStruct(w.shape, BF16)


def _modnorm(x, g, scp, sh):
    y = x * lax.rsqrt(jnp.mean(x * x, axis=-1, keepdims=True) + EPS)
    return (y * g) * scp + sh


def _mod_kernel(c_ref, w_ref, b_ref, o_ref):
    c = c_ref[...]
    s = c * jax.nn.sigmoid(c)
    o_ref[0] = jnp.dot(s.astype(BF16), w_ref[0].astype(BF16), preferred_element_type=F32) + b_ref[0]


def _modulation(cc, w_mod, b_mod):
    depth, d, n = w_mod.shape
    tn = 1536
    return pl.pallas_call(
        _mod_kernel,
        grid=(depth, n // tn),
        in_specs=[
            pl.BlockSpec((SUBLANES, d), lambda l, j: (0, 0)),
            pl.BlockSpec((1, d, tn), lambda l, j: (l, 0, j)),
            pl.BlockSpec((1, 1, tn), lambda l, j: (l, 0, j)),
        ],
        out_specs=pl.BlockSpec((1, SUBLANES, tn), lambda l, j: (l, 0, j)),
        out_shape=jax.ShapeDtypeStruct((depth, SUBLANES, n), F32),
        compiler_params=_cparams(("arbitrary", "arbitrary")),
        name="modulation",
    )(cc, w_mod, b_mod.reshape(depth, 1, n))


def _head_norm_rope_t(xh, g, cos_t, sin_t):
    xn = xh * lax.rsqrt(jnp.mean(xh * xh, axis=0, keepdims=True) + EPS) * g
    sw = jnp.concatenate([xn[16:32], xn[0:16], xn[48:64], xn[32:48]], axis=0)
    return xn * cos_t + sw * sin_t


def _proj_even_kernel(x_ref, scp_ref, sh_ref, ng_ref, w_ref, ang_ref, qg_ref, kg_ref, cos_ref, sin_ref,
                      u_ref, vn_ref, qt_ref, k_ref, vt_ref, *, tm):
    h = _modnorm(x_ref[...], ng_ref[...], scp_ref[0], sh_ref[0])
    y = jnp.dot(h.astype(BF16), w_ref[...], preferred_element_type=F32)
    ua = y[:, 0:A_WIDTH]
    va = y[:, A_WIDTH:2 * A_WIDTH]
    q = y[:, 2 * A_WIDTH:2 * A_WIDTH + B_WIDTH]
    k = y[:, 2 * A_WIDTH + B_WIDTH:2 * A_WIDTH + 2 * B_WIDTH]
    v = y[:, 2 * A_WIDTH + 2 * B_WIDTH:]
    u_ref[...] = jax.nn.gelu(ua)
    gv = jax.nn.gelu(va)
    vn = gv * lax.rsqrt(jnp.mean(gv * gv, axis=-1, keepdims=True) + EPS) * ang_ref[...]
    vn_ref[...] = vn.astype(BF16)
    cos_t = cos_ref[...]
    sin_t = sin_ref[...]
    q_t = q.T
    k_t = k.T
    zeros = jnp.zeros((HEAD_DIM, tm), F32)
    k_parts = []
    for j in range(2 * B_HEADS):
        sl = slice(j * HEAD_DIM, (j + 1) * HEAD_DIM)
        qh = _head_norm_rope_t(q_t[sl], qg_ref[...], cos_t, sin_t) * QUERY_SCALE
        blk = jnp.concatenate([qh, zeros] if j % 2 == 0 else [zeros, qh], axis=0)
        qt_ref[j] = blk.astype(BF16)
        k_parts.append(_head_norm_rope_t(k_t[sl], kg_ref[...], cos_t, sin_t))
    k_ref[...] = jnp.concatenate(k_parts, axis=0).T.astype(BF16)
    v_t = v.T.astype(BF16)
    for c in range(tm // KV_BLOCK):
        vt_ref[c] = v_t[:, c * KV_BLOCK:(c + 1) * KV_BLOCK]


def _proj_odd_kernel(x_ref, scp_ref, sh_ref, ng_ref, w_ref, qg_ref, kg_ref, cos_ref, sin_ref,
                     qt_ref, k_ref, vt_ref, gg_ref, xr_ref, *, tm):
    h = _modnorm(x_ref[...], ng_ref[...], scp_ref[0], sh_ref[0])
    y = jnp.dot(h.astype(BF16), w_ref[...], preferred_element_type=F32)
    q = y[:, 0:C_WIDTH]
    k = y[:, C_WIDTH:C_WIDTH + C_KV_WIDTH]
    v = y[:, C_WIDTH + C_KV_WIDTH:C_WIDTH + 2 * C_KV_WIDTH]
    gate = y[:, C_WIDTH + 2 * C_KV_WIDTH:C_WIDTH + 2 * C_KV_WIDTH + D_WIDTH]
    xr = y[:, C_WIDTH + 2 * C_KV_WIDTH + D_WIDTH:]
    gg_ref[...] = jax.nn.gelu(gate)
    xr_ref[...] = xr
    cos_t = cos_ref[...]
    sin_t = sin_ref[...]
    q_t = q.T
    k_t = k.T
    zeros = jnp.zeros((HEAD_DIM, tm), F32)
    for j in range(C_HEADS):
        sl = slice(j * HEAD_DIM, (j + 1) * HEAD_DIM)
        qh = _head_norm_rope_t(q_t[sl], qg_ref[...], cos_t, sin_t) * QUERY_SCALE
        blk = jnp.concatenate([qh, zeros] if j // C_GROUP == 0 else [zeros, qh], axis=0)
        qt_ref[j] = blk.astype(BF16)
    k_parts = [_head_norm_rope_t(k_t[j * HEAD_DIM:(j + 1) * HEAD_DIM], kg_ref[...], cos_t, sin_t)
               for j in range(C_KV_HEADS)]
    k_ref[...] = jnp.concatenate(k_parts, axis=0).T.astype(BF16)
    v_t = v.T.astype(BF16)
    for c in range(tm // KV_BLOCK):
        vt_ref[c] = v_t[:, c * KV_BLOCK:(c + 1) * KV_BLOCK]


def _projection(even, x2, scp, sh, ng, w, gains, cos_t, sin_t, tm):
    r = x2.shape[0]
    nt = r // tm
    tiles_per_group = nt // scp.shape[0]
    pos_tiles = cos_t.shape[1] // tm
    n_in = w.shape[1]
    row = lambda i: (i, 0)
    grp = lambda i: (i // tiles_per_group, 0, 0)
    fixed = lambda i: (0, 0)
    pos = lambda i: (0, i % pos_tiles)
    in_specs = [
        pl.BlockSpec((tm, D_MODEL), row),
        pl.BlockSpec((1, 1, D_MODEL), grp),
        pl.BlockSpec((1, 1, D_MODEL), grp),
        pl.BlockSpec((1, D_MODEL), fixed),
        pl.BlockSpec((D_MODEL, n_in), fixed),
    ]
    in_specs += [pl.BlockSpec(g.shape, fixed) for g in gains]
    in_specs += [pl.BlockSpec((HEAD_DIM, tm), pos), pl.BlockSpec((HEAD_DIM, tm), pos)]
    if even:
        kw, vw, nh = B_WIDTH, B_WIDTH, 2 * B_HEADS
        body = functools.partial(_proj_even_kernel, tm=tm)
        out_shape = [
            jax.ShapeDtypeStruct((r, A_WIDTH), F32),
            jax.ShapeDtypeStruct((r, A_WIDTH), BF16),
            jax.ShapeDtypeStruct((nh, LANES, r), BF16),
            jax.ShapeDtypeStruct((r, kw), BF16),
            jax.ShapeDtypeStruct((r // KV_BLOCK, vw, KV_BLOCK), BF16),
        ]
        out_specs = [
            pl.BlockSpec((tm, A_WIDTH), row),
            pl.BlockSpec((tm, A_WIDTH), row),
            pl.BlockSpec((nh, LANES, tm), lambda i: (0, 0, i)),
            pl.BlockSpec((tm, kw), row),
            pl.BlockSpec((tm // KV_BLOCK, vw, KV_BLOCK), lambda i: (i, 0, 0)),
        ]
    else:
        kw, vw, nh = C_KV_WIDTH, C_KV_WIDTH, C_HEADS
        body = functools.partial(_proj_odd_kernel, tm=tm)
        out_shape = [
            jax.ShapeDtypeStruct((nh, LANES, r), BF16),
            jax.ShapeDtypeStruct((r, kw), BF16),
            jax.ShapeDtypeStruct((r // KV_BLOCK, vw, KV_BLOCK), BF16),
            jax.ShapeDtypeStruct((r, D_WIDTH), F32),
            jax.ShapeDtypeStruct((r, D_WIDTH), F32),
        ]
        out_specs = [
            pl.BlockSpec((nh, LANES, tm), lambda i: (0, 0, i)),
            pl.BlockSpec((tm, kw), row),
            pl.BlockSpec((tm // KV_BLOCK, vw, KV_BLOCK), lambda i: (i, 0, 0)),
            pl.BlockSpec((tm, D_WIDTH), row),
            pl.BlockSpec((tm, D_WIDTH), row),
        ]
    return pl.pallas_call(
        body, grid=(nt,), in_specs=in_specs, out_specs=out_specs, out_shape=out_shape,
        compiler_params=_cparams(("arbitrary",)),
        name="proj_even" if even else "proj_odd",
    )(x2, scp, sh, ng, w, *gains, cos_t, sin_t)


def _running_maximum(q, kc_ref, s_ref, p_ref, acc_ref, ones, k_lat, vt_lat, vt_ctx, n_lat, tk):
    def scores(kb, slot):
        n = kb.shape[0]
        mb = []
        for c in range(2):
            s = jnp.dot(kb, q[c], preferred_element_type=F32)
            s_ref[slot, c, 0:n] = s
            mb.append(jnp.max(s, axis=0, keepdims=True))
        return tuple(mb)

    def softmax(slot, n, mb, m):
        out, alphas = [], []
        for c in range(2):
            m_new = jnp.maximum(m[c], mb[c])
            p_ref[slot, c, 0:n] = jnp.exp2(s_ref[slot, c, 0:n] - m_new).astype(BF16)
            alphas.append(jnp.exp2(m[c] - m_new))
            out.append(m_new)
        return tuple(out), tuple(alphas)

    def values(vt_blocks, slot, alphas):
        for c in range(2):
            upd = alphas[c] * acc_ref[c]
            for i, vtb in enumerate(vt_blocks):
                lhs = jnp.concatenate([vtb, ones], axis=0)
                upd = upd + jnp.dot(lhs, p_ref[slot, c, i * KV_BLOCK:(i + 1) * KV_BLOCK],
                                    preferred_element_type=F32)
            acc_ref[c] = upd

    lc = kc_ref.shape[0]
    tq = q[0].shape[-1]
    neg = jnp.full((1, tq), -jnp.inf, F32)
    m = (neg, neg)
    mb = scores(kc_ref[...], 0)
    if not n_lat:
        m, al = softmax(0, lc, mb, m)
        values(vt_ctx, 0, al)
    else:
        assert n_lat % 2 == 0
        mb_next = scores(k_lat(0), 1)
        m, al = softmax(0, lc, mb, m)
        mb = mb_next
        mb_next = scores(k_lat(1), 0)
        m, al_next = softmax(1, tk, mb, m)
        values(vt_ctx, 0, al)
        mb, al = mb_next, al_next

        def body(i, carry):
            m, mb, al = carry
            t = 2 * i + 2
            mb_next = scores(k_lat(t), 1)
            m, al_next = softmax(0, tk, mb, m)
            values(vt_lat(t - 2), 1, al)
            mb, al = mb_next, al_next
            mb_next = scores(k_lat(t + 1), 0)
            m, al_next = softmax(1, tk, mb, m)
            values(vt_lat(t - 1), 0, al)
            return m, mb_next, al_next

        m, mb, al = lax.fori_loop(0, (n_lat - 2) // 2, body, (m, mb, al))
        m, al_next = softmax(0, tk, mb, m)
        values(vt_lat(n_lat - 2), 1, al)
        values(vt_lat(n_lat - 1), 0, al_next)


def _attn_kernel(par_ref, *refs, mode, n_lat, tk, lam_init):
    qt_ref, kc_ref, vtc_ref = refs[:3]
    pos = 3
    if n_lat:
        kl_ref, vtl_ref = refs[3:5]
        pos = 5
    if mode == "diff":
        lamp_ref, sg_ref = refs[pos:pos + 2]
        pos += 2
    o_ref, acc_ref, s_ref, p_ref = refs[pos:pos + 4]
    tq = qt_ref.shape[-1]
    q = (qt_ref[0], qt_ref[1])
    acc_ref[...] = jnp.zeros(acc_ref.shape, F32)

    lc = kc_ref.shape[0]
    ones = jnp.ones((DEN_ROWS, KV_BLOCK), BF16)
    per = tk // KV_BLOCK

    def k_lat(j):
        return kl_ref[pl.ds(pl.multiple_of(j * tk, tk), tk), :]

    def vt_lat(j):
        return [vtl_ref[j * per + i] for i in range(per)]

    vt_ctx = [vtc_ref[i] for i in range(lc // KV_BLOCK)]

    def fixed_reference():
        bound = par_ref[0]
        vrows = vtc_ref.shape[1]
        den_on_mxu = vrows + DEN_ROWS <= LANES

        def numer(kb, slot, base):
            n = kb.shape[0]
            for c in range(2):
                s = jnp.dot(kb, q[c], preferred_element_type=F32)
                p = jnp.exp2(s - bound)
                p_ref[slot, c, base:base + n] = p.astype(BF16)
                if not den_on_mxu:
                    part = jnp.sum(p.reshape(n // SUBLANES, SUBLANES, tq), axis=0)
                    acc_ref[c, vrows:vrows + SUBLANES, :] += part

        def accum(vt_blocks, slot, n):
            lhs = jnp.concatenate(vt_blocks, axis=1)
            if den_on_mxu:
                lhs = jnp.concatenate([lhs, jnp.ones((DEN_ROWS, n), BF16)], axis=0)
            rows = lhs.shape[0]
            for c in range(2):
                acc_ref[c, 0:rows, :] += jnp.dot(lhs, p_ref[slot, c, 0:n], preferred_element_type=F32)

        def finish():
            if not den_on_mxu:
                for c in range(2):
                    total = jnp.sum(acc_ref[c, vrows:vrows + SUBLANES, :], axis=0, keepdims=True)
                    acc_ref[c, vrows:vrows + SUBLANES, :] = jnp.broadcast_to(total, (SUBLANES, tq))

        def numer_pair(j, slot):
            numer(k_lat(2 * j), slot, 0)
            numer(k_lat(2 * j + 1), slot, tk)

        def accum_pair(j, slot):
            accum(vt_lat(2 * j) + vt_lat(2 * j + 1), slot, 2 * tk)

        numer(kc_ref[...], 1, 0)
        if not n_lat:
            accum(vt_ctx, 1, lc)
            finish()
            return
        n_pairs = n_lat // 2
        assert n_lat % 4 == 0
        numer_pair(0, 0)
        accum(vt_ctx, 1, lc)

        def body(j, _):
            numer_pair(2 * j + 1, 1)
            accum_pair(2 * j, 0)
            numer_pair(2 * j + 2, 0)
            accum_pair(2 * j + 1, 1)
            return 0

        lax.fori_loop(0, n_pairs // 2 - 1, body, 0)
        numer_pair(n_pairs - 1, 1)
        accum_pair(n_pairs - 2, 0)
        accum_pair(n_pairs - 1, 1)
        finish()

    def running_maximum():
        _running_maximum(q, kc_ref, s_ref, p_ref, acc_ref, ones, k_lat, vt_lat, vt_ctx, n_lat, tk)

    use_fixed = par_ref[1] > 0.5
    pl.when(use_fixed)(fixed_reference)
    pl.when(jnp.logical_not(use_fixed))(running_maximum)

    vrows = vtc_ref.shape[1]
    l0 = acc_ref[0, vrows:vrows + 1, :]
    l1 = acc_ref[1, vrows:vrows + 1, :]
    if mode == "diff":
        lp = lamp_ref[...]
        lam = (jnp.exp(jnp.sum(lp[0:1] * lp[1:2], axis=-1, keepdims=True))
               - jnp.exp(jnp.sum(lp[2:3] * lp[3:4], axis=-1, keepdims=True)) + lam_init)
        o = acc_ref[0, 0:LANES, :] * (1.0 / l0) - lam * (acc_ref[1, 0:LANES, :] * (1.0 / l1))
        o = o * lax.rsqrt(jnp.mean(o * o, axis=0, keepdims=True) + EPS) * sg_ref[...] * (1.0 - lam_init)
    else:
        o = jnp.concatenate([acc_ref[0, 0:HEAD_DIM, :] * (1.0 / l0),
                             acc_ref[1, 0:HEAD_DIM, :] * (1.0 / l1)], axis=0)
    o_ref[...] = o.T.astype(o_ref.dtype)


def _attention(mode, par, qt, kc, vtc, kl, vtl, extra, batch, tq, tk, lam_init=0.0):
    units = qt.shape[0] // 2
    rq = qt.shape[2]
    nq = rq // batch // tq
    lc = kc.shape[0] // batch
    col = (lambda u: u) if mode == "diff" else (lambda u: 0)
    vrows = LANES if mode == "diff" else HEAD_DIM
    vcol = (lambda u: u) if mode == "diff" else (lambda u: u // (C_GROUP // 2))
    in_specs = [
        pl.BlockSpec(memory_space=pltpu.SMEM),
        pl.BlockSpec((2, LANES, tq), lambda b, u, i: (u, 0, b * nq + i)),
        pl.BlockSpec((lc, LANES), lambda b, u, i: (b, col(u))),
        pl.BlockSpec((lc // KV_BLOCK, vrows, KV_BLOCK), lambda b, u, i: (b, vcol(u), 0)),
    ]
    args = [par, qt, kc, vtc]
    n_lat = 0
    if kl is not None:
        ll = kl.shape[0] // batch
        n_lat = ll // tk
        in_specs += [
            pl.BlockSpec((ll, LANES), lambda b, u, i: (b, col(u))),
            pl.BlockSpec((ll // KV_BLOCK, vrows, KV_BLOCK), lambda b, u, i: (b, vcol(u), 0)),
        ]
        args += [kl, vtl]
    for e in extra:
        in_specs.append(pl.BlockSpec(e.shape, lambda b, u, i: (0, 0)))
        args.append(e)
    rows = max(tk, lc)
    return pl.pallas_call(
        functools.partial(_attn_kernel, mode=mode, n_lat=n_lat, tk=tk, lam_init=lam_init),
        grid=(batch, units, nq),
        in_specs=in_specs,
        out_specs=pl.BlockSpec((tq, LANES), lambda b, u, i: (b * nq + i, u)),
        out_shape=jax.ShapeDtypeStruct((rq, units * LANES), BF16),
        scratch_shapes=[pltpu.VMEM((2, vrows + DEN_ROWS, tq), F32),
                        pltpu.VMEM((2, 2, rows, tq), F32),
                        pltpu.VMEM((2, 2, max(2 * tk, lc), tq), BF16)],
        compiler_params=_cparams(("arbitrary", "arbitrary", "arbitrary")),
        name="attn_" + mode,
    )(*args)


def _gmlp_rows(u_ref, vn_ref, ws_ref, bias_ref, o_ref):
    lane = lax.broadcasted_iota(jnp.int32, (CHUNK, LANES), 1)
    for c in range(u_ref.shape[0] // CHUNK):
        rows = slice(c * CHUNK, (c + 1) * CHUNK)
        for j in range(A_GROUPS // 2):
            cols = slice(j * LANES, (j + 1) * LANES)
            rhs = vn_ref[rows, cols]
            lo = jnp.dot(ws_ref[2 * j], rhs, preferred_element_type=F32)
            hi = jnp.dot(ws_ref[2 * j + 1], rhs, preferred_element_type=F32)
            mixed = jnp.where(lane < HEAD_DIM, lo, hi) + bias_ref[:, cols]
            o_ref[rows, cols] = (u_ref[rows, cols] * mixed).astype(o_ref.dtype)


def _out_mix(x, ya, yb, w_ref, g):
    half = ya.shape[1]
    y = (jnp.dot(ya, w_ref[0:half, :], preferred_element_type=F32)
         + jnp.dot(yb, w_ref[half:, :], preferred_element_type=F32))
    return x + g * y


def _out_even_kernel(x_ref, u_ref, vn_ref, ws_ref, bias_ref, yb_ref, w_ref, g_ref, o_ref, ya_ref):
    _gmlp_rows(u_ref, vn_ref, ws_ref, bias_ref, ya_ref)
    o_ref[...] = _out_mix(x_ref[...], ya_ref[...], yb_ref[...], w_ref, g_ref[0])


def _out_proj_even(x2, u, vn, ws, bias_map, yb, w, g, tm):
    r = x2.shape[0]
    tiles_per_group = (r // tm) // g.shape[0]
    row = lambda i: (i, 0)
    return pl.pallas_call(
        _out_even_kernel,
        grid=(r // tm,),
        in_specs=[
            pl.BlockSpec((tm, D_MODEL), row),
            pl.BlockSpec((tm, A_WIDTH), row),
            pl.BlockSpec((tm, A_WIDTH), row),
            pl.BlockSpec(ws.shape, lambda i: (0, 0, 0)),
            pl.BlockSpec(bias_map.shape, lambda i: (0, 0)),
            pl.BlockSpec((tm, yb.shape[1]), row),
            pl.BlockSpec(w.shape, lambda i: (0, 0)),
            pl.BlockSpec((1, 1, D_MODEL), lambda i: (i // tiles_per_group, 0, 0)),
        ],
        out_specs=pl.BlockSpec((tm, D_MODEL), row),
        out_shape=jax.ShapeDtypeStruct((r, D_MODEL), F32),
        scratch_shapes=[pltpu.VMEM((tm, A_WIDTH), BF16)],
        compiler_params=_cparams(("arbitrary",)),
        name="out_proj_even",
    )(x2, u, vn, ws, bias_map, yb, w, g)


def _swiglu(h, w1, w3, w2):
    a = jnp.dot(h, w1, preferred_element_type=F32)
    b = jnp.dot(h, w3, preferred_element_type=F32)
    t = (a * jax.nn.sigmoid(a)) * b
    return jnp.dot(t.astype(BF16), w2, preferred_element_type=F32)


def _ffn_kernel(x_ref, scp_ref, sh_ref, g_ref, ng_ref, w1_ref, w3_ref, w2_ref, *rest):
    o_ref = rest[-1]
    n_cast = len(rest) // 2
    for src, dst in zip(rest[:n_cast], rest[n_cast:2 * n_cast]):
        dst[...] = src[...].astype(BF16)
    x = x_ref[...]
    h = _modnorm(x, ng_ref[...], scp_ref[0], sh_ref[0]).astype(BF16)
    o_ref[...] = x + g_ref[0] * _swiglu(h, w1_ref[...], w3_ref[...], w2_ref[...])


def _ffn(x2, scp, sh, g, ng, w1, w3, w2, tm, casts=()):
    r = x2.shape[0]
    tiles_per_group = (r // tm) // g.shape[0]
    row = lambda i: (i, 0)
    grp = lambda i: (i // tiles_per_group, 0, 0)
    fixed = lambda i: (0, 0)
    resident = lambda w: pl.BlockSpec(w.shape, fixed, pipeline_mode=pl.Buffered(1))
    in_specs = [
        pl.BlockSpec((tm, D_MODEL), row),
        pl.BlockSpec((1, 1, D_MODEL), grp),
        pl.BlockSpec((1, 1, D_MODEL), grp),
        pl.BlockSpec((1, 1, D_MODEL), grp),
        pl.BlockSpec((1, D_MODEL), fixed),
        resident(w1), resident(w3), resident(w2),
    ]
    out_specs = [pl.BlockSpec((tm, D_MODEL), row)]
    out_shape = [jax.ShapeDtypeStruct((r, D_MODEL), F32)]
    args = [x2, scp, sh, g, ng, w1, w3, w2]
    for n, c in enumerate(casts):
        cin, cout, cshape = _side_cast_specs(c, r // tm, row)
        in_specs.append(cin)
        out_specs.insert(n, cout)
        out_shape.insert(n, cshape)
        args.append(c)
    out = pl.pallas_call(
        _ffn_kernel, grid=(r // tm,), in_specs=in_specs, out_specs=out_specs, out_shape=out_shape,
        compiler_params=_cparams(("arbitrary",)),
        name="ffn",
    )(*args)
    return out if casts else out[0]


META_E1, META_E2, META_G1, META_G2, META_R1, META_R2 = range(6)


def _lane_col(x, lane, k):
    return jnp.sum(jnp.where(lane == k, x, 0.0), axis=-1, keepdims=True)


def _router_kernel(x_ref, ya_ref, s1_ref, s2_ref, agg_ref, hb_ref, wo_ref, g1_ref, scp_ref, sh_ref, ng_ref,
                   whi_ref, wlo_ref, b_ref, x1_ref, hn_ref, meta_ref, cnt_ref, carry_ref, rev_ref):
    n_tiles = agg_ref.shape[0]
    batch = hb_ref.shape[0]

    @pl.when(pl.program_id(0) == 0)
    def _():
        carry_ref[...] = jnp.zeros(carry_ref.shape, F32)
        per = n_tiles // batch
        for b in range(batch):
            c = hb_ref[b, 0:1, :]
            for i in range(per - 1, -1, -1):
                t = b * per + i
                rev_ref[t] = c
                c = agg_ref[t, 1:2, :] + agg_ref[t, 0:1, :] * c

    yb = (s1_ref[...] + s2_ref[...] * rev_ref[pl.program_id(0)]).astype(BF16)
    x1 = _out_mix(x_ref[...], ya_ref[...], yb, wo_ref, g1_ref[0])
    x1_ref[...] = x1
    h = _modnorm(x1, ng_ref[...], scp_ref[0], sh_ref[0])
    hn_ref[...] = h
    h_hi = h.astype(BF16)
    h_lo = (h - h_hi.astype(F32)).astype(BF16)
    logits = (jnp.dot(h_hi, whi_ref[...], preferred_element_type=F32)
              + jnp.dot(h_lo, whi_ref[...], preferred_element_type=F32)
              + jnp.dot(h_hi, wlo_ref[...], preferred_element_type=F32)) + b_ref[...]
    tm = logits.shape[0]
    lane = lax.broadcasted_iota(jnp.int32, logits.shape, 1).astype(F32)
    logits = jnp.where(lane < N_EXPERTS, logits, -jnp.inf)
    m1 = jnp.max(logits, axis=-1, keepdims=True)
    i1 = jnp.min(jnp.where(logits == m1, lane, float(LANES)), axis=-1, keepdims=True)
    rest = jnp.where(lane == i1, -jnp.inf, logits)
    m2 = jnp.max(rest, axis=-1, keepdims=True)
    i2 = jnp.min(jnp.where(rest == m2, lane, float(LANES)), axis=-1, keepdims=True)
    e2 = jnp.exp(m2 - m1)
    den = 1.0 + e2
    oh1 = lane == i1
    oh2 = lane == i2
    both = jnp.where(jnp.logical_or(oh1, oh2), 1.0, 0.0)
    tri = (lax.broadcasted_iota(jnp.int32, (tm, tm), 1) < lax.broadcasted_iota(jnp.int32, (tm, tm), 0))
    before = jnp.dot(jnp.where(tri, 1.0, 0.0).astype(BF16), both.astype(BF16),
                     preferred_element_type=F32) + carry_ref[0:1, :]
    r1 = jnp.sum(jnp.where(oh1, before, 0.0), axis=-1, keepdims=True)
    r2 = jnp.sum(jnp.where(oh2, before, 0.0), axis=-1, keepdims=True)
    meta = jnp.zeros(logits.shape, F32)
    for k, v in ((META_E1, i1), (META_E2, i2), (META_G1, 1.0 / den), (META_G2, e2 / den),
                 (META_R1, r1), (META_R2, r2)):
        meta = jnp.where(lane == k, v, meta)
    meta_ref[...] = meta
    carry_ref[...] = carry_ref[...] + jnp.sum(both, axis=0, keepdims=True)
    cnt_ref[...] = carry_ref[...]


def _router(x2, ya, s1, s2, agg, h_rev, w_out, g1, scp, sh, ng, w_hi, w_lo, b, tm):
    r = x2.shape[0]
    assert agg.shape[0] == r // tm
    tiles_per_group = (r // tm) // scp.shape[0]
    row = lambda i: (i, 0)
    grp = lambda i: (i // tiles_per_group, 0, 0)
    fixed = lambda i: (0, 0)
    return pl.pallas_call(
        _router_kernel,
        grid=(r // tm,),
        in_specs=[
            pl.BlockSpec((tm, D_MODEL), row),
            pl.BlockSpec((tm, ya.shape[1]), row),
            pl.BlockSpec((tm, D_WIDTH), row),
            pl.BlockSpec((tm, D_WIDTH), row),
            pl.BlockSpec(agg.shape, lambda i: (0, 0, 0)),
            pl.BlockSpec(h_rev.shape, lambda i: (0, 0, 0)),
            pl.BlockSpec(w_out.shape, fixed),
            pl.BlockSpec((1, 1, D_MODEL), grp),
            pl.BlockSpec((1, 1, D_MODEL), grp),
            pl.BlockSpec((1, 1, D_MODEL), grp),
            pl.BlockSpec((1, D_MODEL), fixed),
            pl.BlockSpec(w_hi.shape, fixed),
            pl.BlockSpec(w_lo.shape, fixed),
            pl.BlockSpec(b.shape, fixed),
        ],
        out_specs=[
            pl.BlockSpec((tm, D_MODEL), row),
            pl.BlockSpec((tm, D_MODEL), row),
            pl.BlockSpec((tm, LANES), row),
            pl.BlockSpec((SUBLANES, LANES), fixed),
        ],
        out_shape=[
            jax.ShapeDtypeStruct((r, D_MODEL), F32),
            jax.ShapeDtypeStruct((r, D_MODEL), F32),
            jax.ShapeDtypeStruct((r, LANES), F32),
            jax.ShapeDtypeStruct((SUBLANES, LANES), F32),
        ],
        scratch_shapes=[pltpu.VMEM((SUBLANES, LANES), F32), pltpu.VMEM((agg.shape[0], 1, D_WIDTH), F32)],
        compiler_params=_cparams(("arbitrary",)),
        name="router",
    )(x2, ya, s1, s2, agg, h_rev, w_out, g1, scp, sh, ng, w_hi, w_lo, b)


def _row_copy(src_ref, src_row, dst_ref, dst_row, sem):
    return pltpu.make_async_copy(src_ref.at[pl.ds(src_row, 1)], dst_ref.at[pl.ds(dst_row, 1)], sem)


def _dispatch_kernel(pos_ref, pb_ref, pn_ref, nu_ref, h_ref, xs_ref, zero_ref, sem, zsem):
    tm = h_ref.shape[0]
    tile = zero_ref.shape[0]
    n_tiles = xs_ref.shape[0] // tile
    base = pl.program_id(0) * (2 * tm)

    def zero_fill(op):
        for e in range(N_EXPERTS):
            first = pb_ref[e]
            head = (-first) & (SUBLANES - 1)
            for j in range(SUBLANES - 1):
                @pl.when(j < head)
                def _():
                    getattr(_row_copy(zero_ref, 0, xs_ref, first + j, zsem), op)()
            n = pn_ref[e] - head
            for bit in range(tile.bit_length() - 2, SUBLANES.bit_length() - 2, -1):
                size = 1 << bit
                off = pl.multiple_of(first + head + n - (n & (2 * size - 1)), SUBLANES)

                @pl.when((n & size) != 0)
                def _():
                    cp = pltpu.make_async_copy(zero_ref.at[pl.ds(0, size)], xs_ref.at[pl.ds(off, size)], zsem)
                    getattr(cp, op)()
        for j in range(N_EXPERTS):
            t = nu_ref[0] + j

            @pl.when(t < n_tiles)
            def _():
                cp = pltpu.make_async_copy(zero_ref, xs_ref.at[pl.ds(pl.multiple_of(t * tile, tile), tile)], zsem)
                getattr(cp, op)()

    @pl.when(pl.program_id(0) == 0)
    def _():
        zero_ref[...] = jnp.zeros(zero_ref.shape, F32)
        zero_fill("start")

    def issue(r, _):
        for k in range(2):
            _row_copy(h_ref, r, xs_ref, pos_ref[base + 2 * r + k], sem).start()
        return 0

    lax.fori_loop(0, tm, issue, 0, unroll=8)
    for k in range(2):
        pltpu.make_async_copy(h_ref, xs_ref.at[pl.ds(0, tm)], sem).wait()

    @pl.when(pl.program_id(0) == 0)
    def _():
        zero_fill("wait")


def _dispatch(pos_flat, pad_base, pad_n, n_used, hn, n_rows, tm, tile):
    r = hn.shape[0]
    return pl.pallas_call(
        _dispatch_kernel,
        grid_spec=pltpu.PrefetchScalarGridSpec(
            num_scalar_prefetch=4,
            grid=(r // tm,),
            in_specs=[pl.BlockSpec((tm, D_MODEL), lambda i, *_: (i, 0))],
            out_specs=pl.BlockSpec(memory_space=pl.ANY),
            scratch_shapes=[pltpu.VMEM((tile, D_MODEL), F32), pltpu.SemaphoreType.DMA(()),
                            pltpu.SemaphoreType.DMA(())],
        ),
        out_shape=jax.ShapeDtypeStruct((n_rows, D_MODEL), F32),
        compiler_params=_cparams(("arbitrary",)),
        name="dispatch",
    )(pos_flat, pad_base, pad_n, n_used, hn)


def _expert_kernel(te_ref, nu_ref, xs_ref, w1_ref, w3_ref, w2_ref, o_ref):
    del te_ref

    @pl.when(pl.program_id(0) < nu_ref[0])
    def _():
        o_ref[...] = _swiglu(xs_ref[...].astype(BF16), w1_ref[0], w3_ref[0], w2_ref[0])

    @pl.when(pl.program_id(0) >= nu_ref[0])
    def _():
        o_ref[...] = jnp.zeros(o_ref.shape, F32)


def _experts(tile_expert, n_used, xs, w1, w3, w2, tm):
    n_tiles = xs.shape[0] // tm
    wspec = lambda w: pl.BlockSpec((1,) + w.shape[1:], lambda i, te, nu: (te[i], 0, 0))
    return pl.pallas_call(
        _expert_kernel,
        grid_spec=pltpu.PrefetchScalarGridSpec(
            num_scalar_prefetch=2,
            grid=(n_tiles,),
            in_specs=[
                pl.BlockSpec((tm, D_MODEL), lambda i, te, nu: (jnp.minimum(i, nu[0] - 1), 0)),
                wspec(w1), wspec(w3), wspec(w2),
            ],
            out_specs=pl.BlockSpec((tm, D_MODEL), lambda i, te, nu: (i, 0)),
        ),
        out_shape=jax.ShapeDtypeStruct(xs.shape, F32),
        compiler_params=_cparams(("arbitrary",)),
        name="experts",
    )(tile_expert, n_used, xs, w1, w3, w2)


def _combine_kernel(pos_ref, x_ref, meta_ref, g_ref, ys_ref, o_ref, buf_ref, sem):
    tm = x_ref.shape[0]
    i = pl.program_id(0)

    def gather(tile, slot):
        base = tile * (2 * tm)

        def issue(r, _):
            for k in range(2):
                _row_copy(ys_ref, pos_ref[base + 2 * r + k], buf_ref.at[slot, k], r, sem.at[slot]).start()
            return 0

        lax.fori_loop(0, tm, issue, 0, unroll=8)

    @pl.when(i == 0)
    def _():
        gather(0, 0)

    slot = i % 2

    @pl.when(i + 1 < pl.num_programs(0))
    def _():
        gather(i + 1, 1 - slot)

    for k in range(2):
        pltpu.make_async_copy(ys_ref.at[pl.ds(0, tm)], buf_ref.at[slot, k], sem.at[slot]).wait()
    meta = meta_ref[...]
    lane = lax.broadcasted_iota(jnp.int32, meta.shape, 1)
    y = _lane_col(meta, lane, META_G1) * buf_ref[slot, 0] + _lane_col(meta, lane, META_G2) * buf_ref[slot, 1]
    o_ref[...] = x_ref[...] + g_ref[0] * y


def _combine(pos_flat, x2, meta, g, ys, tm):
    r = x2.shape[0]
    tiles_per_group = (r // tm) // g.shape[0]
    return pl.pallas_call(
        _combine_kernel,
        grid_spec=pltpu.PrefetchScalarGridSpec(
            num_scalar_prefetch=1,
            grid=(r // tm,),
            in_specs=[
                pl.BlockSpec((tm, D_MODEL), lambda i, pos: (i, 0)),
                pl.BlockSpec((tm, LANES), lambda i, pos: (i, 0)),
                pl.BlockSpec((1, 1, D_MODEL), lambda i, pos: (i // tiles_per_group, 0, 0)),
                pl.BlockSpec(memory_space=pl.ANY),
            ],
            out_specs=pl.BlockSpec((tm, D_MODEL), lambda i, pos: (i, 0)),
            scratch_shapes=[pltpu.VMEM((2, 2, tm, D_MODEL), F32), pltpu.SemaphoreType.DMA((2,))],
        ),
        out_shape=jax.ShapeDtypeStruct((r, D_MODEL), F32),
        compiler_params=_cparams(("arbitrary",)),
        name="combine",
    )(pos_flat, x2, meta, g, ys)


def _routing_plan(meta, counts, tile):
    n = meta.shape[0]
    n_tiles = 2 * n // tile + N_EXPERTS
    e = meta[:, META_E1:META_E2 + 1].astype(jnp.int32)
    rank = meta[:, META_R1:META_R2 + 1].astype(jnp.int32)
    cnt = counts[0, :N_EXPERTS].astype(jnp.int32)
    tiles_e = (cnt + tile - 1) // tile
    ends = jnp.cumsum(tiles_e)
    starts = (ends - tiles_e) * tile
    pos = (starts[e] + rank).reshape(-1)
    n_used = ends[-1:]
    t_idx = jnp.minimum(jnp.arange(n_tiles, dtype=jnp.int32), n_used[0] - 1)
    tile_expert = jnp.sum((t_idx[:, None] >= ends[None, :]).astype(jnp.int32), axis=1)
    return (pos, tile_expert.astype(jnp.int32), n_used.astype(jnp.int32), (starts + cnt).astype(jnp.int32),
            (tiles_e * tile - cnt).astype(jnp.int32), n_tiles * tile)


LRU_TILE = 512
LRU_CHUNKS = D_WIDTH // LANES
LRU_PAD = SUBLANES


def _lru_pitch(tile):
    seg = tile // SUBLANES
    assert (seg // SUBLANES) % 2 == 0
    return seg, seg + LRU_PAD


def _tile_scans(af_ref, bf_ref, ab_ref, bb_ref, seg, pitch, carry_f):
    nc = LRU_CHUNKS

    def rows_f(t):
        return pl.ds(t, SUBLANES, stride=pitch)

    def rows_b(t):
        return pl.ds(seg - 1 - t, SUBLANES, stride=pitch)

    def ends(t, c):
        hf, pf, hb, pb = c
        rf, rb = rows_f(t), rows_b(t)
        nhf, npf, nhb, npb = [], [], [], []
        for cc in range(nc):
            a = af_ref[cc, rf, :]
            nhf.append(a * hf[cc] + bf_ref[cc, rf, :])
            npf.append(a * pf[cc])
            a = ab_ref[cc, rb, :]
            nhb.append(a * hb[cc] + bb_ref[cc, rb, :])
            npb.append(a * pb[cc])
        return tuple(nhf), tuple(npf), tuple(nhb), tuple(npb)

    z = tuple(jnp.zeros((SUBLANES, LANES), F32) for _ in range(nc))
    o = tuple(jnp.ones((SUBLANES, LANES), F32) for _ in range(nc))
    hf, pf, hb, pb = lax.fori_loop(0, seg, ends, (z, o, z, o), unroll=8)

    in_f, in_b, in_p, out_f, tile_p, tile_h = [], [], [], [], [], []
    for cc in range(nc):
        cin = carry_f[cc]
        rows = [None] * SUBLANES
        for s in range(SUBLANES):
            rows[s] = cin
            cin = hf[cc][s:s + 1] + pf[cc][s:s + 1] * cin
        in_f.append(jnp.concatenate(rows, axis=0))
        out_f.append(cin)
        cin = jnp.zeros((1, LANES), F32)
        pin = jnp.ones((1, LANES), F32)
        rows_h = [None] * SUBLANES
        rows_p = [None] * SUBLANES
        for s in range(SUBLANES - 1, -1, -1):
            rows_h[s] = cin
            rows_p[s] = pin
            cin = hb[cc][s:s + 1] + pb[cc][s:s + 1] * cin
            pin = pb[cc][s:s + 1] * pin
        in_b.append(jnp.concatenate(rows_h, axis=0))
        in_p.append(jnp.concatenate(rows_p, axis=0))
        tile_h.append(cin)
        tile_p.append(pin)

    def final(t, c):
        hf, hb, pb = c
        rf, rb = rows_f(t), rows_b(t)
        nhf, nhb, npb = [], [], []
        for cc in range(nc):
            h = af_ref[cc, rf, :] * hf[cc] + bf_ref[cc, rf, :]
            bf_ref[cc, rf, :] = h
            nhf.append(h)
            a = ab_ref[cc, rb, :]
            h = a * hb[cc] + bb_ref[cc, rb, :]
            p = a * pb[cc]
            bb_ref[cc, rb, :] = h
            ab_ref[cc, rb, :] = p
            nhb.append(h)
            npb.append(p)
        return tuple(nhf), tuple(nhb), tuple(npb)

    lax.fori_loop(0, seg, final, (tuple(in_f), tuple(in_b), tuple(in_p)), unroll=8)
    return out_f, tile_p, tile_h


def _lru_main_kernel(prev_ref, cur_ref, next_ref, gg_ref, cw_ref, cb_ref, w_ref, bias_ref, lam_ref, h0_ref, *rest):
    s1_ref, s2_ref, agg_ref, hl_ref, ext_ref, af_ref, bf_ref, ab_ref, bb_ref, st_ref = rest[-10:]
    if len(rest) == 12:
        rest[1][...] = rest[0][...].astype(BF16)
    i = pl.program_id(1)
    nt = pl.num_programs(1)
    t = cur_ref.shape[0]
    seg, pitch = _lru_pitch(t)
    pad = SUBLANES

    @pl.when(i == 0)
    def _():
        st_ref[...] = h0_ref[0]

    ext_ref[0:pad, :] = jnp.where(i == 0, 0.0, prev_ref[...])
    ext_ref[pad:pad + t, :] = cur_ref[...]
    ext_ref[pad + t:, :] = jnp.where(i == nt - 1, 0.0, next_ref[...])
    left = CONV_W // 2
    xd = cb_ref[...] + sum(cw_ref[j:j + 1, :] * ext_ref[pad + j - left:pad + j - left + t, :] for j in range(CONV_W))
    z = jnp.dot(xd.astype(BF16), w_ref[...], preferred_element_type=F32) + bias_ref[...]
    decay = LRU_C * jax.nn.softplus(-lam_ref[...])
    for d, (a_ref, b_ref) in enumerate(((af_ref, bf_ref), (ab_ref, bb_ref))):
        r = 0.5 * jnp.tanh(0.5 * z[:, (2 * d) * D_WIDTH:(2 * d + 1) * D_WIDTH]) + 0.5
        g = 0.5 * jnp.tanh(0.5 * z[:, (2 * d + 1) * D_WIDTH:(2 * d + 2) * D_WIDTH]) + 0.5
        nla = r * decay[:, d * D_WIDTH:(d + 1) * D_WIDTH]
        a = jnp.exp(-nla)
        m = jnp.tanh(nla) * (a * a + 1.0)
        b = jnp.where(m > 0.0, m * lax.rsqrt(m), 0.0) * (g * xd)
        for cc in range(LRU_CHUNKS):
            for s in range(SUBLANES):
                a_ref[cc, s * pitch:s * pitch + seg, :] = a[s * seg:(s + 1) * seg, cc * LANES:(cc + 1) * LANES]
                b_ref[cc, s * pitch:s * pitch + seg, :] = b[s * seg:(s + 1) * seg, cc * LANES:(cc + 1) * LANES]
    carry_f = [st_ref[0:1, cc * LANES:(cc + 1) * LANES] for cc in range(LRU_CHUNKS)]
    out_f, tile_p, tile_h = _tile_scans(af_ref, bf_ref, ab_ref, bb_ref, seg, pitch, carry_f)
    for cc in range(LRU_CHUNKS):
        cols = slice(cc * LANES, (cc + 1) * LANES)
        for s in range(SUBLANES):
            src = slice(s * pitch, s * pitch + seg)
            dst = slice(s * seg, (s + 1) * seg)
            gg = gg_ref[dst, cols]
            s1_ref[dst, cols] = (bf_ref[cc, src, :] + bb_ref[cc, src, :]) * gg
            s2_ref[dst, cols] = ab_ref[cc, src, :] * gg
    state = jnp.concatenate(out_f, axis=-1)
    st_ref[...] = jnp.broadcast_to(state, st_ref.shape)
    hl_ref[0] = jnp.broadcast_to(state, st_ref.shape)
    agg_ref[0] = jnp.concatenate([jnp.concatenate(tile_p, axis=-1), jnp.concatenate(tile_h, axis=-1),
                                  jnp.zeros((SUBLANES - 2, D_WIDTH), F32)], axis=0)


def _lru_main(xr, gg, cw, cb, w_cat, b_cat, lam_cat, h0, batch, tile, cast=None):
    r = xr.shape[0]
    nt = r // batch // tile
    sub = tile // SUBLANES
    _, pitch = _lru_pitch(tile)
    row = lambda b, i: (b * nt + i, 0)
    fixed = lambda b, i: (0, 0)
    state = pl.BlockSpec((1, SUBLANES, D_WIDTH), lambda b, i: (b, 0, 0))
    chunked = pltpu.VMEM((LRU_CHUNKS, SUBLANES * pitch, LANES), F32)
    in_specs = [
        pl.BlockSpec((SUBLANES, D_WIDTH), lambda b, i: ((b * nt + jnp.maximum(i, 1)) * sub - 1, 0)),
        pl.BlockSpec((tile, D_WIDTH), row),
        pl.BlockSpec((SUBLANES, D_WIDTH), lambda b, i: ((b * nt + jnp.minimum(i + 1, nt - 1)) * sub, 0)),
        pl.BlockSpec((tile, D_WIDTH), row),
        pl.BlockSpec(cw.shape, fixed),
        pl.BlockSpec(cb.shape, fixed),
        pl.BlockSpec(w_cat.shape, fixed),
        pl.BlockSpec(b_cat.shape, fixed),
        pl.BlockSpec(lam_cat.shape, fixed),
        state,
    ]
    out_specs = [
        pl.BlockSpec((tile, D_WIDTH), row),
        pl.BlockSpec((tile, D_WIDTH), row),
        pl.BlockSpec((1, SUBLANES, D_WIDTH), lambda b, i: (b * nt + i, 0, 0)),
        state,
    ]
    out_shape = [
        jax.ShapeDtypeStruct((r, D_WIDTH), F32),
        jax.ShapeDtypeStruct((r, D_WIDTH), F32),
        jax.ShapeDtypeStruct((batch * nt, SUBLANES, D_WIDTH), F32),
        jax.ShapeDtypeStruct((batch, SUBLANES, D_WIDTH), F32),
    ]
    args = [xr, xr, xr, gg, cw, cb, w_cat, b_cat, lam_cat, h0]
    if cast is not None:
        cin, cout, cshape = _side_cast_specs(cast, batch * nt, row)
        in_specs.append(cin)
        out_specs.insert(0, cout)
        out_shape.insert(0, cshape)
        args.append(cast)
    return pl.pallas_call(
        _lru_main_kernel,
        grid=(batch, nt),
        in_specs=in_specs,
        out_specs=out_specs,
        out_shape=out_shape,
        scratch_shapes=[pltpu.VMEM((tile + 2 * SUBLANES, D_WIDTH), F32), chunked, chunked, chunked, chunked,
                        pltpu.VMEM((SUBLANES, D_WIDTH), F32)],
        compiler_params=_cparams(("arbitrary", "arbitrary")),
        name="lru_main",
    )(*args)


def _score_bound(q_gain, k_gain):
    bound = 1.02 * QUERY_SCALE * HEAD_DIM * jnp.max(jnp.abs(q_gain)) * jnp.max(jnp.abs(k_gain))
    return jnp.stack([bound, (bound <= FIXED_REFERENCE_MAX).astype(F32)]).astype(F32)


def _rope_tables(n_lat):
    half = HEAD_DIM // 4
    freqs = ROPE_THETA ** (-jnp.arange(half, dtype=F32) / half)
    t = jnp.arange(n_lat, dtype=jnp.int32)
    row = (t // GRID_W).astype(F32)
    col = (t % GRID_W).astype(F32)
    ar = freqs[:, None] * row[None, :]
    ac = freqs[:, None] * col[None, :]
    cos_t = jnp.concatenate([jnp.cos(ar), jnp.cos(ar), jnp.cos(ac), jnp.cos(ac)], axis=0)
    sin_t = jnp.concatenate([-jnp.sin(ar), jnp.sin(ar), -jnp.sin(ac), jnp.sin(ac)], axis=0)
    return cos_t, sin_t


def _block_diag(w):
    nb, bs, _ = w.shape
    eye = jnp.eye(nb, dtype=w.dtype)
    return (eye[:, None, :, None] * w[:, :, None, :]).reshape(nb * bs, nb * bs)


def kernel(x, c, ctx, c_ctx, w_mod, b_mod, norm_mix_g, norm_ffn_g, w_out, e_w_in, a_norm_g, a_ws, a_bs, b_qnorm_g, b_knorm_g, b_lq1, b_lk1, b_lq2, b_lk2, b_subln_g, ffn_w1, ffn_w3, ffn_w2, o_w_in, c_qnorm_g, c_knorm_g, d_conv_w, d_conv_b, d_wa, d_ba, d_wx, d_bx, d_lambda, router_w, router_b, moe_w1, moe_w3, moe_w2):
    batch, n_lat, d = x.shape
    n_ctx = ctx.shape[1]
    depth = w_mod.shape[0]
    assert depth == 2 and d == D_MODEL and batch + 1 <= SUBLANES
    xl = x.reshape(batch * n_lat, d)
    xc = ctx.reshape(batch * n_ctx, d)

    cc = jnp.concatenate([c, c_ctx[None, :], jnp.zeros((SUBLANES - batch - 1, d), F32)], axis=0)
    mod = _modulation(cc, w_mod, b_mod)

    def mods(layer):
        parts = [mod[layer, :, k * d:(k + 1) * d] for k in range(6)]
        lat = [p[:batch, None, :] for p in parts]
        cx = [p[batch:batch + 1, None, :] for p in parts]
        return lat, cx

    cos_t, sin_t = _rope_tables(n_lat)
    cos_c = jnp.ones((HEAD_DIM, n_ctx), F32)
    sin_c = jnp.zeros((HEAD_DIM, n_ctx), F32)
    col = lambda v: v.reshape(-1, 1)
    rowv = lambda v: v.reshape(1, -1)
    tm = 512
    tm_c = 256

    (sh1, sc1, g1, sh2, sc2, g2), (sh1c, sc1c, g1c, sh2c, sc2c, g2c) = mods(0)
    lam_init = 0.8 - 0.6 * math.exp(-0.3 * 0)
    w_in = e_w_in[0].astype(BF16)
    gains = [rowv(a_norm_g[0]), col(b_qnorm_g[0]), col(b_knorm_g[0])]
    ng = rowv(norm_mix_g[0])
    u, vn, qt, k, vt = _projection(True, xl, 1.0 + sc1, sh1, ng, w_in, gains, cos_t, sin_t, tm)
    uc, vnc, qtc, kc, vtc = _projection(True, xc, 1.0 + sc1c, sh1c, ng, w_in, gains, cos_c, sin_c, tm_c)
    lamp = jnp.stack([b_lq1[0], b_lk1[0], b_lq2[0], b_lk2[0]], axis=0)
    extra = [lamp, col(b_subln_g[0])]
    par = _score_bound(b_qnorm_g[0], b_knorm_g[0])
    yb = _attention("diff", par, qt, kc, vtc, k, vt, extra, batch, ATTN_TQ, ATTN_TK, lam_init)
    ybc = _attention("diff", par, qtc, kc, vtc, None, None, extra, batch, n_ctx, ATTN_TK, lam_init)
    ws = a_ws[0].astype(BF16)
    bias_map = jnp.repeat(a_bs[0].T, HEAD_DIM, axis=1)
    wo = w_out[0].astype(BF16)
    xl = _out_proj_even(xl, u, vn, ws, bias_map, yb, wo, g1, tm)
    xc = _out_proj_even(xc, uc, vnc, ws, bias_map, ybc, wo, g1c, tm_c)
    w1 = ffn_w1[0].astype(BF16)
    w3 = ffn_w3[0].astype(BF16)
    w2 = ffn_w2[0].astype(BF16)
    ngf = rowv(norm_ffn_g[0])
    ne, _, fdim = moe_w1.shape[1:]
    m1, m3, xl = _ffn(xl, 1.0 + sc2, sh2, g2, ngf, w1, w3, w2, tm,
                      casts=(moe_w1[0].reshape(ne * d, fdim), moe_w3[0].reshape(ne * d, fdim)))
    xc = _ffn(xc, 1.0 + sc2c, sh2c, g2c, ngf, w1, w3, w2, tm_c)

    (sh1, sc1, g1, sh2, sc2, g2), (sh1c, sc1c, _, _, _, _) = mods(1)
    w_in = o_w_in[0].astype(BF16)
    gains = [col(c_qnorm_g[0]), col(c_knorm_g[0])]
    ng = rowv(norm_mix_g[1])
    qt, k, vt, gg, xr = _projection(False, xl, 1.0 + sc1, sh1, ng, w_in, gains, cos_t, sin_t, tm)
    _, kc, vtc, _, xrc = _projection(False, xc, 1.0 + sc1c, sh1c, ng, w_in, gains, cos_c, sin_c, tm_c)
    y_attn = _attention("gqa", _score_bound(c_qnorm_g[0], c_knorm_g[0]), qt, kc, vtc, k, vt, [], batch,
                        ATTN_TQ, ATTN_TK)

    w_cat = jnp.concatenate([_block_diag(d_wa[0, 0]), _block_diag(d_wx[0, 0]),
                             _block_diag(d_wa[0, 1]), _block_diag(d_wx[0, 1])], axis=1).astype(BF16)
    b_cat = jnp.concatenate([d_ba[0, 0], d_bx[0, 0], d_ba[0, 1], d_bx[0, 1]]).reshape(1, -1)
    lam_cat = d_lambda[0].reshape(1, -1)
    cw = d_conv_w[0]
    cb = rowv(d_conv_b[0])
    zero_state = jnp.zeros((batch, SUBLANES, D_WIDTH), F32)
    _, _, agg_c, st_f = _lru_main(xrc, xrc, cw, cb, w_cat, b_cat, lam_cat, zero_state, batch, n_ctx)
    st_b = jnp.broadcast_to(agg_c[:, 1:2, :], (batch, SUBLANES, D_WIDTH))
    m2, s1, s2, agg, _ = _lru_main(xr, gg, cw, cb, w_cat, b_cat, lam_cat, st_f, batch, LRU_TILE,
                                   cast=moe_w2[0].reshape(ne * fdim, d))

    ngf = rowv(norm_ffn_g[1])
    wr = jnp.pad(router_w[0], ((0, 0), (0, LANES - N_EXPERTS)))
    wr_hi = wr.astype(BF16)
    wr_lo = (wr - wr_hi.astype(F32)).astype(BF16)
    br = jnp.pad(router_b[0], (0, LANES - N_EXPERTS)).reshape(1, LANES)
    xl, hn, meta, counts = _router(xl, y_attn, s1, s2, agg, st_b, w_out[1].astype(BF16), g1, 1.0 + sc2, sh2, ngf,
                                   wr_hi, wr_lo, br, LRU_TILE)
    pos, tile_expert, n_used, pad_base, pad_n, n_rows = _routing_plan(meta, counts, EXPERT_TILE)
    xs = _dispatch(pos, pad_base, pad_n, n_used, hn, n_rows, 2 * tm, EXPERT_TILE)
    ys = _experts(tile_expert, n_used, xs, m1.reshape(ne, d, fdim), m3.reshape(ne, d, fdim),
                  m2.reshape(ne, fdim, d), EXPERT_TILE)
    xl = _combine(pos, xl, meta, g2, ys, COMBINE_TILE)
    return xl.reshape(batch, n_lat, d)
```
